```python
import math
import jax, jax.numpy as jnp
from jax import lax
import numpy as np

D_MODEL = 1024
BATCH = 16
SEQ = 256
DEPTH = 4
DEC_BATCH = 8
DEC_SEQ = 4096
PAST_LEN = 256

GRID_W = 64
HEAD_DIM = 64
N_EVEN = (DEPTH + 1) // 2
N_ODD = DEPTH // 2
H_A = D_MODEL // (2 * HEAD_DIM)
WIN_R = 8
WIN_C = 16
H_B = D_MODEL // (2 * HEAD_DIM)
DK_B = HEAD_DIM
DV_B = HEAD_DIM
GLA_RANK = 16
GLA_TAU = 16.0
H_C = D_MODEL // (2 * HEAD_DIM)
KV_C = H_C // 4
WIN_1D = 128
ROPE_BASE = 10000.0
H_D = D_MODEL // (2 * HEAD_DIM)
DK_D = HEAD_DIM
DV_D = HEAD_DIM
SHORT_CONV = 3
CHUNK = 64
Q_BLOCK = 128
D_FF = ((8 * D_MODEL // 3 + 127) // 128) * 128
FFN_CONV = 3
EPS = 1e-6
EV_SIZES = (H_A * HEAD_DIM, H_A * HEAD_DIM, H_A * HEAD_DIM, H_B * DK_B, H_B * DK_B, H_B * DV_B, 2 * GLA_RANK, H_B * DV_B)
OD_SIZES = (H_C * HEAD_DIM, KV_C * HEAD_DIM, KV_C * HEAD_DIM, H_D * DK_D, H_D * DK_D, H_D * DV_D, 2 * H_D, 2 * H_D, H_D * DV_D)
EV_IN = sum(EV_SIZES)
OD_IN = sum(OD_SIZES)
EV_OUT = H_A * HEAD_DIM + H_B * DV_B
OD_OUT = H_C * HEAD_DIM + H_D * DV_D
F32 = jnp.float32

kernel_name = 'hybrid_flow_trunk_step'


def rmsnorm(x, g):
    xf = x.astype(F32)
    y = xf * lax.rsqrt(jnp.mean(xf * xf, axis=-1, keepdims=True) + EPS)
    return (y * g.astype(F32)).astype(x.dtype)


def l2norm(x):
    return x * lax.rsqrt(jnp.sum(x * x, axis=-1, keepdims=True) + EPS)


def split_cols(p, sizes):
    cuts = [int(s) for s in np.cumsum(sizes)[:-1]]
    return jnp.split(p, cuts, axis=-1)


def adaln(cond, w, b):
    return jnp.split(jax.nn.silu(cond) @ w + b, 6, axis=-1)


def modulate(h, shift, scale):
    return h * (1.0 + scale) + shift


def dwconv(x, w):
    K = w.shape[0]
    T = x.shape[1]
    pad = K // 2
    xp = jnp.pad(x, ((0, 0), (pad, pad), (0, 0)))
    out = xp[:, 0:T] * w[0]
    for i in range(1, K):
        out = out + xp[:, i:i + T] * w[i]
    return out


def softmax_sink(s, sink):
    if sink is None:
        return jax.nn.softmax(s, axis=-1)
    m = jnp.maximum(jnp.max(s, axis=-1, keepdims=True), sink)
    e = jnp.exp(s - m)
    return e / (jnp.sum(e, axis=-1, keepdims=True) + jnp.exp(sink - m))


def axial_rope(x):
    T = x.shape[1]
    t = jnp.arange(T)
    half = HEAD_DIM // 2
    quarter = half // 2
    inv = 1.0 / (ROPE_BASE ** (jnp.arange(quarter, dtype=F32) / quarter))

    def rot(xa, pos):
        ang = pos.astype(F32)[:, None] * inv[None, :]
        cos = jnp.cos(ang)[None, :, None, :]
        sin = jnp.sin(ang)[None, :, None, :]
        x1, x2 = xa[..., :quarter], xa[..., quarter:]
        return jnp.concatenate([x1 * cos - x2 * sin, x1 * sin + x2 * cos], axis=-1)

    xf = x.astype(F32)
    out = jnp.concatenate([rot(xf[..., :half], t // GRID_W), rot(xf[..., half:], t % GRID_W)], axis=-1)
    return out.astype(x.dtype)


def dense_attn(q, k, v, sink):
    B, Tq, Hq, Dh = q.shape
    Hk = k.shape[1]
    G = Hq // Hk
    nb = Tq // Q_BLOCK
    qb = q.reshape(B, nb, Q_BLOCK, Hk, G, Dh).swapaxes(0, 1)
    sk = None if sink is None else sink.astype(F32).reshape(Hk, G, 1, 1)
    scale = Dh ** -0.5

    def one(qblk):
        s = jnp.einsum('bqkgd,bksd->bkgqs', qblk, k).astype(F32) * scale
        p = softmax_sink(s, sk).astype(v.dtype)
        return jnp.einsum('bkgqs,bksd->bqkgd', p, v)

    o = lax.map(one, qb)
    return o.swapaxes(0, 1).reshape(B, Tq, Hq, Dh)


def neighborhood_attn(q, k, v, rpb, ck, cv):
    B, T, H, Dh = q.shape
    R = T // GRID_W
    kr = min(WIN_R, R)
    kc = WIN_C
    qg = q.reshape(B, R, GRID_W, H, Dh)
    kg = k.reshape(B, R, GRID_W, H, Dh)
    vg = v.reshape(B, R, GRID_W, H, Dh)
    col = jnp.arange(GRID_W)
    col_idx = jnp.clip(col - kc // 2, 0, GRID_W - kc)[:, None] + jnp.arange(kc)[None, :]
    dc = col_idx - col[:, None] + (WIN_C - 1)
    scale = Dh ** -0.5
    rpb32 = rpb.astype(F32)

    def one(i):
        r0 = jnp.clip(i - kr // 2, 0, R - kr)
        kb = lax.dynamic_slice_in_dim(kg, r0, kr, axis=1)[:, :, col_idx]
        vb = lax.dynamic_slice_in_dim(vg, r0, kr, axis=1)[:, :, col_idx]
        qi = lax.dynamic_index_in_dim(qg, i, axis=1, keepdims=False)
        dr = r0 + jnp.arange(kr) - i + (WIN_R - 1)
        bias = rpb32[:, dr[None, :, None], dc[:, None, :]]
        s_loc = jnp.einsum('bwhd,brwchd->bhwrc', qi, kb).astype(F32) * scale + bias
        s_loc = s_loc.reshape(B, H, GRID_W, kr * kc)
        s_ctx = jnp.einsum('bwhd,bhld->bhwl', qi, ck).astype(F32) * scale
        p = jax.nn.softmax(jnp.concatenate([s_loc, s_ctx], axis=-1), axis=-1).astype(v.dtype)
        p_loc = p[..., :kr * kc].reshape(B, H, GRID_W, kr, kc)
        return (jnp.einsum('bhwrc,brwchd->bwhd', p_loc, vb)
                + jnp.einsum('bhwl,bhld->bwhd', p[..., kr * kc:], cv))

    o = lax.map(one, jnp.arange(R))
    return o.swapaxes(0, 1).reshape(B, T, H, Dh)


def window_attn(q, k, v, sink, ck, cv):
    B, T, Hq, Dh = q.shape
    Hk = k.shape[2]
    G = Hq // Hk
    blk = Q_BLOCK
    nb = T // blk
    pad = ((0, 0), (blk, blk), (0, 0), (0, 0))
    kp = jnp.pad(k, pad)
    vp = jnp.pad(v, pad)
    sk = sink.astype(F32).reshape(1, Hk, G, 1, 1)
    scale = Dh ** -0.5

    def one(n):
        q0 = n * blk
        qn = lax.dynamic_slice_in_dim(q, q0, blk, axis=1).reshape(B, blk, Hk, G, Dh)
        kn = lax.dynamic_slice_in_dim(kp, q0, 3 * blk, axis=1)
        vn = lax.dynamic_slice_in_dim(vp, q0, 3 * blk, axis=1)
        qpos = q0 + jnp.arange(blk)
        kpos = q0 - blk + jnp.arange(3 * blk)
        valid = ((jnp.abs(kpos[None, :] - qpos[:, None]) <= WIN_1D)
                 & (kpos >= 0)[None, :] & (kpos < T)[None, :])
        s_loc = jnp.einsum('bqkgd,bskd->bkgqs', qn, kn).astype(F32) * scale
        s_loc = jnp.where(valid, s_loc, -jnp.inf)
        s_ctx = jnp.einsum('bqkgd,bksd->bkgqs', qn, ck).astype(F32) * scale
        p = softmax_sink(jnp.concatenate([s_loc, s_ctx], axis=-1), sk).astype(v.dtype)
        o = (jnp.einsum('bkgqs,bskd->bqkgd', p[..., :3 * blk], vn)
             + jnp.einsum('bkgqs,bksd->bqkgd', p[..., 3 * blk:], cv))
        return o.reshape(B, blk, Hq, Dh)

    o = lax.map(one, jnp.arange(nb))
    return o.swapaxes(0, 1).reshape(B, T, Hq, Dh)


def gla_scan(q, k, v, g, s0):
    B, T, H, dk = q.shape
    dv = v.shape[-1]
    n = T // CHUNK
    q, k, v, g = [a.reshape(B, n, CHUNK, H, a.shape[-1]) for a in (q, k, v, g)]
    G = jnp.cumsum(g, axis=2)
    G_last = G[:, :, -1:]
    q_t = q * jnp.exp(G)
    k_t = k * jnp.exp(-G)
    k_end = k * jnp.exp(G_last - G)
    causal = jnp.tril(jnp.ones((CHUNK, CHUNK), bool))
    att = jnp.where(causal, jnp.einsum('bnchd,bnshd->bnhcs', q_t, k_t), 0.0)
    o_intra = jnp.einsum('bnhcs,bnshv->bnchv', att, v)
    u = jnp.einsum('bnshd,bnshv->bnhdv', k_end, v)
    decay = jnp.exp(G_last[:, :, 0])

    def step(S, inp):
        d, du = inp
        return d[..., None] * S + du, S

    S_fin, S_start = lax.scan(step, s0, (decay.swapaxes(0, 1), u.swapaxes(0, 1)))
    o_inter = jnp.einsum('bnchd,bnhdv->bnchv', q_t, S_start.swapaxes(0, 1))
    return (o_intra + o_inter).reshape(B, T, H, dv), S_fin


def delta_scan(q, k, v, beta, g, s0):
    B, T, H, dk = q.shape
    dv = v.shape[-1]
    n = T // CHUNK
    blk = lambda a: a.reshape((B, n, CHUNK) + a.shape[2:]).swapaxes(2, 3)
    q, k, v, beta, g = [blk(a) for a in (q, k, v, beta, g)]
    G = jnp.cumsum(g, axis=-1)
    lower = jnp.tril(jnp.ones((CHUNK, CHUNK), bool))
    strict = jnp.tril(jnp.ones((CHUNK, CHUNK), bool), -1)
    diff = G[..., :, None] - G[..., None, :]
    gam = jnp.where(lower, jnp.exp(jnp.where(lower, diff, 0.0)), 0.0)
    k_beta = k * beta[..., None]
    a_mat = jnp.where(strict, jnp.einsum('bnhcd,bnhsd->bnhcs', k_beta, k) * gam, 0.0)
    m_mat = a_mat + jnp.eye(CHUNK, dtype=F32)
    rhs = jnp.concatenate([v * beta[..., None], k_beta * jnp.exp(G)[..., None]], axis=-1)
    sol = lax.linalg.triangular_solve(m_mat, rhs, left_side=True, lower=True, unit_diagonal=True)
    w_val, k_cum = sol[..., :dv], sol[..., dv:]
    a_qk = jnp.einsum('bnhcd,bnhsd->bnhcs', q, k) * gam
    q_g = q * jnp.exp(G)[..., None]
    k_end = k * jnp.exp(G[..., -1:] - G)[..., None]
    d_last = jnp.exp(G[..., -1])

    def step(S, inp):
        aqk, wv, kc, qg, ke, d = inp
        v_new = wv - jnp.einsum('bhcd,bhdv->bhcv', kc, S)
        o = jnp.einsum('bhcd,bhdv->bhcv', qg, S) + jnp.einsum('bhcs,bhsv->bhcv', aqk, v_new)
        S = S * d[..., None, None] + jnp.einsum('bhcd,bhcv->bhdv', ke, v_new)
        return S, o

    xs = tuple(a.swapaxes(0, 1) for a in (a_qk, w_val, k_cum, q_g, k_end, d_last))
    S_fin, o = lax.scan(step, s0, xs)
    return o.transpose(1, 0, 3, 2, 4).reshape(B, T, H, dv), S_fin


def gla_mixer(bq, bk, bv, glr, br, w_g2, b_g, norm_g, s0):
    B, T, _ = bq.shape
    q = bq.reshape(B, T, H_B, DK_B).astype(F32) * DK_B ** -0.5
    k = bk.reshape(B, T, H_B, DK_B).astype(F32)
    v = bv.reshape(B, T, H_B, DV_B).astype(F32)
    z = jnp.einsum('btzr,zrc->btzc', glr.reshape(B, T, 2, GLA_RANK).astype(F32), w_g2.astype(F32)) + b_g.astype(F32)
    g = (jax.nn.log_sigmoid(z) / GLA_TAU).reshape(B, T, 2, H_B, DK_B)
    s0 = s0.astype(F32)
    fl = lambda a: jnp.flip(a, 1)
    o_f, s_f = gla_scan(q, k, v, g[:, :, 0], s0[:, 0])
    o_b, s_b = gla_scan(fl(q), fl(k), fl(v), fl(g[:, :, 1]), s0[:, 1])
    o = o_f + fl(o_b)
    o = rmsnorm(o, norm_g.reshape(H_B, DV_B)) * jax.nn.silu(br.reshape(B, T, H_B, DV_B).astype(F32))
    return o.reshape(B, T, H_B * DV_B).astype(bq.dtype), jnp.stack([s_f, s_b], axis=1).astype(bq.dtype)


def delta_mixer(dq, dk, dv, da, db, dz, w_conv, a_log, dt_bias, norm_g, s0):
    B, T, _ = dq.shape
    qkv = jax.nn.silu(dwconv(jnp.concatenate([dq, dk, dv], axis=-1), w_conv).astype(F32))
    q, k, v = split_cols(qkv, (H_D * DK_D, H_D * DK_D, H_D * DV_D))
    q = l2norm(q.reshape(B, T, H_D, DK_D)) * DK_D ** -0.5
    k = l2norm(k.reshape(B, T, H_D, DK_D))
    v = v.reshape(B, T, H_D, DV_D)
    beta = jax.nn.sigmoid(db.reshape(B, T, 2, H_D).astype(F32))
    g = -jnp.exp(a_log.astype(F32)) * jax.nn.softplus(da.reshape(B, T, 2, H_D).astype(F32) + dt_bias.astype(F32))
    s0 = s0.astype(F32)
    fl = lambda a: jnp.flip(a, 1)
    o_f, s_f = delta_scan(q, k, v, beta[:, :, 0], g[:, :, 0], s0[:, 0])
    o_b, s_b = delta_scan(fl(q), fl(k), fl(v), fl(beta[:, :, 1]), fl(g[:, :, 1]), s0[:, 1])
    o = o_f + fl(o_b)
    o = rmsnorm(o, norm_g) * jax.nn.silu(dz.reshape(B, T, H_D, DV_D).astype(F32))
    return o.reshape(B, T, H_D * DV_D).astype(dq.dtype), jnp.stack([s_f, s_b], axis=1).astype(dq.dtype)


def even_context(h, w_in, w_out, w_g2, b_g, norm_g):
    B, T, _ = h.shape
    aq, ak, av, bq, bk, bv, glr, br = split_cols(h @ w_in, EV_SIZES)
    k_h = ak.reshape(B, T, H_A, HEAD_DIM).transpose(0, 2, 1, 3)
    v_h = av.reshape(B, T, H_A, HEAD_DIM).transpose(0, 2, 1, 3)
    o_a = dense_attn(aq.reshape(B, T, H_A, HEAD_DIM), k_h, v_h, None)
    o_b, s_b = gla_mixer(bq, bk, bv, glr, br, w_g2, b_g, norm_g, jnp.zeros((B, 2, H_B, DK_B, DV_B), F32))
    out = jnp.concatenate([o_a.reshape(B, T, H_A * HEAD_DIM), o_b], axis=-1) @ w_out
    return out, k_h, v_h, s_b


def even_latent(h, w_in, w_out, rpb, w_g2, b_g, norm_g, ak_cache, av_cache, sb_cache):
    B, T, _ = h.shape
    aq, ak, av, bq, bk, bv, glr, br = split_cols(h @ w_in, EV_SIZES)
    o_a = neighborhood_attn(aq.reshape(B, T, H_A, HEAD_DIM), ak.reshape(B, T, H_A, HEAD_DIM),
                            av.reshape(B, T, H_A, HEAD_DIM), rpb, ak_cache, av_cache)
    o_b, _ = gla_mixer(bq, bk, bv, glr, br, w_g2, b_g, norm_g, sb_cache)
    return jnp.concatenate([o_a.reshape(B, T, H_A * HEAD_DIM), o_b], axis=-1) @ w_out


def odd_context(h, w_in, w_out, sink, w_conv, a_log, dt_bias, norm_g):
    B, T, _ = h.shape
    cq, ck, cv, dq, dk, dv, da, db, dz = split_cols(h @ w_in, OD_SIZES)
    k_h = ck.reshape(B, T, KV_C, HEAD_DIM).transpose(0, 2, 1, 3)
    v_h = cv.reshape(B, T, KV_C, HEAD_DIM).transpose(0, 2, 1, 3)
    o_c = dense_attn(cq.reshape(B, T, H_C, HEAD_DIM), k_h, v_h, sink)
    o_d, s_d = delta_mixer(dq, dk, dv, da, db, dz, w_conv, a_log, dt_bias, norm_g,
                           jnp.zeros((B, 2, H_D, DK_D, DV_D), F32))
    out = jnp.concatenate([o_c.reshape(B, T, H_C * HEAD_DIM), o_d], axis=-1) @ w_out
    return out, k_h, v_h, s_d


def odd_latent(h, w_in, w_out, sink, w_conv, a_log, dt_bias, norm_g, ck_cache, cv_cache, sd_cache):
    B, T, _ = h.shape
    cq, ck, cv, dq, dk, dv, da, db, dz = split_cols(h @ w_in, OD_SIZES)
    q = axial_rope(cq.reshape(B, T, H_C, HEAD_DIM))
    k = axial_rope(ck.reshape(B, T, KV_C, HEAD_DIM))
    o_c = window_attn(q, k, cv.reshape(B, T, KV_C, HEAD_DIM), sink, ck_cache, cv_cache)
    o_d, _ = delta_mixer(dq, dk, dv, da, db, dz, w_conv, a_log, dt_bias, norm_g, sd_cache)
    return jnp.concatenate([o_c.reshape(B, T, H_C * HEAD_DIM), o_d], axis=-1) @ w_out


def conv_ffn(h, w_up, w_conv, w_down):
    u = dwconv(h @ w_up, w_conv)
    a, gt = jnp.split(u, 2, axis=-1)
    return (a * jax.nn.silu(gt)) @ w_down


def setup_inputs(seed: int = 0) -> dict:
    key = jax.random.key(seed)
    ks = iter(jax.random.split(key, 40))
    nrm = lambda shape, s: jax.random.normal(next(ks), shape, F32) * s
    gain = lambda shape: 1.0 + nrm(shape, 0.02)
    x_prompt = nrm((BATCH, SEQ, D_MODEL), 1.0)
    x_sample = nrm((DEC_BATCH, DEC_SEQ, D_MODEL), 1.0)
    cache_a_k = nrm((DEC_BATCH, N_EVEN, H_A, PAST_LEN, HEAD_DIM), 1.0)
    cache_a_v = nrm((DEC_BATCH, N_EVEN, H_A, PAST_LEN, HEAD_DIM), 1.0)
    state_b = nrm((DEC_BATCH, N_EVEN, 2, H_B, DK_B, DV_B), 0.5)
    cache_c_k = nrm((DEC_BATCH, N_ODD, KV_C, PAST_LEN, HEAD_DIM), 1.0)
    cache_c_v = nrm((DEC_BATCH, N_ODD, KV_C, PAST_LEN, HEAD_DIM), 1.0)
    state_d = nrm((DEC_BATCH, N_ODD, 2, H_D, DK_D, DV_D), 0.5)
    c = nrm((DEC_BATCH, D_MODEL), 1.0)
    c_ctx = nrm((D_MODEL,), 1.0)
    ada_w = nrm((DEPTH, D_MODEL, 6 * D_MODEL), 0.5 * D_MODEL ** -0.5)
    ada_b = nrm((DEPTH, 6 * D_MODEL), 0.02)
    norm1_g = gain((DEPTH, D_MODEL))
    norm2_g = gain((DEPTH, D_MODEL))
    ffn_up = nrm((DEPTH, D_MODEL, 2 * D_FF), D_MODEL ** -0.5)
    ffn_conv = nrm((DEPTH, FFN_CONV, 2 * D_FF), FFN_CONV ** -0.5)
    ffn_down = nrm((DEPTH, D_FF, D_MODEL), D_FF ** -0.5)
    ev_w_in = nrm((N_EVEN, D_MODEL, EV_IN), D_MODEL ** -0.5)
    ev_w_out = nrm((N_EVEN, EV_OUT, D_MODEL), EV_OUT ** -0.5)
    a_rpb = nrm((N_EVEN, H_A, 2 * WIN_R - 1, 2 * WIN_C - 1), 0.1)
    b_w_g2 = nrm((N_EVEN, 2, GLA_RANK, H_B * DK_B), GLA_RANK ** -0.5)
    b_b_g = nrm((N_EVEN, 2, H_B * DK_B), 0.1)
    b_norm_g = gain((N_EVEN, H_B * DV_B))
    od_w_in = nrm((N_ODD, D_MODEL, OD_IN), D_MODEL ** -0.5)
    od_w_out = nrm((N_ODD, OD_OUT, D_MODEL), OD_OUT ** -0.5)
    c_sink = nrm((N_ODD, H_C), 0.5)
    d_conv = nrm((N_ODD, SHORT_CONV, H_D * (2 * DK_D + DV_D)), SHORT_CONV ** -0.5)
    d_a_log = jnp.log(jax.random.uniform(next(ks), (N_ODD, 2, H_D), F32, 1.0, 16.0))
    u = jax.random.uniform(next(ks), (N_ODD, 2, H_D), F32)
    dt = jnp.exp(u * (math.log(0.1) - math.log(0.001)) + math.log(0.001))
    d_dt_bias = dt + jnp.log(-jnp.expm1(-dt))
    d_norm_g = gain((N_ODD, DV_D))
    final_g = gain((D_MODEL,))
    return {'x_prompt': x_prompt, 'x_sample': x_sample,
            'cache_a_k': cache_a_k, 'cache_a_v': cache_a_v, 'state_b': state_b,
            'cache_c_k': cache_c_k, 'cache_c_v': cache_c_v, 'state_d': state_d,
            'c': c, 'c_ctx': c_ctx, 'ada_w': ada_w, 'ada_b': ada_b,
            'norm1_g': norm1_g, 'norm2_g': norm2_g,
            'ffn_up': ffn_up, 'ffn_conv': ffn_conv, 'ffn_down': ffn_down,
            'ev_w_in': ev_w_in, 'ev_w_out': ev_w_out, 'a_rpb': a_rpb,
            'b_w_g2': b_w_g2, 'b_b_g': b_b_g, 'b_norm_g': b_norm_g,
            'od_w_in': od_w_in, 'od_w_out': od_w_out, 'c_sink': c_sink,
            'd_conv': d_conv, 'd_a_log': d_a_log, 'd_dt_bias': d_dt_bias, 'd_norm_g': d_norm_g,
            'final_g': final_g}


def reference(x_prompt, x_sample, cache_a_k, cache_a_v, state_b, cache_c_k, cache_c_v, state_d, c,
              c_ctx, ada_w, ada_b, norm1_g, norm2_g, ffn_up, ffn_conv, ffn_down,
              ev_w_in, ev_w_out, a_rpb, b_w_g2, b_b_g, b_norm_g,
              od_w_in, od_w_out, c_sink, d_conv, d_a_log, d_dt_bias, d_norm_g, final_g):
    xp = x_prompt
    xs = x_sample
    ak_l, av_l, sb_l, ck_l, cv_l, sd_l = [], [], [], [], [], []
    for l in range(DEPTH):
        j = l // 2
        sh1p, sc1p, g1p, sh2p, sc2p, g2p = adaln(c_ctx, ada_w[l], ada_b[l])
        sh1s, sc1s, g1s, sh2s, sc2s, g2s = [m[:, None, :] for m in adaln(c, ada_w[l], ada_b[l])]
        hp = modulate(rmsnorm(xp, norm1_g[l]), sh1p, sc1p)
        hs = modulate(rmsnorm(xs, norm1_g[l]), sh1s, sc1s)
        if l % 2 == 0:
            mp, ak, av, sb = even_context(hp, ev_w_in[j], ev_w_out[j], b_w_g2[j], b_b_g[j], b_norm_g[j])
            ms = even_latent(hs, ev_w_in[j], ev_w_out[j], a_rpb[j], b_w_g2[j], b_b_g[j], b_norm_g[j],
                             cache_a_k[:, j], cache_a_v[:, j], state_b[:, j])
            ak_l.append(ak)
            av_l.append(av)
            sb_l.append(sb)
        else:
            mp, ck, cv, sd = odd_context(hp, od_w_in[j], od_w_out[j], c_sink[j], d_conv[j],
                                         d_a_log[j], d_dt_bias[j], d_norm_g[j])
            ms = odd_latent(hs, od_w_in[j], od_w_out[j], c_sink[j], d_conv[j], d_a_log[j], d_dt_bias[j],
                            d_norm_g[j], cache_c_k[:, j], cache_c_v[:, j], state_d[:, j])
            ck_l.append(ck)
            cv_l.append(cv)
            sd_l.append(sd)
        xp = xp + g1p * mp
        xs = xs + g1s * ms
        hp = modulate(rmsnorm(xp, norm2_g[l]), sh2p, sc2p)
        hs = modulate(rmsnorm(xs, norm2_g[l]), sh2s, sc2s)
        xp = xp + g2p * conv_ffn(hp, ffn_up[l], ffn_conv[l], ffn_down[l])
        xs = xs + g2s * conv_ffn(hs, ffn_up[l], ffn_conv[l], ffn_down[l])
    y_prompt = rmsnorm(xp, final_g)
    y_sample = rmsnorm(xs, final_g)
    new_a_k = jnp.stack(ak_l, axis=1)
    new_a_v = jnp.stack(av_l, axis=1)
    new_state_b = jnp.stack(sb_l, axis=1)
    new_c_k = jnp.stack(ck_l, axis=1)
    new_c_v = jnp.stack(cv_l, axis=1)
    new_state_d = jnp.stack(sd_l, axis=1)
    return (y_prompt, y_sample, new_a_k, new_a_v, new_state_b, new_c_k, new_c_v, new_state_d)
```

```python
import functools
import math

import jax
import jax.numpy as jnp
import numpy as np
from jax import lax
from jax.experimental import pallas as pl
from jax.experimental.pallas import tpu as pltpu

D_MODEL = 1024
DEPTH = 4
GRID_W = 64
HEAD_DIM = 64
H_A = D_MODEL // (2 * HEAD_DIM)
WIN_R = 8
WIN_C = 16
H_B = D_MODEL // (2 * HEAD_DIM)
DK_B = HEAD_DIM
DV_B = HEAD_DIM
GLA_RANK = 16
GLA_TAU = 16.0
H_C = D_MODEL // (2 * HEAD_DIM)
KV_C = H_C // 4
WIN_1D = 128
ROPE_BASE = 10000.0
H_D = D_MODEL // (2 * HEAD_DIM)
DK_D = HEAD_DIM
DV_D = HEAD_DIM
SHORT_CONV = 3
CHUNK = 64
Q_BLOCK = 128
D_FF = ((8 * D_MODEL // 3 + 127) // 128) * 128
EPS = 1e-6
EV_SIZES = (H_A * HEAD_DIM, H_A * HEAD_DIM, H_A * HEAD_DIM, H_B * DK_B, H_B * DK_B, H_B * DV_B, 2 * GLA_RANK, H_B * DV_B)
OD_SIZES = (H_C * HEAD_DIM, KV_C * HEAD_DIM, KV_C * HEAD_DIM, H_D * DK_D, H_D * DK_D, H_D * DV_D, 2 * H_D, 2 * H_D, H_D * DV_D)
F32 = jnp.float32
BF16 = jnp.bfloat16

V7X_VMEM_LIMIT_BYTES = 52 * 1024 * 1024
SUBLANES = 8
LANES = 128
COL_CHUNK = 512


def _params(*sem):
    return pltpu.CompilerParams(dimension_semantics=sem, vmem_limit_bytes=V7X_VMEM_LIMIT_BYTES)


def _norm_mod(x, g, shift, scale):
    y = x * lax.rsqrt(jnp.mean(x * x, axis=-1, keepdims=True) + EPS)
    return (y * g) * (1.0 + scale) + shift


def _adaln_kernel(c_ref, w_ref, b_ref, o_ref):
    cond = c_ref[...]
    a = cond * jax.nn.sigmoid(cond)
    a_hi = a.astype(BF16)
    a_lo = (a - a_hi.astype(F32)).astype(BF16)
    w = w_ref[0]
    w_hi = w.astype(BF16)
    w_lo = (w - w_hi.astype(F32)).astype(BF16)
    acc = jnp.dot(a_hi, w_hi, preferred_element_type=F32)
    acc += jnp.dot(a_lo, w_hi, preferred_element_type=F32)
    acc += jnp.dot(a_hi, w_lo, preferred_element_type=F32)
    o_ref[0] = acc + b_ref[0]


def _adaln(cond, ada_w, ada_b):
    R = cond.shape[0]
    L, D, N = ada_w.shape
    tn = 1024
    return pl.pallas_call(
        _adaln_kernel,
        grid=(L, N // tn),
        in_specs=[pl.BlockSpec((R, D), lambda l, j: (0, 0)),
                  pl.BlockSpec((1, D, tn), lambda l, j: (l, 0, j)),
                  pl.BlockSpec((1, 1, tn), lambda l, j: (l, 0, j))],
        out_specs=pl.BlockSpec((1, R, tn), lambda l, j: (l, 0, j)),
        out_shape=jax.ShapeDtypeStruct((L, R, N), F32),
        compiler_params=_params("parallel", "parallel"),
    )(cond, ada_w, ada_b.reshape(L, 1, N))


def _nm_matmul_kernel(x_ref, g_ref, sh_ref, sc_ref, w_ref, o_ref):
    h = _norm_mod(x_ref[0], g_ref[...], sh_ref[0], sc_ref[0]).astype(BF16)
    n = o_ref.shape[-1]
    for c0 in range(0, n, COL_CHUNK):
        c1 = min(c0 + COL_CHUNK, n)
        o_ref[0, :, c0:c1] = jnp.dot(h, w_ref[:, c0:c1], preferred_element_type=F32)


def _nm_matmul(x, g, shift, scale, w, tm):
    B, T, D = x.shape
    N = w.shape[1]
    return pl.pallas_call(
        _nm_matmul_kernel,
        grid=(B, T // tm),
        in_specs=[pl.BlockSpec((1, tm, D), lambda b, i: (b, i, 0)),
                  pl.BlockSpec((1, D), lambda b, i: (0, 0)),
                  pl.BlockSpec((1, 1, D), lambda b, i: (b, 0, 0)),
                  pl.BlockSpec((1, 1, D), lambda b, i: (b, 0, 0)),
                  pl.BlockSpec((D, N), lambda b, i: (0, 0))],
        out_specs=pl.BlockSpec((1, tm, N), lambda b, i: (b, i, 0)),
        out_shape=jax.ShapeDtypeStruct((B, T, N), F32),
        compiler_params=_params("parallel", "parallel"),
    )(x, g.reshape(1, D), shift.reshape(B, 1, D), scale.reshape(B, 1, D), w)


def _proj_res_kernel(a_ref, w_ref, x_ref, gate_ref, o_ref):
    acc = jnp.dot(a_ref[0].astype(BF16), w_ref[...], preferred_element_type=F32)
    o_ref[0] = x_ref[0] + gate_ref[0] * acc


def _proj_res(a, w, x, gate, tm):
    B, T, K = a.shape
    D = w.shape[1]
    return pl.pallas_call(
        _proj_res_kernel,
        grid=(B, T // tm),
        in_specs=[pl.BlockSpec((1, tm, K), lambda b, i: (b, i, 0)),
                  pl.BlockSpec((K, D), lambda b, i: (0, 0)),
                  pl.BlockSpec((1, tm, D), lambda b, i: (b, i, 0)),
                  pl.BlockSpec((1, 1, D), lambda b, i: (b, 0, 0))],
        out_specs=pl.BlockSpec((1, tm, D), lambda b, i: (b, i, 0)),
        out_shape=jax.ShapeDtypeStruct((B, T, D), F32),
        compiler_params=_params("parallel", "parallel"),
    )(a, w, x, gate.reshape(B, 1, D))


def _ffn_kernel(x_ref, xp_ref, xn_ref, g_ref, sh_ref, sc_ref, gate_ref, wa_ref, wg_ref,
                ca_ref, cg_ref, wd_ref, o_ref, h_ref, halo_ref, acc_ref):
    i = pl.program_id(1)
    c = pl.program_id(2)
    tm = x_ref.shape[1]

    @pl.when(c == 0)
    def _():
        g, sh, sc = g_ref[...], sh_ref[0], sc_ref[0]
        h_ref[...] = _norm_mod(x_ref[0], g, sh, sc).astype(BF16)
        hp = jnp.where(i > 0, _norm_mod(xp_ref[0], g, sh, sc), 0.0)
        hn = jnp.where(i < pl.num_programs(1) - 1, _norm_mod(xn_ref[0], g, sh, sc), 0.0)
        halo_ref[...] = jnp.concatenate([hp, hn], axis=0).astype(BF16)
        acc_ref[...] = jnp.zeros_like(acc_ref)

    h = h_ref[...]
    halo = halo_ref[...]
    row = lax.broadcasted_iota(jnp.int32, (tm, 1), 0)

    def conv_branch(w_ref, cw_ref):
        w = w_ref[...]
        u = jnp.dot(h, w, preferred_element_type=F32)
        uh = jnp.dot(halo, w, preferred_element_type=F32)
        u_prev = jnp.where(row == 0, uh[SUBLANES - 1:SUBLANES], pltpu.roll(u, 1, axis=0))
        u_next = jnp.where(row == tm - 1, uh[SUBLANES:SUBLANES + 1], pltpu.roll(u, tm - 1, axis=0))
        cw = cw_ref[...]
        return u_prev * cw[0:1] + u * cw[1:2] + u_next * cw[2:3]

    a = conv_branch(wa_ref, ca_ref)
    gt = conv_branch(wg_ref, cg_ref)
    act = a * (gt * jax.nn.sigmoid(gt))
    acc_ref[...] += jnp.dot(act.astype(BF16), wd_ref[...], preferred_element_type=F32)

    @pl.when(c == pl.num_programs(2) - 1)
    def _():
        o_ref[0] = x_ref[0] + gate_ref[0] * acc_ref[...]


def _conv_ffn_res(x, g, shift, scale, gate, w_up, w_conv, w_down, tm, tf):
    B, T, D = x.shape
    F = w_down.shape[0]
    nf = F // tf
    nt = T // tm
    rows8 = tm // SUBLANES
    vec = lambda: pl.BlockSpec((1, 1, D), lambda b, i, c: (b, 0, 0))
    return pl.pallas_call(
        _ffn_kernel,
        grid=(B, nt, nf),
        in_specs=[pl.BlockSpec((1, tm, D), lambda b, i, c: (b, i, 0)),
                  pl.BlockSpec((1, SUBLANES, D), lambda b, i, c: (b, jnp.maximum(i * rows8 - 1, 0), 0)),
                  pl.BlockSpec((1, SUBLANES, D), lambda b, i, c: (b, jnp.minimum((i + 1) * rows8, T // SUBLANES - 1), 0)),
                  pl.BlockSpec((1, D), lambda b, i, c: (0, 0)),
                  vec(), vec(), vec(),
                  pl.BlockSpec((D, tf), lambda b, i, c: (0, c)),
                  pl.BlockSpec((D, tf), lambda b, i, c: (0, nf + c)),
                  pl.BlockSpec((3, tf), lambda b, i, c: (0, c)),
                  pl.BlockSpec((3, tf), lambda b, i, c: (0, nf + c)),
                  pl.BlockSpec((tf, D), lambda b, i, c: (c, 0))],
        out_specs=pl.BlockSpec((1, tm, D), lambda b, i, c: (b, i, 0)),
        out_shape=jax.ShapeDtypeStruct((B, T, D), F32),
        scratch_shapes=[pltpu.VMEM((tm, D), BF16),
                        pltpu.VMEM((2 * SUBLANES, D), BF16),
                        pltpu.VMEM((tm, D), F32)],
        compiler_params=_params("parallel", "parallel", "arbitrary"),
    )(x, x, x, g.reshape(1, D), shift.reshape(B, 1, D), scale.reshape(B, 1, D), gate.reshape(B, 1, D),
      w_up, w_up, w_conv, w_conv, w_down)


def _rmsnorm_kernel(x_ref, g_ref, o_ref):
    x = x_ref[0]
    o_ref[0] = (x * lax.rsqrt(jnp.mean(x * x, axis=-1, keepdims=True) + EPS)) * g_ref[...]


def _rmsnorm_call(x, g, tm):
    B, T, D = x.shape
    return pl.pallas_call(
        _rmsnorm_kernel,
        grid=(B, T // tm),
        in_specs=[pl.BlockSpec((1, tm, D), lambda b, i: (b, i, 0)),
                  pl.BlockSpec((1, D), lambda b, i: (0, 0))],
        out_specs=pl.BlockSpec((1, tm, D), lambda b, i: (b, i, 0)),
        out_shape=jax.ShapeDtypeStruct((B, T, D), F32),
        compiler_params=_params("parallel", "parallel"),
    )(x, g.reshape(1, D))


def _rmsnorm(x, g):
    y = x * lax.rsqrt(jnp.mean(x * x, axis=-1, keepdims=True) + EPS)
    return y * g


def _l2norm(x):
    return x * lax.rsqrt(jnp.sum(x * x, axis=-1, keepdims=True) + EPS)


def _split_cols(p, sizes):
    cuts = [int(s) for s in np.cumsum(sizes)[:-1]]
    return jnp.split(p, cuts, axis=-1)


def _dwconv(x, w):
    K = w.shape[0]
    T = x.shape[1]
    pad = K // 2
    xp = jnp.pad(x, ((0, 0), (pad, pad), (0, 0)))
    out = xp[:, 0:T] * w[0]
    for i in range(1, K):
        out = out + xp[:, i:i + T] * w[i]
    return out


def _softmax_sink(s, sink):
    if sink is None:
        return jax.nn.softmax(s, axis=-1)
    m = jnp.maximum(jnp.max(s, axis=-1, keepdims=True), sink)
    e = jnp.exp(s - m)
    return e / (jnp.sum(e, axis=-1, keepdims=True) + jnp.exp(sink - m))


def _axial_rope(x):
    T = x.shape[1]
    t = jnp.arange(T)
    half = HEAD_DIM // 2
    quarter = half // 2
    inv = 1.0 / (ROPE_BASE ** (jnp.arange(quarter, dtype=F32) / quarter))

    def rot(xa, pos):
        ang = pos.astype(F32)[:, None] * inv[None, :]
        cos = jnp.cos(ang)[None, :, None, :]
        sin = jnp.sin(ang)[None, :, None, :]
        x1, x2 = xa[..., :quarter], xa[..., quarter:]
        return jnp.concatenate([x1 * cos - x2 * sin, x1 * sin + x2 * cos], axis=-1)

    return jnp.concatenate([rot(x[..., :half], t // GRID_W), rot(x[..., half:], t % GRID_W)], axis=-1)


def _dense_attn(q, k, v, sink):
    B, Tq, Hq, Dh = q.shape
    Hk = k.shape[1]
    G = Hq // Hk
    nb = Tq // Q_BLOCK
    qb = q.reshape(B, nb, Q_BLOCK, Hk, G, Dh).swapaxes(0, 1)
    sk = None if sink is None else sink.reshape(Hk, G, 1, 1)
    scale = Dh ** -0.5

    def one(qblk):
        s = jnp.einsum('bqkgd,bksd->bkgqs', qblk, k) * scale
        p = _softmax_sink(s, sk)
        return jnp.einsum('bkgqs,bksd->bqkgd', p, v)

    o = lax.map(one, qb)
    return o.swapaxes(0, 1).reshape(B, Tq, Hq, Dh)


def _neighborhood_attn(q, k, v, rpb, ck, cv):
    B, T, H, Dh = q.shape
    R = T // GRID_W
    kr = min(WIN_R, R)
    kc = WIN_C
    qg = q.reshape(B, R, GRID_W, H, Dh)
    kg = k.reshape(B, R, GRID_W, H, Dh)
    vg = v.reshape(B, R, GRID_W, H, Dh)
    col = jnp.arange(GRID_W)
    col_idx = jnp.clip(col - kc // 2, 0, GRID_W - kc)[:, None] + jnp.arange(kc)[None, :]
    dc = col_idx - col[:, None] + (WIN_C - 1)
    scale = Dh ** -0.5

    def one(i):
        r0 = jnp.clip(i - kr // 2, 0, R - kr)
        kb = lax.dynamic_slice_in_dim(kg, r0, kr, axis=1)[:, :, col_idx]
        vb = lax.dynamic_slice_in_dim(vg, r0, kr, axis=1)[:, :, col_idx]
        qi = lax.dynamic_index_in_dim(qg, i, axis=1, keepdims=False)
        dr = r0 + jnp.arange(kr) - i + (WIN_R - 1)
        bias = rpb[:, dr[None, :, None], dc[:, None, :]]
        s_loc = jnp.einsum('bwhd,brwchd->bhwrc', qi, kb) * scale + bias
        s_loc = s_loc.reshape(B, H, GRID_W, kr * kc)
        s_ctx = jnp.einsum('bwhd,bhld->bhwl', qi, ck) * scale
        p = jax.nn.softmax(jnp.concatenate([s_loc, s_ctx], axis=-1), axis=-1)
        p_loc = p[..., :kr * kc].reshape(B, H, GRID_W, kr, kc)
        return (jnp.einsum('bhwrc,brwchd->bwhd', p_loc, vb)
                + jnp.einsum('bhwl,bhld->bwhd', p[..., kr * kc:], cv))

    o = lax.map(one, jnp.arange(R))
    return o.swapaxes(0, 1).reshape(B, T, H, Dh)


def _window_attn(q, k, v, sink, ck, cv):
    B, T, Hq, Dh = q.shape
    Hk = k.shape[2]
    G = Hq // Hk
    blk = Q_BLOCK
    nb = T // blk
    pad = ((0, 0), (blk, blk), (0, 0), (0, 0))
    kp = jnp.pad(k, pad)
    vp = jnp.pad(v, pad)
    sk = sink.reshape(1, Hk, G, 1, 1)
    scale = Dh ** -0.5

    def one(n):
        q0 = n * blk
        qn = lax.dynamic_slice_in_dim(q, q0, blk, axis=1).reshape(B, blk, Hk, G, Dh)
        kn = lax.dynamic_slice_in_dim(kp, q0, 3 * blk, axis=1)
        vn = lax.dynamic_slice_in_dim(vp, q0, 3 * blk, axis=1)
        qpos = q0 + jnp.arange(blk)
        kpos = q0 - blk + jnp.arange(3 * blk)
        valid = ((jnp.abs(kpos[None, :] - qpos[:, None]) <= WIN_1D)
                 & (kpos >= 0)[None, :] & (kpos < T)[None, :])
        s_loc = jnp.einsum('bqkgd,bskd->bkgqs', qn, kn) * scale
        s_loc = jnp.where(valid, s_loc, -jnp.inf)
        s_ctx = jnp.einsum('bqkgd,bksd->bkgqs', qn, ck) * scale
        p = _softmax_sink(jnp.concatenate([s_loc, s_ctx], axis=-1), sk)
        o = (jnp.einsum('bkgqs,bskd->bqkgd', p[..., :3 * blk], vn)
             + jnp.einsum('bkgqs,bksd->bqkgd', p[..., 3 * blk:], cv))
        return o.reshape(B, blk, Hq, Dh)

    o = lax.map(one, jnp.arange(nb))
    return o.swapaxes(0, 1).reshape(B, T, Hq, Dh)


def _gla_scan(q, k, v, g, s0):
    B, T, H, dk = q.shape
    dv = v.shape[-1]
    n = T // CHUNK
    q, k, v, g = [a.reshape(B, n, CHUNK, H, a.shape[-1]) for a in (q, k, v, g)]
    G = jnp.cumsum(g, axis=2)
    G_last = G[:, :, -1:]
    q_t = q * jnp.exp(G)
    k_t = k * jnp.exp(-G)
    k_end = k * jnp.exp(G_last - G)
    causal = jnp.tril(jnp.ones((CHUNK, CHUNK), bool))
    att = jnp.where(causal, jnp.einsum('bnchd,bnshd->bnhcs', q_t, k_t), 0.0)
    o_intra = jnp.einsum('bnhcs,bnshv->bnchv', att, v)
    u = jnp.einsum('bnshd,bnshv->bnhdv', k_end, v)
    decay = jnp.exp(G_last[:, :, 0])

    def step(S, inp):
        d, du = inp
        return d[..., None] * S + du, S

    S_fin, S_start = lax.scan(step, s0, (decay.swapaxes(0, 1), u.swapaxes(0, 1)))
    o_inter = jnp.einsum('bnchd,bnhdv->bnchv', q_t, S_start.swapaxes(0, 1))
    return (o_intra + o_inter).reshape(B, T, H, dv), S_fin


def _delta_scan(q, k, v, beta, g, s0):
    B, T, H, dk = q.shape
    dv = v.shape[-1]
    n = T // CHUNK
    blk = lambda a: a.reshape((B, n, CHUNK) + a.shape[2:]).swapaxes(2, 3)
    q, k, v, beta, g = [blk(a) for a in (q, k, v, beta, g)]
    G = jnp.cumsum(g, axis=-1)
    lower = jnp.tril(jnp.ones((CHUNK, CHUNK), bool))
    strict = jnp.tril(jnp.ones((CHUNK, CHUNK), bool), -1)
    diff = G[..., :, None] - G[..., None, :]
    gam = jnp.where(lower, jnp.exp(jnp.where(lower, diff, 0.0)), 0.0)
    k_beta = k * beta[..., None]
    a_mat = jnp.where(strict, jnp.einsum('bnhcd,bnhsd->bnhcs', k_beta, k) * gam, 0.0)
    m_mat = a_mat + jnp.eye(CHUNK, dtype=F32)
    rhs = jnp.concatenate([v * beta[..., None], k_beta * jnp.exp(G)[..., None]], axis=-1)
    sol = lax.linalg.triangular_solve(m_mat, rhs, left_side=True, lower=True, unit_diagonal=True)
    w_val, k_cum = sol[..., :dv], sol[..., dv:]
    a_qk = jnp.einsum('bnhcd,bnhsd->bnhcs', q, k) * gam
    q_g = q * jnp.exp(G)[..., None]
    k_end = k * jnp.exp(G[..., -1:] - G)[..., None]
    d_last = jnp.exp(G[..., -1])

    def step(S, inp):
        aqk, wv, kc, qg, ke, d = inp
        v_new = wv - jnp.einsum('bhcd,bhdv->bhcv', kc, S)
        o = jnp.einsum('bhcd,bhdv->bhcv', qg, S) + jnp.einsum('bhcs,bhsv->bhcv', aqk, v_new)
        S = S * d[..., None, None] + jnp.einsum('bhcd,bhcv->bhdv', ke, v_new)
        return S, o

    xs = tuple(a.swapaxes(0, 1) for a in (a_qk, w_val, k_cum, q_g, k_end, d_last))
    S_fin, o = lax.scan(step, s0, xs)
    return o.transpose(1, 0, 3, 2, 4).reshape(B, T, H, dv), S_fin


def _gla_mixer(bq, bk, bv, glr, br, w_g2, b_g, norm_g, s0):
    B, T, _ = bq.shape
    q = bq.reshape(B, T, H_B, DK_B) * DK_B ** -0.5
    k = bk.reshape(B, T, H_B, DK_B)
    v = bv.reshape(B, T, H_B, DV_B)
    z = jnp.einsum('btzr,zrc->btzc', glr.reshape(B, T, 2, GLA_RANK), w_g2) + b_g
    g = (jax.nn.log_sigmoid(z) / GLA_TAU).reshape(B, T, 2, H_B, DK_B)
    fl = lambda a: jnp.flip(a, 1)
    o_f, s_f = _gla_scan(q, k, v, g[:, :, 0], s0[:, 0])
    o_b, s_b = _gla_scan(fl(q), fl(k), fl(v), fl(g[:, :, 1]), s0[:, 1])
    o = o_f + fl(o_b)
    o = _rmsnorm(o, norm_g.reshape(H_B, DV_B)) * jax.nn.silu(br.reshape(B, T, H_B, DV_B))
    return o.reshape(B, T, H_B * DV_B), jnp.stack([s_f, s_b], axis=1)


def _delta_mixer(dq, dk, dv, da, db, dz, w_conv, a_log, dt_bias, norm_g, s0):
    B, T, _ = dq.shape
    qkv = jax.nn.silu(_dwconv(jnp.concatenate([dq, dk, dv], axis=-1), w_conv))
    q, k, v = _split_cols(qkv, (H_D * DK_D, H_D * DK_D, H_D * DV_D))
    q = _l2norm(q.reshape(B, T, H_D, DK_D)) * DK_D ** -0.5
    k = _l2norm(k.reshape(B, T, H_D, DK_D))
    v = v.reshape(B, T, H_D, DV_D)
    beta = jax.nn.sigmoid(db.reshape(B, T, 2, H_D))
    g = -jnp.exp(a_log) * jax.nn.softplus(da.reshape(B, T, 2, H_D) + dt_bias)
    fl = lambda a: jnp.flip(a, 1)
    o_f, s_f = _delta_scan(q, k, v, beta[:, :, 0], g[:, :, 0], s0[:, 0])
    o_b, s_b = _delta_scan(fl(q), fl(k), fl(v), fl(beta[:, :, 1]), fl(g[:, :, 1]), s0[:, 1])
    o = o_f + fl(o_b)
    o = _rmsnorm(o, norm_g) * jax.nn.silu(dz.reshape(B, T, H_D, DV_D))
    return o.reshape(B, T, H_D * DV_D), jnp.stack([s_f, s_b], axis=1)


def _even_mix(p, latent, rpb, w_g2, b_g, norm_g, ak_cache, av_cache, sb):
    B, T, _ = p.shape
    aq, ak, av, bq, bk, bv, glr, br = _split_cols(p, EV_SIZES)
    if latent:
        o_a = _neighborhood_attn(aq.reshape(B, T, H_A, HEAD_DIM), ak.reshape(B, T, H_A, HEAD_DIM),
                                 av.reshape(B, T, H_A, HEAD_DIM), rpb, ak_cache, av_cache)
        k_h = v_h = None
    else:
        k_h = ak.reshape(B, T, H_A, HEAD_DIM).transpose(0, 2, 1, 3)
        v_h = av.reshape(B, T, H_A, HEAD_DIM).transpose(0, 2, 1, 3)
        o_a = _dense_attn(aq.reshape(B, T, H_A, HEAD_DIM), k_h, v_h, None)
    o_b, s_b = _gla_mixer(bq, bk, bv, glr, br, w_g2, b_g, norm_g, sb)
    return jnp.concatenate([o_a.reshape(B, T, H_A * HEAD_DIM), o_b], axis=-1), k_h, v_h, s_b


def _odd_mix(p, latent, sink, w_conv, a_log, dt_bias, norm_g, ck_cache, cv_cache, sd):
    B, T, _ = p.shape
    cq, ck, cv, dq, dk, dv, da, db, dz = _split_cols(p, OD_SIZES)
    if latent:
        q = _axial_rope(cq.reshape(B, T, H_C, HEAD_DIM))
        k = _axial_rope(ck.reshape(B, T, KV_C, HEAD_DIM))
        o_c = _window_attn(q, k, cv.reshape(B, T, KV_C, HEAD_DIM), sink, ck_cache, cv_cache)
        k_h = v_h = None
    else:
        k_h = ck.reshape(B, T, KV_C, HEAD_DIM).transpose(0, 2, 1, 3)
        v_h = cv.reshape(B, T, KV_C, HEAD_DIM).transpose(0, 2, 1, 3)
        o_c = _dense_attn(cq.reshape(B, T, H_C, HEAD_DIM), k_h, v_h, sink)
    o_d, s_d = _delta_mixer(dq, dk, dv, da, db, dz, w_conv, a_log, dt_bias, norm_g, sd)
    return jnp.concatenate([o_c.reshape(B, T, H_C * HEAD_DIM), o_d], axis=-1), k_h, v_h, s_d


TM_LATENT = 512
TM_CONTEXT = 256
TF_FFN = 1408


def kernel(x_prompt, x_sample, cache_a_k, cache_a_v, state_b, cache_c_k, cache_c_v, state_d, c, c_ctx, ada_w, ada_b, norm1_g, norm2_g, ffn_up, ffn_conv, ffn_down, ev_w_in, ev_w_out, a_rpb, b_w_g2, b_b_g, b_norm_g, od_w_in, od_w_out, c_sink, d_conv, d_a_log, d_dt_bias, d_norm_g, final_g):
    Bp, Tp, D = x_prompt.shape
    Bs, Ts, _ = x_sample.shape
    cond = jnp.concatenate([c, c_ctx[None, :], jnp.zeros((2 * SUBLANES - Bs - 1, D), F32)], axis=0)
    mods = _adaln(cond, ada_w, ada_b)
    xp, xs = x_prompt, x_sample
    streams = ((False, Bp, TM_CONTEXT), (True, Bs, TM_LATENT))
    ak_l, av_l, sb_l, ck_l, cv_l, sd_l = [], [], [], [], [], []
    for l in range(DEPTH):
        j = l // 2
        mod_s = [mods[l, :Bs, i * D:(i + 1) * D] for i in range(6)]
        mod_p = [jnp.broadcast_to(mods[l, Bs:Bs + 1, i * D:(i + 1) * D], (Bp, D)) for i in range(6)]
        w_in = (ev_w_in[j] if l % 2 == 0 else od_w_in[j]).astype(BF16)
        w_out = (ev_w_out[j] if l % 2 == 0 else od_w_out[j]).astype(BF16)
        w_up = ffn_up[l].astype(BF16)
        w_down = ffn_down[l].astype(BF16)
        new_x = []
        for (latent, B, tm), x, mod in zip(streams, (xp, xs), (mod_p, mod_s)):
            sh1, sc1, g1, sh2, sc2, g2 = mod
            p = _nm_matmul(x, norm1_g[l], sh1, sc1, w_in, tm)
            if l % 2 == 0:
                s0 = state_b[:, j] if latent else jnp.zeros((B, 2, H_B, DK_B, DV_B), F32)
                o, k_h, v_h, s_new = _even_mix(p, latent, a_rpb[j], b_w_g2[j], b_b_g[j], b_norm_g[j],
                                               cache_a_k[:, j], cache_a_v[:, j], s0)
                if not latent:
                    ak_l.append(k_h)
                    av_l.append(v_h)
                    sb_l.append(s_new)
            else:
                s0 = state_d[:, j] if latent else jnp.zeros((B, 2, H_D, DK_D, DV_D), F32)
                o, k_h, v_h, s_new = _odd_mix(p, latent, c_sink[j], d_conv[j], d_a_log[j], d_dt_bias[j],
                                              d_norm_g[j], cache_c_k[:, j], cache_c_v[:, j], s0)
                if not latent:
                    ck_l.append(k_h)
                    cv_l.append(v_h)
                    sd_l.append(s_new)
            x = _proj_res(o, w_out, x, g1, tm)
            x = _conv_ffn_res(x, norm2_g[l], sh2, sc2, g2, w_up, ffn_conv[l], w_down, tm, TF_FFN)
            new_x.append(x)
        xp, xs = new_x
    y_prompt = _rmsnorm_call(xp, final_g, TM_CONTEXT)
    y_sample = _rmsnorm_call(xs, final_g, TM_LATENT)
    return (y_prompt, y_sample, jnp.stack(ak_l, axis=1), jnp.stack(av_l, axis=1), jnp.stack(sb_l, axis=1),
            jnp.stack(ck_l, axis=1), jnp.stack(cv_l, axis=1), jnp.stack(sd_l, axis=1))
```

```python
import functools
import math

import jax
import jax.numpy as jnp
import numpy as np
from jax import lax
from jax.experimental import pallas as pl
from jax.experimental.pallas import tpu as pltpu

D_MODEL = 1024
DEPTH = 4
GRID_W = 64
HEAD_DIM = 64
H_A = D_MODEL // (2 * HEAD_DIM)
WIN_R = 8
WIN_C = 16
H_B = D_MODEL // (2 * HEAD_DIM)
DK_B = HEAD_DIM
DV_B = HEAD_DIM
GLA_RANK = 16
GLA_TAU = 16.0
H_C = D_MODEL // (2 * HEAD_DIM)
KV_C = H_C // 4
WIN_1D = 128
ROPE_BASE = 10000.0
H_D = D_MODEL // (2 * HEAD_DIM)
DK_D = HEAD_DIM
DV_D = HEAD_DIM
SHORT_CONV = 3
CHUNK = 64
Q_BLOCK = 128
D_FF = ((8 * D_MODEL // 3 + 127) // 128) * 128
EPS = 1e-6
EV_SIZES = (H_A * HEAD_DIM, H_A * HEAD_DIM, H_A * HEAD_DIM, H_B * DK_B, H_B * DK_B, H_B * DV_B, 2 * GLA_RANK, H_B * DV_B)
OD_SIZES = (H_C * HEAD_DIM, KV_C * HEAD_DIM, KV_C * HEAD_DIM, H_D * DK_D, H_D * DK_D, H_D * DV_D, 2 * H_D, 2 * H_D, H_D * DV_D)
F32 = jnp.float32
BF16 = jnp.bfloat16

V7X_VMEM_LIMIT_BYTES = 52 * 1024 * 1024
SUBLANES = 8
LANES = 128
COL_CHUNK = 512


def _params(*sem):
    return pltpu.CompilerParams(dimension_semantics=sem, vmem_limit_bytes=V7X_VMEM_LIMIT_BYTES)


def _norm_mod(x, g, shift, scale):
    y = x * lax.rsqrt(jnp.mean(x * x, axis=-1, keepdims=True) + EPS)
    return (y * g) * (1.0 + scale) + shift


def _adaln_kernel(c_ref, w_ref, b_ref, o_ref):
    cond = c_ref[...]
    a = cond * jax.nn.sigmoid(cond)
    a_hi = a.astype(BF16)
    a_lo = (a - a_hi.astype(F32)).astype(BF16)
    w = w_ref[0]
    w_hi = w.astype(BF16)
    w_lo = (w - w_hi.astype(F32)).astype(BF16)
    acc = jnp.dot(a_hi, w_hi, preferred_element_type=F32)
    acc += jnp.dot(a_lo, w_hi, preferred_element_type=F32)
    acc += jnp.dot(a_hi, w_lo, preferred_element_type=F32)
    o_ref[0] = acc + b_ref[0]


def _adaln(cond, ada_w, ada_b):
    R = cond.shape[0]
    L, D, N = ada_w.shape
    tn = 1024
    return pl.pallas_call(
        _adaln_kernel,
        grid=(L, N // tn),
        in_specs=[pl.BlockSpec((R, D), lambda l, j: (0, 0)),
                  pl.BlockSpec((1, D, tn), lambda l, j: (l, 0, j)),
                  pl.BlockSpec((1, 1, tn), lambda l, j: (l, 0, j))],
        out_specs=pl.BlockSpec((1, R, tn), lambda l, j: (l, 0, j)),
        out_shape=jax.ShapeDtypeStruct((L, R, N), F32),
        compiler_params=_params("parallel", "parallel"),
    )(cond, ada_w, ada_b.reshape(L, 1, N))


def _nm_matmul_kernel(x_ref, g_ref, sh_ref, sc_ref, w_ref, o_ref):
    h = _norm_mod(x_ref[0], g_ref[...], sh_ref[0], sc_ref[0]).astype(BF16)
    n = o_ref.shape[-1]
    for c0 in range(0, n, COL_CHUNK):
        c1 = min(c0 + COL_CHUNK, n)
        o_ref[0, :, c0:c1] = jnp.dot(h, w_ref[:, c0:c1], preferred_element_type=F32)


def _nm_matmul(x, g, shift, scale, w, tm):
    B, T, D = x.shape
    N = w.shape[1]
    return pl.pallas_call(
        _nm_matmul_kernel,
        grid=(B, T // tm),
        in_specs=[pl.BlockSpec((1, tm, D), lambda b, i: (b, i, 0)),
                  pl.BlockSpec((1, D), lambda b, i: (0, 0)),
                  pl.BlockSpec((1, 1, D), lambda b, i: (b, 0, 0)),
                  pl.BlockSpec((1, 1, D), lambda b, i: (b, 0, 0)),
                  pl.BlockSpec((D, N), lambda b, i: (0, 0))],
        out_specs=pl.BlockSpec((1, tm, N), lambda b, i: (b, i, 0)),
        out_shape=jax.ShapeDtypeStruct((B, T, N), F32),
        compiler_params=_params("parallel", "parallel"),
    )(x, g.reshape(1, D), shift.reshape(B, 1, D), scale.reshape(B, 1, D), w)


def _proj_res_kernel(a1_ref, a2_ref, w1_ref, w2_ref, x_ref, gate_ref, o_ref):
    acc = jnp.dot(a1_ref[0].astype(BF16), w1_ref[...], preferred_element_type=F32)
    acc += jnp.dot(a2_ref[0].astype(BF16), w2_ref[...], preferred_element_type=F32)
    o_ref[0] = x_ref[0] + gate_ref[0] * acc


def _proj_res(a1, a2, w, x, gate, tm):
    B, T, K = a1.shape
    D = w.shape[1]
    return pl.pallas_call(
        _proj_res_kernel,
        grid=(B, T // tm),
        in_specs=[pl.BlockSpec((1, tm, K), lambda b, i: (b, i, 0)),
                  pl.BlockSpec((1, tm, K), lambda b, i: (b, i, 0)),
                  pl.BlockSpec((K, D), lambda b, i: (0, 0)),
                  pl.BlockSpec((K, D), lambda b, i: (1, 0)),
                  pl.BlockSpec((1, tm, D), lambda b, i: (b, i, 0)),
                  pl.BlockSpec((1, 1, D), lambda b, i: (b, 0, 0))],
        out_specs=pl.BlockSpec((1, tm, D), lambda b, i: (b, i, 0)),
        out_shape=jax.ShapeDtypeStruct((B, T, D), F32),
        compiler_params=_params("parallel", "parallel"),
    )(a1, a2, w, w, x, gate.reshape(B, 1, D))


def _ffn_kernel(x_ref, xp_ref, xn_ref, g_ref, sh_ref, sc_ref, gate_ref, wa_ref, wg_ref,
                ca_ref, cg_ref, wd_ref, o_ref, h_ref, halo_ref, acc_ref):
    i = pl.program_id(1)
    c = pl.program_id(2)
    tm = x_ref.shape[1]

    @pl.when(c == 0)
    def _():
        g, sh, sc = g_ref[...], sh_ref[0], sc_ref[0]
        h_ref[...] = _norm_mod(x_ref[0], g, sh, sc).astype(BF16)
        hp = jnp.where(i > 0, _norm_mod(xp_ref[0], g, sh, sc), 0.0)
        hn = jnp.where(i < pl.num_programs(1) - 1, _norm_mod(xn_ref[0], g, sh, sc), 0.0)
        halo_ref[...] = jnp.concatenate([hp, hn], axis=0).astype(BF16)
        acc_ref[...] = jnp.zeros_like(acc_ref)

    h = h_ref[...]
    halo = halo_ref[...]
    row = lax.broadcasted_iota(jnp.int32, (tm, 1), 0)

    def conv_branch(w_ref, cw_ref):
        w = w_ref[...]
        u = jnp.dot(h, w, preferred_element_type=F32)
        uh = jnp.dot(halo, w, preferred_element_type=F32)
        u_prev = jnp.where(row == 0, uh[SUBLANES - 1:SUBLANES], pltpu.roll(u, 1, axis=0))
        u_next = jnp.where(row == tm - 1, uh[SUBLANES:SUBLANES + 1], pltpu.roll(u, tm - 1, axis=0))
        cw = cw_ref[...]
        return u_prev * cw[0:1] + u * cw[1:2] + u_next * cw[2:3]

    a = conv_branch(wa_ref, ca_ref)
    gt = conv_branch(wg_ref, cg_ref)
    act = a * (gt * jax.nn.sigmoid(gt))
    acc_ref[...] += jnp.dot(act.astype(BF16), wd_ref[...], preferred_element_type=F32)

    @pl.when(c == pl.num_programs(2) - 1)
    def _():
        o_ref[0] = x_ref[0] + gate_ref[0] * acc_ref[...]


def _conv_ffn_res(x, g, shift, scale, gate, w_up, w_conv, w_down, tm, tf):
    B, T, D = x.shape
    F = w_down.shape[0]
    nf = F // tf
    nt = T // tm
    rows8 = tm // SUBLANES
    vec = lambda: pl.BlockSpec((1, 1, D), lambda b, i, c: (b, 0, 0))
    return pl.pallas_call(
        _ffn_kernel,
        grid=(B, nt, nf),
        in_specs=[pl.BlockSpec((1, tm, D), lambda b, i, c: (b, i, 0)),
                  pl.BlockSpec((1, SUBLANES, D), lambda b, i, c: (b, jnp.maximum(i * rows8 - 1, 0), 0)),
                  pl.BlockSpec((1, SUBLANES, D), lambda b, i, c: (b, jnp.minimum((i + 1) * rows8, T // SUBLANES - 1), 0)),
                  pl.BlockSpec((1, D), lambda b, i, c: (0, 0)),
                  vec(), vec(), vec(),
                  pl.BlockSpec((D, tf), lambda b, i, c: (0, c)),
                  pl.BlockSpec((D, tf), lambda b, i, c: (0, nf + c)),
                  pl.BlockSpec((3, tf), lambda b, i, c: (0, c)),
                  pl.BlockSpec((3, tf), lambda b, i, c: (0, nf + c)),
                  pl.BlockSpec((tf, D), lambda b, i, c: (c, 0))],
        out_specs=pl.BlockSpec((1, tm, D), lambda b, i, c: (b, i, 0)),
        out_shape=jax.ShapeDtypeStruct((B, T, D), F32),
        scratch_shapes=[pltpu.VMEM((tm, D), BF16),
                        pltpu.VMEM((2 * SUBLANES, D), BF16),
                        pltpu.VMEM((tm, D), F32)],
        compiler_params=_params("parallel", "parallel", "arbitrary"),
    )(x, x, x, g.reshape(1, D), shift.reshape(B, 1, D), scale.reshape(B, 1, D), gate.reshape(B, 1, D),
      w_up, w_up, w_conv, w_conv, w_down)


def _rmsnorm_kernel(x_ref, g_ref, o_ref):
    x = x_ref[0]
    o_ref[0] = (x * lax.rsqrt(jnp.mean(x * x, axis=-1, keepdims=True) + EPS)) * g_ref[...]


def _rmsnorm_call(x, g, tm):
    B, T, D = x.shape
    return pl.pallas_call(
        _rmsnorm_kernel,
        grid=(B, T // tm),
        in_specs=[pl.BlockSpec((1, tm, D), lambda b, i: (b, i, 0)),
                  pl.BlockSpec((1, D), lambda b, i: (0, 0))],
        out_specs=pl.BlockSpec((1, tm, D), lambda b, i: (b, i, 0)),
        out_shape=jax.ShapeDtypeStruct((B, T, D), F32),
        compiler_params=_params("parallel", "parallel"),
    )(x, g.reshape(1, D))


NEG_BIG = -1e30


def _dot_nt(a, b):
    return lax.dot_general(a, b, (((1,), (1,)), ((), ())), preferred_element_type=F32)


def _lane_group():
    return lax.broadcasted_iota(jnp.int32, (1, LANES), 1) // HEAD_DIM


def _softmax_pv(parts, sink):
    m = functools.reduce(jnp.maximum, [jnp.max(s, axis=-1, keepdims=True) for s, _ in parts])
    if sink is not None:
        m = jnp.maximum(m, sink)
    den = None
    out = None
    for s, v in parts:
        e = jnp.exp(s - m)
        d = jnp.sum(e, axis=-1, keepdims=True)
        o = jnp.dot(e.astype(BF16), v, preferred_element_type=F32)
        den = d if den is None else den + d
        out = o if out is None else out + o
    if sink is not None:
        den = den + jnp.exp(sink - m)
    return out / den


def _dense_attn_kernel(q_ref, k_ref, v_ref, sink_ref, o_ref, *, gqa, use_sink):
    grp = _lane_group()
    q = q_ref[0]
    k = k_ref[0]
    v = v_ref[0]
    if gqa:
        kvg = pl.program_id(1) // 2
        k = jnp.where(grp == kvg, k, pltpu.roll(k, HEAD_DIM, axis=1))
        v = jnp.where(grp == kvg, v, pltpu.roll(v, HEAD_DIM, axis=1))
    kb = k.astype(BF16)
    vb = v.astype(BF16)
    out = jnp.zeros(q.shape, F32)
    for h in range(2):
        qh = jnp.where(grp == h, q, 0.0).astype(BF16)
        s = _dot_nt(qh, kb) * (HEAD_DIM ** -0.5)
        sink = sink_ref[0, h][:, :1] if use_sink else None
        out = jnp.where(grp == h, _softmax_pv([(s, vb)], sink), out)
    o_ref[0] = out


def _dense_attn_call(p, q_blk, k_blk, v_blk, sink, gqa):
    B, T, _ = p.shape
    n_pair = H_A // 2
    use_sink = sink is not None
    if use_sink:
        sink_arr = jnp.broadcast_to(sink.reshape(n_pair, 2, 1, 1), (n_pair, 2, 1, LANES))
    else:
        sink_arr = jnp.zeros((n_pair, 2, 1, LANES), F32)
    kv_idx = (lambda hp: 0) if gqa else (lambda hp: hp)
    return pl.pallas_call(
        functools.partial(_dense_attn_kernel, gqa=gqa, use_sink=use_sink),
        grid=(B, n_pair),
        in_specs=[pl.BlockSpec((1, T, LANES), lambda b, hp: (b, 0, q_blk + hp)),
                  pl.BlockSpec((1, T, LANES), lambda b, hp: (b, 0, k_blk + kv_idx(hp))),
                  pl.BlockSpec((1, T, LANES), lambda b, hp: (b, 0, v_blk + kv_idx(hp))),
                  pl.BlockSpec((1, 2, 1, LANES), lambda b, hp: (hp, 0, 0, 0))],
        out_specs=pl.BlockSpec((1, T, LANES), lambda b, hp: (b, 0, hp)),
        out_shape=jax.ShapeDtypeStruct((B, T, n_pair * LANES), F32),
        compiler_params=_params("parallel", "parallel"),
    )(p, p, p, sink_arr)


NBR_SPAN = WIN_R * GRID_W


def _nbr_attn_kernel(q_ref, k_ref, v_ref, ck_ref, cv_ref, bias_ref, o_ref, kb_ref, vb_ref):
    kb_ref[...] = k_ref[0].astype(BF16)
    vb_ref[...] = v_ref[0].astype(BF16)
    ckb = ck_ref[0].astype(BF16)
    cvb = cv_ref[0].astype(BF16)
    grp = _lane_group()
    n_rows = q_ref.shape[1] // GRID_W

    def row(i, carry):
        r0 = jnp.clip(i - WIN_R // 2, 0, n_rows - WIN_R)
        q = q_ref[0, pl.ds(pl.multiple_of(i * GRID_W, GRID_W), GRID_W), :]
        start = pl.multiple_of(r0 * GRID_W, GRID_W)
        kw = kb_ref[pl.ds(start, NBR_SPAN), :]
        vw = vb_ref[pl.ds(start, NBR_SPAN), :]
        out = jnp.zeros(q.shape, F32)
        for h in range(2):
            qh = jnp.where(grp == h, q, 0.0).astype(BF16)
            s_loc = _dot_nt(qh, kw) * (HEAD_DIM ** -0.5) + bias_ref[i - r0, h]
            s_ctx = _dot_nt(qh, ckb) * (HEAD_DIM ** -0.5)
            out = jnp.where(grp == h, _softmax_pv([(s_loc, vw), (s_ctx, cvb)], None), out)
        o_ref[0, pl.ds(pl.multiple_of(i * GRID_W, GRID_W), GRID_W), :] = out
        return carry

    lax.fori_loop(0, n_rows, row, 0)


def _nbr_bias_table(rpb):
    var = np.arange(WIN_R)[:, None, None, None]
    w = np.arange(GRID_W)[None, :, None, None]
    r = np.arange(WIN_R)[None, None, :, None]
    cc = np.arange(GRID_W)[None, None, None, :]
    c0 = np.clip(w - WIN_C // 2, 0, GRID_W - WIN_C)
    valid = np.broadcast_to((cc >= c0) & (cc < c0 + WIN_C), (WIN_R, GRID_W, WIN_R, GRID_W))
    dr = np.broadcast_to(r - var + (WIN_R - 1), valid.shape)
    dc = np.broadcast_to(np.clip(cc - w + (WIN_C - 1), 0, 2 * WIN_C - 2), valid.shape)
    tbl = jnp.where(valid[None], rpb[:, dr, dc], NEG_BIG)
    return tbl.reshape(rpb.shape[0], WIN_R, GRID_W, NBR_SPAN).transpose(1, 0, 2, 3)


def _nbr_attn_call(p, rpb, ck_cache, cv_cache):
    B, T, _ = p.shape
    n_pair = H_A // 2
    L = ck_cache.shape[2]
    ckp = ck_cache.transpose(0, 2, 1, 3).reshape(B, L, H_A * HEAD_DIM)
    cvp = cv_cache.transpose(0, 2, 1, 3).reshape(B, L, H_A * HEAD_DIM)
    bias = _nbr_bias_table(rpb)
    col = lambda off: pl.BlockSpec((1, T, LANES), lambda b, hp: (b, 0, off + hp))
    ctx = lambda: pl.BlockSpec((1, L, LANES), lambda b, hp: (b, 0, hp))
    return pl.pallas_call(
        _nbr_attn_kernel,
        grid=(B, n_pair),
        in_specs=[col(0), col(n_pair), col(2 * n_pair), ctx(), ctx(),
                  pl.BlockSpec((WIN_R, 2, GRID_W, NBR_SPAN), lambda b, hp: (0, hp, 0, 0))],
        out_specs=pl.BlockSpec((1, T, LANES), lambda b, hp: (b, 0, hp)),
        out_shape=jax.ShapeDtypeStruct((B, T, n_pair * LANES), F32),
        scratch_shapes=[pltpu.VMEM((T, LANES), BF16), pltpu.VMEM((T, LANES), BF16)],
        compiler_params=_params("parallel", "parallel"),
    )(p, p, p, ckp, cvp, bias)


WIN_SPAN = 3 * Q_BLOCK
ROPE_SWAP = HEAD_DIM // 4


def _rope(x, cos, sin_signed):
    lane = lax.broadcasted_iota(jnp.int32, (1, LANES), 1)
    first = lane % (2 * ROPE_SWAP) < ROPE_SWAP
    partner = jnp.where(first, pltpu.roll(x, LANES - ROPE_SWAP, axis=1), pltpu.roll(x, ROPE_SWAP, axis=1))
    return x * cos + partner * sin_signed


def _win_attn_kernel(q_ref, k_ref, v_ref, ck_ref, cv_ref, cos_ref, sin_ref, sink_ref, o_ref, kb_ref, vb_ref):
    grp = _lane_group()
    kvg = pl.program_id(1) // 2
    both = lambda a: jnp.where(grp == kvg, a, pltpu.roll(a, HEAD_DIM, axis=1))
    kb_ref[...] = _rope(both(k_ref[0]), cos_ref[...], sin_ref[...]).astype(BF16)
    vb_ref[...] = both(v_ref[0]).astype(BF16)
    ckb = both(ck_ref[0]).astype(BF16)
    cvb = both(cv_ref[0]).astype(BF16)
    T = q_ref.shape[1]
    rel = (lax.broadcasted_iota(jnp.int32, (Q_BLOCK, WIN_SPAN), 1)
           - lax.broadcasted_iota(jnp.int32, (Q_BLOCK, WIN_SPAN), 0))

    def block(n, carry):
        q0 = pl.multiple_of(n * Q_BLOCK, Q_BLOCK)
        start = pl.multiple_of(jnp.clip(q0 - Q_BLOCK, 0, T - WIN_SPAN), Q_BLOCK)
        q = _rope(q_ref[0, pl.ds(q0, Q_BLOCK), :], cos_ref[pl.ds(q0, Q_BLOCK), :], sin_ref[pl.ds(q0, Q_BLOCK), :])
        kw = kb_ref[pl.ds(start, WIN_SPAN), :]
        vw = vb_ref[pl.ds(start, WIN_SPAN), :]
        valid = jnp.abs(rel + (start - q0)) <= WIN_1D
        out = jnp.zeros(q.shape, F32)
        for h in range(2):
            qh = jnp.where(grp == h, q, 0.0).astype(BF16)
            s_loc = jnp.where(valid, _dot_nt(qh, kw) * (HEAD_DIM ** -0.5), NEG_BIG)
            s_ctx = _dot_nt(qh, ckb) * (HEAD_DIM ** -0.5)
            sink = sink_ref[0, h][:, :1]
            out = jnp.where(grp == h, _softmax_pv([(s_loc, vw), (s_ctx, cvb)], sink), out)
        o_ref[0, pl.ds(q0, Q_BLOCK), :] = out
        return carry

    lax.fori_loop(0, T // Q_BLOCK, block, 0)


def _rope_tables(T):
    t = jnp.arange(T)
    quarter = HEAD_DIM // 4
    inv = 1.0 / (ROPE_BASE ** (jnp.arange(quarter, dtype=F32) / quarter))
    ang_r = (t // GRID_W).astype(F32)[:, None] * inv[None, :]
    ang_c = (t % GRID_W).astype(F32)[:, None] * inv[None, :]
    cos = jnp.concatenate([jnp.cos(ang_r), jnp.cos(ang_r), jnp.cos(ang_c), jnp.cos(ang_c)], axis=-1)
    sin = jnp.concatenate([-jnp.sin(ang_r), jnp.sin(ang_r), -jnp.sin(ang_c), jnp.sin(ang_c)], axis=-1)
    return jnp.tile(cos, (1, 2)), jnp.tile(sin, (1, 2))


def _win_attn_call(p, sink, ck_cache, cv_cache):
    B, T, _ = p.shape
    n_pair = H_C // 2
    L = ck_cache.shape[2]
    ckp = ck_cache.transpose(0, 2, 1, 3).reshape(B, L, KV_C * HEAD_DIM)
    cvp = cv_cache.transpose(0, 2, 1, 3).reshape(B, L, KV_C * HEAD_DIM)
    cos, sin = _rope_tables(T)
    sink_arr = jnp.broadcast_to(sink.reshape(n_pair, 2, 1, 1), (n_pair, 2, 1, LANES))
    col = lambda blk: pl.BlockSpec((1, T, LANES), lambda b, hp: (b, 0, blk))
    ctx = lambda: pl.BlockSpec((1, L, LANES), lambda b, hp: (b, 0, 0))
    tab = lambda: pl.BlockSpec((T, LANES), lambda b, hp: (0, 0))
    return pl.pallas_call(
        _win_attn_kernel,
        grid=(B, n_pair),
        in_specs=[pl.BlockSpec((1, T, LANES), lambda b, hp: (b, 0, hp)), col(n_pair), col(n_pair + 1),
                  ctx(), ctx(), tab(), tab(),
                  pl.BlockSpec((1, 2, 1, LANES), lambda b, hp: (hp, 0, 0, 0))],
        out_specs=pl.BlockSpec((1, T, LANES), lambda b, hp: (b, 0, hp)),
        out_shape=jax.ShapeDtypeStruct((B, T, n_pair * LANES), F32),
        scratch_shapes=[pltpu.VMEM((T, LANES), BF16), pltpu.VMEM((T, LANES), BF16)],
        compiler_params=_params("parallel", "parallel"),
    )(p, p, p, ckp, cvp, cos, sin, sink_arr)


def _rmsnorm(x, g):
    y = x * lax.rsqrt(jnp.mean(x * x, axis=-1, keepdims=True) + EPS)
    return y * g


def _l2norm(x):
    return x * lax.rsqrt(jnp.sum(x * x, axis=-1, keepdims=True) + EPS)


def _split_cols(p, sizes):
    cuts = [int(s) for s in np.cumsum(sizes)[:-1]]
    return jnp.split(p, cuts, axis=-1)


def _dwconv(x, w):
    K = w.shape[0]
    T = x.shape[1]
    pad = K // 2
    xp = jnp.pad(x, ((0, 0), (pad, pad), (0, 0)))
    out = xp[:, 0:T] * w[0]
    for i in range(1, K):
        out = out + xp[:, i:i + T] * w[i]
    return out


def _softmax_sink(s, sink):
    if sink is None:
        return jax.nn.softmax(s, axis=-1)
    m = jnp.maximum(jnp.max(s, axis=-1, keepdims=True), sink)
    e = jnp.exp(s - m)
    return e / (jnp.sum(e, axis=-1, keepdims=True) + jnp.exp(sink - m))


def _axial_rope(x):
    T = x.shape[1]
    t = jnp.arange(T)
    half = HEAD_DIM // 2
    quarter = half // 2
    inv = 1.0 / (ROPE_BASE ** (jnp.arange(quarter, dtype=F32) / quarter))

    def rot(xa, pos):
        ang = pos.astype(F32)[:, None] * inv[None, :]
        cos = jnp.cos(ang)[None, :, None, :]
        sin = jnp.sin(ang)[None, :, None, :]
        x1, x2 = xa[..., :quarter], xa[..., quarter:]
        return jnp.concatenate([x1 * cos - x2 * sin, x1 * sin + x2 * cos], axis=-1)

    return jnp.concatenate([rot(x[..., :half], t // GRID_W), rot(x[..., half:], t % GRID_W)], axis=-1)


def _dense_attn(q, k, v, sink):
    B, Tq, Hq, Dh = q.shape
    Hk = k.shape[1]
    G = Hq // Hk
    nb = Tq // Q_BLOCK
    qb = q.reshape(B, nb, Q_BLOCK, Hk, G, Dh).swapaxes(0, 1)
    sk = None if sink is None else sink.reshape(Hk, G, 1, 1)
    scale = Dh ** -0.5

    def one(qblk):
        s = jnp.einsum('bqkgd,bksd->bkgqs', qblk, k) * scale
        p = _softmax_sink(s, sk)
        return jnp.einsum('bkgqs,bksd->bqkgd', p, v)

    o = lax.map(one, qb)
    return o.swapaxes(0, 1).reshape(B, Tq, Hq, Dh)


def _neighborhood_attn(q, k, v, rpb, ck, cv):
    B, T, H, Dh = q.shape
    R = T // GRID_W
    kr = min(WIN_R, R)
    kc = WIN_C
    qg = q.reshape(B, R, GRID_W, H, Dh)
    kg = k.reshape(B, R, GRID_W, H, Dh)
    vg = v.reshape(B, R, GRID_W, H, Dh)
    col = jnp.arange(GRID_W)
    col_idx = jnp.clip(col - kc // 2, 0, GRID_W - kc)[:, None] + jnp.arange(kc)[None, :]
    dc = col_idx - col[:, None] + (WIN_C - 1)
    scale = Dh ** -0.5

    def one(i):
        r0 = jnp.clip(i - kr // 2, 0, R - kr)
        kb = lax.dynamic_slice_in_dim(kg, r0, kr, axis=1)[:, :, col_idx]
        vb = lax.dynamic_slice_in_dim(vg, r0, kr, axis=1)[:, :, col_idx]
        qi = lax.dynamic_index_in_dim(qg, i, axis=1, keepdims=False)
        dr = r0 + jnp.arange(kr) - i + (WIN_R - 1)
        bias = rpb[:, dr[None, :, None], dc[:, None, :]]
        s_loc = jnp.einsum('bwhd,brwchd->bhwrc', qi, kb) * scale + bias
        s_loc = s_loc.reshape(B, H, GRID_W, kr * kc)
        s_ctx = jnp.einsum('bwhd,bhld->bhwl', qi, ck) * scale
        p = jax.nn.softmax(jnp.concatenate([s_loc, s_ctx], axis=-1), axis=-1)
        p_loc = p[..., :kr * kc].reshape(B, H, GRID_W, kr, kc)
        return (jnp.einsum('bhwrc,brwchd->bwhd', p_loc, vb)
                + jnp.einsum('bhwl,bhld->bwhd', p[..., kr * kc:], cv))

    o = lax.map(one, jnp.arange(R))
    return o.swapaxes(0, 1).reshape(B, T, H, Dh)


def _window_attn(q, k, v, sink, ck, cv):
    B, T, Hq, Dh = q.shape
    Hk = k.shape[2]
    G = Hq // Hk
    blk = Q_BLOCK
    nb = T // blk
    pad = ((0, 0), (blk, blk), (0, 0), (0, 0))
    kp = jnp.pad(k, pad)
    vp = jnp.pad(v, pad)
    sk = sink.reshape(1, Hk, G, 1, 1)
    scale = Dh ** -0.5

    def one(n):
        q0 = n * blk
        qn = lax.dynamic_slice_in_dim(q, q0, blk, axis=1).reshape(B, blk, Hk, G, Dh)
        kn = lax.dynamic_slice_in_dim(kp, q0, 3 * blk, axis=1)
        vn = lax.dynamic_slice_in_dim(vp, q0, 3 * blk, axis=1)
        qpos = q0 + jnp.arange(blk)
        kpos = q0 - blk + jnp.arange(3 * blk)
        valid = ((jnp.abs(kpos[None, :] - qpos[:, None]) <= WIN_1D)
                 & (kpos >= 0)[None, :] & (kpos < T)[None, :])
        s_loc = jnp.einsum('bqkgd,bskd->bkgqs', qn, kn) * scale
        s_loc = jnp.where(valid, s_loc, -jnp.inf)
        s_ctx = jnp.einsum('bqkgd,bksd->bkgqs', qn, ck) * scale
        p = _softmax_sink(jnp.concatenate([s_loc, s_ctx], axis=-1), sk)
        o = (jnp.einsum('bkgqs,bskd->bqkgd', p[..., :3 * blk], vn)
             + jnp.einsum('bkgqs,bksd->bqkgd', p[..., 3 * blk:], cv))
        return o.reshape(B, blk, Hq, Dh)

    o = lax.map(one, jnp.arange(nb))
    return o.swapaxes(0, 1).reshape(B, T, Hq, Dh)


def _gla_scan(q, k, v, g, s0):
    B, T, H, dk = q.shape
    dv = v.shape[-1]
    n = T // CHUNK
    q, k, v, g = [a.reshape(B, n, CHUNK, H, a.shape[-1]) for a in (q, k, v, g)]
    G = jnp.cumsum(g, axis=2)
    G_last = G[:, :, -1:]
    q_t = q * jnp.exp(G)
    k_t = k * jnp.exp(-G)
    k_end = k * jnp.exp(G_last - G)
    causal = jnp.tril(jnp.ones((CHUNK, CHUNK), bool))
    att = jnp.where(causal, jnp.einsum('bnchd,bnshd->bnhcs', q_t, k_t), 0.0)
    o_intra = jnp.einsum('bnhcs,bnshv->bnchv', att, v)
    u = jnp.einsum('bnshd,bnshv->bnhdv', k_end, v)
    decay = jnp.exp(G_last[:, :, 0])

    def step(S, inp):
        d, du = inp
        return d[..., None] * S + du, S

    S_fin, S_start = lax.scan(step, s0, (decay.swapaxes(0, 1), u.swapaxes(0, 1)))
    o_inter = jnp.einsum('bnchd,bnhdv->bnchv', q_t, S_start.swapaxes(0, 1))
    return (o_intra + o_inter).reshape(B, T, H, dv), S_fin


def _delta_scan(q, k, v, beta, g, s0):
    B, T, H, dk = q.shape
    dv = v.shape[-1]
    n = T // CHUNK
    blk = lambda a: a.reshape((B, n, CHUNK) + a.shape[2:]).swapaxes(2, 3)
    q, k, v, beta, g = [blk(a) for a in (q, k, v, beta, g)]
    G = jnp.cumsum(g, axis=-1)
    lower = jnp.tril(jnp.ones((CHUNK, CHUNK), bool))
    strict = jnp.tril(jnp.ones((CHUNK, CHUNK), bool), -1)
    diff = G[..., :, None] - G[..., None, :]
    gam = jnp.where(lower, jnp.exp(jnp.where(lower, diff, 0.0)), 0.0)
    k_beta = k * beta[..., None]
    a_mat = jnp.where(strict, jnp.einsum('bnhcd,bnhsd->bnhcs', k_beta, k) * gam, 0.0)
    m_mat = a_mat + jnp.eye(CHUNK, dtype=F32)
    rhs = jnp.concatenate([v * beta[..., None], k_beta * jnp.exp(G)[..., None]], axis=-1)
    sol = lax.linalg.triangular_solve(m_mat, rhs, left_side=True, lower=True, unit_diagonal=True)
    w_val, k_cum = sol[..., :dv], sol[..., dv:]
    a_qk = jnp.einsum('bnhcd,bnhsd->bnhcs', q, k) * gam
    q_g = q * jnp.exp(G)[..., None]
    k_end = k * jnp.exp(G[..., -1:] - G)[..., None]
    d_last = jnp.exp(G[..., -1])

    def step(S, inp):
        aqk, wv, kc, qg, ke, d = inp
        v_new = wv - jnp.einsum('bhcd,bhdv->bhcv', kc, S)
        o = jnp.einsum('bhcd,bhdv->bhcv', qg, S) + jnp.einsum('bhcs,bhsv->bhcv', aqk, v_new)
        S = S * d[..., None, None] + jnp.einsum('bhcd,bhcv->bhdv', ke, v_new)
        return S, o

    xs = tuple(a.swapaxes(0, 1) for a in (a_qk, w_val, k_cum, q_g, k_end, d_last))
    S_fin, o = lax.scan(step, s0, xs)
    return o.transpose(1, 0, 3, 2, 4).reshape(B, T, H, dv), S_fin


def _gla_mixer(bq, bk, bv, glr, br, w_g2, b_g, norm_g, s0):
    B, T, _ = bq.shape
    q = bq.reshape(B, T, H_B, DK_B) * DK_B ** -0.5
    k = bk.reshape(B, T, H_B, DK_B)
    v = bv.reshape(B, T, H_B, DV_B)
    z = jnp.einsum('btzr,zrc->btzc', glr.reshape(B, T, 2, GLA_RANK), w_g2) + b_g
    g = (jax.nn.log_sigmoid(z) / GLA_TAU).reshape(B, T, 2, H_B, DK_B)
    fl = lambda a: jnp.flip(a, 1)
    o_f, s_f = _gla_scan(q, k, v, g[:, :, 0], s0[:, 0])
    o_b, s_b = _gla_scan(fl(q), fl(k), fl(v), fl(g[:, :, 1]), s0[:, 1])
    o = o_f + fl(o_b)
    o = _rmsnorm(o, norm_g.reshape(H_B, DV_B)) * jax.nn.silu(br.reshape(B, T, H_B, DV_B))
    return o.reshape(B, T, H_B * DV_B), jnp.stack([s_f, s_b], axis=1)


def _delta_mixer(dq, dk, dv, da, db, dz, w_conv, a_log, dt_bias, norm_g, s0):
    B, T, _ = dq.shape
    qkv = jax.nn.silu(_dwconv(jnp.concatenate([dq, dk, dv], axis=-1), w_conv))
    q, k, v = _split_cols(qkv, (H_D * DK_D, H_D * DK_D, H_D * DV_D))
    q = _l2norm(q.reshape(B, T, H_D, DK_D)) * DK_D ** -0.5
    k = _l2norm(k.reshape(B, T, H_D, DK_D))
    v = v.reshape(B, T, H_D, DV_D)
    beta = jax.nn.sigmoid(db.reshape(B, T, 2, H_D))
    g = -jnp.exp(a_log) * jax.nn.softplus(da.reshape(B, T, 2, H_D) + dt_bias)
    fl = lambda a: jnp.flip(a, 1)
    o_f, s_f = _delta_scan(q, k, v, beta[:, :, 0], g[:, :, 0], s0[:, 0])
    o_b, s_b = _delta_scan(fl(q), fl(k), fl(v), fl(beta[:, :, 1]), fl(g[:, :, 1]), s0[:, 1])
    o = o_f + fl(o_b)
    o = _rmsnorm(o, norm_g) * jax.nn.silu(dz.reshape(B, T, H_D, DV_D))
    return o.reshape(B, T, H_D * DV_D), jnp.stack([s_f, s_b], axis=1)


def _even_mix(p, latent, rpb, w_g2, b_g, norm_g, ak_cache, av_cache, sb):
    B, T, _ = p.shape
    aq, ak, av, bq, bk, bv, glr, br = _split_cols(p, EV_SIZES)
    n_pair = H_A // 2
    if latent:
        o_a = _nbr_attn_call(p, rpb, ak_cache, av_cache)
        k_h = v_h = None
    else:
        k_h = ak.reshape(B, T, H_A, HEAD_DIM).transpose(0, 2, 1, 3)
        v_h = av.reshape(B, T, H_A, HEAD_DIM).transpose(0, 2, 1, 3)
        o_a = _dense_attn_call(p, 0, n_pair, 2 * n_pair, None, False)
    o_b, s_b = _gla_mixer(bq, bk, bv, glr, br, w_g2, b_g, norm_g, sb)
    return o_a, o_b, k_h, v_h, s_b


def _odd_mix(p, latent, sink, w_conv, a_log, dt_bias, norm_g, ck_cache, cv_cache, sd):
    B, T, _ = p.shape
    cq, ck, cv, dq, dk, dv, da, db, dz = _split_cols(p, OD_SIZES)
    n_pair = H_C // 2
    if latent:
        o_c = _win_attn_call(p, sink, ck_cache, cv_cache)
        k_h = v_h = None
    else:
        k_h = ck.reshape(B, T, KV_C, HEAD_DIM).transpose(0, 2, 1, 3)
        v_h = cv.reshape(B, T, KV_C, HEAD_DIM).transpose(0, 2, 1, 3)
        o_c = _dense_attn_call(p, 0, n_pair, n_pair + 1, sink, True)
    o_d, s_d = _delta_mixer(dq, dk, dv, da, db, dz, w_conv, a_log, dt_bias, norm_g, sd)
    return o_c, o_d, k_h, v_h, s_d


TM_LATENT = 512
TM_CONTEXT = 256
TF_FFN = 1408


def kernel(x_prompt, x_sample, cache_a_k, cache_a_v, state_b, cache_c_k, cache_c_v, state_d, c, c_ctx, ada_w, ada_b, norm1_g, norm2_g, ffn_up, ffn_conv, ffn_down, ev_w_in, ev_w_out, a_rpb, b_w_g2, b_b_g, b_norm_g, od_w_in, od_w_out, c_sink, d_conv, d_a_log, d_dt_bias, d_norm_g, final_g):
    Bp, Tp, D = x_prompt.shape
    Bs, Ts, _ = x_sample.shape
    cond = jnp.concatenate([c, c_ctx[None, :], jnp.zeros((2 * SUBLANES - Bs - 1, D), F32)], axis=0)
    mods = _adaln(cond, ada_w, ada_b)
    xp, xs = x_prompt, x_sample
    streams = ((False, Bp, TM_CONTEXT), (True, Bs, TM_LATENT))
    ak_l, av_l, sb_l, ck_l, cv_l, sd_l = [], [], [], [], [], []
    for l in range(DEPTH):
        j = l // 2
        mod_s = [mods[l, :Bs, i * D:(i + 1) * D] for i in range(6)]
        mod_p = [jnp.broadcast_to(mods[l, Bs:Bs + 1, i * D:(i + 1) * D], (Bp, D)) for i in range(6)]
        w_in = (ev_w_in[j] if l % 2 == 0 else od_w_in[j]).astype(BF16)
        w_out = (ev_w_out[j] if l % 2 == 0 else od_w_out[j]).astype(BF16)
        w_up = ffn_up[l].astype(BF16)
        w_down = ffn_down[l].astype(BF16)
        new_x = []
        for (latent, B, tm), x, mod in zip(streams, (xp, xs), (mod_p, mod_s)):
            sh1, sc1, g1, sh2, sc2, g2 = mod
            p = _nm_matmul(x, norm1_g[l], sh1, sc1, w_in, tm)
            if l % 2 == 0:
                s0 = state_b[:, j] if latent else jnp.zeros((B, 2, H_B, DK_B, DV_B), F32)
                o1, o2, k_h, v_h, s_new = _even_mix(p, latent, a_rpb[j], b_w_g2[j], b_b_g[j], b_norm_g[j],
                                                    cache_a_k[:, j], cache_a_v[:, j], s0)
                if not latent:
                    ak_l.append(k_h)
                    av_l.append(v_h)
                    sb_l.append(s_new)
            else:
                s0 = state_d[:, j] if latent else jnp.zeros((B, 2, H_D, DK_D, DV_D), F32)
                o1, o2, k_h, v_h, s_new = _odd_mix(p, latent, c_sink[j], d_conv[j], d_a_log[j], d_dt_bias[j],
                                                   d_norm_g[j], cache_c_k[:, j], cache_c_v[:, j], s0)
                if not latent:
                    ck_l.append(k_h)
                    cv_l.append(v_h)
                    sd_l.append(s_new)
            x = _proj_res(o1, o2, w_out, x, g1, tm)
            x = _conv_ffn_res(x, norm2_g[l], sh2, sc2, g2, w_up, ffn_conv[l], w_down, tm, TF_FFN)
            new_x.append(x)
        xp, xs = new_x
    y_prompt = _rmsnorm_call(xp, final_g, TM_CONTEXT)
    y_sample = _rmsnorm_call(xs, final_g, TM_LATENT)
    return (y_prompt, y_sample, jnp.stack(ak_l, axis=1), jnp.stack(av_l, axis=1), jnp.stack(sb_l, axis=1),
            jnp.stack(ck_l, axis=1), jnp.stack(cv_l, axis=1), jnp.stack(sd_l, axis=1))
```

```python
import functools
import math

import jax
import jax.numpy as jnp
import numpy as np
from jax import lax
from jax.experimental import pallas as pl
from jax.experimental.pallas import tpu as pltpu

D_MODEL = 1024
DEPTH = 4
GRID_W = 64
HEAD_DIM = 64
H_A = D_MODEL // (2 * HEAD_DIM)
WIN_R = 8
WIN_C = 16
H_B = D_MODEL // (2 * HEAD_DIM)
DK_B = HEAD_DIM
DV_B = HEAD_DIM
GLA_RANK = 16
GLA_TAU = 16.0
H_C = D_MODEL // (2 * HEAD_DIM)
KV_C = H_C // 4
WIN_1D = 128
ROPE_BASE = 10000.0
H_D = D_MODEL // (2 * HEAD_DIM)
DK_D = HEAD_DIM
DV_D = HEAD_DIM
SHORT_CONV = 3
CHUNK = 64
Q_BLOCK = 128
D_FF = ((8 * D_MODEL // 3 + 127) // 128) * 128
EPS = 1e-6
EV_SIZES = (H_A * HEAD_DIM, H_A * HEAD_DIM, H_A * HEAD_DIM, H_B * DK_B, H_B * DK_B, H_B * DV_B, 2 * GLA_RANK, H_B * DV_B)
OD_SIZES = (H_C * HEAD_DIM, KV_C * HEAD_DIM, KV_C * HEAD_DIM, H_D * DK_D, H_D * DK_D, H_D * DV_D, 2 * H_D, 2 * H_D, H_D * DV_D)
F32 = jnp.float32
BF16 = jnp.bfloat16

V7X_VMEM_LIMIT_BYTES = 52 * 1024 * 1024
SUBLANES = 8
LANES = 128
COL_CHUNK = 512


def _params(*sem):
    return pltpu.CompilerParams(dimension_semantics=sem, vmem_limit_bytes=V7X_VMEM_LIMIT_BYTES)


def _norm_mod(x, g, shift, scale):
    y = x * lax.rsqrt(jnp.mean(x * x, axis=-1, keepdims=True) + EPS)
    return (y * g) * (1.0 + scale) + shift


def _adaln_kernel(c_ref, w_ref, b_ref, o_ref):
    cond = c_ref[...]
    a = cond * jax.nn.sigmoid(cond)
    a_hi = a.astype(BF16)
    a_lo = (a - a_hi.astype(F32)).astype(BF16)
    w = w_ref[0]
    w_hi = w.astype(BF16)
    w_lo = (w - w_hi.astype(F32)).astype(BF16)
    acc = jnp.dot(a_hi, w_hi, preferred_element_type=F32)
    acc += jnp.dot(a_lo, w_hi, preferred_element_type=F32)
    acc += jnp.dot(a_hi, w_lo, preferred_element_type=F32)
    o_ref[0] = acc + b_ref[0]


def _adaln(cond, ada_w, ada_b):
    R = cond.shape[0]
    L, D, N = ada_w.shape
    tn = 1024
    return pl.pallas_call(
        _adaln_kernel,
        grid=(L, N // tn),
        in_specs=[pl.BlockSpec((R, D), lambda l, j: (0, 0)),
                  pl.BlockSpec((1, D, tn), lambda l, j: (l, 0, j)),
                  pl.BlockSpec((1, 1, tn), lambda l, j: (l, 0, j))],
        out_specs=pl.BlockSpec((1, R, tn), lambda l, j: (l, 0, j)),
        out_shape=jax.ShapeDtypeStruct((L, R, N), F32),
        compiler_params=_params("parallel", "parallel"),
    )(cond, ada_w, ada_b.reshape(L, 1, N))


def _nm_matmul_kernel(x_ref, g_ref, sh_ref, sc_ref, w_ref, o_ref):
    h = _norm_mod(x_ref[0], g_ref[...], sh_ref[0], sc_ref[0]).astype(BF16)
    n = o_ref.shape[-1]
    for c0 in range(0, n, COL_CHUNK):
        c1 = min(c0 + COL_CHUNK, n)
        o_ref[0, :, c0:c1] = jnp.dot(h, w_ref[:, c0:c1], preferred_element_type=F32)


def _nm_matmul(x, g, shift, scale, w, tm):
    B, T, D = x.shape
    N = w.shape[1]
    return pl.pallas_call(
        _nm_matmul_kernel,
        grid=(B, T // tm),
        in_specs=[pl.BlockSpec((1, tm, D), lambda b, i: (b, i, 0)),
                  pl.BlockSpec((1, D), lambda b, i: (0, 0)),
                  pl.BlockSpec((1, 1, D), lambda b, i: (b, 0, 0)),
                  pl.BlockSpec((1, 1, D), lambda b, i: (b, 0, 0)),
                  pl.BlockSpec((D, N), lambda b, i: (0, 0))],
        out_specs=pl.BlockSpec((1, tm, N), lambda b, i: (b, i, 0)),
        out_shape=jax.ShapeDtypeStruct((B, T, N), F32),
        compiler_params=_params("parallel", "parallel"),
    )(x, g.reshape(1, D), shift.reshape(B, 1, D), scale.reshape(B, 1, D), w)


def _proj_res_kernel(a1_ref, a2_ref, w1_ref, w2_ref, x_ref, gate_ref, o_ref):
    acc = jnp.dot(a1_ref[0].astype(BF16), w1_ref[...], preferred_element_type=F32)
    acc += jnp.dot(a2_ref[0].astype(BF16), w2_ref[...], preferred_element_type=F32)
    o_ref[0] = x_ref[0] + gate_ref[0] * acc


def _proj_res(a1, a2, w, x, gate, tm):
    B, T, K = a1.shape
    D = w.shape[1]
    return pl.pallas_call(
        _proj_res_kernel,
        grid=(B, T // tm),
        in_specs=[pl.BlockSpec((1, tm, K), lambda b, i: (b, i, 0)),
                  pl.BlockSpec((1, tm, K), lambda b, i: (b, i, 0)),
                  pl.BlockSpec((K, D), lambda b, i: (0, 0)),
                  pl.BlockSpec((K, D), lambda b, i: (1, 0)),
                  pl.BlockSpec((1, tm, D), lambda b, i: (b, i, 0)),
                  pl.BlockSpec((1, 1, D), lambda b, i: (b, 0, 0))],
        out_specs=pl.BlockSpec((1, tm, D), lambda b, i: (b, i, 0)),
        out_shape=jax.ShapeDtypeStruct((B, T, D), F32),
        compiler_params=_params("parallel", "parallel"),
    )(a1, a2, w, w, x, gate.reshape(B, 1, D))


def _ffn_kernel(x_ref, xp_ref, xn_ref, g_ref, sh_ref, sc_ref, gate_ref, wa_ref, wg_ref,
                ca_ref, cg_ref, wd_ref, o_ref, h_ref, halo_ref, acc_ref):
    i = pl.program_id(1)
    c = pl.program_id(2)
    tm = x_ref.shape[1]

    @pl.when(c == 0)
    def _():
        g, sh, sc = g_ref[...], sh_ref[0], sc_ref[0]
        h_ref[...] = _norm_mod(x_ref[0], g, sh, sc).astype(BF16)
        hp = jnp.where(i > 0, _norm_mod(xp_ref[0], g, sh, sc), 0.0)
        hn = jnp.where(i < pl.num_programs(1) - 1, _norm_mod(xn_ref[0], g, sh, sc), 0.0)
        halo_ref[...] = jnp.concatenate([hp, hn], axis=0).astype(BF16)
        acc_ref[...] = jnp.zeros_like(acc_ref)

    h = h_ref[...]
    halo = halo_ref[...]
    row = lax.broadcasted_iota(jnp.int32, (tm, 1), 0)

    def conv_branch(w_ref, cw_ref):
        w = w_ref[...]
        u = jnp.dot(h, w, preferred_element_type=F32)
        uh = jnp.dot(halo, w, preferred_element_type=F32)
        u_prev = jnp.where(row == 0, uh[SUBLANES - 1:SUBLANES], pltpu.roll(u, 1, axis=0))
        u_next = jnp.where(row == tm - 1, uh[SUBLANES:SUBLANES + 1], pltpu.roll(u, tm - 1, axis=0))
        cw = cw_ref[...]
        return u_prev * cw[0:1] + u * cw[1:2] + u_next * cw[2:3]

    a = conv_branch(wa_ref, ca_ref)
    gt = conv_branch(wg_ref, cg_ref)
    act = a * (gt * jax.nn.sigmoid(gt))
    acc_ref[...] += jnp.dot(act.astype(BF16), wd_ref[...], preferred_element_type=F32)

    @pl.when(c == pl.num_programs(2) - 1)
    def _():
        o_ref[0] = x_ref[0] + gate_ref[0] * acc_ref[...]


def _conv_ffn_res(x, g, shift, scale, gate, w_up, w_conv, w_down, tm, tf):
    B, T, D = x.shape
    F = w_down.shape[0]
    nf = F // tf
    nt = T // tm
    rows8 = tm // SUBLANES
    vec = lambda: pl.BlockSpec((1, 1, D), lambda b, i, c: (b, 0, 0))
    return pl.pallas_call(
        _ffn_kernel,
        grid=(B, nt, nf),
        in_specs=[pl.BlockSpec((1, tm, D), lambda b, i, c: (b, i, 0)),
                  pl.BlockSpec((1, SUBLANES, D), lambda b, i, c: (b, jnp.maximum(i * rows8 - 1, 0), 0)),
                  pl.BlockSpec((1, SUBLANES, D), lambda b, i, c: (b, jnp.minimum((i + 1) * rows8, T // SUBLANES - 1), 0)),
                  pl.BlockSpec((1, D), lambda b, i, c: (0, 0)),
                  vec(), vec(), vec(),
                  pl.BlockSpec((D, tf), lambda b, i, c: (0, c)),
                  pl.BlockSpec((D, tf), lambda b, i, c: (0, nf + c)),
                  pl.BlockSpec((3, tf), lambda b, i, c: (0, c)),
                  pl.BlockSpec((3, tf), lambda b, i, c: (0, nf + c)),
                  pl.BlockSpec((tf, D), lambda b, i, c: (c, 0))],
        out_specs=pl.BlockSpec((1, tm, D), lambda b, i, c: (b, i, 0)),
        out_shape=jax.ShapeDtypeStruct((B, T, D), F32),
        scratch_shapes=[pltpu.VMEM((tm, D), BF16),
                        pltpu.VMEM((2 * SUBLANES, D), BF16),
                        pltpu.VMEM((tm, D), F32)],
        compiler_params=_params("parallel", "parallel", "arbitrary"),
    )(x, x, x, g.reshape(1, D), shift.reshape(B, 1, D), scale.reshape(B, 1, D), gate.reshape(B, 1, D),
      w_up, w_up, w_conv, w_conv, w_down)


def _rmsnorm_kernel(x_ref, g_ref, o_ref):
    x = x_ref[0]
    o_ref[0] = (x * lax.rsqrt(jnp.mean(x * x, axis=-1, keepdims=True) + EPS)) * g_ref[...]


def _rmsnorm_call(x, g, tm):
    B, T, D = x.shape
    return pl.pallas_call(
        _rmsnorm_kernel,
        grid=(B, T // tm),
        in_specs=[pl.BlockSpec((1, tm, D), lambda b, i: (b, i, 0)),
                  pl.BlockSpec((1, D), lambda b, i: (0, 0))],
        out_specs=pl.BlockSpec((1, tm, D), lambda b, i: (b, i, 0)),
        out_shape=jax.ShapeDtypeStruct((B, T, D), F32),
        compiler_params=_params("parallel", "parallel"),
    )(x, g.reshape(1, D))


NEG_BIG = -1e30


def _dot_nt(a, b):
    return lax.dot_general(a, b, (((1,), (1,)), ((), ())), preferred_element_type=F32)


def _lane_group():
    return lax.broadcasted_iota(jnp.int32, (1, LANES), 1) // HEAD_DIM


def _softmax_pv(parts, sink):
    m = functools.reduce(jnp.maximum, [jnp.max(s, axis=-1, keepdims=True) for s, _ in parts])
    if sink is not None:
        m = jnp.maximum(m, sink)
    den = None
    out = None
    for s, v in parts:
        e = jnp.exp(s - m)
        d = jnp.sum(e, axis=-1, keepdims=True)
        o = jnp.dot(e.astype(BF16), v, preferred_element_type=F32)
        den = d if den is None else den + d
        out = o if out is None else out + o
    if sink is not None:
        den = den + jnp.exp(sink - m)
    return out / den


def _dense_attn_kernel(q_ref, k_ref, v_ref, sink_ref, o_ref, *, gqa, use_sink):
    grp = _lane_group()
    q = q_ref[0]
    k = k_ref[0]
    v = v_ref[0]
    if gqa:
        kvg = pl.program_id(1) // 2
        k = jnp.where(grp == kvg, k, pltpu.roll(k, HEAD_DIM, axis=1))
        v = jnp.where(grp == kvg, v, pltpu.roll(v, HEAD_DIM, axis=1))
    kb = k.astype(BF16)
    vb = v.astype(BF16)
    out = jnp.zeros(q.shape, F32)
    for h in range(2):
        qh = jnp.where(grp == h, q, 0.0).astype(BF16)
        s = _dot_nt(qh, kb) * (HEAD_DIM ** -0.5)
        sink = sink_ref[0, h][:, :1] if use_sink else None
        out = jnp.where(grp == h, _softmax_pv([(s, vb)], sink), out)
    o_ref[0] = out


def _dense_attn_call(p, q_blk, k_blk, v_blk, sink, gqa):
    B, T, _ = p.shape
    n_pair = H_A // 2
    use_sink = sink is not None
    if use_sink:
        sink_arr = jnp.broadcast_to(sink.reshape(n_pair, 2, 1, 1), (n_pair, 2, 1, LANES))
    else:
        sink_arr = jnp.zeros((n_pair, 2, 1, LANES), F32)
    kv_idx = (lambda hp: 0) if gqa else (lambda hp: hp)
    return pl.pallas_call(
        functools.partial(_dense_attn_kernel, gqa=gqa, use_sink=use_sink),
        grid=(B, n_pair),
        in_specs=[pl.BlockSpec((1, T, LANES), lambda b, hp: (b, 0, q_blk + hp)),
                  pl.BlockSpec((1, T, LANES), lambda b, hp: (b, 0, k_blk + kv_idx(hp))),
                  pl.BlockSpec((1, T, LANES), lambda b, hp: (b, 0, v_blk + kv_idx(hp))),
                  pl.BlockSpec((1, 2, 1, LANES), lambda b, hp: (hp, 0, 0, 0))],
        out_specs=pl.BlockSpec((1, T, LANES), lambda b, hp: (b, 0, hp)),
        out_shape=jax.ShapeDtypeStruct((B, T, n_pair * LANES), F32),
        compiler_params=_params("parallel", "parallel"),
    )(p, p, p, sink_arr)


NBR_SPAN = WIN_R * GRID_W


def _nbr_attn_kernel(q_ref, k_ref, v_ref, ck_ref, cv_ref, bias_ref, o_ref, kb_ref, vb_ref):
    kb_ref[...] = k_ref[0].astype(BF16)
    vb_ref[...] = v_ref[0].astype(BF16)
    ckb = ck_ref[0].astype(BF16)
    cvb = cv_ref[0].astype(BF16)
    grp = _lane_group()
    n_rows = q_ref.shape[1] // GRID_W

    def row(i, carry):
        r0 = jnp.clip(i - WIN_R // 2, 0, n_rows - WIN_R)
        q = q_ref[0, pl.ds(pl.multiple_of(i * GRID_W, GRID_W), GRID_W), :]
        start = pl.multiple_of(r0 * GRID_W, GRID_W)
        kw = kb_ref[pl.ds(start, NBR_SPAN), :]
        vw = vb_ref[pl.ds(start, NBR_SPAN), :]
        out = jnp.zeros(q.shape, F32)
        for h in range(2):
            qh = jnp.where(grp == h, q, 0.0).astype(BF16)
            s_loc = _dot_nt(qh, kw) * (HEAD_DIM ** -0.5) + bias_ref[i - r0, h]
            s_ctx = _dot_nt(qh, ckb) * (HEAD_DIM ** -0.5)
            out = jnp.where(grp == h, _softmax_pv([(s_loc, vw), (s_ctx, cvb)], None), out)
        o_ref[0, pl.ds(pl.multiple_of(i * GRID_W, GRID_W), GRID_W), :] = out
        return carry

    lax.fori_loop(0, n_rows, row, 0)


def _nbr_bias_table(rpb):
    var = np.arange(WIN_R)[:, None, None]
    r = np.arange(WIN_R)[None, :, None]
    row_sel = (r - var + (WIN_R - 1) == np.arange(2 * WIN_R - 1)[None, None, :]).astype(np.float32)
    w = np.arange(GRID_W)[:, None, None]
    cc = np.arange(GRID_W)[None, :, None]
    c0 = np.clip(w - WIN_C // 2, 0, GRID_W - WIN_C)
    valid = (cc >= c0) & (cc < c0 + WIN_C)
    col_sel = (valid & (cc - w + (WIN_C - 1) == np.arange(2 * WIN_C - 1)[None, None, :])).astype(np.float32)
    t = jnp.einsum('vra,hab->vhrb', row_sel, rpb, precision=lax.Precision.HIGHEST)
    tbl = jnp.einsum('vhrb,wcb->vhwrc', t, col_sel, precision=lax.Precision.HIGHEST)
    tbl = jnp.where(valid[None, None, :, None, :, 0], tbl, NEG_BIG)
    return tbl.reshape(WIN_R, rpb.shape[0], GRID_W, NBR_SPAN)


def _nbr_attn_call(p, rpb, ck_cache, cv_cache):
    B, T, _ = p.shape
    n_pair = H_A // 2
    L = ck_cache.shape[2]
    ckp = ck_cache.transpose(0, 2, 1, 3).reshape(B, L, H_A * HEAD_DIM)
    cvp = cv_cache.transpose(0, 2, 1, 3).reshape(B, L, H_A * HEAD_DIM)
    bias = _nbr_bias_table(rpb)
    col = lambda off: pl.BlockSpec((1, T, LANES), lambda b, hp: (b, 0, off + hp))
    ctx = lambda: pl.BlockSpec((1, L, LANES), lambda b, hp: (b, 0, hp))
    return pl.pallas_call(
        _nbr_attn_kernel,
        grid=(B, n_pair),
        in_specs=[col(0), col(n_pair), col(2 * n_pair), ctx(), ctx(),
                  pl.BlockSpec((WIN_R, 2, GRID_W, NBR_SPAN), lambda b, hp: (0, hp, 0, 0))],
        out_specs=pl.BlockSpec((1, T, LANES), lambda b, hp: (b, 0, hp)),
        out_shape=jax.ShapeDtypeStruct((B, T, n_pair * LANES), F32),
        scratch_shapes=[pltpu.VMEM((T, LANES), BF16), pltpu.VMEM((T, LANES), BF16)],
        compiler_params=_params("parallel", "parallel"),
    )(p, p, p, ckp, cvp, bias)


WIN_SPAN = 3 * Q_BLOCK
ROPE_SWAP = HEAD_DIM // 4


def _rope(x, cos, sin_signed):
    lane = lax.broadcasted_iota(jnp.int32, (1, LANES), 1)
    first = lane % (2 * ROPE_SWAP) < ROPE_SWAP
    partner = jnp.where(first, pltpu.roll(x, LANES - ROPE_SWAP, axis=1), pltpu.roll(x, ROPE_SWAP, axis=1))
    return x * cos + partner * sin_signed


def _win_attn_kernel(q_ref, k_ref, v_ref, ck_ref, cv_ref, cos_ref, sin_ref, sink_ref, o_ref, kb_ref, vb_ref):
    grp = _lane_group()
    kvg = pl.program_id(1) // 2
    both = lambda a: jnp.where(grp == kvg, a, pltpu.roll(a, HEAD_DIM, axis=1))
    kb_ref[...] = _rope(both(k_ref[0]), cos_ref[...], sin_ref[...]).astype(BF16)
    vb_ref[...] = both(v_ref[0]).astype(BF16)
    ckb = both(ck_ref[0]).astype(BF16)
    cvb = both(cv_ref[0]).astype(BF16)
    T = q_ref.shape[1]
    rel = (lax.broadcasted_iota(jnp.int32, (Q_BLOCK, WIN_SPAN), 1)
           - lax.broadcasted_iota(jnp.int32, (Q_BLOCK, WIN_SPAN), 0))

    def block(n, carry):
        q0 = pl.multiple_of(n * Q_BLOCK, Q_BLOCK)
        start = pl.multiple_of(jnp.clip(q0 - Q_BLOCK, 0, T - WIN_SPAN), Q_BLOCK)
        q = _rope(q_ref[0, pl.ds(q0, Q_BLOCK), :], cos_ref[pl.ds(q0, Q_BLOCK), :], sin_ref[pl.ds(q0, Q_BLOCK), :])
        kw = kb_ref[pl.ds(start, WIN_SPAN), :]
        vw = vb_ref[pl.ds(start, WIN_SPAN), :]
        valid = jnp.abs(rel + (start - q0)) <= WIN_1D
        out = jnp.zeros(q.shape, F32)
        for h in range(2):
            qh = jnp.where(grp == h, q, 0.0).astype(BF16)
            s_loc = jnp.where(valid, _dot_nt(qh, kw) * (HEAD_DIM ** -0.5), NEG_BIG)
            s_ctx = _dot_nt(qh, ckb) * (HEAD_DIM ** -0.5)
            sink = sink_ref[0, h][:, :1]
            out = jnp.where(grp == h, _softmax_pv([(s_loc, vw), (s_ctx, cvb)], sink), out)
        o_ref[0, pl.ds(q0, Q_BLOCK), :] = out
        return carry

    lax.fori_loop(0, T // Q_BLOCK, block, 0)


def _rope_tables(T):
    t = jnp.arange(T)
    quarter = HEAD_DIM // 4
    inv = 1.0 / (ROPE_BASE ** (jnp.arange(quarter, dtype=F32) / quarter))
    ang_r = (t // GRID_W).astype(F32)[:, None] * inv[None, :]
    ang_c = (t % GRID_W).astype(F32)[:, None] * inv[None, :]
    cos = jnp.concatenate([jnp.cos(ang_r), jnp.cos(ang_r), jnp.cos(ang_c), jnp.cos(ang_c)], axis=-1)
    sin = jnp.concatenate([-jnp.sin(ang_r), jnp.sin(ang_r), -jnp.sin(ang_c), jnp.sin(ang_c)], axis=-1)
    return jnp.tile(cos, (1, 2)), jnp.tile(sin, (1, 2))


def _win_attn_call(p, sink, ck_cache, cv_cache):
    B, T, _ = p.shape
    n_pair = H_C // 2
    L = ck_cache.shape[2]
    ckp = ck_cache.transpose(0, 2, 1, 3).reshape(B, L, KV_C * HEAD_DIM)
    cvp = cv_cache.transpose(0, 2, 1, 3).reshape(B, L, KV_C * HEAD_DIM)
    cos, sin = _rope_tables(T)
    sink_arr = jnp.broadcast_to(sink.reshape(n_pair, 2, 1, 1), (n_pair, 2, 1, LANES))
    col = lambda blk: pl.BlockSpec((1, T, LANES), lambda b, hp: (b, 0, blk))
    ctx = lambda: pl.BlockSpec((1, L, LANES), lambda b, hp: (b, 0, 0))
    tab = lambda: pl.BlockSpec((T, LANES), lambda b, hp: (0, 0))
    return pl.pallas_call(
        _win_attn_kernel,
        grid=(B, n_pair),
        in_specs=[pl.BlockSpec((1, T, LANES), lambda b, hp: (b, 0, hp)), col(n_pair), col(n_pair + 1),
                  ctx(), ctx(), tab(), tab(),
                  pl.BlockSpec((1, 2, 1, LANES), lambda b, hp: (hp, 0, 0, 0))],
        out_specs=pl.BlockSpec((1, T, LANES), lambda b, hp: (b, 0, hp)),
        out_shape=jax.ShapeDtypeStruct((B, T, n_pair * LANES), F32),
        scratch_shapes=[pltpu.VMEM((T, LANES), BF16), pltpu.VMEM((T, LANES), BF16)],
        compiler_params=_params("parallel", "parallel"),
    )(p, p, p, ckp, cvp, cos, sin, sink_arr)


def _dot_tn(a, b):
    return lax.dot_general(a, b, (((0,), (0,)), ((), ())), preferred_element_type=F32)


def _split_bf16(x):
    hi = x.astype(BF16)
    return hi, (x - hi.astype(F32)).astype(BF16)


def _block_diag_mask():
    return (lax.broadcasted_iota(jnp.int32, (LANES, 1), 0) // HEAD_DIM) == _lane_group()


def _head_rmsnorm(o, grp):
    sq = o * o
    s0 = jnp.sum(jnp.where(grp == 0, sq, 0.0), axis=-1, keepdims=True)
    s1 = jnp.sum(jnp.where(grp == 1, sq, 0.0), axis=-1, keepdims=True)
    ms = jnp.where(grp == 0, s0, s1) * (1.0 / HEAD_DIM)
    return o * lax.rsqrt(ms + EPS)


def _gla_kernel(q_ref, k_ref, v_ref, r_ref, glr_ref, w2_ref, bg_ref, ng_ref, s0_ref, o_ref, sfin_ref,
                ob_ref, st_ref):
    T = q_ref.shape[1]
    n = T // CHUNK
    grp = _lane_group()
    bd = _block_diag_mask()
    ri = lax.broadcasted_iota(jnp.int32, (CHUNK, CHUNK), 0)
    ci = lax.broadcasted_iota(jnp.int32, (CHUNK, CHUNK), 1)
    ones_tok = jnp.ones((CHUNK, LANES), BF16)
    st_ref[...] = s0_ref[0, :, 0]

    def chunk(c0, d):
        sl = pl.ds(c0, CHUNK)
        q = q_ref[0, sl, :] * (DK_B ** -0.5)
        k = k_ref[0, sl, :]
        vb = v_ref[0, sl, :].astype(BF16)
        z = jnp.dot(glr_ref[0, sl, :].astype(BF16), w2_ref[d], preferred_element_type=F32) + bg_ref[d]
        g = (jnp.minimum(z, 0.0) - jnp.log(1.0 + jnp.exp(-jnp.abs(z)))) * (1.0 / GLA_TAU)
        g_hi, g_lo = _split_bf16(g)
        mask = (ri <= ci) if d else (ri >= ci)
        tri = jnp.where(mask, 1.0, 0.0).astype(BF16)
        G = jnp.dot(tri, g_hi, preferred_element_type=F32) + jnp.dot(tri, g_lo, preferred_element_type=F32)
        g_tot = G[0:1] if d else G[CHUNK - 1:CHUNK]
        g_tot_rows = _dot_tn(g_hi, ones_tok) + _dot_tn(g_lo, ones_tok)
        q_t = q * jnp.exp(G)
        k_t = (k * jnp.exp(-G)).astype(BF16)
        k_e = (k * jnp.exp(g_tot - G)).astype(BF16)
        o = jnp.zeros(q.shape, F32)
        for h in range(2):
            qm = jnp.where(grp == h, q_t, 0.0).astype(BF16)
            att = jnp.where(mask, _dot_nt(qm, k_t), 0.0)
            o = jnp.where(grp == h, jnp.dot(att.astype(BF16), vb, preferred_element_type=F32), o)
        S = st_ref[d]
        o = o + jnp.dot(q_t.astype(BF16), S.astype(BF16), preferred_element_type=F32)
        st_ref[d] = jnp.exp(g_tot_rows) * S + jnp.where(bd, _dot_tn(k_e, vb), 0.0)
        return o

    def body(i, carry):
        c_f = pl.multiple_of(i * CHUNK, CHUNK)
        c_b = pl.multiple_of((n - 1 - i) * CHUNK, CHUNK)
        o_ref[0, pl.ds(c_f, CHUNK), :] = chunk(c_f, 0)
        ob_ref[pl.ds(c_b, CHUNK), :] = chunk(c_b, 1)
        return carry

    lax.fori_loop(0, n, body, 0)
    sfin_ref[0, :, 0] = st_ref[...]

    def gate(i, carry):
        sl = pl.ds(pl.multiple_of(i * CHUNK, CHUNK), CHUNK)
        r = r_ref[0, sl, :]
        y = _head_rmsnorm(o_ref[0, sl, :] + ob_ref[sl, :], grp) * ng_ref[...]
        o_ref[0, sl, :] = y * (r * jax.nn.sigmoid(r))
        return carry

    lax.fori_loop(0, n, gate, 0)


def _pair_block_diag(s):
    B, _, H, dk, dv = s.shape
    s = s.reshape(B, 2, H // 2, 2, dk, dv)
    z = jnp.zeros_like(s[:, :, :, 0])
    top = jnp.concatenate([s[:, :, :, 0], z], axis=-1)
    bot = jnp.concatenate([z, s[:, :, :, 1]], axis=-1)
    return jnp.concatenate([top, bot], axis=-2)


def _pair_block_diag_inv(sp):
    a = sp[..., :HEAD_DIM, :HEAD_DIM]
    b = sp[..., HEAD_DIM:, HEAD_DIM:]
    B, _, P = sp.shape[:3]
    return jnp.stack([a, b], axis=3).reshape(B, 2, 2 * P, HEAD_DIM, HEAD_DIM)


EV_BQ, EV_BK, EV_BV, EV_BR, EV_GLR = 12, 16, 20, 24, 28


def _gla_call(p, w_g2, b_g, norm_g, s0):
    B, T, _ = p.shape
    n_pair = H_B // 2
    w2 = jnp.zeros((2, LANES, H_B * DK_B), F32)
    w2 = w2.at[0, :GLA_RANK].set(w_g2[0]).at[1, GLA_RANK:2 * GLA_RANK].set(w_g2[1]).astype(BF16)
    col = lambda off: pl.BlockSpec((1, T, LANES), lambda b, hp: (b, 0, off + hp))
    st = lambda: pl.BlockSpec((1, 2, 1, LANES, LANES), lambda b, hp: (b, 0, hp, 0, 0))
    o, s_fin = pl.pallas_call(
        _gla_kernel,
        grid=(B, n_pair),
        in_specs=[col(EV_BQ), col(EV_BK), col(EV_BV), col(EV_BR),
                  pl.BlockSpec((1, T, LANES), lambda b, hp: (b, 0, EV_GLR)),
                  pl.BlockSpec((2, LANES, LANES), lambda b, hp: (0, 0, hp)),
                  pl.BlockSpec((2, 1, LANES), lambda b, hp: (0, 0, hp)),
                  pl.BlockSpec((1, LANES), lambda b, hp: (0, hp)),
                  st()],
        out_specs=[pl.BlockSpec((1, T, LANES), lambda b, hp: (b, 0, hp)), st()],
        out_shape=[jax.ShapeDtypeStruct((B, T, n_pair * LANES), F32),
                   jax.ShapeDtypeStruct((B, 2, n_pair, LANES, LANES), F32)],
        scratch_shapes=[pltpu.VMEM((T, LANES), F32), pltpu.VMEM((2, LANES, LANES), F32)],
        compiler_params=_params("parallel", "parallel"),
    )(p, p, p, p, p, w2, b_g.reshape(2, 1, H_B * DK_B), norm_g.reshape(1, H_B * DV_B), _pair_block_diag(s0))
    return o, _pair_block_diag_inv(s_fin)


def _rmsnorm(x, g):
    y = x * lax.rsqrt(jnp.mean(x * x, axis=-1, keepdims=True) + EPS)
    return y * g


def _l2norm(x):
    return x * lax.rsqrt(jnp.sum(x * x, axis=-1, keepdims=True) + EPS)


def _split_cols(p, sizes):
    cuts = [int(s) for s in np.cumsum(sizes)[:-1]]
    return jnp.split(p, cuts, axis=-1)


def _dwconv(x, w):
    K = w.shape[0]
    T = x.shape[1]
    pad = K // 2
    xp = jnp.pad(x, ((0, 0), (pad, pad), (0, 0)))
    out = xp[:, 0:T] * w[0]
    for i in range(1, K):
        out = out + xp[:, i:i + T] * w[i]
    return out


def _softmax_sink(s, sink):
    if sink is None:
        return jax.nn.softmax(s, axis=-1)
    m = jnp.maximum(jnp.max(s, axis=-1, keepdims=True), sink)
    e = jnp.exp(s - m)
    return e / (jnp.sum(e, axis=-1, keepdims=True) + jnp.exp(sink - m))


def _axial_rope(x):
    T = x.shape[1]
    t = jnp.arange(T)
    half = HEAD_DIM // 2
    quarter = half // 2
    inv = 1.0 / (ROPE_BASE ** (jnp.arange(quarter, dtype=F32) / quarter))

    def rot(xa, pos):
        ang = pos.astype(F32)[:, None] * inv[None, :]
        cos = jnp.cos(ang)[None, :, None, :]
        sin = jnp.sin(ang)[None, :, None, :]
        x1, x2 = xa[..., :quarter], xa[..., quarter:]
        return jnp.concatenate([x1 * cos - x2 * sin, x1 * sin + x2 * cos], axis=-1)

    return jnp.concatenate([rot(x[..., :half], t // GRID_W), rot(x[..., half:], t % GRID_W)], axis=-1)


def _dense_attn(q, k, v, sink):
    B, Tq, Hq, Dh = q.shape
    Hk = k.shape[1]
    G = Hq // Hk
    nb = Tq // Q_BLOCK
    qb = q.reshape(B, nb, Q_BLOCK, Hk, G, Dh).swapaxes(0, 1)
    sk = None if sink is None else sink.reshape(Hk, G, 1, 1)
    scale = Dh ** -0.5

    def one(qblk):
        s = jnp.einsum('bqkgd,bksd->bkgqs', qblk, k) * scale
        p = _softmax_sink(s, sk)
        return jnp.einsum('bkgqs,bksd->bqkgd', p, v)

    o = lax.map(one, qb)
    return o.swapaxes(0, 1).reshape(B, Tq, Hq, Dh)


def _neighborhood_attn(q, k, v, rpb, ck, cv):
    B, T, H, Dh = q.shape
    R = T // GRID_W
    kr = min(WIN_R, R)
    kc = WIN_C
    qg = q.reshape(B, R, GRID_W, H, Dh)
    kg = k.reshape(B, R, GRID_W, H, Dh)
    vg = v.reshape(B, R, GRID_W, H, Dh)
    col = jnp.arange(GRID_W)
    col_idx = jnp.clip(col - kc // 2, 0, GRID_W - kc)[:, None] + jnp.arange(kc)[None, :]
    dc = col_idx - col[:, None] + (WIN_C - 1)
    scale = Dh ** -0.5

    def one(i):
        r0 = jnp.clip(i - kr // 2, 0, R - kr)
        kb = lax.dynamic_slice_in_dim(kg, r0, kr, axis=1)[:, :, col_idx]
        vb = lax.dynamic_slice_in_dim(vg, r0, kr, axis=1)[:, :, col_idx]
        qi = lax.dynamic_index_in_dim(qg, i, axis=1, keepdims=False)
        dr = r0 + jnp.arange(kr) - i + (WIN_R - 1)
        bias = rpb[:, dr[None, :, None], dc[:, None, :]]
        s_loc = jnp.einsum('bwhd,brwchd->bhwrc', qi, kb) * scale + bias
        s_loc = s_loc.reshape(B, H, GRID_W, kr * kc)
        s_ctx = jnp.einsum('bwhd,bhld->bhwl', qi, ck) * scale
        p = jax.nn.softmax(jnp.concatenate([s_loc, s_ctx], axis=-1), axis=-1)
        p_loc = p[..., :kr * kc].reshape(B, H, GRID_W, kr, kc)
        return (jnp.einsum('bhwrc,brwchd->bwhd', p_loc, vb)
                + jnp.einsum('bhwl,bhld->bwhd', p[..., kr * kc:], cv))

    o = lax.map(one, jnp.arange(R))
    return o.swapaxes(0, 1).reshape(B, T, H, Dh)


def _window_attn(q, k, v, sink, ck, cv):
    B, T, Hq, Dh = q.shape
    Hk = k.shape[2]
    G = Hq // Hk
    blk = Q_BLOCK
    nb = T // blk
    pad = ((0, 0), (blk, blk), (0, 0), (0, 0))
    kp = jnp.pad(k, pad)
    vp = jnp.pad(v, pad)
    sk = sink.reshape(1, Hk, G, 1, 1)
    scale = Dh ** -0.5

    def one(n):
        q0 = n * blk
        qn = lax.dynamic_slice_in_dim(q, q0, blk, axis=1).reshape(B, blk, Hk, G, Dh)
        kn = lax.dynamic_slice_in_dim(kp, q0, 3 * blk, axis=1)
        vn = lax.dynamic_slice_in_dim(vp, q0, 3 * blk, axis=1)
        qpos = q0 + jnp.arange(blk)
        kpos = q0 - blk + jnp.arange(3 * blk)
        valid = ((jnp.abs(kpos[None, :] - qpos[:, None]) <= WIN_1D)
                 & (kpos >= 0)[None, :] & (kpos < T)[None, :])
        s_loc = jnp.einsum('bqkgd,bskd->bkgqs', qn, kn) * scale
        s_loc = jnp.where(valid, s_loc, -jnp.inf)
        s_ctx = jnp.einsum('bqkgd,bksd->bkgqs', qn, ck) * scale
        p = _softmax_sink(jnp.concatenate([s_loc, s_ctx], axis=-1), sk)
        o = (jnp.einsum('bkgqs,bskd->bqkgd', p[..., :3 * blk], vn)
             + jnp.einsum('bkgqs,bksd->bqkgd', p[..., 3 * blk:], cv))
        return o.reshape(B, blk, Hq, Dh)

    o = lax.map(one, jnp.arange(nb))
    return o.swapaxes(0, 1).reshape(B, T, Hq, Dh)


def _gla_scan(q, k, v, g, s0):
    B, T, H, dk = q.shape
    dv = v.shape[-1]
    n = T // CHUNK
    q, k, v, g = [a.reshape(B, n, CHUNK, H, a.shape[-1]) for a in (q, k, v, g)]
    G = jnp.cumsum(g, axis=2)
    G_last = G[:, :, -1:]
    q_t = q * jnp.exp(G)
    k_t = k * jnp.exp(-G)
    k_end = k * jnp.exp(G_last - G)
    causal = jnp.tril(jnp.ones((CHUNK, CHUNK), bool))
    att = jnp.where(causal, jnp.einsum('bnchd,bnshd->bnhcs', q_t, k_t), 0.0)
    o_intra = jnp.einsum('bnhcs,bnshv->bnchv', att, v)
    u = jnp.einsum('bnshd,bnshv->bnhdv', k_end, v)
    decay = jnp.exp(G_last[:, :, 0])

    def step(S, inp):
        d, du = inp
        return d[..., None] * S + du, S

    S_fin, S_start = lax.scan(step, s0, (decay.swapaxes(0, 1), u.swapaxes(0, 1)))
    o_inter = jnp.einsum('bnchd,bnhdv->bnchv', q_t, S_start.swapaxes(0, 1))
    return (o_intra + o_inter).reshape(B, T, H, dv), S_fin


def _delta_scan(q, k, v, beta, g, s0):
    B, T, H, dk = q.shape
    dv = v.shape[-1]
    n = T // CHUNK
    blk = lambda a: a.reshape((B, n, CHUNK) + a.shape[2:]).swapaxes(2, 3)
    q, k, v, beta, g = [blk(a) for a in (q, k, v, beta, g)]
    G = jnp.cumsum(g, axis=-1)
    lower = jnp.tril(jnp.ones((CHUNK, CHUNK), bool))
    strict = jnp.tril(jnp.ones((CHUNK, CHUNK), bool), -1)
    diff = G[..., :, None] - G[..., None, :]
    gam = jnp.where(lower, jnp.exp(jnp.where(lower, diff, 0.0)), 0.0)
    k_beta = k * beta[..., None]
    a_mat = jnp.where(strict, jnp.einsum('bnhcd,bnhsd->bnhcs', k_beta, k) * gam, 0.0)
    m_mat = a_mat + jnp.eye(CHUNK, dtype=F32)
    rhs = jnp.concatenate([v * beta[..., None], k_beta * jnp.exp(G)[..., None]], axis=-1)
    sol = lax.linalg.triangular_solve(m_mat, rhs, left_side=True, lower=True, unit_diagonal=True)
    w_val, k_cum = sol[..., :dv], sol[..., dv:]
    a_qk = jnp.einsum('bnhcd,bnhsd->bnhcs', q, k) * gam
    q_g = q * jnp.exp(G)[..., None]
    k_end = k * jnp.exp(G[..., -1:] - G)[..., None]
    d_last = jnp.exp(G[..., -1])

    def step(S, inp):
        aqk, wv, kc, qg, ke, d = inp
        v_new = wv - jnp.einsum('bhcd,bhdv->bhcv', kc, S)
        o = jnp.einsum('bhcd,bhdv->bhcv', qg, S) + jnp.einsum('bhcs,bhsv->bhcv', aqk, v_new)
        S = S * d[..., None, None] + jnp.einsum('bhcd,bhcv->bhdv', ke, v_new)
        return S, o

    xs = tuple(a.swapaxes(0, 1) for a in (a_qk, w_val, k_cum, q_g, k_end, d_last))
    S_fin, o = lax.scan(step, s0, xs)
    return o.transpose(1, 0, 3, 2, 4).reshape(B, T, H, dv), S_fin


def _gla_mixer(bq, bk, bv, glr, br, w_g2, b_g, norm_g, s0):
    B, T, _ = bq.shape
    q = bq.reshape(B, T, H_B, DK_B) * DK_B ** -0.5
    k = bk.reshape(B, T, H_B, DK_B)
    v = bv.reshape(B, T, H_B, DV_B)
    z = jnp.einsum('btzr,zrc->btzc', glr.reshape(B, T, 2, GLA_RANK), w_g2) + b_g
    g = (jax.nn.log_sigmoid(z) / GLA_TAU).reshape(B, T, 2, H_B, DK_B)
    fl = lambda a: jnp.flip(a, 1)
    o_f, s_f = _gla_scan(q, k, v, g[:, :, 0], s0[:, 0])
    o_b, s_b = _gla_scan(fl(q), fl(k), fl(v), fl(g[:, :, 1]), s0[:, 1])
    o = o_f + fl(o_b)
    o = _rmsnorm(o, norm_g.reshape(H_B, DV_B)) * jax.nn.silu(br.reshape(B, T, H_B, DV_B))
    return o.reshape(B, T, H_B * DV_B), jnp.stack([s_f, s_b], axis=1)


def _delta_mixer(dq, dk, dv, da, db, dz, w_conv, a_log, dt_bias, norm_g, s0):
    B, T, _ = dq.shape
    qkv = jax.nn.silu(_dwconv(jnp.concatenate([dq, dk, dv], axis=-1), w_conv))
    q, k, v = _split_cols(qkv, (H_D * DK_D, H_D * DK_D, H_D * DV_D))
    q = _l2norm(q.reshape(B, T, H_D, DK_D)) * DK_D ** -0.5
    k = _l2norm(k.reshape(B, T, H_D, DK_D))
    v = v.reshape(B, T, H_D, DV_D)
    beta = jax.nn.sigmoid(db.reshape(B, T, 2, H_D))
    g = -jnp.exp(a_log) * jax.nn.softplus(da.reshape(B, T, 2, H_D) + dt_bias)
    fl = lambda a: jnp.flip(a, 1)
    o_f, s_f = _delta_scan(q, k, v, beta[:, :, 0], g[:, :, 0], s0[:, 0])
    o_b, s_b = _delta_scan(fl(q), fl(k), fl(v), fl(beta[:, :, 1]), fl(g[:, :, 1]), s0[:, 1])
    o = o_f + fl(o_b)
    o = _rmsnorm(o, norm_g) * jax.nn.silu(dz.reshape(B, T, H_D, DV_D))
    return o.reshape(B, T, H_D * DV_D), jnp.stack([s_f, s_b], axis=1)


def _even_mix(p, latent, rpb, w_g2, b_g, norm_g, ak_cache, av_cache, sb):
    B, T, _ = p.shape
    ak = p[..., H_A * HEAD_DIM:2 * H_A * HEAD_DIM]
    av = p[..., 2 * H_A * HEAD_DIM:3 * H_A * HEAD_DIM]
    n_pair = H_A // 2
    if latent:
        o_a = _nbr_attn_call(p, rpb, ak_cache, av_cache)
        k_h = v_h = None
    else:
        k_h = ak.reshape(B, T, H_A, HEAD_DIM).transpose(0, 2, 1, 3)
        v_h = av.reshape(B, T, H_A, HEAD_DIM).transpose(0, 2, 1, 3)
        o_a = _dense_attn_call(p, 0, n_pair, 2 * n_pair, None, False)
    o_b, s_b = _gla_call(p, w_g2, b_g, norm_g, sb)
    return o_a, o_b, k_h, v_h, s_b


def _odd_mix(p, latent, sink, w_conv, a_log, dt_bias, norm_g, ck_cache, cv_cache, sd):
    B, T, _ = p.shape
    cq, ck, cv, dq, dk, dv, dz, da, db, _ = _split_cols(p, OD_ORDER_SIZES)
    n_pair = H_C // 2
    if latent:
        o_c = _win_attn_call(p, sink, ck_cache, cv_cache)
        k_h = v_h = None
    else:
        k_h = ck.reshape(B, T, KV_C, HEAD_DIM).transpose(0, 2, 1, 3)
        v_h = cv.reshape(B, T, KV_C, HEAD_DIM).transpose(0, 2, 1, 3)
        o_c = _dense_attn_call(p, 0, n_pair, n_pair + 1, sink, True)
    o_d, s_d = _delta_mixer(dq, dk, dv, da, db, dz, w_conv, a_log, dt_bias, norm_g, sd)
    return o_c, o_d, k_h, v_h, s_d


def _reorder_cols(w, sizes, order):
    parts = _split_cols(w, sizes)
    out = jnp.concatenate([parts[i] for i in order], axis=-1)
    pad = -out.shape[-1] % LANES
    return jnp.pad(out, ((0, 0), (0, pad)))


def _reorder_even_w_in(w):
    return _reorder_cols(w, EV_SIZES, (0, 1, 2, 3, 4, 5, 7, 6))


def _reorder_odd_w_in(w):
    return _reorder_cols(w, OD_SIZES, (0, 1, 2, 3, 4, 5, 8, 6, 7))


OD_ORDER_SIZES = tuple(OD_SIZES[i] for i in (0, 1, 2, 3, 4, 5, 8, 6, 7)) + (-sum(OD_SIZES) % LANES,)


TM_LATENT = 512
TM_CONTEXT = 256
TF_FFN = 1408


def kernel(x_prompt, x_sample, cache_a_k, cache_a_v, state_b, cache_c_k, cache_c_v, state_d, c, c_ctx, ada_w, ada_b, norm1_g, norm2_g, ffn_up, ffn_conv, ffn_down, ev_w_in, ev_w_out, a_rpb, b_w_g2, b_b_g, b_norm_g, od_w_in, od_w_out, c_sink, d_conv, d_a_log, d_dt_bias, d_norm_g, final_g):
    Bp, Tp, D = x_prompt.shape
    Bs, Ts, _ = x_sample.shape
    cond = jnp.concatenate([c, c_ctx[None, :], jnp.zeros((2 * SUBLANES - Bs - 1, D), F32)], axis=0)
    mods = _adaln(cond, ada_w, ada_b)
    xp, xs = x_prompt, x_sample
    streams = ((False, Bp, TM_CONTEXT), (True, Bs, TM_LATENT))
    ak_l, av_l, sb_l, ck_l, cv_l, sd_l = [], [], [], [], [], []
    for l in range(DEPTH):
        j = l // 2
        mod_s = [mods[l, :Bs, i * D:(i + 1) * D] for i in range(6)]
        mod_p = [jnp.broadcast_to(mods[l, Bs:Bs + 1, i * D:(i + 1) * D], (Bp, D)) for i in range(6)]
        w_in = (_reorder_even_w_in(ev_w_in[j]) if l % 2 == 0 else _reorder_odd_w_in(od_w_in[j])).astype(BF16)
        w_out = (ev_w_out[j] if l % 2 == 0 else od_w_out[j]).astype(BF16)
        w_up = ffn_up[l].astype(BF16)
        w_down = ffn_down[l].astype(BF16)
        new_x = []
        for (latent, B, tm), x, mod in zip(streams, (xp, xs), (mod_p, mod_s)):
            sh1, sc1, g1, sh2, sc2, g2 = mod
            p = _nm_matmul(x, norm1_g[l], sh1, sc1, w_in, tm)
            if l % 2 == 0:
                s0 = state_b[:, j] if latent else jnp.zeros((B, 2, H_B, DK_B, DV_B), F32)
                o1, o2, k_h, v_h, s_new = _even_mix(p, latent, a_rpb[j], b_w_g2[j], b_b_g[j], b_norm_g[j],
                                                    cache_a_k[:, j], cache_a_v[:, j], s0)
                if not latent:
                    ak_l.append(k_h)
                    av_l.append(v_h)
                    sb_l.append(s_new)
            else:
                s0 = state_d[:, j] if latent else jnp.zeros((B, 2, H_D, DK_D, DV_D), F32)
                o1, o2, k_h, v_h, s_new = _odd_mix(p, latent, c_sink[j], d_conv[j], d_a_log[j], d_dt_bias[j],
                                                   d_norm_g[j], cache_c_k[:, j], cache_c_v[:, j], s0)
                if not latent:
                    ck_l.append(k_h)
                    cv_l.append(v_h)
                    sd_l.append(s_new)
            x = _proj_res(o1, o2, w_out, x, g1, tm)
            x = _conv_ffn_res(x, norm2_g[l], sh2, sc2, g2, w_up, ffn_conv[l], w_down, tm, TF_FFN)
            new_x.append(x)
        xp, xs = new_x
    y_prompt = _rmsnorm_call(xp, final_g, TM_CONTEXT)
    y_sample = _rmsnorm_call(xs, final_g, TM_LATENT)
    return (y_prompt, y_sample, jnp.stack(ak_l, axis=1), jnp.stack(av_l, axis=1), jnp.stack(sb_l, axis=1),
            jnp.stack(ck_l, axis=1), jnp.stack(cv_l, axis=1), jnp.stack(sd_l, axis=1))
```

```python
import functools
import math

import jax
import jax.numpy as jnp
import numpy as np
from jax import lax
from jax.experimental import pallas as pl
from jax.experimental.pallas import tpu as pltpu

D_MODEL = 1024
DEPTH = 4
GRID_W = 64
HEAD_DIM = 64
H_A = D_MODEL // (2 * HEAD_DIM)
WIN_R = 8
WIN_C = 16
H_B = D_MODEL // (2 * HEAD_DIM)
DK_B = HEAD_DIM
DV_B = HEAD_DIM
GLA_RANK = 16
GLA_TAU = 16.0
H_C = D_MODEL // (2 * HEAD_DIM)
KV_C = H_C // 4
WIN_1D = 128
ROPE_BASE = 10000.0
H_D = D_MODEL // (2 * HEAD_DIM)
DK_D = HEAD_DIM
DV_D = HEAD_DIM
SHORT_CONV = 3
CHUNK = 64
Q_BLOCK = 128
D_FF = ((8 * D_MODEL // 3 + 127) // 128) * 128
EPS = 1e-6
EV_SIZES = (H_A * HEAD_DIM, H_A * HEAD_DIM, H_A * HEAD_DIM, H_B * DK_B, H_B * DK_B, H_B * DV_B, 2 * GLA_RANK, H_B * DV_B)
OD_SIZES = (H_C * HEAD_DIM, KV_C * HEAD_DIM, KV_C * HEAD_DIM, H_D * DK_D, H_D * DK_D, H_D * DV_D, 2 * H_D, 2 * H_D, H_D * DV_D)
F32 = jnp.float32
BF16 = jnp.bfloat16

V7X_VMEM_LIMIT_BYTES = 52 * 1024 * 1024
SUBLANES = 8
LANES = 128
COL_CHUNK = 512


def _params(*sem):
    return pltpu.CompilerParams(dimension_semantics=sem, vmem_limit_bytes=V7X_VMEM_LIMIT_BYTES)


def _norm_mod(x, g, shift, scale):
    y = x * lax.rsqrt(jnp.mean(x * x, axis=-1, keepdims=True) + EPS)
    return (y * g) * (1.0 + scale) + shift


def _adaln_kernel(c_ref, w_ref, b_ref, o_ref):
    cond = c_ref[...]
    a = cond * jax.nn.sigmoid(cond)
    a_hi = a.astype(BF16)
    a_lo = (a - a_hi.astype(F32)).astype(BF16)
    w = w_ref[0]
    w_hi = w.astype(BF16)
    w_lo = (w - w_hi.astype(F32)).astype(BF16)
    acc = jnp.dot(a_hi, w_hi, preferred_element_type=F32)
    acc += jnp.dot(a_lo, w_hi, preferred_element_type=F32)
    acc += jnp.dot(a_hi, w_lo, preferred_element_type=F32)
    o_ref[0] = acc + b_ref[0]


def _adaln(cond, ada_w, ada_b):
    R = cond.shape[0]
    L, D, N = ada_w.shape
    tn = 1024
    return pl.pallas_call(
        _adaln_kernel,
        grid=(L, N // tn),
        in_specs=[pl.BlockSpec((R, D), lambda l, j: (0, 0)),
                  pl.BlockSpec((1, D, tn), lambda l, j: (l, 0, j)),
                  pl.BlockSpec((1, 1, tn), lambda l, j: (l, 0, j))],
        out_specs=pl.BlockSpec((1, R, tn), lambda l, j: (l, 0, j)),
        out_shape=jax.ShapeDtypeStruct((L, R, N), F32),
        compiler_params=_params("parallel", "parallel"),
    )(cond, ada_w, ada_b.reshape(L, 1, N))


def _nm_matmul_kernel(x_ref, g_ref, sh_ref, sc_ref, w_ref, o_ref):
    h = _norm_mod(x_ref[0], g_ref[...], sh_ref[0], sc_ref[0]).astype(BF16)
    n = o_ref.shape[-1]
    for c0 in range(0, n, COL_CHUNK):
        c1 = min(c0 + COL_CHUNK, n)
        o_ref[0, :, c0:c1] = jnp.dot(h, w_ref[:, c0:c1], preferred_element_type=F32)


def _nm_matmul(x, g, shift, scale, w, tm):
    B, T, D = x.shape
    N = w.shape[1]
    return pl.pallas_call(
        _nm_matmul_kernel,
        grid=(B, T // tm),
        in_specs=[pl.BlockSpec((1, tm, D), lambda b, i: (b, i, 0)),
                  pl.BlockSpec((1, D), lambda b, i: (0, 0)),
                  pl.BlockSpec((1, 1, D), lambda b, i: (b, 0, 0)),
                  pl.BlockSpec((1, 1, D), lambda b, i: (b, 0, 0)),
                  pl.BlockSpec((D, N), lambda b, i: (0, 0))],
        out_specs=pl.BlockSpec((1, tm, N), lambda b, i: (b, i, 0)),
        out_shape=jax.ShapeDtypeStruct((B, T, N), F32),
        compiler_params=_params("parallel", "parallel"),
    )(x, g.reshape(1, D), shift.reshape(B, 1, D), scale.reshape(B, 1, D), w)


def _proj_res_kernel(a1_ref, a2_ref, w1_ref, w2_ref, x_ref, gate_ref, o_ref):
    acc = jnp.dot(a1_ref[0].astype(BF16), w1_ref[...], preferred_element_type=F32)
    acc += jnp.dot(a2_ref[0].astype(BF16), w2_ref[...], preferred_element_type=F32)
    o_ref[0] = x_ref[0] + gate_ref[0] * acc


def _proj_res(a1, a2, w, x, gate, tm):
    B, T, K = a1.shape
    D = w.shape[1]
    return pl.pallas_call(
        _proj_res_kernel,
        grid=(B, T // tm),
        in_specs=[pl.BlockSpec((1, tm, K), lambda b, i: (b, i, 0)),
                  pl.BlockSpec((1, tm, K), lambda b, i: (b, i, 0)),
                  pl.BlockSpec((K, D), lambda b, i: (0, 0)),
                  pl.BlockSpec((K, D), lambda b, i: (1, 0)),
                  pl.BlockSpec((1, tm, D), lambda b, i: (b, i, 0)),
                  pl.BlockSpec((1, 1, D), lambda b, i: (b, 0, 0))],
        out_specs=pl.BlockSpec((1, tm, D), lambda b, i: (b, i, 0)),
        out_shape=jax.ShapeDtypeStruct((B, T, D), F32),
        compiler_params=_params("parallel", "parallel"),
    )(a1, a2, w, w, x, gate.reshape(B, 1, D))


def _ffn_kernel(x_ref, xp_ref, xn_ref, g_ref, sh_ref, sc_ref, gate_ref, wa_ref, wg_ref,
                ca_ref, cg_ref, wd_ref, o_ref, h_ref, halo_ref, acc_ref):
    i = pl.program_id(1)
    c = pl.program_id(2)
    tm = x_ref.shape[1]

    @pl.when(c == 0)
    def _():
        g, sh, sc = g_ref[...], sh_ref[0], sc_ref[0]
        h_ref[...] = _norm_mod(x_ref[0], g, sh, sc).astype(BF16)
        hp = jnp.where(i > 0, _norm_mod(xp_ref[0], g, sh, sc), 0.0)
        hn = jnp.where(i < pl.num_programs(1) - 1, _norm_mod(xn_ref[0], g, sh, sc), 0.0)
        halo_ref[...] = jnp.concatenate([hp, hn], axis=0).astype(BF16)
        acc_ref[...] = jnp.zeros_like(acc_ref)

    h = h_ref[...]
    halo = halo_ref[...]
    row = lax.broadcasted_iota(jnp.int32, (tm, 1), 0)

    def conv_branch(w_ref, cw_ref):
        w = w_ref[...]
        u = jnp.dot(h, w, preferred_element_type=F32)
        uh = jnp.dot(halo, w, preferred_element_type=F32)
        u_prev = jnp.where(row == 0, uh[SUBLANES - 1:SUBLANES], pltpu.roll(u, 1, axis=0))
        u_next = jnp.where(row == tm - 1, uh[SUBLANES:SUBLANES + 1], pltpu.roll(u, tm - 1, axis=0))
        cw = cw_ref[...]
        return u_prev * cw[0:1] + u * cw[1:2] + u_next * cw[2:3]

    a = conv_branch(wa_ref, ca_ref)
    gt = conv_branch(wg_ref, cg_ref)
    act = a * (gt * jax.nn.sigmoid(gt))
    acc_ref[...] += jnp.dot(act.astype(BF16), wd_ref[...], preferred_element_type=F32)

    @pl.when(c == pl.num_programs(2) - 1)
    def _():
        o_ref[0] = x_ref[0] + gate_ref[0] * acc_ref[...]


def _conv_ffn_res(x, g, shift, scale, gate, w_up, w_conv, w_down, tm, tf):
    B, T, D = x.shape
    F = w_down.shape[0]
    nf = F // tf
    nt = T // tm
    rows8 = tm // SUBLANES
    vec = lambda: pl.BlockSpec((1, 1, D), lambda b, i, c: (b, 0, 0))
    return pl.pallas_call(
        _ffn_kernel,
        grid=(B, nt, nf),
        in_specs=[pl.BlockSpec((1, tm, D), lambda b, i, c: (b, i, 0)),
                  pl.BlockSpec((1, SUBLANES, D), lambda b, i, c: (b, jnp.maximum(i * rows8 - 1, 0), 0)),
                  pl.BlockSpec((1, SUBLANES, D), lambda b, i, c: (b, jnp.minimum((i + 1) * rows8, T // SUBLANES - 1), 0)),
                  pl.BlockSpec((1, D), lambda b, i, c: (0, 0)),
                  vec(), vec(), vec(),
                  pl.BlockSpec((D, tf), lambda b, i, c: (0, c)),
                  pl.BlockSpec((D, tf), lambda b, i, c: (0, nf + c)),
                  pl.BlockSpec((3, tf), lambda b, i, c: (0, c)),
                  pl.BlockSpec((3, tf), lambda b, i, c: (0, nf + c)),
                  pl.BlockSpec((tf, D), lambda b, i, c: (c, 0))],
        out_specs=pl.BlockSpec((1, tm, D), lambda b, i, c: (b, i, 0)),
        out_shape=jax.ShapeDtypeStruct((B, T, D), F32),
        scratch_shapes=[pltpu.VMEM((tm, D), BF16),
                        pltpu.VMEM((2 * SUBLANES, D), BF16),
                        pltpu.VMEM((tm, D), F32)],
        compiler_params=_params("parallel", "parallel", "arbitrary"),
    )(x, x, x, g.reshape(1, D), shift.reshape(B, 1, D), scale.reshape(B, 1, D), gate.reshape(B, 1, D),
      w_up, w_up, w_conv, w_conv, w_down)


def _rmsnorm_kernel(x_ref, g_ref, o_ref):
    x = x_ref[0]
    o_ref[0] = (x * lax.rsqrt(jnp.mean(x * x, axis=-1, keepdims=True) + EPS)) * g_ref[...]


def _rmsnorm_call(x, g, tm):
    B, T, D = x.shape
    return pl.pallas_call(
        _rmsnorm_kernel,
        grid=(B, T // tm),
        in_specs=[pl.BlockSpec((1, tm, D), lambda b, i: (b, i, 0)),
                  pl.BlockSpec((1, D), lambda b, i: (0, 0))],
        out_specs=pl.BlockSpec((1, tm, D), lambda b, i: (b, i, 0)),
        out_shape=jax.ShapeDtypeStruct((B, T, D), F32),
        compiler_params=_params("parallel", "parallel"),
    )(x, g.reshape(1, D))


NEG_BIG = -1e30


def _dot_nt(a, b):
    return lax.dot_general(a, b, (((1,), (1,)), ((), ())), preferred_element_type=F32)


def _lane_group():
    return lax.broadcasted_iota(jnp.int32, (1, LANES), 1) // HEAD_DIM


def _softmax_pv(parts, sink):
    m = functools.reduce(jnp.maximum, [jnp.max(s, axis=-1, keepdims=True) for s, _ in parts])
    if sink is not None:
        m = jnp.maximum(m, sink)
    den = None
    out = None
    for s, v in parts:
        e = jnp.exp(s - m)
        d = jnp.sum(e, axis=-1, keepdims=True)
        o = jnp.dot(e.astype(BF16), v, preferred_element_type=F32)
        den = d if den is None else den + d
        out = o if out is None else out + o
    if sink is not None:
        den = den + jnp.exp(sink - m)
    return out / den


def _dense_attn_kernel(q_ref, k_ref, v_ref, sink_ref, o_ref, *, gqa, use_sink):
    grp = _lane_group()
    q = q_ref[0]
    k = k_ref[0]
    v = v_ref[0]
    if gqa:
        kvg = pl.program_id(1) // 2
        k = jnp.where(grp == kvg, k, pltpu.roll(k, HEAD_DIM, axis=1))
        v = jnp.where(grp == kvg, v, pltpu.roll(v, HEAD_DIM, axis=1))
    kb = k.astype(BF16)
    vb = v.astype(BF16)
    out = jnp.zeros(q.shape, F32)
    for h in range(2):
        qh = jnp.where(grp == h, q, 0.0).astype(BF16)
        s = _dot_nt(qh, kb) * (HEAD_DIM ** -0.5)
        sink = sink_ref[0, h][:, :1] if use_sink else None
        out = jnp.where(grp == h, _softmax_pv([(s, vb)], sink), out)
    o_ref[0] = out


def _dense_attn_call(p, q_blk, k_blk, v_blk, sink, gqa):
    B, T, _ = p.shape
    n_pair = H_A // 2
    use_sink = sink is not None
    if use_sink:
        sink_arr = jnp.broadcast_to(sink.reshape(n_pair, 2, 1, 1), (n_pair, 2, 1, LANES))
    else:
        sink_arr = jnp.zeros((n_pair, 2, 1, LANES), F32)
    kv_idx = (lambda hp: 0) if gqa else (lambda hp: hp)
    return pl.pallas_call(
        functools.partial(_dense_attn_kernel, gqa=gqa, use_sink=use_sink),
        grid=(B, n_pair),
        in_specs=[pl.BlockSpec((1, T, LANES), lambda b, hp: (b, 0, q_blk + hp)),
                  pl.BlockSpec((1, T, LANES), lambda b, hp: (b, 0, k_blk + kv_idx(hp))),
                  pl.BlockSpec((1, T, LANES), lambda b, hp: (b, 0, v_blk + kv_idx(hp))),
                  pl.BlockSpec((1, 2, 1, LANES), lambda b, hp: (hp, 0, 0, 0))],
        out_specs=pl.BlockSpec((1, T, LANES), lambda b, hp: (b, 0, hp)),
        out_shape=jax.ShapeDtypeStruct((B, T, n_pair * LANES), F32),
        compiler_params=_params("parallel", "parallel"),
    )(p, p, p, sink_arr)


NBR_SPAN = WIN_R * GRID_W


def _nbr_attn_kernel(q_ref, k_ref, v_ref, ck_ref, cv_ref, bias_ref, o_ref, kb_ref, vb_ref):
    kb_ref[...] = k_ref[0].astype(BF16)
    vb_ref[...] = v_ref[0].astype(BF16)
    ckb = ck_ref[0].astype(BF16)
    cvb = cv_ref[0].astype(BF16)
    grp = _lane_group()
    n_rows = q_ref.shape[1] // GRID_W

    def row(i, carry):
        r0 = jnp.clip(i - WIN_R // 2, 0, n_rows - WIN_R)
        q = q_ref[0, pl.ds(pl.multiple_of(i * GRID_W, GRID_W), GRID_W), :]
        start = pl.multiple_of(r0 * GRID_W, GRID_W)
        kw = kb_ref[pl.ds(start, NBR_SPAN), :]
        vw = vb_ref[pl.ds(start, NBR_SPAN), :]
        out = jnp.zeros(q.shape, F32)
        for h in range(2):
            qh = jnp.where(grp == h, q, 0.0).astype(BF16)
            s_loc = _dot_nt(qh, kw) * (HEAD_DIM ** -0.5) + bias_ref[i - r0, h]
            s_ctx = _dot_nt(qh, ckb) * (HEAD_DIM ** -0.5)
            out = jnp.where(grp == h, _softmax_pv([(s_loc, vw), (s_ctx, cvb)], None), out)
        o_ref[0, pl.ds(pl.multiple_of(i * GRID_W, GRID_W), GRID_W), :] = out
        return carry

    lax.fori_loop(0, n_rows, row, 0)


def _nbr_bias_table(rpb):
    var = np.arange(WIN_R)[:, None, None]
    r = np.arange(WIN_R)[None, :, None]
    row_sel = (r - var + (WIN_R - 1) == np.arange(2 * WIN_R - 1)[None, None, :]).astype(np.float32)
    w = np.arange(GRID_W)[:, None, None]
    cc = np.arange(GRID_W)[None, :, None]
    c0 = np.clip(w - WIN_C // 2, 0, GRID_W - WIN_C)
    valid = (cc >= c0) & (cc < c0 + WIN_C)
    col_sel = (valid & (cc - w + (WIN_C - 1) == np.arange(2 * WIN_C - 1)[None, None, :])).astype(np.float32)
    t = jnp.einsum('vra,hab->vhrb', row_sel, rpb, precision=lax.Precision.HIGHEST)
    tbl = jnp.einsum('vhrb,wcb->vhwrc', t, col_sel, precision=lax.Precision.HIGHEST)
    tbl = jnp.where(valid[None, None, :, None, :, 0], tbl, NEG_BIG)
    return tbl.reshape(WIN_R, rpb.shape[0], GRID_W, NBR_SPAN)


def _nbr_attn_call(p, rpb, ck_cache, cv_cache):
    B, T, _ = p.shape
    n_pair = H_A // 2
    L = ck_cache.shape[2]
    ckp = ck_cache.transpose(0, 2, 1, 3).reshape(B, L, H_A * HEAD_DIM)
    cvp = cv_cache.transpose(0, 2, 1, 3).reshape(B, L, H_A * HEAD_DIM)
    bias = _nbr_bias_table(rpb)
    col = lambda off: pl.BlockSpec((1, T, LANES), lambda b, hp: (b, 0, off + hp))
    ctx = lambda: pl.BlockSpec((1, L, LANES), lambda b, hp: (b, 0, hp))
    return pl.pallas_call(
        _nbr_attn_kernel,
        grid=(B, n_pair),
        in_specs=[col(0), col(n_pair), col(2 * n_pair), ctx(), ctx(),
                  pl.BlockSpec((WIN_R, 2, GRID_W, NBR_SPAN), lambda b, hp: (0, hp, 0, 0))],
        out_specs=pl.BlockSpec((1, T, LANES), lambda b, hp: (b, 0, hp)),
        out_shape=jax.ShapeDtypeStruct((B, T, n_pair * LANES), F32),
        scratch_shapes=[pltpu.VMEM((T, LANES), BF16), pltpu.VMEM((T, LANES), BF16)],
        compiler_params=_params("parallel", "parallel"),
    )(p, p, p, ckp, cvp, bias)


WIN_SPAN = 3 * Q_BLOCK
ROPE_SWAP = HEAD_DIM // 4


def _rope(x, cos, sin_signed):
    lane = lax.broadcasted_iota(jnp.int32, (1, LANES), 1)
    first = lane % (2 * ROPE_SWAP) < ROPE_SWAP
    partner = jnp.where(first, pltpu.roll(x, LANES - ROPE_SWAP, axis=1), pltpu.roll(x, ROPE_SWAP, axis=1))
    return x * cos + partner * sin_signed


def _win_attn_kernel(q_ref, k_ref, v_ref, ck_ref, cv_ref, cos_ref, sin_ref, sink_ref, o_ref, kb_ref, vb_ref):
    grp = _lane_group()
    kvg = pl.program_id(1) // 2
    both = lambda a: jnp.where(grp == kvg, a, pltpu.roll(a, HEAD_DIM, axis=1))
    kb_ref[...] = _rope(both(k_ref[0]), cos_ref[...], sin_ref[...]).astype(BF16)
    vb_ref[...] = both(v_ref[0]).astype(BF16)
    ckb = both(ck_ref[0]).astype(BF16)
    cvb = both(cv_ref[0]).astype(BF16)
    T = q_ref.shape[1]
    rel = (lax.broadcasted_iota(jnp.int32, (Q_BLOCK, WIN_SPAN), 1)
           - lax.broadcasted_iota(jnp.int32, (Q_BLOCK, WIN_SPAN), 0))

    def block(n, carry):
        q0 = pl.multiple_of(n * Q_BLOCK, Q_BLOCK)
        start = pl.multiple_of(jnp.clip(q0 - Q_BLOCK, 0, T - WIN_SPAN), Q_BLOCK)
        q = _rope(q_ref[0, pl.ds(q0, Q_BLOCK), :], cos_ref[pl.ds(q0, Q_BLOCK), :], sin_ref[pl.ds(q0, Q_BLOCK), :])
        kw = kb_ref[pl.ds(start, WIN_SPAN), :]
        vw = vb_ref[pl.ds(start, WIN_SPAN), :]
        valid = jnp.abs(rel + (start - q0)) <= WIN_1D
        out = jnp.zeros(q.shape, F32)
        for h in range(2):
            qh = jnp.where(grp == h, q, 0.0).astype(BF16)
            s_loc = jnp.where(valid, _dot_nt(qh, kw) * (HEAD_DIM ** -0.5), NEG_BIG)
            s_ctx = _dot_nt(qh, ckb) * (HEAD_DIM ** -0.5)
            sink = sink_ref[0, h][:, :1]
            out = jnp.where(grp == h, _softmax_pv([(s_loc, vw), (s_ctx, cvb)], sink), out)
        o_ref[0, pl.ds(q0, Q_BLOCK), :] = out
        return carry

    lax.fori_loop(0, T // Q_BLOCK, block, 0)


def _rope_tables(T):
    t = jnp.arange(T)
    quarter = HEAD_DIM // 4
    inv = 1.0 / (ROPE_BASE ** (jnp.arange(quarter, dtype=F32) / quarter))
    ang_r = (t // GRID_W).astype(F32)[:, None] * inv[None, :]
    ang_c = (t % GRID_W).astype(F32)[:, None] * inv[None, :]
    cos = jnp.concatenate([jnp.cos(ang_r), jnp.cos(ang_r), jnp.cos(ang_c), jnp.cos(ang_c)], axis=-1)
    sin = jnp.concatenate([-jnp.sin(ang_r), jnp.sin(ang_r), -jnp.sin(ang_c), jnp.sin(ang_c)], axis=-1)
    return jnp.tile(cos, (1, 2)), jnp.tile(sin, (1, 2))


def _win_attn_call(p, sink, ck_cache, cv_cache):
    B, T, _ = p.shape
    n_pair = H_C // 2
    L = ck_cache.shape[2]
    ckp = ck_cache.transpose(0, 2, 1, 3).reshape(B, L, KV_C * HEAD_DIM)
    cvp = cv_cache.transpose(0, 2, 1, 3).reshape(B, L, KV_C * HEAD_DIM)
    cos, sin = _rope_tables(T)
    sink_arr = jnp.broadcast_to(sink.reshape(n_pair, 2, 1, 1), (n_pair, 2, 1, LANES))
    col = lambda blk: pl.BlockSpec((1, T, LANES), lambda b, hp: (b, 0, blk))
    ctx = lambda: pl.BlockSpec((1, L, LANES), lambda b, hp: (b, 0, 0))
    tab = lambda: pl.BlockSpec((T, LANES), lambda b, hp: (0, 0))
    return pl.pallas_call(
        _win_attn_kernel,
        grid=(B, n_pair),
        in_specs=[pl.BlockSpec((1, T, LANES), lambda b, hp: (b, 0, hp)), col(n_pair), col(n_pair + 1),
                  ctx(), ctx(), tab(), tab(),
                  pl.BlockSpec((1, 2, 1, LANES), lambda b, hp: (hp, 0, 0, 0))],
        out_specs=pl.BlockSpec((1, T, LANES), lambda b, hp: (b, 0, hp)),
        out_shape=jax.ShapeDtypeStruct((B, T, n_pair * LANES), F32),
        scratch_shapes=[pltpu.VMEM((T, LANES), BF16), pltpu.VMEM((T, LANES), BF16)],
        compiler_params=_params("parallel", "parallel"),
    )(p, p, p, ckp, cvp, cos, sin, sink_arr)


def _dot_tn(a, b):
    return lax.dot_general(a, b, (((0,), (0,)), ((), ())), preferred_element_type=F32)


def _split_bf16(x):
    hi = x.astype(BF16)
    return hi, (x - hi.astype(F32)).astype(BF16)


def _block_diag_mask():
    return (lax.broadcasted_iota(jnp.int32, (LANES, 1), 0) // HEAD_DIM) == _lane_group()


def _head_rmsnorm(o, grp):
    sq = o * o
    s0 = jnp.sum(jnp.where(grp == 0, sq, 0.0), axis=-1, keepdims=True)
    s1 = jnp.sum(jnp.where(grp == 1, sq, 0.0), axis=-1, keepdims=True)
    ms = jnp.where(grp == 0, s0, s1) * (1.0 / HEAD_DIM)
    return o * lax.rsqrt(ms + EPS)


def _gla_kernel(q_ref, k_ref, v_ref, r_ref, glr_ref, w2_ref, bg_ref, ng_ref, s0_ref, o_ref, sfin_ref,
                ob_ref, st_ref):
    T = q_ref.shape[1]
    n = T // CHUNK
    grp = _lane_group()
    bd = _block_diag_mask()
    ri = lax.broadcasted_iota(jnp.int32, (CHUNK, CHUNK), 0)
    ci = lax.broadcasted_iota(jnp.int32, (CHUNK, CHUNK), 1)
    ones_tok = jnp.ones((CHUNK, LANES), BF16)
    st_ref[...] = s0_ref[0, :, 0]

    def chunk(c0, d):
        sl = pl.ds(c0, CHUNK)
        q = q_ref[0, sl, :] * (DK_B ** -0.5)
        k = k_ref[0, sl, :]
        vb = v_ref[0, sl, :].astype(BF16)
        z = jnp.dot(glr_ref[0, sl, :].astype(BF16), w2_ref[d], preferred_element_type=F32) + bg_ref[d]
        g = (jnp.minimum(z, 0.0) - jnp.log(1.0 + jnp.exp(-jnp.abs(z)))) * (1.0 / GLA_TAU)
        g_hi, g_lo = _split_bf16(g)
        mask = (ri <= ci) if d else (ri >= ci)
        tri = jnp.where(mask, 1.0, 0.0).astype(BF16)
        G = jnp.dot(tri, g_hi, preferred_element_type=F32) + jnp.dot(tri, g_lo, preferred_element_type=F32)
        g_tot = G[0:1] if d else G[CHUNK - 1:CHUNK]
        g_tot_rows = _dot_tn(g_hi, ones_tok) + _dot_tn(g_lo, ones_tok)
        q_t = q * jnp.exp(G)
        k_t = (k * jnp.exp(-G)).astype(BF16)
        k_e = (k * jnp.exp(g_tot - G)).astype(BF16)
        o = jnp.zeros(q.shape, F32)
        for h in range(2):
            qm = jnp.where(grp == h, q_t, 0.0).astype(BF16)
            att = jnp.where(mask, _dot_nt(qm, k_t), 0.0)
            o = jnp.where(grp == h, jnp.dot(att.astype(BF16), vb, preferred_element_type=F32), o)
        S = st_ref[d]
        o = o + jnp.dot(q_t.astype(BF16), S.astype(BF16), preferred_element_type=F32)
        st_ref[d] = jnp.exp(g_tot_rows) * S + jnp.where(bd, _dot_tn(k_e, vb), 0.0)
        return o

    def body(i, carry):
        c_f = pl.multiple_of(i * CHUNK, CHUNK)
        c_b = pl.multiple_of((n - 1 - i) * CHUNK, CHUNK)
        o_ref[0, pl.ds(c_f, CHUNK), :] = chunk(c_f, 0)
        ob_ref[pl.ds(c_b, CHUNK), :] = chunk(c_b, 1)
        return carry

    lax.fori_loop(0, n, body, 0)
    sfin_ref[0, :, 0] = st_ref[...]

    def gate(i, carry):
        sl = pl.ds(pl.multiple_of(i * CHUNK, CHUNK), CHUNK)
        r = r_ref[0, sl, :]
        y = _head_rmsnorm(o_ref[0, sl, :] + ob_ref[sl, :], grp) * ng_ref[...]
        o_ref[0, sl, :] = y * (r * jax.nn.sigmoid(r))
        return carry

    lax.fori_loop(0, n, gate, 0)


def _pair_block_diag(s):
    B, _, H, dk, dv = s.shape
    s = s.reshape(B, 2, H // 2, 2, dk, dv)
    z = jnp.zeros_like(s[:, :, :, 0])
    top = jnp.concatenate([s[:, :, :, 0], z], axis=-1)
    bot = jnp.concatenate([z, s[:, :, :, 1]], axis=-1)
    return jnp.concatenate([top, bot], axis=-2)


def _pair_block_diag_inv(sp):
    a = sp[..., :HEAD_DIM, :HEAD_DIM]
    b = sp[..., HEAD_DIM:, HEAD_DIM:]
    B, _, P = sp.shape[:3]
    return jnp.stack([a, b], axis=3).reshape(B, 2, 2 * P, HEAD_DIM, HEAD_DIM)


EV_BQ, EV_BK, EV_BV, EV_BR, EV_GLR = 12, 16, 20, 24, 28


def _gla_call(p, w_g2, b_g, norm_g, s0):
    B, T, _ = p.shape
    n_pair = H_B // 2
    w2 = jnp.zeros((2, LANES, H_B * DK_B), F32)
    w2 = w2.at[0, :GLA_RANK].set(w_g2[0]).at[1, GLA_RANK:2 * GLA_RANK].set(w_g2[1]).astype(BF16)
    col = lambda off: pl.BlockSpec((1, T, LANES), lambda b, hp: (b, 0, off + hp))
    st = lambda: pl.BlockSpec((1, 2, 1, LANES, LANES), lambda b, hp: (b, 0, hp, 0, 0))
    o, s_fin = pl.pallas_call(
        _gla_kernel,
        grid=(B, n_pair),
        in_specs=[col(EV_BQ), col(EV_BK), col(EV_BV), col(EV_BR),
                  pl.BlockSpec((1, T, LANES), lambda b, hp: (b, 0, EV_GLR)),
                  pl.BlockSpec((2, LANES, LANES), lambda b, hp: (0, 0, hp)),
                  pl.BlockSpec((2, 1, LANES), lambda b, hp: (0, 0, hp)),
                  pl.BlockSpec((1, LANES), lambda b, hp: (0, hp)),
                  st()],
        out_specs=[pl.BlockSpec((1, T, LANES), lambda b, hp: (b, 0, hp)), st()],
        out_shape=[jax.ShapeDtypeStruct((B, T, n_pair * LANES), F32),
                   jax.ShapeDtypeStruct((B, 2, n_pair, LANES, LANES), F32)],
        scratch_shapes=[pltpu.VMEM((T, LANES), F32), pltpu.VMEM((2, LANES, LANES), F32)],
        compiler_params=_params("parallel", "parallel"),
    )(p, p, p, p, p, w2, b_g.reshape(2, 1, H_B * DK_B), norm_g.reshape(1, H_B * DV_B), _pair_block_diag(s0))
    return o, _pair_block_diag_inv(s_fin)


SOLVE_PASSES = 3


def _mm(a, b, passes):
    if passes == 1:
        return jnp.dot(a.astype(BF16), b.astype(BF16), preferred_element_type=F32)
    a_hi, a_lo = _split_bf16(a)
    b_hi, b_lo = _split_bf16(b)
    return (jnp.dot(a_hi, b_hi, preferred_element_type=F32) + jnp.dot(a_lo, b_hi, preferred_element_type=F32)
            + jnp.dot(a_hi, b_lo, preferred_element_type=F32))


def _unit_tri_inverse(a, eye):
    b = -a
    p = eye + b
    for _ in range(5):
        b = _mm(b, b, SOLVE_PASSES)
        p = p + _mm(p, b, SOLVE_PASSES)
    return p


def _softplus(x):
    return jnp.maximum(x, 0.0) + jnp.log(1.0 + jnp.exp(-jnp.abs(x)))


def _delta_kernel(q_ref, k_ref, v_ref, z_ref, dab_ref, cq_ref, ck_ref, cv_ref, nega_ref, dtb_ref, ng_ref, s0_ref,
                  o_ref, sfin_ref, qs_ref, ks_ref, vs_ref, ob_ref, st_ref):
    T = q_ref.shape[1]
    n = T // CHUNK
    hp = pl.program_id(1)
    grp = _lane_group()
    lane = lax.broadcasted_iota(jnp.int32, (1, LANES), 1)
    subl = lax.broadcasted_iota(jnp.int32, (LANES, 1), 0)
    bd = _block_diag_mask()
    ri = lax.broadcasted_iota(jnp.int32, (CHUNK, CHUNK), 0)
    ci = lax.broadcasted_iota(jnp.int32, (CHUNK, CHUNK), 1)
    eye = jnp.where(ri == ci, 1.0, 0.0)
    row = lax.broadcasted_iota(jnp.int32, (CHUNK, 1), 0)
    st_ref[...] = s0_ref[0, :, 0]

    def prep(i, carry):
        c0 = pl.multiple_of(i * CHUNK, CHUNK)
        lo = pl.multiple_of(jnp.maximum(c0 - SUBLANES, 0), SUBLANES)
        hi = pl.multiple_of(jnp.minimum(c0 + CHUNK, T - SUBLANES), SUBLANES)
        for src, cw_ref, dst, scale in ((q_ref, cq_ref, qs_ref, DK_D ** -0.5), (k_ref, ck_ref, ks_ref, 1.0),
                                        (v_ref, cv_ref, vs_ref, None)):
            x = src[0, pl.ds(c0, CHUNK), :]
            before = jnp.where(i > 0, src[0, pl.ds(lo, SUBLANES), :][SUBLANES - 1:], 0.0)
            after = jnp.where(i < n - 1, src[0, pl.ds(hi, SUBLANES), :][:1], 0.0)
            x_prev = jnp.where(row == 0, before, pltpu.roll(x, 1, axis=0))
            x_next = jnp.where(row == CHUNK - 1, after, pltpu.roll(x, CHUNK - 1, axis=0))
            cw = cw_ref[...]
            y = x_prev * cw[0:1] + x * cw[1:2] + x_next * cw[2:3]
            y = y * jax.nn.sigmoid(y)
            if scale is not None:
                sq = y * y
                s0 = jnp.sum(jnp.where(grp == 0, sq, 0.0), axis=-1, keepdims=True)
                s1 = jnp.sum(jnp.where(grp == 1, sq, 0.0), axis=-1, keepdims=True)
                y = y * lax.rsqrt(jnp.where(grp == 0, s0, s1) + EPS) * scale
            dst[pl.ds(c0, CHUNK), :] = y
        return carry

    lax.fori_loop(0, n, prep, 0)

    def chunk(c0, d):
        sl = pl.ds(c0, CHUNK)
        q = qs_ref[sl, :]
        k = ks_ref[sl, :]
        v = vs_ref[sl, :]
        ab = dab_ref[0, sl, :]
        gb = jnp.where(lane < 2 * H_D, nega_ref[...] * _softplus(ab + dtb_ref[...]), jax.nn.sigmoid(ab))
        gb_t = gb.T
        incl = (ri <= ci) if d else (ri >= ci)
        incl_t = (ri >= ci) if d else (ri <= ci)
        strict = (ri < ci) if d else (ri > ci)
        w_pair = kc_pair = qg_pair = ke_pair = jnp.zeros((CHUNK, LANES), F32)
        a_qk, d_last = [], []
        for h in range(2):
            jg = d * H_D + 2 * hp + h
            g_c = jnp.sum(jnp.where(lane == jg, gb, 0.0), axis=1, keepdims=True)
            beta_c = jnp.sum(jnp.where(lane == 2 * H_D + jg, gb, 0.0), axis=1, keepdims=True)
            g_r = jnp.sum(jnp.where(subl == jg, gb_t, 0.0), axis=0, keepdims=True)
            G_c = jnp.sum(jnp.where(incl, g_r, 0.0), axis=1, keepdims=True)
            G_r = jnp.sum(jnp.where(incl_t, g_c, 0.0), axis=0, keepdims=True)
            tot = jnp.sum(g_r, axis=1, keepdims=True)
            gam = jnp.where(incl, jnp.exp(jnp.where(incl, G_c - G_r, 0.0)), 0.0)
            kh = jnp.where(grp == h, k, 0.0)
            qh = jnp.where(grp == h, q, 0.0)
            vh = jnp.where(grp == h, v, 0.0)
            kb = kh * beta_c
            khb = kh.astype(BF16)
            a = jnp.where(strict, _dot_nt(kb.astype(BF16), khb) * gam, 0.0)
            t_inv = _unit_tri_inverse(a, eye)
            e_g = jnp.exp(G_c)
            w_pair = w_pair + _mm(t_inv, vh * beta_c, SOLVE_PASSES)
            kc_pair = kc_pair + _mm(t_inv, kb * e_g, SOLVE_PASSES)
            a_qk.append((_dot_nt(qh.astype(BF16), khb) * gam).astype(BF16))
            qg_pair = qg_pair + qh * e_g
            ke_pair = ke_pair + kh * jnp.exp(tot - G_c)
            d_last.append(jnp.exp(tot))
        S = st_ref[d]
        Sb = S.astype(BF16)
        v_new = w_pair - jnp.dot(kc_pair.astype(BF16), Sb, preferred_element_type=F32)
        vnb = v_new.astype(BF16)
        o = jnp.dot(qg_pair.astype(BF16), Sb, preferred_element_type=F32)
        for h in range(2):
            o = o + jnp.where(grp == h, jnp.dot(a_qk[h], vnb, preferred_element_type=F32), 0.0)
        d_rows = jnp.where(subl // HEAD_DIM == 0, d_last[0], d_last[1])
        st_ref[d] = S * d_rows + jnp.where(bd, _dot_tn(ke_pair.astype(BF16), vnb), 0.0)
        return o

    def body(i, carry):
        c_f = pl.multiple_of(i * CHUNK, CHUNK)
        c_b = pl.multiple_of((n - 1 - i) * CHUNK, CHUNK)
        o_ref[0, pl.ds(c_f, CHUNK), :] = chunk(c_f, 0)
        ob_ref[pl.ds(c_b, CHUNK), :] = chunk(c_b, 1)
        return carry

    lax.fori_loop(0, n, body, 0)
    sfin_ref[0, :, 0] = st_ref[...]

    def gate(i, carry):
        sl = pl.ds(pl.multiple_of(i * CHUNK, CHUNK), CHUNK)
        z = z_ref[0, sl, :]
        y = _head_rmsnorm(o_ref[0, sl, :] + ob_ref[sl, :], grp) * ng_ref[...]
        o_ref[0, sl, :] = y * (z * jax.nn.sigmoid(z))
        return carry

    lax.fori_loop(0, n, gate, 0)


OD_DQ, OD_DK, OD_DV, OD_DZ, OD_DAB = 6, 10, 14, 18, 22


def _delta_call(p, w_conv, a_log, dt_bias, norm_g, s0):
    B, T, _ = p.shape
    n_pair = H_D // 2
    nega = jnp.zeros((1, LANES), F32).at[0, :2 * H_D].set(-jnp.exp(a_log.reshape(-1)))
    dtb = jnp.zeros((1, LANES), F32).at[0, :2 * H_D].set(dt_bias.reshape(-1))
    ng = jnp.tile(norm_g, 2).reshape(1, LANES)
    col = lambda off: pl.BlockSpec((1, T, LANES), lambda b, hp: (b, 0, off + hp))
    cw = lambda off: pl.BlockSpec((SHORT_CONV, LANES), lambda b, hp: (0, off + hp))
    vec = lambda: pl.BlockSpec((1, LANES), lambda b, hp: (0, 0))
    st = lambda: pl.BlockSpec((1, 2, 1, LANES, LANES), lambda b, hp: (b, 0, hp, 0, 0))
    o, s_fin = pl.pallas_call(
        _delta_kernel,
        grid=(B, n_pair),
        in_specs=[col(OD_DQ), col(OD_DK), col(OD_DV), col(OD_DZ),
                  pl.BlockSpec((1, T, LANES), lambda b, hp: (b, 0, OD_DAB)),
                  cw(0), cw(n_pair), cw(2 * n_pair), vec(), vec(), vec(), st()],
        out_specs=[pl.BlockSpec((1, T, LANES), lambda b, hp: (b, 0, hp)), st()],
        out_shape=[jax.ShapeDtypeStruct((B, T, n_pair * LANES), F32),
                   jax.ShapeDtypeStruct((B, 2, n_pair, LANES, LANES), F32)],
        scratch_shapes=[pltpu.VMEM((T, LANES), F32), pltpu.VMEM((T, LANES), F32), pltpu.VMEM((T, LANES), F32),
                        pltpu.VMEM((T, LANES), F32), pltpu.VMEM((2, LANES, LANES), F32)],
        compiler_params=_params("parallel", "parallel"),
    )(p, p, p, p, p, w_conv, w_conv, w_conv, nega, dtb, ng, _pair_block_diag(s0))
    return o, _pair_block_diag_inv(s_fin)


def _rmsnorm(x, g):
    y = x * lax.rsqrt(jnp.mean(x * x, axis=-1, keepdims=True) + EPS)
    return y * g


def _l2norm(x):
    return x * lax.rsqrt(jnp.sum(x * x, axis=-1, keepdims=True) + EPS)


def _split_cols(p, sizes):
    cuts = [int(s) for s in np.cumsum(sizes)[:-1]]
    return jnp.split(p, cuts, axis=-1)


def _dwconv(x, w):
    K = w.shape[0]
    T = x.shape[1]
    pad = K // 2
    xp = jnp.pad(x, ((0, 0), (pad, pad), (0, 0)))
    out = xp[:, 0:T] * w[0]
    for i in range(1, K):
        out = out + xp[:, i:i + T] * w[i]
    return out


def _softmax_sink(s, sink):
    if sink is None:
        return jax.nn.softmax(s, axis=-1)
    m = jnp.maximum(jnp.max(s, axis=-1, keepdims=True), sink)
    e = jnp.exp(s - m)
    return e / (jnp.sum(e, axis=-1, keepdims=True) + jnp.exp(sink - m))


def _axial_rope(x):
    T = x.shape[1]
    t = jnp.arange(T)
    half = HEAD_DIM // 2
    quarter = half // 2
    inv = 1.0 / (ROPE_BASE ** (jnp.arange(quarter, dtype=F32) / quarter))

    def rot(xa, pos):
        ang = pos.astype(F32)[:, None] * inv[None, :]
        cos = jnp.cos(ang)[None, :, None, :]
        sin = jnp.sin(ang)[None, :, None, :]
        x1, x2 = xa[..., :quarter], xa[..., quarter:]
        return jnp.concatenate([x1 * cos - x2 * sin, x1 * sin + x2 * cos], axis=-1)

    return jnp.concatenate([rot(x[..., :half], t // GRID_W), rot(x[..., half:], t % GRID_W)], axis=-1)


def _dense_attn(q, k, v, sink):
    B, Tq, Hq, Dh = q.shape
    Hk = k.shape[1]
    G = Hq // Hk
    nb = Tq // Q_BLOCK
    qb = q.reshape(B, nb, Q_BLOCK, Hk, G, Dh).swapaxes(0, 1)
    sk = None if sink is None else sink.reshape(Hk, G, 1, 1)
    scale = Dh ** -0.5

    def one(qblk):
        s = jnp.einsum('bqkgd,bksd->bkgqs', qblk, k) * scale
        p = _softmax_sink(s, sk)
        return jnp.einsum('bkgqs,bksd->bqkgd', p, v)

    o = lax.map(one, qb)
    return o.swapaxes(0, 1).reshape(B, Tq, Hq, Dh)


def _neighborhood_attn(q, k, v, rpb, ck, cv):
    B, T, H, Dh = q.shape
    R = T // GRID_W
    kr = min(WIN_R, R)
    kc = WIN_C
    qg = q.reshape(B, R, GRID_W, H, Dh)
    kg = k.reshape(B, R, GRID_W, H, Dh)
    vg = v.reshape(B, R, GRID_W, H, Dh)
    col = jnp.arange(GRID_W)
    col_idx = jnp.clip(col - kc // 2, 0, GRID_W - kc)[:, None] + jnp.arange(kc)[None, :]
    dc = col_idx - col[:, None] + (WIN_C - 1)
    scale = Dh ** -0.5

    def one(i):
        r0 = jnp.clip(i - kr // 2, 0, R - kr)
        kb = lax.dynamic_slice_in_dim(kg, r0, kr, axis=1)[:, :, col_idx]
        vb = lax.dynamic_slice_in_dim(vg, r0, kr, axis=1)[:, :, col_idx]
        qi = lax.dynamic_index_in_dim(qg, i, axis=1, keepdims=False)
        dr = r0 + jnp.arange(kr) - i + (WIN_R - 1)
        bias = rpb[:, dr[None, :, None], dc[:, None, :]]
        s_loc = jnp.einsum('bwhd,brwchd->bhwrc', qi, kb) * scale + bias
        s_loc = s_loc.reshape(B, H, GRID_W, kr * kc)
        s_ctx = jnp.einsum('bwhd,bhld->bhwl', qi, ck) * scale
        p = jax.nn.softmax(jnp.concatenate([s_loc, s_ctx], axis=-1), axis=-1)
        p_loc = p[..., :kr * kc].reshape(B, H, GRID_W, kr, kc)
        return (jnp.einsum('bhwrc,brwchd->bwhd', p_loc, vb)
                + jnp.einsum('bhwl,bhld->bwhd', p[..., kr * kc:], cv))

    o = lax.map(one, jnp.arange(R))
    return o.swapaxes(0, 1).reshape(B, T, H, Dh)


def _window_attn(q, k, v, sink, ck, cv):
    B, T, Hq, Dh = q.shape
    Hk = k.shape[2]
    G = Hq // Hk
    blk = Q_BLOCK
    nb = T // blk
    pad = ((0, 0), (blk, blk), (0, 0), (0, 0))
    kp = jnp.pad(k, pad)
    vp = jnp.pad(v, pad)
    sk = sink.reshape(1, Hk, G, 1, 1)
    scale = Dh ** -0.5

    def one(n):
        q0 = n * blk
        qn = lax.dynamic_slice_in_dim(q, q0, blk, axis=1).reshape(B, blk, Hk, G, Dh)
        kn = lax.dynamic_slice_in_dim(kp, q0, 3 * blk, axis=1)
        vn = lax.dynamic_slice_in_dim(vp, q0, 3 * blk, axis=1)
        qpos = q0 + jnp.arange(blk)
        kpos = q0 - blk + jnp.arange(3 * blk)
        valid = ((jnp.abs(kpos[None, :] - qpos[:, None]) <= WIN_1D)
                 & (kpos >= 0)[None, :] & (kpos < T)[None, :])
        s_loc = jnp.einsum('bqkgd,bskd->bkgqs', qn, kn) * scale
        s_loc = jnp.where(valid, s_loc, -jnp.inf)
        s_ctx = jnp.einsum('bqkgd,bksd->bkgqs', qn, ck) * scale
        p = _softmax_sink(jnp.concatenate([s_loc, s_ctx], axis=-1), sk)
        o = (jnp.einsum('bkgqs,bskd->bqkgd', p[..., :3 * blk], vn)
             + jnp.einsum('bkgqs,bksd->bqkgd', p[..., 3 * blk:], cv))
        return o.reshape(B, blk, Hq, Dh)

    o = lax.map(one, jnp.arange(nb))
    return o.swapaxes(0, 1).reshape(B, T, Hq, Dh)


def _gla_scan(q, k, v, g, s0):
    B, T, H, dk = q.shape
    dv = v.shape[-1]
    n = T // CHUNK
    q, k, v, g = [a.reshape(B, n, CHUNK, H, a.shape[-1]) for a in (q, k, v, g)]
    G = jnp.cumsum(g, axis=2)
    G_last = G[:, :, -1:]
    q_t = q * jnp.exp(G)
    k_t = k * jnp.exp(-G)
    k_end = k * jnp.exp(G_last - G)
    causal = jnp.tril(jnp.ones((CHUNK, CHUNK), bool))
    att = jnp.where(causal, jnp.einsum('bnchd,bnshd->bnhcs', q_t, k_t), 0.0)
    o_intra = jnp.einsum('bnhcs,bnshv->bnchv', att, v)
    u = jnp.einsum('bnshd,bnshv->bnhdv', k_end, v)
    decay = jnp.exp(G_last[:, :, 0])

    def step(S, inp):
        d, du = inp
        return d[..., None] * S + du, S

    S_fin, S_start = lax.scan(step, s0, (decay.swapaxes(0, 1), u.swapaxes(0, 1)))
    o_inter = jnp.einsum('bnchd,bnhdv->bnchv', q_t, S_start.swapaxes(0, 1))
    return (o_intra + o_inter).reshape(B, T, H, dv), S_fin


def _delta_scan(q, k, v, beta, g, s0):
    B, T, H, dk = q.shape
    dv = v.shape[-1]
    n = T // CHUNK
    blk = lambda a: a.reshape((B, n, CHUNK) + a.shape[2:]).swapaxes(2, 3)
    q, k, v, beta, g = [blk(a) for a in (q, k, v, beta, g)]
    G = jnp.cumsum(g, axis=-1)
    lower = jnp.tril(jnp.ones((CHUNK, CHUNK), bool))
    strict = jnp.tril(jnp.ones((CHUNK, CHUNK), bool), -1)
    diff = G[..., :, None] - G[..., None, :]
    gam = jnp.where(lower, jnp.exp(jnp.where(lower, diff, 0.0)), 0.0)
    k_beta = k * beta[..., None]
    a_mat = jnp.where(strict, jnp.einsum('bnhcd,bnhsd->bnhcs', k_beta, k) * gam, 0.0)
    m_mat = a_mat + jnp.eye(CHUNK, dtype=F32)
    rhs = jnp.concatenate([v * beta[..., None], k_beta * jnp.exp(G)[..., None]], axis=-1)
    sol = lax.linalg.triangular_solve(m_mat, rhs, left_side=True, lower=True, unit_diagonal=True)
    w_val, k_cum = sol[..., :dv], sol[..., dv:]
    a_qk = jnp.einsum('bnhcd,bnhsd->bnhcs', q, k) * gam
    q_g = q * jnp.exp(G)[..., None]
    k_end = k * jnp.exp(G[..., -1:] - G)[..., None]
    d_last = jnp.exp(G[..., -1])

    def step(S, inp):
        aqk, wv, kc, qg, ke, d = inp
        v_new = wv - jnp.einsum('bhcd,bhdv->bhcv', kc, S)
        o = jnp.einsum('bhcd,bhdv->bhcv', qg, S) + jnp.einsum('bhcs,bhsv->bhcv', aqk, v_new)
        S = S * d[..., None, None] + jnp.einsum('bhcd,bhcv->bhdv', ke, v_new)
        return S, o

    xs = tuple(a.swapaxes(0, 1) for a in (a_qk, w_val, k_cum, q_g, k_end, d_last))
    S_fin, o = lax.scan(step, s0, xs)
    return o.transpose(1, 0, 3, 2, 4).reshape(B, T, H, dv), S_fin


def _gla_mixer(bq, bk, bv, glr, br, w_g2, b_g, norm_g, s0):
    B, T, _ = bq.shape
    q = bq.reshape(B, T, H_B, DK_B) * DK_B ** -0.5
    k = bk.reshape(B, T, H_B, DK_B)
    v = bv.reshape(B, T, H_B, DV_B)
    z = jnp.einsum('btzr,zrc->btzc', glr.reshape(B, T, 2, GLA_RANK), w_g2) + b_g
    g = (jax.nn.log_sigmoid(z) / GLA_TAU).reshape(B, T, 2, H_B, DK_B)
    fl = lambda a: jnp.flip(a, 1)
    o_f, s_f = _gla_scan(q, k, v, g[:, :, 0], s0[:, 0])
    o_b, s_b = _gla_scan(fl(q), fl(k), fl(v), fl(g[:, :, 1]), s0[:, 1])
    o = o_f + fl(o_b)
    o = _rmsnorm(o, norm_g.reshape(H_B, DV_B)) * jax.nn.silu(br.reshape(B, T, H_B, DV_B))
    return o.reshape(B, T, H_B * DV_B), jnp.stack([s_f, s_b], axis=1)


def _delta_mixer(dq, dk, dv, da, db, dz, w_conv, a_log, dt_bias, norm_g, s0):
    B, T, _ = dq.shape
    qkv = jax.nn.silu(_dwconv(jnp.concatenate([dq, dk, dv], axis=-1), w_conv))
    q, k, v = _split_cols(qkv, (H_D * DK_D, H_D * DK_D, H_D * DV_D))
    q = _l2norm(q.reshape(B, T, H_D, DK_D)) * DK_D ** -0.5
    k = _l2norm(k.reshape(B, T, H_D, DK_D))
    v = v.reshape(B, T, H_D, DV_D)
    beta = jax.nn.sigmoid(db.reshape(B, T, 2, H_D))
    g = -jnp.exp(a_log) * jax.nn.softplus(da.reshape(B, T, 2, H_D) + dt_bias)
    fl = lambda a: jnp.flip(a, 1)
    o_f, s_f = _delta_scan(q, k, v, beta[:, :, 0], g[:, :, 0], s0[:, 0])
    o_b, s_b = _delta_scan(fl(q), fl(k), fl(v), fl(beta[:, :, 1]), fl(g[:, :, 1]), s0[:, 1])
    o = o_f + fl(o_b)
    o = _rmsnorm(o, norm_g) * jax.nn.silu(dz.reshape(B, T, H_D, DV_D))
    return o.reshape(B, T, H_D * DV_D), jnp.stack([s_f, s_b], axis=1)


def _even_mix(p, latent, rpb, w_g2, b_g, norm_g, ak_cache, av_cache, sb):
    B, T, _ = p.shape
    ak = p[..., H_A * HEAD_DIM:2 * H_A * HEAD_DIM]
    av = p[..., 2 * H_A * HEAD_DIM:3 * H_A * HEAD_DIM]
    n_pair = H_A // 2
    if latent:
        o_a = _nbr_attn_call(p, rpb, ak_cache, av_cache)
        k_h = v_h = None
    else:
        k_h = ak.reshape(B, T, H_A, HEAD_DIM).transpose(0, 2, 1, 3)
        v_h = av.reshape(B, T, H_A, HEAD_DIM).transpose(0, 2, 1, 3)
        o_a = _dense_attn_call(p, 0, n_pair, 2 * n_pair, None, False)
    o_b, s_b = _gla_call(p, w_g2, b_g, norm_g, sb)
    return o_a, o_b, k_h, v_h, s_b


def _odd_mix(p, latent, sink, w_conv, a_log, dt_bias, norm_g, ck_cache, cv_cache, sd):
    B, T, _ = p.shape
    ck = p[..., H_C * HEAD_DIM:(H_C + KV_C) * HEAD_DIM]
    cv = p[..., (H_C + KV_C) * HEAD_DIM:(H_C + 2 * KV_C) * HEAD_DIM]
    n_pair = H_C // 2
    if latent:
        o_c = _win_attn_call(p, sink, ck_cache, cv_cache)
        k_h = v_h = None
    else:
        k_h = ck.reshape(B, T, KV_C, HEAD_DIM).transpose(0, 2, 1, 3)
        v_h = cv.reshape(B, T, KV_C, HEAD_DIM).transpose(0, 2, 1, 3)
        o_c = _dense_attn_call(p, 0, n_pair, n_pair + 1, sink, True)
    o_d, s_d = _delta_call(p, w_conv, a_log, dt_bias, norm_g, sd)
    return o_c, o_d, k_h, v_h, s_d


def _reorder_cols(w, sizes, order):
    parts = _split_cols(w, sizes)
    out = jnp.concatenate([parts[i] for i in order], axis=-1)
    pad = -out.shape[-1] % LANES
    return jnp.pad(out, ((0, 0), (0, pad)))


def _reorder_even_w_in(w):
    return _reorder_cols(w, EV_SIZES, (0, 1, 2, 3, 4, 5, 7, 6))


def _reorder_odd_w_in(w):
    return _reorder_cols(w, OD_SIZES, (0, 1, 2, 3, 4, 5, 8, 6, 7))


OD_ORDER_SIZES = tuple(OD_SIZES[i] for i in (0, 1, 2, 3, 4, 5, 8, 6, 7)) + (-sum(OD_SIZES) % LANES,)


TM_LATENT = 512
TM_CONTEXT = 256
TF_FFN = 1408


def kernel(x_prompt, x_sample, cache_a_k, cache_a_v, state_b, cache_c_k, cache_c_v, state_d, c, c_ctx, ada_w, ada_b, norm1_g, norm2_g, ffn_up, ffn_conv, ffn_down, ev_w_in, ev_w_out, a_rpb, b_w_g2, b_b_g, b_norm_g, od_w_in, od_w_out, c_sink, d_conv, d_a_log, d_dt_bias, d_norm_g, final_g):
    Bp, Tp, D = x_prompt.shape
    Bs, Ts, _ = x_sample.shape
    cond = jnp.concatenate([c, c_ctx[None, :], jnp.zeros((2 * SUBLANES - Bs - 1, D), F32)], axis=0)
    mods = _adaln(cond, ada_w, ada_b)
    xp, xs = x_prompt, x_sample
    streams = ((False, Bp, TM_CONTEXT), (True, Bs, TM_LATENT))
    ak_l, av_l, sb_l, ck_l, cv_l, sd_l = [], [], [], [], [], []
    for l in range(DEPTH):
        j = l // 2
        mod_s = [mods[l, :Bs, i * D:(i + 1) * D] for i in range(6)]
        mod_p = [jnp.broadcast_to(mods[l, Bs:Bs + 1, i * D:(i + 1) * D], (Bp, D)) for i in range(6)]
        w_in = (_reorder_even_w_in(ev_w_in[j]) if l % 2 == 0 else _reorder_odd_w_in(od_w_in[j])).astype(BF16)
        w_out = (ev_w_out[j] if l % 2 == 0 else od_w_out[j]).astype(BF16)
        w_up = ffn_up[l].astype(BF16)
        w_down = ffn_down[l].astype(BF16)
        new_x = []
        for (latent, B, tm), x, mod in zip(streams, (xp, xs), (mod_p, mod_s)):
            sh1, sc1, g1, sh2, sc2, g2 = mod
            p = _nm_matmul(x, norm1_g[l], sh1, sc1, w_in, tm)
            if l % 2 == 0:
                s0 = state_b[:, j] if latent else jnp.zeros((B, 2, H_B, DK_B, DV_B), F32)
                o1, o2, k_h, v_h, s_new = _even_mix(p, latent, a_rpb[j], b_w_g2[j], b_b_g[j], b_norm_g[j],
                                                    cache_a_k[:, j], cache_a_v[:, j], s0)
                if not latent:
                    ak_l.append(k_h)
                    av_l.append(v_h)
                    sb_l.append(s_new)
            else:
                s0 = state_d[:, j] if latent else jnp.zeros((B, 2, H_D, DK_D, DV_D), F32)
                o1, o2, k_h, v_h, s_new = _odd_mix(p, latent, c_sink[j], d_conv[j], d_a_log[j], d_dt_bias[j],
                                                   d_norm_g[j], cache_c_k[:, j], cache_c_v[:, j], s0)
                if not latent:
                    ck_l.append(k_h)
                    cv_l.append(v_h)
                    sd_l.append(s_new)
            x = _proj_res(o1, o2, w_out, x, g1, tm)
            x = _conv_ffn_res(x, norm2_g[l], sh2, sc2, g2, w_up, ffn_conv[l], w_down, tm, TF_FFN)
            new_x.append(x)
        xp, xs = new_x
    y_prompt = _rmsnorm_call(xp, final_g, TM_CONTEXT)
    y_sample = _rmsnorm_call(xs, final_g, TM_LATENT)
    return (y_prompt, y_sample, jnp.stack(ak_l, axis=1), jnp.stack(av_l, axis=1), jnp.stack(sb_l, axis=1),
            jnp.stack(ck_l, axis=1), jnp.stack(cv_l, axis=1), jnp.stack(sd_l, axis=1))
```

```python
import functools
import math

import jax
import jax.numpy as jnp
import numpy as np
from jax import lax
from jax.experimental import pallas as pl
from jax.experimental.pallas import tpu as pltpu

D_MODEL = 1024
DEPTH = 4
GRID_W = 64
HEAD_DIM = 64
H_A = D_MODEL // (2 * HEAD_DIM)
WIN_R = 8
WIN_C = 16
H_B = D_MODEL // (2 * HEAD_DIM)
DK_B = HEAD_DIM
DV_B = HEAD_DIM
GLA_RANK = 16
GLA_TAU = 16.0
H_C = D_MODEL // (2 * HEAD_DIM)
KV_C = H_C // 4
WIN_1D = 128
ROPE_BASE = 10000.0
H_D = D_MODEL // (2 * HEAD_DIM)
DK_D = HEAD_DIM
DV_D = HEAD_DIM
SHORT_CONV = 3
CHUNK = 64
Q_BLOCK = 128
D_FF = ((8 * D_MODEL // 3 + 127) // 128) * 128
EPS = 1e-6
EV_SIZES = (H_A * HEAD_DIM, H_A * HEAD_DIM, H_A * HEAD_DIM, H_B * DK_B, H_B * DK_B, H_B * DV_B, 2 * GLA_RANK, H_B * DV_B)
OD_SIZES = (H_C * HEAD_DIM, KV_C * HEAD_DIM, KV_C * HEAD_DIM, H_D * DK_D, H_D * DK_D, H_D * DV_D, 2 * H_D, 2 * H_D, H_D * DV_D)
F32 = jnp.float32
BF16 = jnp.bfloat16

V7X_VMEM_LIMIT_BYTES = 52 * 1024 * 1024
SUBLANES = 8
LANES = 128
COL_CHUNK = 512


def _params(*sem):
    return pltpu.CompilerParams(dimension_semantics=sem, vmem_limit_bytes=V7X_VMEM_LIMIT_BYTES)


def _norm_mod(x, g, shift, scale):
    y = x * lax.rsqrt(jnp.mean(x * x, axis=-1, keepdims=True) + EPS)
    return (y * g) * (1.0 + scale) + shift


def _adaln_kernel(c_ref, w_ref, b_ref, o_ref):
    cond = c_ref[...]
    a = cond * jax.nn.sigmoid(cond)
    a_hi = a.astype(BF16)
    a_lo = (a - a_hi.astype(F32)).astype(BF16)
    w = w_ref[0]
    w_hi = w.astype(BF16)
    w_lo = (w - w_hi.astype(F32)).astype(BF16)
    acc = jnp.dot(a_hi, w_hi, preferred_element_type=F32)
    acc += jnp.dot(a_lo, w_hi, preferred_element_type=F32)
    acc += jnp.dot(a_hi, w_lo, preferred_element_type=F32)
    o_ref[0] = acc + b_ref[0]


def _adaln(cond, ada_w, ada_b):
    R = cond.shape[0]
    L, D, N = ada_w.shape
    tn = 1024
    return pl.pallas_call(
        _adaln_kernel,
        grid=(L, N // tn),
        in_specs=[pl.BlockSpec((R, D), lambda l, j: (0, 0)),
                  pl.BlockSpec((1, D, tn), lambda l, j: (l, 0, j)),
                  pl.BlockSpec((1, 1, tn), lambda l, j: (l, 0, j))],
        out_specs=pl.BlockSpec((1, R, tn), lambda l, j: (l, 0, j)),
        out_shape=jax.ShapeDtypeStruct((L, R, N), F32),
        compiler_params=_params("parallel", "parallel"),
    )(cond, ada_w, ada_b.reshape(L, 1, N))


def _nm_matmul_kernel(x_ref, g_ref, sh_ref, sc_ref, w_ref, o_ref):
    h = _norm_mod(x_ref[0], g_ref[...], sh_ref[0], sc_ref[0]).astype(BF16)
    n = o_ref.shape[-1]
    for c0 in range(0, n, COL_CHUNK):
        c1 = min(c0 + COL_CHUNK, n)
        o_ref[0, :, c0:c1] = jnp.dot(h, w_ref[:, c0:c1], preferred_element_type=F32)


def _nm_matmul(x, g, shift, scale, w, tm):
    B, T, D = x.shape
    N = w.shape[1]
    return pl.pallas_call(
        _nm_matmul_kernel,
        grid=(B, T // tm),
        in_specs=[pl.BlockSpec((1, tm, D), lambda b, i: (b, i, 0)),
                  pl.BlockSpec((1, D), lambda b, i: (0, 0)),
                  pl.BlockSpec((1, 1, D), lambda b, i: (b, 0, 0)),
                  pl.BlockSpec((1, 1, D), lambda b, i: (b, 0, 0)),
                  pl.BlockSpec((D, N), lambda b, i: (0, 0))],
        out_specs=pl.BlockSpec((1, tm, N), lambda b, i: (b, i, 0)),
        out_shape=jax.ShapeDtypeStruct((B, T, N), F32),
        compiler_params=_params("parallel", "parallel"),
    )(x, g.reshape(1, D), shift.reshape(B, 1, D), scale.reshape(B, 1, D), w)


def _proj_res_kernel(a1_ref, a2_ref, w1_ref, w2_ref, x_ref, gate_ref, o_ref):
    acc = jnp.dot(a1_ref[0].astype(BF16), w1_ref[...], preferred_element_type=F32)
    acc += jnp.dot(a2_ref[0].astype(BF16), w2_ref[...], preferred_element_type=F32)
    o_ref[0] = x_ref[0] + gate_ref[0] * acc


def _proj_res(a1, a2, w, x, gate, tm):
    B, T, K = a1.shape
    D = w.shape[1]
    return pl.pallas_call(
        _proj_res_kernel,
        grid=(B, T // tm),
        in_specs=[pl.BlockSpec((1, tm, K), lambda b, i: (b, i, 0)),
                  pl.BlockSpec((1, tm, K), lambda b, i: (b, i, 0)),
                  pl.BlockSpec((K, D), lambda b, i: (0, 0)),
                  pl.BlockSpec((K, D), lambda b, i: (1, 0)),
                  pl.BlockSpec((1, tm, D), lambda b, i: (b, i, 0)),
                  pl.BlockSpec((1, 1, D), lambda b, i: (b, 0, 0))],
        out_specs=pl.BlockSpec((1, tm, D), lambda b, i: (b, i, 0)),
        out_shape=jax.ShapeDtypeStruct((B, T, D), F32),
        compiler_params=_params("parallel", "parallel"),
    )(a1, a2, w, w, x, gate.reshape(B, 1, D))


def _ffn_kernel(x_ref, xp_ref, xn_ref, g_ref, sh_ref, sc_ref, gate_ref, wa_ref, wg_ref,
                ca_ref, cg_ref, wd_ref, o_ref, h_ref, halo_ref, acc_ref):
    i = pl.program_id(1)
    c = pl.program_id(2)
    tm = x_ref.shape[1]

    @pl.when(c == 0)
    def _():
        g, sh, sc = g_ref[...], sh_ref[0], sc_ref[0]
        h_ref[...] = _norm_mod(x_ref[0], g, sh, sc).astype(BF16)
        hp = jnp.where(i > 0, _norm_mod(xp_ref[0], g, sh, sc), 0.0)
        hn = jnp.where(i < pl.num_programs(1) - 1, _norm_mod(xn_ref[0], g, sh, sc), 0.0)
        halo_ref[...] = jnp.concatenate([hp, hn], axis=0).astype(BF16)
        acc_ref[...] = jnp.zeros_like(acc_ref)

    h = h_ref[...]
    halo = halo_ref[...]
    row = lax.broadcasted_iota(jnp.int32, (tm, 1), 0)

    def conv_branch(w_ref, cw_ref):
        w = w_ref[...]
        u = jnp.dot(h, w, preferred_element_type=F32)
        uh = jnp.dot(halo, w, preferred_element_type=F32)
        u_prev = jnp.where(row == 0, uh[SUBLANES - 1:SUBLANES], pltpu.roll(u, 1, axis=0))
        u_next = jnp.where(row == tm - 1, uh[SUBLANES:SUBLANES + 1], pltpu.roll(u, tm - 1, axis=0))
        cw = cw_ref[...]
        return u_prev * cw[0:1] + u * cw[1:2] + u_next * cw[2:3]

    a = conv_branch(wa_ref, ca_ref)
    gt = conv_branch(wg_ref, cg_ref)
    act = a * (gt * jax.nn.sigmoid(gt))
    acc_ref[...] += jnp.dot(act.astype(BF16), wd_ref[...], preferred_element_type=F32)

    @pl.when(c == pl.num_programs(2) - 1)
    def _():
        o_ref[0] = x_ref[0] + gate_ref[0] * acc_ref[...]


def _conv_ffn_res(x, g, shift, scale, gate, w_up, w_conv, w_down, tm, tf):
    B, T, D = x.shape
    F = w_down.shape[0]
    nf = F // tf
    nt = T // tm
    rows8 = tm // SUBLANES
    vec = lambda: pl.BlockSpec((1, 1, D), lambda b, i, c: (b, 0, 0))
    return pl.pallas_call(
        _ffn_kernel,
        grid=(B, nt, nf),
        in_specs=[pl.BlockSpec((1, tm, D), lambda b, i, c: (b, i, 0)),
                  pl.BlockSpec((1, SUBLANES, D), lambda b, i, c: (b, jnp.maximum(i * rows8 - 1, 0), 0)),
                  pl.BlockSpec((1, SUBLANES, D), lambda b, i, c: (b, jnp.minimum((i + 1) * rows8, T // SUBLANES - 1), 0)),
                  pl.BlockSpec((1, D), lambda b, i, c: (0, 0)),
                  vec(), vec(), vec(),
                  pl.BlockSpec((D, tf), lambda b, i, c: (0, c)),
                  pl.BlockSpec((D, tf), lambda b, i, c: (0, nf + c)),
                  pl.BlockSpec((3, tf), lambda b, i, c: (0, c)),
                  pl.BlockSpec((3, tf), lambda b, i, c: (0, nf + c)),
                  pl.BlockSpec((tf, D), lambda b, i, c: (c, 0))],
        out_specs=pl.BlockSpec((1, tm, D), lambda b, i, c: (b, i, 0)),
        out_shape=jax.ShapeDtypeStruct((B, T, D), F32),
        scratch_shapes=[pltpu.VMEM((tm, D), BF16),
                        pltpu.VMEM((2 * SUBLANES, D), BF16),
                        pltpu.VMEM((tm, D), F32)],
        compiler_params=_params("parallel", "parallel", "arbitrary"),
    )(x, x, x, g.reshape(1, D), shift.reshape(B, 1, D), scale.reshape(B, 1, D), gate.reshape(B, 1, D),
      w_up, w_up, w_conv, w_conv, w_down)


def _rmsnorm_kernel(x_ref, g_ref, o_ref):
    x = x_ref[0]
    o_ref[0] = (x * lax.rsqrt(jnp.mean(x * x, axis=-1, keepdims=True) + EPS)) * g_ref[...]


def _rmsnorm_call(x, g, tm):
    B, T, D = x.shape
    return pl.pallas_call(
        _rmsnorm_kernel,
        grid=(B, T // tm),
        in_specs=[pl.BlockSpec((1, tm, D), lambda b, i: (b, i, 0)),
                  pl.BlockSpec((1, D), lambda b, i: (0, 0))],
        out_specs=pl.BlockSpec((1, tm, D), lambda b, i: (b, i, 0)),
        out_shape=jax.ShapeDtypeStruct((B, T, D), F32),
        compiler_params=_params("parallel", "parallel"),
    )(x, g.reshape(1, D))


NEG_BIG = -1e30


def _dot_nt(a, b):
    return lax.dot_general(a, b, (((1,), (1,)), ((), ())), preferred_element_type=F32)


def _lane_group():
    return lax.broadcasted_iota(jnp.int32, (1, LANES), 1) // HEAD_DIM


def _attend(score_fns, values, sink):
    scores = [fn() for fn in score_fns]
    yield
    m = functools.reduce(jnp.maximum, [jnp.max(s, axis=-1, keepdims=True) for s in scores])
    if sink is not None:
        m = jnp.maximum(m, sink)
    den = None
    out = None
    for s, v in zip(scores, values):
        e = jnp.exp(s - m)
        d = jnp.sum(e, axis=-1, keepdims=True)
        o = jnp.dot(e.astype(BF16), v, preferred_element_type=F32)
        den = d if den is None else den + d
        out = o if out is None else out + o
    if sink is not None:
        den = den + jnp.exp(sink - m)
    yield
    return out / den


def _merge_heads(outs, grp):
    return jnp.where(grp == 0, outs[0], outs[1])


def _dense_attn_kernel(q_ref, k_ref, v_ref, sink_ref, o_ref, *, gqa, use_sink):
    grp = _lane_group()
    q = q_ref[0]
    k = k_ref[0]
    v = v_ref[0]
    if gqa:
        kvg = pl.program_id(1) // 2
        k = jnp.where(grp == kvg, k, pltpu.roll(k, HEAD_DIM, axis=1))
        v = jnp.where(grp == kvg, v, pltpu.roll(v, HEAD_DIM, axis=1))
    kb = k.astype(BF16)
    vb = v.astype(BF16)

    def head(h):
        qh = jnp.where(grp == h, q, 0.0).astype(BF16)
        sink = sink_ref[0, h][:, :1] if use_sink else None
        return _attend([lambda: _dot_nt(qh, kb) * (HEAD_DIM ** -0.5)], [vb], sink)

    o_ref[0] = _merge_heads(_lockstep([head(0), head(1)]), grp)


def _dense_attn_call(p, q_blk, k_blk, v_blk, sink, gqa):
    B, T, _ = p.shape
    n_pair = H_A // 2
    use_sink = sink is not None
    if use_sink:
        sink_arr = jnp.broadcast_to(sink.reshape(n_pair, 2, 1, 1), (n_pair, 2, 1, LANES))
    else:
        sink_arr = jnp.zeros((n_pair, 2, 1, LANES), F32)
    kv_idx = (lambda hp: 0) if gqa else (lambda hp: hp)
    return pl.pallas_call(
        functools.partial(_dense_attn_kernel, gqa=gqa, use_sink=use_sink),
        grid=(B, n_pair),
        in_specs=[pl.BlockSpec((1, T, LANES), lambda b, hp: (b, 0, q_blk + hp)),
                  pl.BlockSpec((1, T, LANES), lambda b, hp: (b, 0, k_blk + kv_idx(hp))),
                  pl.BlockSpec((1, T, LANES), lambda b, hp: (b, 0, v_blk + kv_idx(hp))),
                  pl.BlockSpec((1, 2, 1, LANES), lambda b, hp: (hp, 0, 0, 0))],
        out_specs=pl.BlockSpec((1, T, LANES), lambda b, hp: (b, 0, hp)),
        out_shape=jax.ShapeDtypeStruct((B, T, n_pair * LANES), F32),
        compiler_params=_params("parallel", "parallel"),
    )(p, p, p, sink_arr)


NBR_SPAN = WIN_R * GRID_W
ATTN_BLOCKS_PER_STEP = 4


def _nbr_attn_kernel(q_ref, k_ref, v_ref, ck_ref, cv_ref, bias_ref, o_ref, kb_ref, vb_ref):
    kb_ref[...] = k_ref[0].astype(BF16)
    vb_ref[...] = v_ref[0].astype(BF16)
    ckb = ck_ref[0].astype(BF16)
    cvb = cv_ref[0].astype(BF16)
    grp = _lane_group()
    n_rows = q_ref.shape[1] // GRID_W

    def head(i, h):
        r0 = jnp.clip(i - WIN_R // 2, 0, n_rows - WIN_R)
        q = q_ref[0, pl.ds(pl.multiple_of(i * GRID_W, GRID_W), GRID_W), :]
        start = pl.multiple_of(r0 * GRID_W, GRID_W)
        kw = kb_ref[pl.ds(start, NBR_SPAN), :]
        vw = vb_ref[pl.ds(start, NBR_SPAN), :]
        qh = jnp.where(grp == h, q, 0.0).astype(BF16)
        return _attend([lambda: _dot_nt(qh, kw) * (HEAD_DIM ** -0.5) + bias_ref[i - r0, h],
                        lambda: _dot_nt(qh, ckb) * (HEAD_DIM ** -0.5)], [vw, cvb], None)

    def rows(step, carry):
        ids = [step * ATTN_BLOCKS_PER_STEP + j for j in range(ATTN_BLOCKS_PER_STEP)]
        outs = _lockstep([head(i, h) for i in ids for h in range(2)])
        for j, i in enumerate(ids):
            o_ref[0, pl.ds(pl.multiple_of(i * GRID_W, GRID_W), GRID_W), :] = _merge_heads(outs[2 * j:2 * j + 2], grp)
        return carry

    lax.fori_loop(0, n_rows // ATTN_BLOCKS_PER_STEP, rows, 0)


def _nbr_bias_table(rpb):
    var = np.arange(WIN_R)[:, None, None]
    r = np.arange(WIN_R)[None, :, None]
    row_sel = (r - var + (WIN_R - 1) == np.arange(2 * WIN_R - 1)[None, None, :]).astype(np.float32)
    w = np.arange(GRID_W)[:, None, None]
    cc = np.arange(GRID_W)[None, :, None]
    c0 = np.clip(w - WIN_C // 2, 0, GRID_W - WIN_C)
    valid = (cc >= c0) & (cc < c0 + WIN_C)
    col_sel = (valid & (cc - w + (WIN_C - 1) == np.arange(2 * WIN_C - 1)[None, None, :])).astype(np.float32)
    t = jnp.einsum('vra,hab->vhrb', row_sel, rpb, precision=lax.Precision.HIGHEST)
    tbl = jnp.einsum('vhrb,wcb->vhwrc', t, col_sel, precision=lax.Precision.HIGHEST)
    tbl = jnp.where(valid[None, None, :, None, :, 0], tbl, NEG_BIG)
    return tbl.reshape(WIN_R, rpb.shape[0], GRID_W, NBR_SPAN)


def _nbr_attn_call(p, rpb, ck_cache, cv_cache):
    B, T, _ = p.shape
    n_pair = H_A // 2
    L = ck_cache.shape[2]
    ckp = ck_cache.transpose(0, 2, 1, 3).reshape(B, L, H_A * HEAD_DIM)
    cvp = cv_cache.transpose(0, 2, 1, 3).reshape(B, L, H_A * HEAD_DIM)
    bias = _nbr_bias_table(rpb)
    col = lambda off: pl.BlockSpec((1, T, LANES), lambda b, hp: (b, 0, off + hp))
    ctx = lambda: pl.BlockSpec((1, L, LANES), lambda b, hp: (b, 0, hp))
    return pl.pallas_call(
        _nbr_attn_kernel,
        grid=(B, n_pair),
        in_specs=[col(0), col(n_pair), col(2 * n_pair), ctx(), ctx(),
                  pl.BlockSpec((WIN_R, 2, GRID_W, NBR_SPAN), lambda b, hp: (0, hp, 0, 0))],
        out_specs=pl.BlockSpec((1, T, LANES), lambda b, hp: (b, 0, hp)),
        out_shape=jax.ShapeDtypeStruct((B, T, n_pair * LANES), F32),
        scratch_shapes=[pltpu.VMEM((T, LANES), BF16), pltpu.VMEM((T, LANES), BF16)],
        compiler_params=_params("parallel", "parallel"),
    )(p, p, p, ckp, cvp, bias)


WIN_SPAN = 3 * Q_BLOCK
ROPE_SWAP = HEAD_DIM // 4


def _rope(x, cos, sin_signed):
    lane = lax.broadcasted_iota(jnp.int32, (1, LANES), 1)
    first = lane % (2 * ROPE_SWAP) < ROPE_SWAP
    partner = jnp.where(first, pltpu.roll(x, LANES - ROPE_SWAP, axis=1), pltpu.roll(x, ROPE_SWAP, axis=1))
    return x * cos + partner * sin_signed


def _win_attn_kernel(q_ref, k_ref, v_ref, ck_ref, cv_ref, cos_ref, sin_ref, sink_ref, o_ref, kb_ref, vb_ref):
    grp = _lane_group()
    kvg = pl.program_id(1) // 2
    both = lambda a: jnp.where(grp == kvg, a, pltpu.roll(a, HEAD_DIM, axis=1))
    kb_ref[...] = _rope(both(k_ref[0]), cos_ref[...], sin_ref[...]).astype(BF16)
    vb_ref[...] = both(v_ref[0]).astype(BF16)
    ckb = both(ck_ref[0]).astype(BF16)
    cvb = both(cv_ref[0]).astype(BF16)
    T = q_ref.shape[1]
    rel = (lax.broadcasted_iota(jnp.int32, (Q_BLOCK, WIN_SPAN), 1)
           - lax.broadcasted_iota(jnp.int32, (Q_BLOCK, WIN_SPAN), 0))

    def block(n, carry):
        q0 = pl.multiple_of(n * Q_BLOCK, Q_BLOCK)
        start = pl.multiple_of(jnp.clip(q0 - Q_BLOCK, 0, T - WIN_SPAN), Q_BLOCK)
        q = _rope(q_ref[0, pl.ds(q0, Q_BLOCK), :], cos_ref[pl.ds(q0, Q_BLOCK), :], sin_ref[pl.ds(q0, Q_BLOCK), :])
        kw = kb_ref[pl.ds(start, WIN_SPAN), :]
        vw = vb_ref[pl.ds(start, WIN_SPAN), :]
        valid = jnp.abs(rel + (start - q0)) <= WIN_1D

        def head(h):
            qh = jnp.where(grp == h, q, 0.0).astype(BF16)
            return _attend([lambda: jnp.where(valid, _dot_nt(qh, kw) * (HEAD_DIM ** -0.5), NEG_BIG),
                            lambda: _dot_nt(qh, ckb) * (HEAD_DIM ** -0.5)], [vw, cvb], sink_ref[0, h][:, :1])

        o_ref[0, pl.ds(q0, Q_BLOCK), :] = _merge_heads(_lockstep([head(0), head(1)]), grp)
        return carry

    lax.fori_loop(0, T // Q_BLOCK, block, 0)


def _rope_tables(T):
    t = jnp.arange(T)
    quarter = HEAD_DIM // 4
    inv = 1.0 / (ROPE_BASE ** (jnp.arange(quarter, dtype=F32) / quarter))
    ang_r = (t // GRID_W).astype(F32)[:, None] * inv[None, :]
    ang_c = (t % GRID_W).astype(F32)[:, None] * inv[None, :]
    cos = jnp.concatenate([jnp.cos(ang_r), jnp.cos(ang_r), jnp.cos(ang_c), jnp.cos(ang_c)], axis=-1)
    sin = jnp.concatenate([-jnp.sin(ang_r), jnp.sin(ang_r), -jnp.sin(ang_c), jnp.sin(ang_c)], axis=-1)
    return jnp.tile(cos, (1, 2)), jnp.tile(sin, (1, 2))


def _win_attn_call(p, sink, ck_cache, cv_cache):
    B, T, _ = p.shape
    n_pair = H_C // 2
    L = ck_cache.shape[2]
    ckp = ck_cache.transpose(0, 2, 1, 3).reshape(B, L, KV_C * HEAD_DIM)
    cvp = cv_cache.transpose(0, 2, 1, 3).reshape(B, L, KV_C * HEAD_DIM)
    cos, sin = _rope_tables(T)
    sink_arr = jnp.broadcast_to(sink.reshape(n_pair, 2, 1, 1), (n_pair, 2, 1, LANES))
    col = lambda blk: pl.BlockSpec((1, T, LANES), lambda b, hp: (b, 0, blk))
    ctx = lambda: pl.BlockSpec((1, L, LANES), lambda b, hp: (b, 0, 0))
    tab = lambda: pl.BlockSpec((T, LANES), lambda b, hp: (0, 0))
    return pl.pallas_call(
        _win_attn_kernel,
        grid=(B, n_pair),
        in_specs=[pl.BlockSpec((1, T, LANES), lambda b, hp: (b, 0, hp)), col(n_pair), col(n_pair + 1),
                  ctx(), ctx(), tab(), tab(),
                  pl.BlockSpec((1, 2, 1, LANES), lambda b, hp: (hp, 0, 0, 0))],
        out_specs=pl.BlockSpec((1, T, LANES), lambda b, hp: (b, 0, hp)),
        out_shape=jax.ShapeDtypeStruct((B, T, n_pair * LANES), F32),
        scratch_shapes=[pltpu.VMEM((T, LANES), BF16), pltpu.VMEM((T, LANES), BF16)],
        compiler_params=_params("parallel", "parallel"),
    )(p, p, p, ckp, cvp, cos, sin, sink_arr)


def _dot_tn(a, b):
    return lax.dot_general(a, b, (((0,), (0,)), ((), ())), preferred_element_type=F32)


def _split_bf16(x):
    hi = x.astype(BF16)
    return hi, (x - hi.astype(F32)).astype(BF16)


GLA_CHUNKS_PER_STEP = 4
DELTA_CHUNKS_PER_STEP = 2


def _lockstep(programs):
    results = [None] * len(programs)
    live = list(range(len(programs)))
    while live:
        for idx in list(live):
            try:
                next(programs[idx])
            except StopIteration as stop:
                results[idx] = stop.value
                live.remove(idx)
    return results


def _block_diag_mask():
    return (lax.broadcasted_iota(jnp.int32, (LANES, 1), 0) // HEAD_DIM) == _lane_group()


def _head_rmsnorm(o, grp):
    sq = o * o
    s0 = jnp.sum(jnp.where(grp == 0, sq, 0.0), axis=-1, keepdims=True)
    s1 = jnp.sum(jnp.where(grp == 1, sq, 0.0), axis=-1, keepdims=True)
    ms = jnp.where(grp == 0, s0, s1) * (1.0 / HEAD_DIM)
    return o * lax.rsqrt(ms + EPS)


def _gla_kernel(q_ref, k_ref, v_ref, r_ref, glr_ref, w2_ref, bg_ref, ng_ref, s0_ref, o_ref, sfin_ref,
                ob_ref, st_ref):
    T = q_ref.shape[1]
    n = T // CHUNK
    grp = _lane_group()
    bd = _block_diag_mask()
    ri = lax.broadcasted_iota(jnp.int32, (CHUNK, CHUNK), 0)
    ci = lax.broadcasted_iota(jnp.int32, (CHUNK, CHUNK), 1)
    ones_tok = jnp.ones((CHUNK, LANES), BF16)
    st_ref[...] = s0_ref[0, :, 0]

    def chunk(c0, d):
        sl = pl.ds(c0, CHUNK)
        q = q_ref[0, sl, :] * (DK_B ** -0.5)
        k = k_ref[0, sl, :]
        vb = v_ref[0, sl, :].astype(BF16)
        z = jnp.dot(glr_ref[0, sl, :].astype(BF16), w2_ref[d], preferred_element_type=F32) + bg_ref[d]
        yield
        g = (jnp.minimum(z, 0.0) - jnp.log(1.0 + jnp.exp(-jnp.abs(z)))) * (1.0 / GLA_TAU)
        g_hi, g_lo = _split_bf16(g)
        mask = (ri <= ci) if d else (ri >= ci)
        tri = jnp.where(mask, 1.0, 0.0).astype(BF16)
        G = jnp.dot(tri, g_hi, preferred_element_type=F32) + jnp.dot(tri, g_lo, preferred_element_type=F32)
        g_tot_rows = _dot_tn(g_hi, ones_tok) + _dot_tn(g_lo, ones_tok)
        yield
        g_tot = G[0:1] if d else G[CHUNK - 1:CHUNK]
        q_t = q * jnp.exp(G)
        k_t = (k * jnp.exp(-G)).astype(BF16)
        k_e = (k * jnp.exp(g_tot - G)).astype(BF16)
        S = st_ref[d]
        att = [jnp.where(mask, _dot_nt(jnp.where(grp == h, q_t, 0.0).astype(BF16), k_t), 0.0) for h in range(2)]
        o = jnp.dot(q_t.astype(BF16), S.astype(BF16), preferred_element_type=F32)
        st_ref[d] = jnp.exp(g_tot_rows) * S + jnp.where(bd, _dot_tn(k_e, vb), 0.0)
        yield
        for h in range(2):
            o = o + jnp.where(grp == h, jnp.dot(att[h].astype(BF16), vb, preferred_element_type=F32), 0.0)
        return o

    def body(i, carry):
        c_f = [pl.multiple_of((i * GLA_CHUNKS_PER_STEP + j) * CHUNK, CHUNK) for j in range(GLA_CHUNKS_PER_STEP)]
        c_b = [pl.multiple_of((n - 1 - i * GLA_CHUNKS_PER_STEP - j) * CHUNK, CHUNK)
               for j in range(GLA_CHUNKS_PER_STEP)]
        outs = _lockstep([chunk(c0, 0) for c0 in c_f] + [chunk(c0, 1) for c0 in c_b])
        for j in range(GLA_CHUNKS_PER_STEP):
            o_ref[0, pl.ds(c_f[j], CHUNK), :] = outs[j]
            ob_ref[pl.ds(c_b[j], CHUNK), :] = outs[GLA_CHUNKS_PER_STEP + j]
        return carry

    lax.fori_loop(0, n // GLA_CHUNKS_PER_STEP, body, 0)
    sfin_ref[0, :, 0] = st_ref[...]

    def gate(i, carry):
        sl = pl.ds(pl.multiple_of(i * CHUNK, CHUNK), CHUNK)
        r = r_ref[0, sl, :]
        y = _head_rmsnorm(o_ref[0, sl, :] + ob_ref[sl, :], grp) * ng_ref[...]
        o_ref[0, sl, :] = y * (r * jax.nn.sigmoid(r))
        return carry

    lax.fori_loop(0, n, gate, 0)


def _pair_block_diag(s):
    B, _, H, dk, dv = s.shape
    s = s.reshape(B, 2, H // 2, 2, dk, dv)
    z = jnp.zeros_like(s[:, :, :, 0])
    top = jnp.concatenate([s[:, :, :, 0], z], axis=-1)
    bot = jnp.concatenate([z, s[:, :, :, 1]], axis=-1)
    return jnp.concatenate([top, bot], axis=-2)


def _pair_block_diag_inv(sp):
    a = sp[..., :HEAD_DIM, :HEAD_DIM]
    b = sp[..., HEAD_DIM:, HEAD_DIM:]
    B, _, P = sp.shape[:3]
    return jnp.stack([a, b], axis=3).reshape(B, 2, 2 * P, HEAD_DIM, HEAD_DIM)


EV_BQ, EV_BK, EV_BV, EV_BR, EV_GLR = 12, 16, 20, 24, 28


def _gla_call(p, w_g2, b_g, norm_g, s0):
    B, T, _ = p.shape
    n_pair = H_B // 2
    w2 = jnp.zeros((2, LANES, H_B * DK_B), F32)
    w2 = w2.at[0, :GLA_RANK].set(w_g2[0]).at[1, GLA_RANK:2 * GLA_RANK].set(w_g2[1]).astype(BF16)
    col = lambda off: pl.BlockSpec((1, T, LANES), lambda b, hp: (b, 0, off + hp))
    st = lambda: pl.BlockSpec((1, 2, 1, LANES, LANES), lambda b, hp: (b, 0, hp, 0, 0))
    o, s_fin = pl.pallas_call(
        _gla_kernel,
        grid=(B, n_pair),
        in_specs=[col(EV_BQ), col(EV_BK), col(EV_BV), col(EV_BR),
                  pl.BlockSpec((1, T, LANES), lambda b, hp: (b, 0, EV_GLR)),
                  pl.BlockSpec((2, LANES, LANES), lambda b, hp: (0, 0, hp)),
                  pl.BlockSpec((2, 1, LANES), lambda b, hp: (0, 0, hp)),
                  pl.BlockSpec((1, LANES), lambda b, hp: (0, hp)),
                  st()],
        out_specs=[pl.BlockSpec((1, T, LANES), lambda b, hp: (b, 0, hp)), st()],
        out_shape=[jax.ShapeDtypeStruct((B, T, n_pair * LANES), F32),
                   jax.ShapeDtypeStruct((B, 2, n_pair, LANES, LANES), F32)],
        scratch_shapes=[pltpu.VMEM((T, LANES), F32), pltpu.VMEM((2, LANES, LANES), F32)],
        compiler_params=_params("parallel", "parallel"),
    )(p, p, p, p, p, w2, b_g.reshape(2, 1, H_B * DK_B), norm_g.reshape(1, H_B * DV_B), _pair_block_diag(s0))
    return o, _pair_block_diag_inv(s_fin)


SOLVE_PASSES = 3


def _mm(a, b, passes):
    if passes == 1:
        return jnp.dot(a.astype(BF16), b.astype(BF16), preferred_element_type=F32)
    a_hi, a_lo = _split_bf16(a)
    b_hi, b_lo = _split_bf16(b)
    return (jnp.dot(a_hi, b_hi, preferred_element_type=F32) + jnp.dot(a_lo, b_hi, preferred_element_type=F32)
            + jnp.dot(a_hi, b_lo, preferred_element_type=F32))


def _unit_tri_inverse(a, eye):
    b = -a
    p = eye + b
    for _ in range(5):
        b = _mm(b, b, SOLVE_PASSES)
        p = p + _mm(p, b, SOLVE_PASSES)
    return p


def _softplus(x):
    return jnp.maximum(x, 0.0) + jnp.log(1.0 + jnp.exp(-jnp.abs(x)))


def _delta_kernel(q_ref, k_ref, v_ref, z_ref, dab_ref, cq_ref, ck_ref, cv_ref, nega_ref, dtb_ref, ng_ref, s0_ref,
                  o_ref, sfin_ref, qs_ref, ks_ref, vs_ref, ob_ref, st_ref):
    T = q_ref.shape[1]
    n = T // CHUNK
    hp = pl.program_id(1)
    grp = _lane_group()
    lane = lax.broadcasted_iota(jnp.int32, (1, LANES), 1)
    subl = lax.broadcasted_iota(jnp.int32, (LANES, 1), 0)
    bd = _block_diag_mask()
    ri = lax.broadcasted_iota(jnp.int32, (CHUNK, CHUNK), 0)
    ci = lax.broadcasted_iota(jnp.int32, (CHUNK, CHUNK), 1)
    eye = jnp.where(ri == ci, 1.0, 0.0)
    row = lax.broadcasted_iota(jnp.int32, (CHUNK, 1), 0)
    st_ref[...] = s0_ref[0, :, 0]

    def prep(i, carry):
        c0 = pl.multiple_of(i * CHUNK, CHUNK)
        lo = pl.multiple_of(jnp.maximum(c0 - SUBLANES, 0), SUBLANES)
        hi = pl.multiple_of(jnp.minimum(c0 + CHUNK, T - SUBLANES), SUBLANES)
        for src, cw_ref, dst, scale in ((q_ref, cq_ref, qs_ref, DK_D ** -0.5), (k_ref, ck_ref, ks_ref, 1.0),
                                        (v_ref, cv_ref, vs_ref, None)):
            x = src[0, pl.ds(c0, CHUNK), :]
            before = jnp.where(i > 0, src[0, pl.ds(lo, SUBLANES), :][SUBLANES - 1:], 0.0)
            after = jnp.where(i < n - 1, src[0, pl.ds(hi, SUBLANES), :][:1], 0.0)
            x_prev = jnp.where(row == 0, before, pltpu.roll(x, 1, axis=0))
            x_next = jnp.where(row == CHUNK - 1, after, pltpu.roll(x, CHUNK - 1, axis=0))
            cw = cw_ref[...]
            y = x_prev * cw[0:1] + x * cw[1:2] + x_next * cw[2:3]
            y = y * jax.nn.sigmoid(y)
            if scale is not None:
                sq = y * y
                s0 = jnp.sum(jnp.where(grp == 0, sq, 0.0), axis=-1, keepdims=True)
                s1 = jnp.sum(jnp.where(grp == 1, sq, 0.0), axis=-1, keepdims=True)
                y = y * lax.rsqrt(jnp.where(grp == 0, s0, s1) + EPS) * scale
            dst[pl.ds(c0, CHUNK), :] = y
        return carry

    lax.fori_loop(0, n, prep, 0)

    def chunk(c0, d):
        sl = pl.ds(c0, CHUNK)
        q = qs_ref[sl, :]
        k = ks_ref[sl, :]
        v = vs_ref[sl, :]
        ab = dab_ref[0, sl, :]
        gb = jnp.where(lane < 2 * H_D, nega_ref[...] * _softplus(ab + dtb_ref[...]), jax.nn.sigmoid(ab))
        gb_t = gb.T
        incl = (ri <= ci) if d else (ri >= ci)
        incl_t = (ri >= ci) if d else (ri <= ci)
        strict = (ri < ci) if d else (ri > ci)

        def head(h):
            jg = d * H_D + 2 * hp + h
            g_c = jnp.sum(jnp.where(lane == jg, gb, 0.0), axis=1, keepdims=True)
            beta_c = jnp.sum(jnp.where(lane == 2 * H_D + jg, gb, 0.0), axis=1, keepdims=True)
            g_r = jnp.sum(jnp.where(subl == jg, gb_t, 0.0), axis=0, keepdims=True)
            G_c = jnp.sum(jnp.where(incl, g_r, 0.0), axis=1, keepdims=True)
            G_r = jnp.sum(jnp.where(incl_t, g_c, 0.0), axis=0, keepdims=True)
            tot = jnp.sum(g_r, axis=1, keepdims=True)
            gam = jnp.where(incl, jnp.exp(jnp.where(incl, G_c - G_r, 0.0)), 0.0)
            kh = jnp.where(grp == h, k, 0.0)
            qh = jnp.where(grp == h, q, 0.0)
            vh = jnp.where(grp == h, v, 0.0)
            kb = kh * beta_c
            khb = kh.astype(BF16)
            a = jnp.where(strict, _dot_nt(kb.astype(BF16), khb) * gam, 0.0)
            a_qk = (_dot_nt(qh.astype(BF16), khb) * gam).astype(BF16)
            yield
            b = -a
            p = eye + b
            for _ in range(5):
                b = _mm(b, b, SOLVE_PASSES)
                yield
                p = p + _mm(p, b, SOLVE_PASSES)
            yield
            e_g = jnp.exp(G_c)
            w_val = _mm(p, vh * beta_c, SOLVE_PASSES)
            k_cum = _mm(p, kb * e_g, SOLVE_PASSES)
            return w_val, k_cum, a_qk, qh * e_g, kh * jnp.exp(tot - G_c), jnp.exp(tot)

        return head

    def scan_step(d, heads):
        (w0, kc0, aqk0, qg0, ke0, d0), (w1, kc1, aqk1, qg1, ke1, d1) = heads
        S = st_ref[d]
        Sb = S.astype(BF16)
        v_new = (w0 + w1) - jnp.dot((kc0 + kc1).astype(BF16), Sb, preferred_element_type=F32)
        o = jnp.dot((qg0 + qg1).astype(BF16), Sb, preferred_element_type=F32)
        yield
        vnb = v_new.astype(BF16)
        o = o + jnp.where(grp == 0, jnp.dot(aqk0, vnb, preferred_element_type=F32), 0.0)
        o = o + jnp.where(grp == 1, jnp.dot(aqk1, vnb, preferred_element_type=F32), 0.0)
        d_rows = jnp.where(subl // HEAD_DIM == 0, d0, d1)
        st_ref[d] = S * d_rows + jnp.where(bd, _dot_tn((ke0 + ke1).astype(BF16), vnb), 0.0)
        return o

    def scan_dir(d, heads):
        outs = []
        for j in range(DELTA_CHUNKS_PER_STEP):
            o = yield from scan_step(d, heads[2 * j:2 * j + 2])
            outs.append(o)
        return outs

    def body(i, carry):
        c_f = [pl.multiple_of((i * DELTA_CHUNKS_PER_STEP + j) * CHUNK, CHUNK) for j in range(DELTA_CHUNKS_PER_STEP)]
        c_b = [pl.multiple_of((n - 1 - i * DELTA_CHUNKS_PER_STEP - j) * CHUNK, CHUNK)
               for j in range(DELTA_CHUNKS_PER_STEP)]
        programs = []
        for d, starts in ((0, c_f), (1, c_b)):
            for c0 in starts:
                head = chunk(c0, d)
                programs += [head(0), head(1)]
        heads = _lockstep(programs)
        half = 2 * DELTA_CHUNKS_PER_STEP
        o_f, o_b = _lockstep([scan_dir(0, heads[:half]), scan_dir(1, heads[half:])])
        for j in range(DELTA_CHUNKS_PER_STEP):
            o_ref[0, pl.ds(c_f[j], CHUNK), :] = o_f[j]
            ob_ref[pl.ds(c_b[j], CHUNK), :] = o_b[j]
        return carry

    lax.fori_loop(0, n // DELTA_CHUNKS_PER_STEP, body, 0)
    sfin_ref[0, :, 0] = st_ref[...]

    def gate(i, carry):
        sl = pl.ds(pl.multiple_of(i * CHUNK, CHUNK), CHUNK)
        z = z_ref[0, sl, :]
        y = _head_rmsnorm(o_ref[0, sl, :] + ob_ref[sl, :], grp) * ng_ref[...]
        o_ref[0, sl, :] = y * (z * jax.nn.sigmoid(z))
        return carry

    lax.fori_loop(0, n, gate, 0)


OD_DQ, OD_DK, OD_DV, OD_DZ, OD_DAB = 6, 10, 14, 18, 22


def _delta_call(p, w_conv, a_log, dt_bias, norm_g, s0):
    B, T, _ = p.shape
    n_pair = H_D // 2
    nega = jnp.zeros((1, LANES), F32).at[0, :2 * H_D].set(-jnp.exp(a_log.reshape(-1)))
    dtb = jnp.zeros((1, LANES), F32).at[0, :2 * H_D].set(dt_bias.reshape(-1))
    ng = jnp.tile(norm_g, 2).reshape(1, LANES)
    col = lambda off: pl.BlockSpec((1, T, LANES), lambda b, hp: (b, 0, off + hp))
    cw = lambda off: pl.BlockSpec((SHORT_CONV, LANES), lambda b, hp: (0, off + hp))
    vec = lambda: pl.BlockSpec((1, LANES), lambda b, hp: (0, 0))
    st = lambda: pl.BlockSpec((1, 2, 1, LANES, LANES), lambda b, hp: (b, 0, hp, 0, 0))
    o, s_fin = pl.pallas_call(
        _delta_kernel,
        grid=(B, n_pair),
        in_specs=[col(OD_DQ), col(OD_DK), col(OD_DV), col(OD_DZ),
                  pl.BlockSpec((1, T, LANES), lambda b, hp: (b, 0, OD_DAB)),
                  cw(0), cw(n_pair), cw(2 * n_pair), vec(), vec(), vec(), st()],
        out_specs=[pl.BlockSpec((1, T, LANES), lambda b, hp: (b, 0, hp)), st()],
        out_shape=[jax.ShapeDtypeStruct((B, T, n_pair * LANES), F32),
                   jax.ShapeDtypeStruct((B, 2, n_pair, LANES, LANES), F32)],
        scratch_shapes=[pltpu.VMEM((T, LANES), F32), pltpu.VMEM((T, LANES), F32), pltpu.VMEM((T, LANES), F32),
                        pltpu.VMEM((T, LANES), F32), pltpu.VMEM((2, LANES, LANES), F32)],
        compiler_params=_params("parallel", "parallel"),
    )(p, p, p, p, p, w_conv, w_conv, w_conv, nega, dtb, ng, _pair_block_diag(s0))
    return o, _pair_block_diag_inv(s_fin)


def _rmsnorm(x, g):
    y = x * lax.rsqrt(jnp.mean(x * x, axis=-1, keepdims=True) + EPS)
    return y * g


def _l2norm(x):
    return x * lax.rsqrt(jnp.sum(x * x, axis=-1, keepdims=True) + EPS)


def _split_cols(p, sizes):
    cuts = [int(s) for s in np.cumsum(sizes)[:-1]]
    return jnp.split(p, cuts, axis=-1)


def _dwconv(x, w):
    K = w.shape[0]
    T = x.shape[1]
    pad = K // 2
    xp = jnp.pad(x, ((0, 0), (pad, pad), (0, 0)))
    out = xp[:, 0:T] * w[0]
    for i in range(1, K):
        out = out + xp[:, i:i + T] * w[i]
    return out


def _softmax_sink(s, sink):
    if sink is None:
        return jax.nn.softmax(s, axis=-1)
    m = jnp.maximum(jnp.max(s, axis=-1, keepdims=True), sink)
    e = jnp.exp(s - m)
    return e / (jnp.sum(e, axis=-1, keepdims=True) + jnp.exp(sink - m))


def _axial_rope(x):
    T = x.shape[1]
    t = jnp.arange(T)
    half = HEAD_DIM // 2
    quarter = half // 2
    inv = 1.0 / (ROPE_BASE ** (jnp.arange(quarter, dtype=F32) / quarter))

    def rot(xa, pos):
        ang = pos.astype(F32)[:, None] * inv[None, :]
        cos = jnp.cos(ang)[None, :, None, :]
        sin = jnp.sin(ang)[None, :, None, :]
        x1, x2 = xa[..., :quarter], xa[..., quarter:]
        return jnp.concatenate([x1 * cos - x2 * sin, x1 * sin + x2 * cos], axis=-1)

    return jnp.concatenate([rot(x[..., :half], t // GRID_W), rot(x[..., half:], t % GRID_W)], axis=-1)


def _dense_attn(q, k, v, sink):
    B, Tq, Hq, Dh = q.shape
    Hk = k.shape[1]
    G = Hq // Hk
    nb = Tq // Q_BLOCK
    qb = q.reshape(B, nb, Q_BLOCK, Hk, G, Dh).swapaxes(0, 1)
    sk = None if sink is None else sink.reshape(Hk, G, 1, 1)
    scale = Dh ** -0.5

    def one(qblk):
        s = jnp.einsum('bqkgd,bksd->bkgqs', qblk, k) * scale
        p = _softmax_sink(s, sk)
        return jnp.einsum('bkgqs,bksd->bqkgd', p, v)

    o = lax.map(one, qb)
    return o.swapaxes(0, 1).reshape(B, Tq, Hq, Dh)


def _neighborhood_attn(q, k, v, rpb, ck, cv):
    B, T, H, Dh = q.shape
    R = T // GRID_W
    kr = min(WIN_R, R)
    kc = WIN_C
    qg = q.reshape(B, R, GRID_W, H, Dh)
    kg = k.reshape(B, R, GRID_W, H, Dh)
    vg = v.reshape(B, R, GRID_W, H, Dh)
    col = jnp.arange(GRID_W)
    col_idx = jnp.clip(col - kc // 2, 0, GRID_W - kc)[:, None] + jnp.arange(kc)[None, :]
    dc = col_idx - col[:, None] + (WIN_C - 1)
    scale = Dh ** -0.5

    def one(i):
        r0 = jnp.clip(i - kr // 2, 0, R - kr)
        kb = lax.dynamic_slice_in_dim(kg, r0, kr, axis=1)[:, :, col_idx]
        vb = lax.dynamic_slice_in_dim(vg, r0, kr, axis=1)[:, :, col_idx]
        qi = lax.dynamic_index_in_dim(qg, i, axis=1, keepdims=False)
        dr = r0 + jnp.arange(kr) - i + (WIN_R - 1)
        bias = rpb[:, dr[None, :, None], dc[:, None, :]]
        s_loc = jnp.einsum('bwhd,brwchd->bhwrc', qi, kb) * scale + bias
        s_loc = s_loc.reshape(B, H, GRID_W, kr * kc)
        s_ctx = jnp.einsum('bwhd,bhld->bhwl', qi, ck) * scale
        p = jax.nn.softmax(jnp.concatenate([s_loc, s_ctx], axis=-1), axis=-1)
        p_loc = p[..., :kr * kc].reshape(B, H, GRID_W, kr, kc)
        return (jnp.einsum('bhwrc,brwchd->bwhd', p_loc, vb)
                + jnp.einsum('bhwl,bhld->bwhd', p[..., kr * kc:], cv))

    o = lax.map(one, jnp.arange(R))
    return o.swapaxes(0, 1).reshape(B, T, H, Dh)


def _window_attn(q, k, v, sink, ck, cv):
    B, T, Hq, Dh = q.shape
    Hk = k.shape[2]
    G = Hq // Hk
    blk = Q_BLOCK
    nb = T // blk
    pad = ((0, 0), (blk, blk), (0, 0), (0, 0))
    kp = jnp.pad(k, pad)
    vp = jnp.pad(v, pad)
    sk = sink.reshape(1, Hk, G, 1, 1)
    scale = Dh ** -0.5

    def one(n):
        q0 = n * blk
        qn = lax.dynamic_slice_in_dim(q, q0, blk, axis=1).reshape(B, blk, Hk, G, Dh)
        kn = lax.dynamic_slice_in_dim(kp, q0, 3 * blk, axis=1)
        vn = lax.dynamic_slice_in_dim(vp, q0, 3 * blk, axis=1)
        qpos = q0 + jnp.arange(blk)
        kpos = q0 - blk + jnp.arange(3 * blk)
        valid = ((jnp.abs(kpos[None, :] - qpos[:, None]) <= WIN_1D)
                 & (kpos >= 0)[None, :] & (kpos < T)[None, :])
        s_loc = jnp.einsum('bqkgd,bskd->bkgqs', qn, kn) * scale
        s_loc = jnp.where(valid, s_loc, -jnp.inf)
        s_ctx = jnp.einsum('bqkgd,bksd->bkgqs', qn, ck) * scale
        p = _softmax_sink(jnp.concatenate([s_loc, s_ctx], axis=-1), sk)
        o = (jnp.einsum('bkgqs,bskd->bqkgd', p[..., :3 * blk], vn)
             + jnp.einsum('bkgqs,bksd->bqkgd', p[..., 3 * blk:], cv))
        return o.reshape(B, blk, Hq, Dh)

    o = lax.map(one, jnp.arange(nb))
    return o.swapaxes(0, 1).reshape(B, T, Hq, Dh)


def _gla_scan(q, k, v, g, s0):
    B, T, H, dk = q.shape
    dv = v.shape[-1]
    n = T // CHUNK
    q, k, v, g = [a.reshape(B, n, CHUNK, H, a.shape[-1]) for a in (q, k, v, g)]
    G = jnp.cumsum(g, axis=2)
    G_last = G[:, :, -1:]
    q_t = q * jnp.exp(G)
    k_t = k * jnp.exp(-G)
    k_end = k * jnp.exp(G_last - G)
    causal = jnp.tril(jnp.ones((CHUNK, CHUNK), bool))
    att = jnp.where(causal, jnp.einsum('bnchd,bnshd->bnhcs', q_t, k_t), 0.0)
    o_intra = jnp.einsum('bnhcs,bnshv->bnchv', att, v)
    u = jnp.einsum('bnshd,bnshv->bnhdv', k_end, v)
    decay = jnp.exp(G_last[:, :, 0])

    def step(S, inp):
        d, du = inp
        return d[..., None] * S + du, S

    S_fin, S_start = lax.scan(step, s0, (decay.swapaxes(0, 1), u.swapaxes(0, 1)))
    o_inter = jnp.einsum('bnchd,bnhdv->bnchv', q_t, S_start.swapaxes(0, 1))
    return (o_intra + o_inter).reshape(B, T, H, dv), S_fin


def _delta_scan(q, k, v, beta, g, s0):
    B, T, H, dk = q.shape
    dv = v.shape[-1]
    n = T // CHUNK
    blk = lambda a: a.reshape((B, n, CHUNK) + a.shape[2:]).swapaxes(2, 3)
    q, k, v, beta, g = [blk(a) for a in (q, k, v, beta, g)]
    G = jnp.cumsum(g, axis=-1)
    lower = jnp.tril(jnp.ones((CHUNK, CHUNK), bool))
    strict = jnp.tril(jnp.ones((CHUNK, CHUNK), bool), -1)
    diff = G[..., :, None] - G[..., None, :]
    gam = jnp.where(lower, jnp.exp(jnp.where(lower, diff, 0.0)), 0.0)
    k_beta = k * beta[..., None]
    a_mat = jnp.where(strict, jnp.einsum('bnhcd,bnhsd->bnhcs', k_beta, k) * gam, 0.0)
    m_mat = a_mat + jnp.eye(CHUNK, dtype=F32)
    rhs = jnp.concatenate([v * beta[..., None], k_beta * jnp.exp(G)[..., None]], axis=-1)
    sol = lax.linalg.triangular_solve(m_mat, rhs, left_side=True, lower=True, unit_diagonal=True)
    w_val, k_cum = sol[..., :dv], sol[..., dv:]
    a_qk = jnp.einsum('bnhcd,bnhsd->bnhcs', q, k) * gam
    q_g = q * jnp.exp(G)[..., None]
    k_end = k * jnp.exp(G[..., -1:] - G)[..., None]
    d_last = jnp.exp(G[..., -1])

    def step(S, inp):
        aqk, wv, kc, qg, ke, d = inp
        v_new = wv - jnp.einsum('bhcd,bhdv->bhcv', kc, S)
        o = jnp.einsum('bhcd,bhdv->bhcv', qg, S) + jnp.einsum('bhcs,bhsv->bhcv', aqk, v_new)
        S = S * d[..., None, None] + jnp.einsum('bhcd,bhcv->bhdv', ke, v_new)
        return S, o

    xs = tuple(a.swapaxes(0, 1) for a in (a_qk, w_val, k_cum, q_g, k_end, d_last))
    S_fin, o = lax.scan(step, s0, xs)
    return o.transpose(1, 0, 3, 2, 4).reshape(B, T, H, dv), S_fin


def _gla_mixer(bq, bk, bv, glr, br, w_g2, b_g, norm_g, s0):
    B, T, _ = bq.shape
    q = bq.reshape(B, T, H_B, DK_B) * DK_B ** -0.5
    k = bk.reshape(B, T, H_B, DK_B)
    v = bv.reshape(B, T, H_B, DV_B)
    z = jnp.einsum('btzr,zrc->btzc', glr.reshape(B, T, 2, GLA_RANK), w_g2) + b_g
    g = (jax.nn.log_sigmoid(z) / GLA_TAU).reshape(B, T, 2, H_B, DK_B)
    fl = lambda a: jnp.flip(a, 1)
    o_f, s_f = _gla_scan(q, k, v, g[:, :, 0], s0[:, 0])
    o_b, s_b = _gla_scan(fl(q), fl(k), fl(v), fl(g[:, :, 1]), s0[:, 1])
    o = o_f + fl(o_b)
    o = _rmsnorm(o, norm_g.reshape(H_B, DV_B)) * jax.nn.silu(br.reshape(B, T, H_B, DV_B))
    return o.reshape(B, T, H_B * DV_B), jnp.stack([s_f, s_b], axis=1)


def _delta_mixer(dq, dk, dv, da, db, dz, w_conv, a_log, dt_bias, norm_g, s0):
    B, T, _ = dq.shape
    qkv = jax.nn.silu(_dwconv(jnp.concatenate([dq, dk, dv], axis=-1), w_conv))
    q, k, v = _split_cols(qkv, (H_D * DK_D, H_D * DK_D, H_D * DV_D))
    q = _l2norm(q.reshape(B, T, H_D, DK_D)) * DK_D ** -0.5
    k = _l2norm(k.reshape(B, T, H_D, DK_D))
    v = v.reshape(B, T, H_D, DV_D)
    beta = jax.nn.sigmoid(db.reshape(B, T, 2, H_D))
    g = -jnp.exp(a_log) * jax.nn.softplus(da.reshape(B, T, 2, H_D) + dt_bias)
    fl = lambda a: jnp.flip(a, 1)
    o_f, s_f = _delta_scan(q, k, v, beta[:, :, 0], g[:, :, 0], s0[:, 0])
    o_b, s_b = _delta_scan(fl(q), fl(k), fl(v), fl(beta[:, :, 1]), fl(g[:, :, 1]), s0[:, 1])
    o = o_f + fl(o_b)
    o = _rmsnorm(o, norm_g) * jax.nn.silu(dz.reshape(B, T, H_D, DV_D))
    return o.reshape(B, T, H_D * DV_D), jnp.stack([s_f, s_b], axis=1)


def _even_mix(p, latent, rpb, w_g2, b_g, norm_g, ak_cache, av_cache, sb):
    B, T, _ = p.shape
    ak = p[..., H_A * HEAD_DIM:2 * H_A * HEAD_DIM]
    av = p[..., 2 * H_A * HEAD_DIM:3 * H_A * HEAD_DIM]
    n_pair = H_A // 2
    if latent:
        o_a = _nbr_attn_call(p, rpb, ak_cache, av_cache)
        k_h = v_h = None
    else:
        k_h = ak.reshape(B, T, H_A, HEAD_DIM).transpose(0, 2, 1, 3)
        v_h = av.reshape(B, T, H_A, HEAD_DIM).transpose(0, 2, 1, 3)
        o_a = _dense_attn_call(p, 0, n_pair, 2 * n_pair, None, False)
    o_b, s_b = _gla_call(p, w_g2, b_g, norm_g, sb)
    return o_a, o_b, k_h, v_h, s_b


def _odd_mix(p, latent, sink, w_conv, a_log, dt_bias, norm_g, ck_cache, cv_cache, sd):
    B, T, _ = p.shape
    ck = p[..., H_C * HEAD_DIM:(H_C + KV_C) * HEAD_DIM]
    cv = p[..., (H_C + KV_C) * HEAD_DIM:(H_C + 2 * KV_C) * HEAD_DIM]
    n_pair = H_C // 2
    if latent:
        o_c = _win_attn_call(p, sink, ck_cache, cv_cache)
        k_h = v_h = None
    else:
        k_h = ck.reshape(B, T, KV_C, HEAD_DIM).transpose(0, 2, 1, 3)
        v_h = cv.reshape(B, T, KV_C, HEAD_DIM).transpose(0, 2, 1, 3)
        o_c = _dense_attn_call(p, 0, n_pair, n_pair + 1, sink, True)
    o_d, s_d = _delta_call(p, w_conv, a_log, dt_bias, norm_g, sd)
    return o_c, o_d, k_h, v_h, s_d


def _reorder_cols(w, sizes, order):
    parts = _split_cols(w, sizes)
    out = jnp.concatenate([parts[i] for i in order], axis=-1)
    pad = -out.shape[-1] % LANES
    return jnp.pad(out, ((0, 0), (0, pad)))


def _reorder_even_w_in(w):
    return _reorder_cols(w, EV_SIZES, (0, 1, 2, 3, 4, 5, 7, 6))


def _reorder_odd_w_in(w):
    return _reorder_cols(w, OD_SIZES, (0, 1, 2, 3, 4, 5, 8, 6, 7))


OD_ORDER_SIZES = tuple(OD_SIZES[i] for i in (0, 1, 2, 3, 4, 5, 8, 6, 7)) + (-sum(OD_SIZES) % LANES,)


TM_LATENT = 512
TM_CONTEXT = 256
TF_FFN = 1408


def kernel(x_prompt, x_sample, cache_a_k, cache_a_v, state_b, cache_c_k, cache_c_v, state_d, c, c_ctx, ada_w, ada_b, norm1_g, norm2_g, ffn_up, ffn_conv, ffn_down, ev_w_in, ev_w_out, a_rpb, b_w_g2, b_b_g, b_norm_g, od_w_in, od_w_out, c_sink, d_conv, d_a_log, d_dt_bias, d_norm_g, final_g):
    Bp, Tp, D = x_prompt.shape
    Bs, Ts, _ = x_sample.shape
    cond = jnp.concatenate([c, c_ctx[None, :], jnp.zeros((2 * SUBLANES - Bs - 1, D), F32)], axis=0)
    mods = _adaln(cond, ada_w, ada_b)
    xp, xs = x_prompt, x_sample
    streams = ((False, Bp, TM_CONTEXT), (True, Bs, TM_LATENT))
    ak_l, av_l, sb_l, ck_l, cv_l, sd_l = [], [], [], [], [], []
    for l in range(DEPTH):
        j = l // 2
        mod_s = [mods[l, :Bs, i * D:(i + 1) * D] for i in range(6)]
        mod_p = [jnp.broadcast_to(mods[l, Bs:Bs + 1, i * D:(i + 1) * D], (Bp, D)) for i in range(6)]
        w_in = (_reorder_even_w_in(ev_w_in[j]) if l % 2 == 0 else _reorder_odd_w_in(od_w_in[j])).astype(BF16)
        w_out = (ev_w_out[j] if l % 2 == 0 else od_w_out[j]).astype(BF16)
        w_up = ffn_up[l].astype(BF16)
        w_down = ffn_down[l].astype(BF16)
        new_x = []
        for (latent, B, tm), x, mod in zip(streams, (xp, xs), (mod_p, mod_s)):
            sh1, sc1, g1, sh2, sc2, g2 = mod
            p = _nm_matmul(x, norm1_g[l], sh1, sc1, w_in, tm)
            if l % 2 == 0:
                s0 = state_b[:, j] if latent else jnp.zeros((B, 2, H_B, DK_B, DV_B), F32)
                o1, o2, k_h, v_h, s_new = _even_mix(p, latent, a_rpb[j], b_w_g2[j], b_b_g[j], b_norm_g[j],
                                                    cache_a_k[:, j], cache_a_v[:, j], s0)
                if not latent:
                    ak_l.append(k_h)
                    av_l.append(v_h)
                    sb_l.append(s_new)
            else:
                s0 = state_d[:, j] if latent else jnp.zeros((B, 2, H_D, DK_D, DV_D), F32)
                o1, o2, k_h, v_h, s_new = _odd_mix(p, latent, c_sink[j], d_conv[j], d_a_log[j], d_dt_bias[j],
                                                   d_norm_g[j], cache_c_k[:, j], cache_c_v[:, j], s0)
                if not latent:
                    ck_l.append(k_h)
                    cv_l.append(v_h)
                    sd_l.append(s_new)
            x = _proj_res(o1, o2, w_out, x, g1, tm)
            x = _conv_ffn_res(x, norm2_g[l], sh2, sc2, g2, w_up, ffn_conv[l], w_down, tm, TF_FFN)
            new_x.append(x)
        xp, xs = new_x
    y_prompt = _rmsnorm_call(xp, final_g, TM_CONTEXT)
    y_sample = _rmsnorm_call(xs, final_g, TM_LATENT)
    return (y_prompt, y_sample, jnp.stack(ak_l, axis=1), jnp.stack(av_l, axis=1), jnp.stack(sb_l, axis=1),
            jnp.stack(ck_l, axis=1), jnp.stack(cv_l, axis=1), jnp.stack(sd_l, axis=1))
```

```python
import functools
import math

import jax
import jax.numpy as jnp
import numpy as np
from jax import lax
from jax.experimental import pallas as pl
from jax.experimental.pallas import tpu as pltpu

D_MODEL = 1024
DEPTH = 4
GRID_W = 64
HEAD_DIM = 64
H_A = D_MODEL // (2 * HEAD_DIM)
WIN_R = 8
WIN_C = 16
H_B = D_MODEL // (2 * HEAD_DIM)
DK_B = HEAD_DIM
DV_B = HEAD_DIM
GLA_RANK = 16
GLA_TAU = 16.0
H_C = D_MODEL // (2 * HEAD_DIM)
KV_C = H_C // 4
WIN_1D = 128
ROPE_BASE = 10000.0
H_D = D_MODEL // (2 * HEAD_DIM)
DK_D = HEAD_DIM
DV_D = HEAD_DIM
SHORT_CONV = 3
CHUNK = 64
Q_BLOCK = 128
D_FF = ((8 * D_MODEL // 3 + 127) // 128) * 128
EPS = 1e-6
EV_SIZES = (H_A * HEAD_DIM, H_A * HEAD_DIM, H_A * HEAD_DIM, H_B * DK_B, H_B * DK_B, H_B * DV_B, 2 * GLA_RANK, H_B * DV_B)
OD_SIZES = (H_C * HEAD_DIM, KV_C * HEAD_DIM, KV_C * HEAD_DIM, H_D * DK_D, H_D * DK_D, H_D * DV_D, 2 * H_D, 2 * H_D, H_D * DV_D)
F32 = jnp.float32
BF16 = jnp.bfloat16

V7X_VMEM_LIMIT_BYTES = 52 * 1024 * 1024
SUBLANES = 8
LANES = 128
COL_CHUNK = 512


def _params(*sem):
    return pltpu.CompilerParams(dimension_semantics=sem, vmem_limit_bytes=V7X_VMEM_LIMIT_BYTES)


def _norm_mod(x, g, shift, scale):
    y = x * lax.rsqrt(jnp.mean(x * x, axis=-1, keepdims=True) + EPS)
    return (y * g) * (1.0 + scale) + shift


def _adaln_kernel(c_ref, w_ref, b_ref, o_ref):
    cond = c_ref[...]
    a = cond * jax.nn.sigmoid(cond)
    a_hi = a.astype(BF16)
    a_lo = (a - a_hi.astype(F32)).astype(BF16)
    w = w_ref[0]
    w_hi = w.astype(BF16)
    w_lo = (w - w_hi.astype(F32)).astype(BF16)
    acc = jnp.dot(a_hi, w_hi, preferred_element_type=F32)
    acc += jnp.dot(a_lo, w_hi, preferred_element_type=F32)
    acc += jnp.dot(a_hi, w_lo, preferred_element_type=F32)
    o_ref[0] = acc + b_ref[0]


def _adaln(cond, ada_w, ada_b):
    R = cond.shape[0]
    L, D, N = ada_w.shape
    tn = 1024
    return pl.pallas_call(
        _adaln_kernel,
        grid=(L, N // tn),
        in_specs=[pl.BlockSpec((R, D), lambda l, j: (0, 0)),
                  pl.BlockSpec((1, D, tn), lambda l, j: (l, 0, j)),
                  pl.BlockSpec((1, 1, tn), lambda l, j: (l, 0, j))],
        out_specs=pl.BlockSpec((1, R, tn), lambda l, j: (l, 0, j)),
        out_shape=jax.ShapeDtypeStruct((L, R, N), F32),
        compiler_params=_params("parallel", "parallel"),
    )(cond, ada_w, ada_b.reshape(L, 1, N))


def _nm_matmul_kernel(x_ref, g_ref, sh_ref, sc_ref, w_ref, o_ref):
    h = _norm_mod(x_ref[0], g_ref[...], sh_ref[0], sc_ref[0]).astype(BF16)
    n = o_ref.shape[-1]
    for c0 in range(0, n, COL_CHUNK):
        c1 = min(c0 + COL_CHUNK, n)
        o_ref[0, :, c0:c1] = jnp.dot(h, w_ref[:, c0:c1], preferred_element_type=F32)


def _nm_matmul(x, g, shift, scale, w, tm):
    B, T, D = x.shape
    N = w.shape[1]
    return pl.pallas_call(
        _nm_matmul_kernel,
        grid=(B, T // tm),
        in_specs=[pl.BlockSpec((1, tm, D), lambda b, i: (b, i, 0)),
                  pl.BlockSpec((1, D), lambda b, i: (0, 0)),
                  pl.BlockSpec((1, 1, D), lambda b, i: (b, 0, 0)),
                  pl.BlockSpec((1, 1, D), lambda b, i: (b, 0, 0)),
                  pl.BlockSpec((D, N), lambda b, i: (0, 0))],
        out_specs=pl.BlockSpec((1, tm, N), lambda b, i: (b, i, 0)),
        out_shape=jax.ShapeDtypeStruct((B, T, N), F32),
        compiler_params=_params("parallel", "parallel"),
    )(x, g.reshape(1, D), shift.reshape(B, 1, D), scale.reshape(B, 1, D), w)


def _proj_res_kernel(a1_ref, a2_ref, w1_ref, w2_ref, x_ref, gate_ref, o_ref):
    acc = jnp.dot(a1_ref[0].astype(BF16), w1_ref[...], preferred_element_type=F32)
    acc += jnp.dot(a2_ref[0].astype(BF16), w2_ref[...], preferred_element_type=F32)
    o_ref[0] = x_ref[0] + gate_ref[0] * acc


def _proj_res(a1, a2, w, x, gate, tm):
    B, T, K = a1.shape
    D = w.shape[1]
    return pl.pallas_call(
        _proj_res_kernel,
        grid=(B, T // tm),
        in_specs=[pl.BlockSpec((1, tm, K), lambda b, i: (b, i, 0)),
                  pl.BlockSpec((1, tm, K), lambda b, i: (b, i, 0)),
                  pl.BlockSpec((K, D), lambda b, i: (0, 0)),
                  pl.BlockSpec((K, D), lambda b, i: (1, 0)),
                  pl.BlockSpec((1, tm, D), lambda b, i: (b, i, 0)),
                  pl.BlockSpec((1, 1, D), lambda b, i: (b, 0, 0))],
        out_specs=pl.BlockSpec((1, tm, D), lambda b, i: (b, i, 0)),
        out_shape=jax.ShapeDtypeStruct((B, T, D), F32),
        compiler_params=_params("parallel", "parallel"),
    )(a1, a2, w, w, x, gate.reshape(B, 1, D))


def _ffn_kernel(x_ref, xp_ref, xn_ref, g_ref, sh_ref, sc_ref, gate_ref, wa_ref, wg_ref,
                ca_ref, cg_ref, wd_ref, o_ref, h_ref, halo_ref, acc_ref):
    i = pl.program_id(1)
    c = pl.program_id(2)
    tm = x_ref.shape[1]

    @pl.when(c == 0)
    def _():
        g, sh, sc = g_ref[...], sh_ref[0], sc_ref[0]
        h_ref[...] = _norm_mod(x_ref[0], g, sh, sc).astype(BF16)
        hp = jnp.where(i > 0, _norm_mod(xp_ref[0], g, sh, sc), 0.0)
        hn = jnp.where(i < pl.num_programs(1) - 1, _norm_mod(xn_ref[0], g, sh, sc), 0.0)
        halo_ref[...] = jnp.concatenate([hp, hn], axis=0).astype(BF16)
        acc_ref[...] = jnp.zeros_like(acc_ref)

    h = h_ref[...]
    halo = halo_ref[...]
    row = lax.broadcasted_iota(jnp.int32, (tm, 1), 0)

    def conv_branch(w_ref, cw_ref):
        w = w_ref[...]
        u = jnp.dot(h, w, preferred_element_type=F32)
        uh = jnp.dot(halo, w, preferred_element_type=F32)
        u_prev = jnp.where(row == 0, uh[SUBLANES - 1:SUBLANES], pltpu.roll(u, 1, axis=0))
        u_next = jnp.where(row == tm - 1, uh[SUBLANES:SUBLANES + 1], pltpu.roll(u, tm - 1, axis=0))
        cw = cw_ref[...]
        return u_prev * cw[0:1] + u * cw[1:2] + u_next * cw[2:3]

    a = conv_branch(wa_ref, ca_ref)
    gt = conv_branch(wg_ref, cg_ref)
    act = a * (gt * jax.nn.sigmoid(gt))
    acc_ref[...] += jnp.dot(act.astype(BF16), wd_ref[...], preferred_element_type=F32)

    @pl.when(c == pl.num_programs(2) - 1)
    def _():
        o_ref[0] = x_ref[0] + gate_ref[0] * acc_ref[...]


def _conv_ffn_res(x, g, shift, scale, gate, w_up, w_conv, w_down, tm, tf):
    B, T, D = x.shape
    F = w_down.shape[0]
    nf = F // tf
    nt = T // tm
    rows8 = tm // SUBLANES
    vec = lambda: pl.BlockSpec((1, 1, D), lambda b, i, c: (b, 0, 0))
    return pl.pallas_call(
        _ffn_kernel,
        grid=(B, nt, nf),
        in_specs=[pl.BlockSpec((1, tm, D), lambda b, i, c: (b, i, 0)),
                  pl.BlockSpec((1, SUBLANES, D), lambda b, i, c: (b, jnp.maximum(i * rows8 - 1, 0), 0)),
                  pl.BlockSpec((1, SUBLANES, D), lambda b, i, c: (b, jnp.minimum((i + 1) * rows8, T // SUBLANES - 1), 0)),
                  pl.BlockSpec((1, D), lambda b, i, c: (0, 0)),
                  vec(), vec(), vec(),
                  pl.BlockSpec((D, tf), lambda b, i, c: (0, c)),
                  pl.BlockSpec((D, tf), lambda b, i, c: (0, nf + c)),
                  pl.BlockSpec((3, tf), lambda b, i, c: (0, c)),
                  pl.BlockSpec((3, tf), lambda b, i, c: (0, nf + c)),
                  pl.BlockSpec((tf, D), lambda b, i, c: (c, 0))],
        out_specs=pl.BlockSpec((1, tm, D), lambda b, i, c: (b, i, 0)),
        out_shape=jax.ShapeDtypeStruct((B, T, D), F32),
        scratch_shapes=[pltpu.VMEM((tm, D), BF16),
                        pltpu.VMEM((2 * SUBLANES, D), BF16),
                        pltpu.VMEM((tm, D), F32)],
        compiler_params=_params("parallel", "parallel", "arbitrary"),
    )(x, x, x, g.reshape(1, D), shift.reshape(B, 1, D), scale.reshape(B, 1, D), gate.reshape(B, 1, D),
      w_up, w_up, w_conv, w_conv, w_down)


def _rmsnorm_kernel(x_ref, g_ref, o_ref):
    x = x_ref[0]
    o_ref[0] = (x * lax.rsqrt(jnp.mean(x * x, axis=-1, keepdims=True) + EPS)) * g_ref[...]


def _rmsnorm_call(x, g, tm):
    B, T, D = x.shape
    return pl.pallas_call(
        _rmsnorm_kernel,
        grid=(B, T // tm),
        in_specs=[pl.BlockSpec((1, tm, D), lambda b, i: (b, i, 0)),
                  pl.BlockSpec((1, D), lambda b, i: (0, 0))],
        out_specs=pl.BlockSpec((1, tm, D), lambda b, i: (b, i, 0)),
        out_shape=jax.ShapeDtypeStruct((B, T, D), F32),
        compiler_params=_params("parallel", "parallel"),
    )(x, g.reshape(1, D))


NEG_BIG = -1e30


def _dot_nt(a, b):
    return lax.dot_general(a, b, (((1,), (1,)), ((), ())), preferred_element_type=F32)


def _lane_group():
    return lax.broadcasted_iota(jnp.int32, (1, LANES), 1) // HEAD_DIM


def _attend(score_fns, values, sink):
    scores = [fn() for fn in score_fns]
    yield
    m = functools.reduce(jnp.maximum, [jnp.max(s, axis=-1, keepdims=True) for s in scores])
    if sink is not None:
        m = jnp.maximum(m, sink)
    den = None
    out = None
    for s, v in zip(scores, values):
        e = jnp.exp(s - m)
        d = jnp.sum(e, axis=-1, keepdims=True)
        o = jnp.dot(e.astype(BF16), v, preferred_element_type=F32)
        den = d if den is None else den + d
        out = o if out is None else out + o
    if sink is not None:
        den = den + jnp.exp(sink - m)
    yield
    return out / den


def _merge_heads(outs, grp):
    return jnp.where(grp == 0, outs[0], outs[1])


def _dense_attn_kernel(q_ref, k_ref, v_ref, sink_ref, o_ref, *, gqa, use_sink):
    grp = _lane_group()
    q = q_ref[0]
    k = k_ref[0]
    v = v_ref[0]
    if gqa:
        kvg = pl.program_id(1) // 2
        k = jnp.where(grp == kvg, k, pltpu.roll(k, HEAD_DIM, axis=1))
        v = jnp.where(grp == kvg, v, pltpu.roll(v, HEAD_DIM, axis=1))
    kb = k.astype(BF16)
    vb = v.astype(BF16)

    def head(h):
        qh = jnp.where(grp == h, q, 0.0).astype(BF16)
        sink = sink_ref[0, h][:, :1] if use_sink else None
        return _attend([lambda: _dot_nt(qh, kb) * (HEAD_DIM ** -0.5)], [vb], sink)

    o_ref[0] = _merge_heads(_lockstep([head(0), head(1)]), grp)


def _dense_attn_call(p, q_blk, k_blk, v_blk, sink, gqa):
    B, T, _ = p.shape
    n_pair = H_A // 2
    use_sink = sink is not None
    if use_sink:
        sink_arr = jnp.broadcast_to(sink.reshape(n_pair, 2, 1, 1), (n_pair, 2, 1, LANES))
    else:
        sink_arr = jnp.zeros((n_pair, 2, 1, LANES), F32)
    kv_idx = (lambda hp: 0) if gqa else (lambda hp: hp)
    return pl.pallas_call(
        functools.partial(_dense_attn_kernel, gqa=gqa, use_sink=use_sink),
        grid=(B, n_pair),
        in_specs=[pl.BlockSpec((1, T, LANES), lambda b, hp: (b, 0, q_blk + hp)),
                  pl.BlockSpec((1, T, LANES), lambda b, hp: (b, 0, k_blk + kv_idx(hp))),
                  pl.BlockSpec((1, T, LANES), lambda b, hp: (b, 0, v_blk + kv_idx(hp))),
                  pl.BlockSpec((1, 2, 1, LANES), lambda b, hp: (hp, 0, 0, 0))],
        out_specs=pl.BlockSpec((1, T, LANES), lambda b, hp: (b, 0, hp)),
        out_shape=jax.ShapeDtypeStruct((B, T, n_pair * LANES), F32),
        compiler_params=_params("parallel", "parallel"),
    )(p, p, p, sink_arr)


NBR_SPAN = WIN_R * GRID_W
ATTN_BLOCKS_PER_STEP = 4


def _nbr_attn_kernel(q_ref, k_ref, v_ref, ck_ref, cv_ref, bias_ref, o_ref, kb_ref, vb_ref):
    kb_ref[...] = k_ref[0].astype(BF16)
    vb_ref[...] = v_ref[0].astype(BF16)
    ckb = ck_ref[0].astype(BF16)
    cvb = cv_ref[0].astype(BF16)
    grp = _lane_group()
    n_rows = q_ref.shape[1] // GRID_W

    def head(i, h):
        r0 = jnp.clip(i - WIN_R // 2, 0, n_rows - WIN_R)
        q = q_ref[0, pl.ds(pl.multiple_of(i * GRID_W, GRID_W), GRID_W), :]
        start = pl.multiple_of(r0 * GRID_W, GRID_W)
        kw = kb_ref[pl.ds(start, NBR_SPAN), :]
        vw = vb_ref[pl.ds(start, NBR_SPAN), :]
        qh = jnp.where(grp == h, q, 0.0).astype(BF16)
        return _attend([lambda: _dot_nt(qh, kw) * (HEAD_DIM ** -0.5) + bias_ref[i - r0, h],
                        lambda: _dot_nt(qh, ckb) * (HEAD_DIM ** -0.5)], [vw, cvb], None)

    def rows(step, carry):
        ids = [step * ATTN_BLOCKS_PER_STEP + j for j in range(ATTN_BLOCKS_PER_STEP)]
        outs = _lockstep([head(i, h) for i in ids for h in range(2)])
        for j, i in enumerate(ids):
            o_ref[0, pl.ds(pl.multiple_of(i * GRID_W, GRID_W), GRID_W), :] = _merge_heads(outs[2 * j:2 * j + 2], grp)
        return carry

    lax.fori_loop(0, n_rows // ATTN_BLOCKS_PER_STEP, rows, 0)


def _nbr_bias_table(rpb):
    var = np.arange(WIN_R)[:, None, None]
    r = np.arange(WIN_R)[None, :, None]
    row_sel = (r - var + (WIN_R - 1) == np.arange(2 * WIN_R - 1)[None, None, :]).astype(np.float32)
    w = np.arange(GRID_W)[:, None, None]
    cc = np.arange(GRID_W)[None, :, None]
    c0 = np.clip(w - WIN_C // 2, 0, GRID_W - WIN_C)
    valid = (cc >= c0) & (cc < c0 + WIN_C)
    col_sel = (valid & (cc - w + (WIN_C - 1) == np.arange(2 * WIN_C - 1)[None, None, :])).astype(np.float32)
    t = jnp.einsum('vra,hab->vhrb', row_sel, rpb, precision=lax.Precision.HIGHEST)
    tbl = jnp.einsum('vhrb,wcb->vhwrc', t, col_sel, precision=lax.Precision.HIGHEST)
    tbl = jnp.where(valid[None, None, :, None, :, 0], tbl, NEG_BIG)
    return tbl.reshape(WIN_R, rpb.shape[0], GRID_W, NBR_SPAN)


def _nbr_attn_call(p, rpb, ck_cache, cv_cache):
    B, T, _ = p.shape
    n_pair = H_A // 2
    L = ck_cache.shape[2]
    ckp = ck_cache.transpose(0, 2, 1, 3).reshape(B, L, H_A * HEAD_DIM)
    cvp = cv_cache.transpose(0, 2, 1, 3).reshape(B, L, H_A * HEAD_DIM)
    bias = _nbr_bias_table(rpb)
    col = lambda off: pl.BlockSpec((1, T, LANES), lambda b, hp: (b, 0, off + hp))
    ctx = lambda: pl.BlockSpec((1, L, LANES), lambda b, hp: (b, 0, hp))
    return pl.pallas_call(
        _nbr_attn_kernel,
        grid=(B, n_pair),
        in_specs=[col(0), col(n_pair), col(2 * n_pair), ctx(), ctx(),
                  pl.BlockSpec((WIN_R, 2, GRID_W, NBR_SPAN), lambda b, hp: (0, hp, 0, 0))],
        out_specs=pl.BlockSpec((1, T, LANES), lambda b, hp: (b, 0, hp)),
        out_shape=jax.ShapeDtypeStruct((B, T, n_pair * LANES), F32),
        scratch_shapes=[pltpu.VMEM((T, LANES), BF16), pltpu.VMEM((T, LANES), BF16)],
        compiler_params=_params("parallel", "parallel"),
    )(p, p, p, ckp, cvp, bias)


WIN_SPAN = 3 * Q_BLOCK
WIN_BLOCKS_PER_STEP = 1
ROPE_SWAP = HEAD_DIM // 4


def _rope(x, cos, sin_signed):
    lane = lax.broadcasted_iota(jnp.int32, (1, LANES), 1)
    first = lane % (2 * ROPE_SWAP) < ROPE_SWAP
    partner = jnp.where(first, pltpu.roll(x, LANES - ROPE_SWAP, axis=1), pltpu.roll(x, ROPE_SWAP, axis=1))
    return x * cos + partner * sin_signed


def _win_attn_kernel(q_ref, k_ref, v_ref, ck_ref, cv_ref, cos_ref, sin_ref, sink_ref, o_ref, kb_ref, vb_ref):
    grp = _lane_group()
    kvg = pl.program_id(1) // 2
    both = lambda a: jnp.where(grp == kvg, a, pltpu.roll(a, HEAD_DIM, axis=1))
    kb_ref[...] = _rope(both(k_ref[0]), cos_ref[...], sin_ref[...]).astype(BF16)
    vb_ref[...] = both(v_ref[0]).astype(BF16)
    ckb = both(ck_ref[0]).astype(BF16)
    cvb = both(cv_ref[0]).astype(BF16)
    T = q_ref.shape[1]
    rel = (lax.broadcasted_iota(jnp.int32, (Q_BLOCK, WIN_SPAN), 1)
           - lax.broadcasted_iota(jnp.int32, (Q_BLOCK, WIN_SPAN), 0))

    def heads(n):
        q0 = pl.multiple_of(n * Q_BLOCK, Q_BLOCK)
        start = pl.multiple_of(jnp.clip(q0 - Q_BLOCK, 0, T - WIN_SPAN), Q_BLOCK)
        q = _rope(q_ref[0, pl.ds(q0, Q_BLOCK), :], cos_ref[pl.ds(q0, Q_BLOCK), :], sin_ref[pl.ds(q0, Q_BLOCK), :])
        kw = kb_ref[pl.ds(start, WIN_SPAN), :]
        vw = vb_ref[pl.ds(start, WIN_SPAN), :]
        valid = jnp.abs(rel + (start - q0)) <= WIN_1D

        def head(h):
            qh = jnp.where(grp == h, q, 0.0).astype(BF16)
            return _attend([lambda: jnp.where(valid, _dot_nt(qh, kw) * (HEAD_DIM ** -0.5), NEG_BIG),
                            lambda: _dot_nt(qh, ckb) * (HEAD_DIM ** -0.5)], [vw, cvb], sink_ref[0, h][:, :1])

        return [head(0), head(1)]

    def blocks(step, carry):
        ids = [step * WIN_BLOCKS_PER_STEP + j for j in range(WIN_BLOCKS_PER_STEP)]
        outs = _lockstep([prog for n in ids for prog in heads(n)])
        for j, n in enumerate(ids):
            o_ref[0, pl.ds(pl.multiple_of(n * Q_BLOCK, Q_BLOCK), Q_BLOCK), :] = _merge_heads(outs[2 * j:2 * j + 2], grp)
        return carry

    lax.fori_loop(0, T // Q_BLOCK // WIN_BLOCKS_PER_STEP, blocks, 0)


def _rope_tables(T):
    t = jnp.arange(T)
    quarter = HEAD_DIM // 4
    inv = 1.0 / (ROPE_BASE ** (jnp.arange(quarter, dtype=F32) / quarter))
    ang_r = (t // GRID_W).astype(F32)[:, None] * inv[None, :]
    ang_c = (t % GRID_W).astype(F32)[:, None] * inv[None, :]
    cos = jnp.concatenate([jnp.cos(ang_r), jnp.cos(ang_r), jnp.cos(ang_c), jnp.cos(ang_c)], axis=-1)
    sin = jnp.concatenate([-jnp.sin(ang_r), jnp.sin(ang_r), -jnp.sin(ang_c), jnp.sin(ang_c)], axis=-1)
    return jnp.tile(cos, (1, 2)), jnp.tile(sin, (1, 2))


def _win_attn_call(p, sink, ck_cache, cv_cache):
    B, T, _ = p.shape
    n_pair = H_C // 2
    L = ck_cache.shape[2]
    ckp = ck_cache.transpose(0, 2, 1, 3).reshape(B, L, KV_C * HEAD_DIM)
    cvp = cv_cache.transpose(0, 2, 1, 3).reshape(B, L, KV_C * HEAD_DIM)
    cos, sin = _rope_tables(T)
    sink_arr = jnp.broadcast_to(sink.reshape(n_pair, 2, 1, 1), (n_pair, 2, 1, LANES))
    col = lambda blk: pl.BlockSpec((1, T, LANES), lambda b, hp: (b, 0, blk))
    ctx = lambda: pl.BlockSpec((1, L, LANES), lambda b, hp: (b, 0, 0))
    tab = lambda: pl.BlockSpec((T, LANES), lambda b, hp: (0, 0))
    return pl.pallas_call(
        _win_attn_kernel,
        grid=(B, n_pair),
        in_specs=[pl.BlockSpec((1, T, LANES), lambda b, hp: (b, 0, hp)), col(n_pair), col(n_pair + 1),
                  ctx(), ctx(), tab(), tab(),
                  pl.BlockSpec((1, 2, 1, LANES), lambda b, hp: (hp, 0, 0, 0))],
        out_specs=pl.BlockSpec((1, T, LANES), lambda b, hp: (b, 0, hp)),
        out_shape=jax.ShapeDtypeStruct((B, T, n_pair * LANES), F32),
        scratch_shapes=[pltpu.VMEM((T, LANES), BF16), pltpu.VMEM((T, LANES), BF16)],
        compiler_params=_params("parallel", "parallel"),
    )(p, p, p, ckp, cvp, cos, sin, sink_arr)


def _dot_tn(a, b):
    return lax.dot_general(a, b, (((0,), (0,)), ((), ())), preferred_element_type=F32)


def _split_bf16(x):
    hi = x.astype(BF16)
    return hi, (x - hi.astype(F32)).astype(BF16)


GLA_CHUNKS_PER_STEP = 4
DELTA_CHUNKS_PER_STEP = 4


def _lockstep(programs):
    results = [None] * len(programs)
    live = list(range(len(programs)))
    while live:
        for idx in list(live):
            try:
                next(programs[idx])
            except StopIteration as stop:
                results[idx] = stop.value
                live.remove(idx)
    return results


def _block_diag_mask():
    return (lax.broadcasted_iota(jnp.int32, (LANES, 1), 0) // HEAD_DIM) == _lane_group()


def _head_rmsnorm(o, grp):
    sq = o * o
    s0 = jnp.sum(jnp.where(grp == 0, sq, 0.0), axis=-1, keepdims=True)
    s1 = jnp.sum(jnp.where(grp == 1, sq, 0.0), axis=-1, keepdims=True)
    ms = jnp.where(grp == 0, s0, s1) * (1.0 / HEAD_DIM)
    return o * lax.rsqrt(ms + EPS)


def _gla_kernel(q_ref, k_ref, v_ref, r_ref, glr_ref, w2_ref, bg_ref, ng_ref, s0_ref, o_ref, sfin_ref,
                ob_ref, st_ref):
    T = q_ref.shape[1]
    n = T // CHUNK
    grp = _lane_group()
    bd = _block_diag_mask()
    ri = lax.broadcasted_iota(jnp.int32, (CHUNK, CHUNK), 0)
    ci = lax.broadcasted_iota(jnp.int32, (CHUNK, CHUNK), 1)
    ones_tok = jnp.ones((CHUNK, LANES), BF16)
    st_ref[...] = s0_ref[0, :, 0]

    def chunk(c0, d):
        sl = pl.ds(c0, CHUNK)
        q = q_ref[0, sl, :] * (DK_B ** -0.5)
        k = k_ref[0, sl, :]
        vb = v_ref[0, sl, :].astype(BF16)
        z = jnp.dot(glr_ref[0, sl, :].astype(BF16), w2_ref[d], preferred_element_type=F32) + bg_ref[d]
        yield
        g = (jnp.minimum(z, 0.0) - jnp.log(1.0 + jnp.exp(-jnp.abs(z)))) * (1.0 / GLA_TAU)
        g_hi, g_lo = _split_bf16(g)
        mask = (ri <= ci) if d else (ri >= ci)
        tri = jnp.where(mask, 1.0, 0.0).astype(BF16)
        G = jnp.dot(tri, g_hi, preferred_element_type=F32) + jnp.dot(tri, g_lo, preferred_element_type=F32)
        g_tot_rows = _dot_tn(g_hi, ones_tok) + _dot_tn(g_lo, ones_tok)
        yield
        g_tot = G[0:1] if d else G[CHUNK - 1:CHUNK]
        q_t = q * jnp.exp(G)
        k_t = (k * jnp.exp(-G)).astype(BF16)
        k_e = (k * jnp.exp(g_tot - G)).astype(BF16)
        S = st_ref[d]
        att = [jnp.where(mask, _dot_nt(jnp.where(grp == h, q_t, 0.0).astype(BF16), k_t), 0.0) for h in range(2)]
        o = jnp.dot(q_t.astype(BF16), S.astype(BF16), preferred_element_type=F32)
        st_ref[d] = jnp.exp(g_tot_rows) * S + jnp.where(bd, _dot_tn(k_e, vb), 0.0)
        yield
        for h in range(2):
            o = o + jnp.where(grp == h, jnp.dot(att[h].astype(BF16), vb, preferred_element_type=F32), 0.0)
        return o

    def body(i, carry):
        c_f = [pl.multiple_of((i * GLA_CHUNKS_PER_STEP + j) * CHUNK, CHUNK) for j in range(GLA_CHUNKS_PER_STEP)]
        c_b = [pl.multiple_of((n - 1 - i * GLA_CHUNKS_PER_STEP - j) * CHUNK, CHUNK)
               for j in range(GLA_CHUNKS_PER_STEP)]
        outs = _lockstep([chunk(c0, 0) for c0 in c_f] + [chunk(c0, 1) for c0 in c_b])
        for j in range(GLA_CHUNKS_PER_STEP):
            o_ref[0, pl.ds(c_f[j], CHUNK), :] = outs[j]
            ob_ref[pl.ds(c_b[j], CHUNK), :] = outs[GLA_CHUNKS_PER_STEP + j]
        return carry

    lax.fori_loop(0, n // GLA_CHUNKS_PER_STEP, body, 0)
    sfin_ref[0, :, 0] = st_ref[...]

    def gate(i, carry):
        sl = pl.ds(pl.multiple_of(i * CHUNK, CHUNK), CHUNK)
        r = r_ref[0, sl, :]
        y = _head_rmsnorm(o_ref[0, sl, :] + ob_ref[sl, :], grp) * ng_ref[...]
        o_ref[0, sl, :] = y * (r * jax.nn.sigmoid(r))
        return carry

    lax.fori_loop(0, n, gate, 0)


def _pair_block_diag(s):
    B, _, H, dk, dv = s.shape
    s = s.reshape(B, 2, H // 2, 2, dk, dv)
    z = jnp.zeros_like(s[:, :, :, 0])
    top = jnp.concatenate([s[:, :, :, 0], z], axis=-1)
    bot = jnp.concatenate([z, s[:, :, :, 1]], axis=-1)
    return jnp.concatenate([top, bot], axis=-2)


def _pair_block_diag_inv(sp):
    a = sp[..., :HEAD_DIM, :HEAD_DIM]
    b = sp[..., HEAD_DIM:, HEAD_DIM:]
    B, _, P = sp.shape[:3]
    return jnp.stack([a, b], axis=3).reshape(B, 2, 2 * P, HEAD_DIM, HEAD_DIM)


EV_BQ, EV_BK, EV_BV, EV_BR, EV_GLR = 12, 16, 20, 24, 28


def _gla_call(p, w_g2, b_g, norm_g, s0):
    B, T, _ = p.shape
    n_pair = H_B // 2
    w2 = jnp.zeros((2, LANES, H_B * DK_B), F32)
    w2 = w2.at[0, :GLA_RANK].set(w_g2[0]).at[1, GLA_RANK:2 * GLA_RANK].set(w_g2[1]).astype(BF16)
    col = lambda off: pl.BlockSpec((1, T, LANES), lambda b, hp: (b, 0, off + hp))
    st = lambda: pl.BlockSpec((1, 2, 1, LANES, LANES), lambda b, hp: (b, 0, hp, 0, 0))
    o, s_fin = pl.pallas_call(
        _gla_kernel,
        grid=(B, n_pair),
        in_specs=[col(EV_BQ), col(EV_BK), col(EV_BV), col(EV_BR),
                  pl.BlockSpec((1, T, LANES), lambda b, hp: (b, 0, EV_GLR)),
                  pl.BlockSpec((2, LANES, LANES), lambda b, hp: (0, 0, hp)),
                  pl.BlockSpec((2, 1, LANES), lambda b, hp: (0, 0, hp)),
                  pl.BlockSpec((1, LANES), lambda b, hp: (0, hp)),
                  st()],
        out_specs=[pl.BlockSpec((1, T, LANES), lambda b, hp: (b, 0, hp)), st()],
        out_shape=[jax.ShapeDtypeStruct((B, T, n_pair * LANES), F32),
                   jax.ShapeDtypeStruct((B, 2, n_pair, LANES, LANES), F32)],
        scratch_shapes=[pltpu.VMEM((T, LANES), F32), pltpu.VMEM((2, LANES, LANES), F32)],
        compiler_params=_params("parallel", "parallel"),
    )(p, p, p, p, p, w2, b_g.reshape(2, 1, H_B * DK_B), norm_g.reshape(1, H_B * DV_B), _pair_block_diag(s0))
    return o, _pair_block_diag_inv(s_fin)


SOLVE_PASSES = 3


def _mm(a, b, passes):
    if passes == 1:
        return jnp.dot(a.astype(BF16), b.astype(BF16), preferred_element_type=F32)
    a_hi, a_lo = _split_bf16(a)
    b_hi, b_lo = _split_bf16(b)
    return (jnp.dot(a_hi, b_hi, preferred_element_type=F32) + jnp.dot(a_lo, b_hi, preferred_element_type=F32)
            + jnp.dot(a_hi, b_lo, preferred_element_type=F32))


def _unit_tri_inverse(a, eye):
    b = -a
    p = eye + b
    for _ in range(5):
        b = _mm(b, b, SOLVE_PASSES)
        p = p + _mm(p, b, SOLVE_PASSES)
    return p


def _softplus(x):
    return jnp.maximum(x, 0.0) + jnp.log(1.0 + jnp.exp(-jnp.abs(x)))


def _delta_kernel(q_ref, k_ref, v_ref, z_ref, dab_ref, cq_ref, ck_ref, cv_ref, nega_ref, dtb_ref, ng_ref, s0_ref,
                  o_ref, sfin_ref, qs_ref, ks_ref, vs_ref, ob_ref, st_ref):
    T = q_ref.shape[1]
    n = T // CHUNK
    hp = pl.program_id(1)
    grp = _lane_group()
    lane = lax.broadcasted_iota(jnp.int32, (1, LANES), 1)
    subl = lax.broadcasted_iota(jnp.int32, (LANES, 1), 0)
    bd = _block_diag_mask()
    ri = lax.broadcasted_iota(jnp.int32, (CHUNK, CHUNK), 0)
    ci = lax.broadcasted_iota(jnp.int32, (CHUNK, CHUNK), 1)
    eye = jnp.where(ri == ci, 1.0, 0.0)
    row = lax.broadcasted_iota(jnp.int32, (CHUNK, 1), 0)
    st_ref[...] = s0_ref[0, :, 0]

    def prep(i, carry):
        c0 = pl.multiple_of(i * CHUNK, CHUNK)
        lo = pl.multiple_of(jnp.maximum(c0 - SUBLANES, 0), SUBLANES)
        hi = pl.multiple_of(jnp.minimum(c0 + CHUNK, T - SUBLANES), SUBLANES)
        for src, cw_ref, dst, scale in ((q_ref, cq_ref, qs_ref, DK_D ** -0.5), (k_ref, ck_ref, ks_ref, 1.0),
                                        (v_ref, cv_ref, vs_ref, None)):
            x = src[0, pl.ds(c0, CHUNK), :]
            before = jnp.where(i > 0, src[0, pl.ds(lo, SUBLANES), :][SUBLANES - 1:], 0.0)
            after = jnp.where(i < n - 1, src[0, pl.ds(hi, SUBLANES), :][:1], 0.0)
            x_prev = jnp.where(row == 0, before, pltpu.roll(x, 1, axis=0))
            x_next = jnp.where(row == CHUNK - 1, after, pltpu.roll(x, CHUNK - 1, axis=0))
            cw = cw_ref[...]
            y = x_prev * cw[0:1] + x * cw[1:2] + x_next * cw[2:3]
            y = y * jax.nn.sigmoid(y)
            if scale is not None:
                sq = y * y
                s0 = jnp.sum(jnp.where(grp == 0, sq, 0.0), axis=-1, keepdims=True)
                s1 = jnp.sum(jnp.where(grp == 1, sq, 0.0), axis=-1, keepdims=True)
                y = y * lax.rsqrt(jnp.where(grp == 0, s0, s1) + EPS) * scale
            dst[pl.ds(c0, CHUNK), :] = y
        return carry

    lax.fori_loop(0, n, prep, 0)

    def chunk(c0, d):
        sl = pl.ds(c0, CHUNK)
        q = qs_ref[sl, :]
        k = ks_ref[sl, :]
        v = vs_ref[sl, :]
        ab = dab_ref[0, sl, :]
        gb = jnp.where(lane < 2 * H_D, nega_ref[...] * _softplus(ab + dtb_ref[...]), jax.nn.sigmoid(ab))
        gb_t = gb.T
        incl = (ri <= ci) if d else (ri >= ci)
        incl_t = (ri >= ci) if d else (ri <= ci)
        strict = (ri < ci) if d else (ri > ci)

        def head(h):
            jg = d * H_D + 2 * hp + h
            g_c = jnp.sum(jnp.where(lane == jg, gb, 0.0), axis=1, keepdims=True)
            beta_c = jnp.sum(jnp.where(lane == 2 * H_D + jg, gb, 0.0), axis=1, keepdims=True)
            g_r = jnp.sum(jnp.where(subl == jg, gb_t, 0.0), axis=0, keepdims=True)
            G_c = jnp.sum(jnp.where(incl, g_r, 0.0), axis=1, keepdims=True)
            G_r = jnp.sum(jnp.where(incl_t, g_c, 0.0), axis=0, keepdims=True)
            tot = jnp.sum(g_r, axis=1, keepdims=True)
            gam = jnp.where(incl, jnp.exp(jnp.where(incl, G_c - G_r, 0.0)), 0.0)
            kh = jnp.where(grp == h, k, 0.0)
            qh = jnp.where(grp == h, q, 0.0)
            vh = jnp.where(grp == h, v, 0.0)
            kb = kh * beta_c
            khb = kh.astype(BF16)
            a = jnp.where(strict, _dot_nt(kb.astype(BF16), khb) * gam, 0.0)
            a_qk = (_dot_nt(qh.astype(BF16), khb) * gam).astype(BF16)
            yield
            wide = jnp.concatenate([-a, eye], axis=1)
            for _ in range(6):
                wide = _mm(wide[:, :CHUNK], wide, SOLVE_PASSES) + jnp.where(lane >= CHUNK, wide, 0.0)
                yield
            t_inv = wide[:, CHUNK:]
            e_g = jnp.exp(G_c)
            rhs = vh * beta_c + pltpu.roll(kb * e_g, HEAD_DIM, axis=1)
            return _mm(t_inv, rhs, SOLVE_PASSES), a_qk, qh * e_g, kh * jnp.exp(tot - G_c), jnp.exp(tot)

        return head

    def scan_step(d, heads):
        (r0, aqk0, qg0, ke0, d0), (r1, aqk1, qg1, ke1, d1) = heads
        S = st_ref[d]
        Sb = S.astype(BF16)
        w_val = jnp.where(grp == 0, r0, r1)
        k_cum = pltpu.roll(jnp.where(grp == 0, r1, r0), HEAD_DIM, axis=1)
        v_new = w_val - jnp.dot(k_cum.astype(BF16), Sb, preferred_element_type=F32)
        o = jnp.dot((qg0 + qg1).astype(BF16), Sb, preferred_element_type=F32)
        yield
        vnb = v_new.astype(BF16)
        o = o + jnp.where(grp == 0, jnp.dot(aqk0, vnb, preferred_element_type=F32), 0.0)
        o = o + jnp.where(grp == 1, jnp.dot(aqk1, vnb, preferred_element_type=F32), 0.0)
        d_rows = jnp.where(subl // HEAD_DIM == 0, d0, d1)
        st_ref[d] = S * d_rows + jnp.where(bd, _dot_tn((ke0 + ke1).astype(BF16), vnb), 0.0)
        return o

    def scan_dir(d, heads):
        outs = []
        for j in range(DELTA_CHUNKS_PER_STEP):
            o = yield from scan_step(d, heads[2 * j:2 * j + 2])
            outs.append(o)
        return outs

    def body(i, carry):
        c_f = [pl.multiple_of((i * DELTA_CHUNKS_PER_STEP + j) * CHUNK, CHUNK) for j in range(DELTA_CHUNKS_PER_STEP)]
        c_b = [pl.multiple_of((n - 1 - i * DELTA_CHUNKS_PER_STEP - j) * CHUNK, CHUNK)
               for j in range(DELTA_CHUNKS_PER_STEP)]
        programs = []
        for d, starts in ((0, c_f), (1, c_b)):
            for c0 in starts:
                head = chunk(c0, d)
                programs += [head(0), head(1)]
        heads = _lockstep(programs)
        half = 2 * DELTA_CHUNKS_PER_STEP
        o_f, o_b = _lockstep([scan_dir(0, heads[:half]), scan_dir(1, heads[half:])])
        for j in range(DELTA_CHUNKS_PER_STEP):
            o_ref[0, pl.ds(c_f[j], CHUNK), :] = o_f[j]
            ob_ref[pl.ds(c_b[j], CHUNK), :] = o_b[j]
        return carry

    lax.fori_loop(0, n // DELTA_CHUNKS_PER_STEP, body, 0)
    sfin_ref[0, :, 0] = st_ref[...]

    def gate(i, carry):
        sl = pl.ds(pl.multiple_of(i * CHUNK, CHUNK), CHUNK)
        z = z_ref[0, sl, :]
        y = _head_rmsnorm(o_ref[0, sl, :] + ob_ref[sl, :], grp) * ng_ref[...]
        o_ref[0, sl, :] = y * (z * jax.nn.sigmoid(z))
        return carry

    lax.fori_loop(0, n, gate, 0)


OD_DQ, OD_DK, OD_DV, OD_DZ, OD_DAB = 6, 10, 14, 18, 22


def _delta_call(p, w_conv, a_log, dt_bias, norm_g, s0):
    B, T, _ = p.shape
    n_pair = H_D // 2
    nega = jnp.zeros((1, LANES), F32).at[0, :2 * H_D].set(-jnp.exp(a_log.reshape(-1)))
    dtb = jnp.zeros((1, LANES), F32).at[0, :2 * H_D].set(dt_bias.reshape(-1))
    ng = jnp.tile(norm_g, 2).reshape(1, LANES)
    col = lambda off: pl.BlockSpec((1, T, LANES), lambda b, hp: (b, 0, off + hp))
    cw = lambda off: pl.BlockSpec((SHORT_CONV, LANES), lambda b, hp: (0, off + hp))
    vec = lambda: pl.BlockSpec((1, LANES), lambda b, hp: (0, 0))
    st = lambda: pl.BlockSpec((1, 2, 1, LANES, LANES), lambda b, hp: (b, 0, hp, 0, 0))
    o, s_fin = pl.pallas_call(
        _delta_kernel,
        grid=(B, n_pair),
        in_specs=[col(OD_DQ), col(OD_DK), col(OD_DV), col(OD_DZ),
                  pl.BlockSpec((1, T, LANES), lambda b, hp: (b, 0, OD_DAB)),
                  cw(0), cw(n_pair), cw(2 * n_pair), vec(), vec(), vec(), st()],
        out_specs=[pl.BlockSpec((1, T, LANES), lambda b, hp: (b, 0, hp)), st()],
        out_shape=[jax.ShapeDtypeStruct((B, T, n_pair * LANES), F32),
                   jax.ShapeDtypeStruct((B, 2, n_pair, LANES, LANES), F32)],
        scratch_shapes=[pltpu.VMEM((T, LANES), F32), pltpu.VMEM((T, LANES), F32), pltpu.VMEM((T, LANES), F32),
                        pltpu.VMEM((T, LANES), F32), pltpu.VMEM((2, LANES, LANES), F32)],
        compiler_params=_params("parallel", "parallel"),
    )(p, p, p, p, p, w_conv, w_conv, w_conv, nega, dtb, ng, _pair_block_diag(s0))
    return o, _pair_block_diag_inv(s_fin)


def _rmsnorm(x, g):
    y = x * lax.rsqrt(jnp.mean(x * x, axis=-1, keepdims=True) + EPS)
    return y * g


def _l2norm(x):
    return x * lax.rsqrt(jnp.sum(x * x, axis=-1, keepdims=True) + EPS)


def _split_cols(p, sizes):
    cuts = [int(s) for s in np.cumsum(sizes)[:-1]]
    return jnp.split(p, cuts, axis=-1)


def _dwconv(x, w):
    K = w.shape[0]
    T = x.shape[1]
    pad = K // 2
    xp = jnp.pad(x, ((0, 0), (pad, pad), (0, 0)))
    out = xp[:, 0:T] * w[0]
    for i in range(1, K):
        out = out + xp[:, i:i + T] * w[i]
    return out


def _softmax_sink(s, sink):
    if sink is None:
        return jax.nn.softmax(s, axis=-1)
    m = jnp.maximum(jnp.max(s, axis=-1, keepdims=True), sink)
    e = jnp.exp(s - m)
    return e / (jnp.sum(e, axis=-1, keepdims=True) + jnp.exp(sink - m))


def _axial_rope(x):
    T = x.shape[1]
    t = jnp.arange(T)
    half = HEAD_DIM // 2
    quarter = half // 2
    inv = 1.0 / (ROPE_BASE ** (jnp.arange(quarter, dtype=F32) / quarter))

    def rot(xa, pos):
        ang = pos.astype(F32)[:, None] * inv[None, :]
        cos = jnp.cos(ang)[None, :, None, :]
        sin = jnp.sin(ang)[None, :, None, :]
        x1, x2 = xa[..., :quarter], xa[..., quarter:]
        return jnp.concatenate([x1 * cos - x2 * sin, x1 * sin + x2 * cos], axis=-1)

    return jnp.concatenate([rot(x[..., :half], t // GRID_W), rot(x[..., half:], t % GRID_W)], axis=-1)


def _dense_attn(q, k, v, sink):
    B, Tq, Hq, Dh = q.shape
    Hk = k.shape[1]
    G = Hq // Hk
    nb = Tq // Q_BLOCK
    qb = q.reshape(B, nb, Q_BLOCK, Hk, G, Dh).swapaxes(0, 1)
    sk = None if sink is None else sink.reshape(Hk, G, 1, 1)
    scale = Dh ** -0.5

    def one(qblk):
        s = jnp.einsum('bqkgd,bksd->bkgqs', qblk, k) * scale
        p = _softmax_sink(s, sk)
        return jnp.einsum('bkgqs,bksd->bqkgd', p, v)

    o = lax.map(one, qb)
    return o.swapaxes(0, 1).reshape(B, Tq, Hq, Dh)


def _neighborhood_attn(q, k, v, rpb, ck, cv):
    B, T, H, Dh = q.shape
    R = T // GRID_W
    kr = min(WIN_R, R)
    kc = WIN_C
    qg = q.reshape(B, R, GRID_W, H, Dh)
    kg = k.reshape(B, R, GRID_W, H, Dh)
    vg = v.reshape(B, R, GRID_W, H, Dh)
    col = jnp.arange(GRID_W)
    col_idx = jnp.clip(col - kc // 2, 0, GRID_W - kc)[:, None] + jnp.arange(kc)[None, :]
    dc = col_idx - col[:, None] + (WIN_C - 1)
    scale = Dh ** -0.5

    def one(i):
        r0 = jnp.clip(i - kr // 2, 0, R - kr)
        kb = lax.dynamic_slice_in_dim(kg, r0, kr, axis=1)[:, :, col_idx]
        vb = lax.dynamic_slice_in_dim(vg, r0, kr, axis=1)[:, :, col_idx]
        qi = lax.dynamic_index_in_dim(qg, i, axis=1, keepdims=False)
        dr = r0 + jnp.arange(kr) - i + (WIN_R - 1)
        bias = rpb[:, dr[None, :, None], dc[:, None, :]]
        s_loc = jnp.einsum('bwhd,brwchd->bhwrc', qi, kb) * scale + bias
        s_loc = s_loc.reshape(B, H, GRID_W, kr * kc)
        s_ctx = jnp.einsum('bwhd,bhld->bhwl', qi, ck) * scale
        p = jax.nn.softmax(jnp.concatenate([s_loc, s_ctx], axis=-1), axis=-1)
        p_loc = p[..., :kr * kc].reshape(B, H, GRID_W, kr, kc)
        return (jnp.einsum('bhwrc,brwchd->bwhd', p_loc, vb)
                + jnp.einsum('bhwl,bhld->bwhd', p[..., kr * kc:], cv))

    o = lax.map(one, jnp.arange(R))
    return o.swapaxes(0, 1).reshape(B, T, H, Dh)


def _window_attn(q, k, v, sink, ck, cv):
    B, T, Hq, Dh = q.shape
    Hk = k.shape[2]
    G = Hq // Hk
    blk = Q_BLOCK
    nb = T // blk
    pad = ((0, 0), (blk, blk), (0, 0), (0, 0))
    kp = jnp.pad(k, pad)
    vp = jnp.pad(v, pad)
    sk = sink.reshape(1, Hk, G, 1, 1)
    scale = Dh ** -0.5

    def one(n):
        q0 = n * blk
        qn = lax.dynamic_slice_in_dim(q, q0, blk, axis=1).reshape(B, blk, Hk, G, Dh)
        kn = lax.dynamic_slice_in_dim(kp, q0, 3 * blk, axis=1)
        vn = lax.dynamic_slice_in_dim(vp, q0, 3 * blk, axis=1)
        qpos = q0 + jnp.arange(blk)
        kpos = q0 - blk + jnp.arange(3 * blk)
        valid = ((jnp.abs(kpos[None, :] - qpos[:, None]) <= WIN_1D)
                 & (kpos >= 0)[None, :] & (kpos < T)[None, :])
        s_loc = jnp.einsum('bqkgd,bskd->bkgqs', qn, kn) * scale
        s_loc = jnp.where(valid, s_loc, -jnp.inf)
        s_ctx = jnp.einsum('bqkgd,bksd->bkgqs', qn, ck) * scale
        p = _softmax_sink(jnp.concatenate([s_loc, s_ctx], axis=-1), sk)
        o = (jnp.einsum('bkgqs,bskd->bqkgd', p[..., :3 * blk], vn)
             + jnp.einsum('bkgqs,bksd->bqkgd', p[..., 3 * blk:], cv))
        return o.reshape(B, blk, Hq, Dh)

    o = lax.map(one, jnp.arange(nb))
    return o.swapaxes(0, 1).reshape(B, T, Hq, Dh)


def _gla_scan(q, k, v, g, s0):
    B, T, H, dk = q.shape
    dv = v.shape[-1]
    n = T // CHUNK
    q, k, v, g = [a.reshape(B, n, CHUNK, H, a.shape[-1]) for a in (q, k, v, g)]
    G = jnp.cumsum(g, axis=2)
    G_last = G[:, :, -1:]
    q_t = q * jnp.exp(G)
    k_t = k * jnp.exp(-G)
    k_end = k * jnp.exp(G_last - G)
    causal = jnp.tril(jnp.ones((CHUNK, CHUNK), bool))
    att = jnp.where(causal, jnp.einsum('bnchd,bnshd->bnhcs', q_t, k_t), 0.0)
    o_intra = jnp.einsum('bnhcs,bnshv->bnchv', att, v)
    u = jnp.einsum('bnshd,bnshv->bnhdv', k_end, v)
    decay = jnp.exp(G_last[:, :, 0])

    def step(S, inp):
        d, du = inp
        return d[..., None] * S + du, S

    S_fin, S_start = lax.scan(step, s0, (decay.swapaxes(0, 1), u.swapaxes(0, 1)))
    o_inter = jnp.einsum('bnchd,bnhdv->bnchv', q_t, S_start.swapaxes(0, 1))
    return (o_intra + o_inter).reshape(B, T, H, dv), S_fin


def _delta_scan(q, k, v, beta, g, s0):
    B, T, H, dk = q.shape
    dv = v.shape[-1]
    n = T // CHUNK
    blk = lambda a: a.reshape((B, n, CHUNK) + a.shape[2:]).swapaxes(2, 3)
    q, k, v, beta, g = [blk(a) for a in (q, k, v, beta, g)]
    G = jnp.cumsum(g, axis=-1)
    lower = jnp.tril(jnp.ones((CHUNK, CHUNK), bool))
    strict = jnp.tril(jnp.ones((CHUNK, CHUNK), bool), -1)
    diff = G[..., :, None] - G[..., None, :]
    gam = jnp.where(lower, jnp.exp(jnp.where(lower, diff, 0.0)), 0.0)
    k_beta = k * beta[..., None]
    a_mat = jnp.where(strict, jnp.einsum('bnhcd,bnhsd->bnhcs', k_beta, k) * gam, 0.0)
    m_mat = a_mat + jnp.eye(CHUNK, dtype=F32)
    rhs = jnp.concatenate([v * beta[..., None], k_beta * jnp.exp(G)[..., None]], axis=-1)
    sol = lax.linalg.triangular_solve(m_mat, rhs, left_side=True, lower=True, unit_diagonal=True)
    w_val, k_cum = sol[..., :dv], sol[..., dv:]
    a_qk = jnp.einsum('bnhcd,bnhsd->bnhcs', q, k) * gam
    q_g = q * jnp.exp(G)[..., None]
    k_end = k * jnp.exp(G[..., -1:] - G)[..., None]
    d_last = jnp.exp(G[..., -1])

    def step(S, inp):
        aqk, wv, kc, qg, ke, d = inp
        v_new = wv - jnp.einsum('bhcd,bhdv->bhcv', kc, S)
        o = jnp.einsum('bhcd,bhdv->bhcv', qg, S) + jnp.einsum('bhcs,bhsv->bhcv', aqk, v_new)
        S = S * d[..., None, None] + jnp.einsum('bhcd,bhcv->bhdv', ke, v_new)
        return S, o

    xs = tuple(a.swapaxes(0, 1) for a in (a_qk, w_val, k_cum, q_g, k_end, d_last))
    S_fin, o = lax.scan(step, s0, xs)
    return o.transpose(1, 0, 3, 2, 4).reshape(B, T, H, dv), S_fin


def _gla_mixer(bq, bk, bv, glr, br, w_g2, b_g, norm_g, s0):
    B, T, _ = bq.shape
    q = bq.reshape(B, T, H_B, DK_B) * DK_B ** -0.5
    k = bk.reshape(B, T, H_B, DK_B)
    v = bv.reshape(B, T, H_B, DV_B)
    z = jnp.einsum('btzr,zrc->btzc', glr.reshape(B, T, 2, GLA_RANK), w_g2) + b_g
    g = (jax.nn.log_sigmoid(z) / GLA_TAU).reshape(B, T, 2, H_B, DK_B)
    fl = lambda a: jnp.flip(a, 1)
    o_f, s_f = _gla_scan(q, k, v, g[:, :, 0], s0[:, 0])
    o_b, s_b = _gla_scan(fl(q), fl(k), fl(v), fl(g[:, :, 1]), s0[:, 1])
    o = o_f + fl(o_b)
    o = _rmsnorm(o, norm_g.reshape(H_B, DV_B)) * jax.nn.silu(br.reshape(B, T, H_B, DV_B))
    return o.reshape(B, T, H_B * DV_B), jnp.stack([s_f, s_b], axis=1)


def _delta_mixer(dq, dk, dv, da, db, dz, w_conv, a_log, dt_bias, norm_g, s0):
    B, T, _ = dq.shape
    qkv = jax.nn.silu(_dwconv(jnp.concatenate([dq, dk, dv], axis=-1), w_conv))
    q, k, v = _split_cols(qkv, (H_D * DK_D, H_D * DK_D, H_D * DV_D))
    q = _l2norm(q.reshape(B, T, H_D, DK_D)) * DK_D ** -0.5
    k = _l2norm(k.reshape(B, T, H_D, DK_D))
    v = v.reshape(B, T, H_D, DV_D)
    beta = jax.nn.sigmoid(db.reshape(B, T, 2, H_D))
    g = -jnp.exp(a_log) * jax.nn.softplus(da.reshape(B, T, 2, H_D) + dt_bias)
    fl = lambda a: jnp.flip(a, 1)
    o_f, s_f = _delta_scan(q, k, v, beta[:, :, 0], g[:, :, 0], s0[:, 0])
    o_b, s_b = _delta_scan(fl(q), fl(k), fl(v), fl(beta[:, :, 1]), fl(g[:, :, 1]), s0[:, 1])
    o = o_f + fl(o_b)
    o = _rmsnorm(o, norm_g) * jax.nn.silu(dz.reshape(B, T, H_D, DV_D))
    return o.reshape(B, T, H_D * DV_D), jnp.stack([s_f, s_b], axis=1)


def _even_mix(p, latent, rpb, w_g2, b_g, norm_g, ak_cache, av_cache, sb):
    B, T, _ = p.shape
    ak = p[..., H_A * HEAD_DIM:2 * H_A * HEAD_DIM]
    av = p[..., 2 * H_A * HEAD_DIM:3 * H_A * HEAD_DIM]
    n_pair = H_A // 2
    if latent:
        o_a = _nbr_attn_call(p, rpb, ak_cache, av_cache)
        k_h = v_h = None
    else:
        k_h = ak.reshape(B, T, H_A, HEAD_DIM).transpose(0, 2, 1, 3)
        v_h = av.reshape(B, T, H_A, HEAD_DIM).transpose(0, 2, 1, 3)
        o_a = _dense_attn_call(p, 0, n_pair, 2 * n_pair, None, False)
    o_b, s_b = _gla_call(p, w_g2, b_g, norm_g, sb)
    return o_a, o_b, k_h, v_h, s_b


def _odd_mix(p, latent, sink, w_conv, a_log, dt_bias, norm_g, ck_cache, cv_cache, sd):
    B, T, _ = p.shape
    ck = p[..., H_C * HEAD_DIM:(H_C + KV_C) * HEAD_DIM]
    cv = p[..., (H_C + KV_C) * HEAD_DIM:(H_C + 2 * KV_C) * HEAD_DIM]
    n_pair = H_C // 2
    if latent:
        o_c = _win_attn_call(p, sink, ck_cache, cv_cache)
        k_h = v_h = None
    else:
        k_h = ck.reshape(B, T, KV_C, HEAD_DIM).transpose(0, 2, 1, 3)
        v_h = cv.reshape(B, T, KV_C, HEAD_DIM).transpose(0, 2, 1, 3)
        o_c = _dense_attn_call(p, 0, n_pair, n_pair + 1, sink, True)
    o_d, s_d = _delta_call(p, w_conv, a_log, dt_bias, norm_g, sd)
    return o_c, o_d, k_h, v_h, s_d


def _reorder_cols(w, sizes, order):
    parts = _split_cols(w, sizes)
    out = jnp.concatenate([parts[i] for i in order], axis=-1)
    pad = -out.shape[-1] % LANES
    return jnp.pad(out, ((0, 0), (0, pad)))


def _reorder_even_w_in(w):
    return _reorder_cols(w, EV_SIZES, (0, 1, 2, 3, 4, 5, 7, 6))


def _reorder_odd_w_in(w):
    return _reorder_cols(w, OD_SIZES, (0, 1, 2, 3, 4, 5, 8, 6, 7))


OD_ORDER_SIZES = tuple(OD_SIZES[i] for i in (0, 1, 2, 3, 4, 5, 8, 6, 7)) + (-sum(OD_SIZES) % LANES,)


TM_LATENT = 512
TM_CONTEXT = 256
TF_FFN = 1408


def kernel(x_prompt, x_sample, cache_a_k, cache_a_v, state_b, cache_c_k, cache_c_v, state_d, c, c_ctx, ada_w, ada_b, norm1_g, norm2_g, ffn_up, ffn_conv, ffn_down, ev_w_in, ev_w_out, a_rpb, b_w_g2, b_b_g, b_norm_g, od_w_in, od_w_out, c_sink, d_conv, d_a_log, d_dt_bias, d_norm_g, final_g):
    Bp, Tp, D = x_prompt.shape
    Bs, Ts, _ = x_sample.shape
    cond = jnp.concatenate([c, c_ctx[None, :], jnp.zeros((2 * SUBLANES - Bs - 1, D), F32)], axis=0)
    mods = _adaln(cond, ada_w, ada_b)
    xp, xs = x_prompt, x_sample
    streams = ((False, Bp, TM_CONTEXT), (True, Bs, TM_LATENT))
    ak_l, av_l, sb_l, ck_l, cv_l, sd_l = [], [], [], [], [], []
    for l in range(DEPTH):
        j = l // 2
        mod_s = [mods[l, :Bs, i * D:(i + 1) * D] for i in range(6)]
        mod_p = [jnp.broadcast_to(mods[l, Bs:Bs + 1, i * D:(i + 1) * D], (Bp, D)) for i in range(6)]
        w_in = (_reorder_even_w_in(ev_w_in[j]) if l % 2 == 0 else _reorder_odd_w_in(od_w_in[j])).astype(BF16)
        w_out = (ev_w_out[j] if l % 2 == 0 else od_w_out[j]).astype(BF16)
        w_up = ffn_up[l].astype(BF16)
        w_down = ffn_down[l].astype(BF16)
        new_x = []
        for (latent, B, tm), x, mod in zip(streams, (xp, xs), (mod_p, mod_s)):
            sh1, sc1, g1, sh2, sc2, g2 = mod
            p = _nm_matmul(x, norm1_g[l], sh1, sc1, w_in, tm)
            if l % 2 == 0:
                s0 = state_b[:, j] if latent else jnp.zeros((B, 2, H_B, DK_B, DV_B), F32)
                o1, o2, k_h, v_h, s_new = _even_mix(p, latent, a_rpb[j], b_w_g2[j], b_b_g[j], b_norm_g[j],
                                                    cache_a_k[:, j], cache_a_v[:, j], s0)
                if not latent:
                    ak_l.append(k_h)
                    av_l.append(v_h)
                    sb_l.append(s_new)
            else:
                s0 = state_d[:, j] if latent else jnp.zeros((B, 2, H_D, DK_D, DV_D), F32)
                o1, o2, k_h, v_h, s_new = _odd_mix(p, latent, c_sink[j], d_conv[j], d_a_log[j], d_dt_bias[j],
                                                   d_norm_g[j], cache_c_k[:, j], cache_c_v[:, j], s0)
                if not latent:
                    ck_l.append(k_h)
                    cv_l.append(v_h)
                    sd_l.append(s_new)
            x = _proj_res(o1, o2, w_out, x, g1, tm)
            x = _conv_ffn_res(x, norm2_g[l], sh2, sc2, g2, w_up, ffn_conv[l], w_down, tm, TF_FFN)
            new_x.append(x)
        xp, xs = new_x
    y_prompt = _rmsnorm_call(xp, final_g, TM_CONTEXT)
    y_sample = _rmsnorm_call(xs, final_g, TM_LATENT)
    return (y_prompt, y_sample, jnp.stack(ak_l, axis=1), jnp.stack(av_l, axis=1), jnp.stack(sb_l, axis=1),
            jnp.stack(ck_l, axis=1), jnp.stack(cv_l, axis=1), jnp.stack(sd_l, axis=1))
```

```python
import functools
import math

import jax
import jax.numpy as jnp
import numpy as np
from jax import lax
from jax.experimental import pallas as pl
from jax.experimental.pallas import tpu as pltpu

D_MODEL = 1024
DEPTH = 4
GRID_W = 64
HEAD_DIM = 64
H_A = D_MODEL // (2 * HEAD_DIM)
WIN_R = 8
WIN_C = 16
H_B = D_MODEL // (2 * HEAD_DIM)
DK_B = HEAD_DIM
DV_B = HEAD_DIM
GLA_RANK = 16
GLA_TAU = 16.0
H_C = D_MODEL // (2 * HEAD_DIM)
KV_C = H_C // 4
WIN_1D = 128
ROPE_BASE = 10000.0
H_D = D_MODEL // (2 * HEAD_DIM)
DK_D = HEAD_DIM
DV_D = HEAD_DIM
SHORT_CONV = 3
CHUNK = 64
Q_BLOCK = 128
D_FF = ((8 * D_MODEL // 3 + 127) // 128) * 128
EPS = 1e-6
EV_SIZES = (H_A * HEAD_DIM, H_A * HEAD_DIM, H_A * HEAD_DIM, H_B * DK_B, H_B * DK_B, H_B * DV_B, 2 * GLA_RANK, H_B * DV_B)
OD_SIZES = (H_C * HEAD_DIM, KV_C * HEAD_DIM, KV_C * HEAD_DIM, H_D * DK_D, H_D * DK_D, H_D * DV_D, 2 * H_D, 2 * H_D, H_D * DV_D)
F32 = jnp.float32
BF16 = jnp.bfloat16

V7X_VMEM_LIMIT_BYTES = 52 * 1024 * 1024
SUBLANES = 8
LANES = 128
COL_CHUNK = 512
FFN_SUB_COLS = 1408


def _params(*sem):
    return pltpu.CompilerParams(dimension_semantics=sem, vmem_limit_bytes=V7X_VMEM_LIMIT_BYTES)


def _norm_mod(x, g, shift, scale):
    y = x * lax.rsqrt(jnp.mean(x * x, axis=-1, keepdims=True) + EPS)
    return (y * g) * (1.0 + scale) + shift


def _adaln_kernel(c_ref, w_ref, b_ref, o_ref):
    cond = c_ref[...]
    a = cond * jax.nn.sigmoid(cond)
    a_hi = a.astype(BF16)
    a_lo = (a - a_hi.astype(F32)).astype(BF16)
    w = w_ref[0]
    w_hi = w.astype(BF16)
    w_lo = (w - w_hi.astype(F32)).astype(BF16)
    acc = jnp.dot(a_hi, w_hi, preferred_element_type=F32)
    acc += jnp.dot(a_lo, w_hi, preferred_element_type=F32)
    acc += jnp.dot(a_hi, w_lo, preferred_element_type=F32)
    o_ref[0] = acc + b_ref[0]


def _adaln(cond, ada_w, ada_b):
    R = cond.shape[0]
    L, D, N = ada_w.shape
    tn = 1024
    return pl.pallas_call(
        _adaln_kernel,
        grid=(L, N // tn),
        in_specs=[pl.BlockSpec((R, D), lambda l, j: (0, 0)),
                  pl.BlockSpec((1, D, tn), lambda l, j: (l, 0, j)),
                  pl.BlockSpec((1, 1, tn), lambda l, j: (l, 0, j))],
        out_specs=pl.BlockSpec((1, R, tn), lambda l, j: (l, 0, j)),
        out_shape=jax.ShapeDtypeStruct((L, R, N), F32),
        compiler_params=_params("parallel", "parallel"),
    )(cond, ada_w, ada_b.reshape(L, 1, N))


def _nm_matmul_kernel(x_ref, g_ref, sh_ref, sc_ref, w_ref, o_ref):
    h = _norm_mod(x_ref[0], g_ref[...], sh_ref[0], sc_ref[0]).astype(BF16)
    n = o_ref.shape[-1]
    for c0 in range(0, n, COL_CHUNK):
        c1 = min(c0 + COL_CHUNK, n)
        o_ref[0, :, c0:c1] = jnp.dot(h, w_ref[:, c0:c1], preferred_element_type=F32)


def _nm_matmul(x, g, shift, scale, w, tm):
    B, T, D = x.shape
    N = w.shape[1]
    return pl.pallas_call(
        _nm_matmul_kernel,
        grid=(B, T // tm),
        in_specs=[pl.BlockSpec((1, tm, D), lambda b, i: (b, i, 0)),
                  pl.BlockSpec((1, D), lambda b, i: (0, 0)),
                  pl.BlockSpec((1, 1, D), lambda b, i: (b, 0, 0)),
                  pl.BlockSpec((1, 1, D), lambda b, i: (b, 0, 0)),
                  pl.BlockSpec((D, N), lambda b, i: (0, 0))],
        out_specs=pl.BlockSpec((1, tm, N), lambda b, i: (b, i, 0)),
        out_shape=jax.ShapeDtypeStruct((B, T, N), F32),
        compiler_params=_params("parallel", "parallel"),
    )(x, g.reshape(1, D), shift.reshape(B, 1, D), scale.reshape(B, 1, D), w)


def _proj_res_kernel(a1_ref, a2_ref, w1_ref, w2_ref, x_ref, gate_ref, o_ref):
    acc = jnp.dot(a1_ref[0].astype(BF16), w1_ref[...], preferred_element_type=F32)
    acc += jnp.dot(a2_ref[0].astype(BF16), w2_ref[...], preferred_element_type=F32)
    o_ref[0] = x_ref[0] + gate_ref[0] * acc


def _proj_res(a1, a2, w, x, gate, tm):
    B, T, K = a1.shape
    D = w.shape[1]
    return pl.pallas_call(
        _proj_res_kernel,
        grid=(B, T // tm),
        in_specs=[pl.BlockSpec((1, tm, K), lambda b, i: (b, i, 0)),
                  pl.BlockSpec((1, tm, K), lambda b, i: (b, i, 0)),
                  pl.BlockSpec((K, D), lambda b, i: (0, 0)),
                  pl.BlockSpec((K, D), lambda b, i: (1, 0)),
                  pl.BlockSpec((1, tm, D), lambda b, i: (b, i, 0)),
                  pl.BlockSpec((1, 1, D), lambda b, i: (b, 0, 0))],
        out_specs=pl.BlockSpec((1, tm, D), lambda b, i: (b, i, 0)),
        out_shape=jax.ShapeDtypeStruct((B, T, D), F32),
        compiler_params=_params("parallel", "parallel"),
    )(a1, a2, w, w, x, gate.reshape(B, 1, D))


def _ffn_kernel(x_ref, xp_ref, xn_ref, g_ref, sh_ref, sc_ref, gate_ref, wa_ref, wg_ref,
                ca_ref, cg_ref, wd_ref, o_ref, h_ref, acc_ref):
    i = pl.program_id(1)
    c = pl.program_id(2)
    tm = x_ref.shape[1]
    tf = wa_ref.shape[1]

    @pl.when(c == 0)
    def _():
        g, sh, sc = g_ref[...], sh_ref[0], sc_ref[0]
        h_ref[0:tm, :] = _norm_mod(x_ref[0], g, sh, sc).astype(BF16)
        hp = jnp.where(i > 0, _norm_mod(xp_ref[0], g, sh, sc), 0.0)
        hn = jnp.where(i < pl.num_programs(1) - 1, _norm_mod(xn_ref[0], g, sh, sc), 0.0)
        h_ref[tm:tm + 2 * SUBLANES, :] = jnp.concatenate([hp, hn], axis=0).astype(BF16)
        acc_ref[...] = jnp.zeros_like(acc_ref)

    h = h_ref[...]
    row = lax.broadcasted_iota(jnp.int32, (tm, 1), 0)

    def conv_branch(w_ref, cw_ref, c0, c1):
        u_all = jnp.dot(h, w_ref[:, c0:c1], preferred_element_type=F32)
        u = u_all[0:tm]
        u_prev = jnp.where(row == 0, u_all[tm + SUBLANES - 1:tm + SUBLANES], pltpu.roll(u, 1, axis=0))
        u_next = jnp.where(row == tm - 1, u_all[tm + SUBLANES:tm + SUBLANES + 1], pltpu.roll(u, tm - 1, axis=0))
        cw = cw_ref[:, c0:c1]
        return u_prev * cw[0:1] + u * cw[1:2] + u_next * cw[2:3]

    for c0 in range(0, tf, FFN_SUB_COLS):
        c1 = min(c0 + FFN_SUB_COLS, tf)
        a = conv_branch(wa_ref, ca_ref, c0, c1)
        gt = conv_branch(wg_ref, cg_ref, c0, c1)
        act = a * (gt * jax.nn.sigmoid(gt))
        acc_ref[...] += jnp.dot(act.astype(BF16), wd_ref[c0:c1, :], preferred_element_type=F32)

    @pl.when(c == pl.num_programs(2) - 1)
    def _():
        o_ref[0] = x_ref[0] + gate_ref[0] * acc_ref[...]


def _conv_ffn_res(x, g, shift, scale, gate, w_up, w_conv, w_down, tm, tf):
    B, T, D = x.shape
    F = w_down.shape[0]
    nf = F // tf
    nt = T // tm
    rows8 = tm // SUBLANES
    vec = lambda: pl.BlockSpec((1, 1, D), lambda b, i, c: (b, 0, 0))
    return pl.pallas_call(
        _ffn_kernel,
        grid=(B, nt, nf),
        in_specs=[pl.BlockSpec((1, tm, D), lambda b, i, c: (b, i, 0)),
                  pl.BlockSpec((1, SUBLANES, D), lambda b, i, c: (b, jnp.maximum(i * rows8 - 1, 0), 0)),
                  pl.BlockSpec((1, SUBLANES, D), lambda b, i, c: (b, jnp.minimum((i + 1) * rows8, T // SUBLANES - 1), 0)),
                  pl.BlockSpec((1, D), lambda b, i, c: (0, 0)),
                  vec(), vec(), vec(),
                  pl.BlockSpec((D, tf), lambda b, i, c: (0, c)),
                  pl.BlockSpec((D, tf), lambda b, i, c: (0, nf + c)),
                  pl.BlockSpec((3, tf), lambda b, i, c: (0, c)),
                  pl.BlockSpec((3, tf), lambda b, i, c: (0, nf + c)),
                  pl.BlockSpec((tf, D), lambda b, i, c: (c, 0))],
        out_specs=pl.BlockSpec((1, tm, D), lambda b, i, c: (b, i, 0)),
        out_shape=jax.ShapeDtypeStruct((B, T, D), F32),
        scratch_shapes=[pltpu.VMEM((tm + 2 * SUBLANES, D), BF16),
                        pltpu.VMEM((tm, D), F32)],
        compiler_params=_params("parallel", "parallel", "arbitrary"),
    )(x, x, x, g.reshape(1, D), shift.reshape(B, 1, D), scale.reshape(B, 1, D), gate.reshape(B, 1, D),
      w_up, w_up, w_conv, w_conv, w_down)


def _rmsnorm_kernel(x_ref, g_ref, o_ref):
    x = x_ref[0]
    o_ref[0] = (x * lax.rsqrt(jnp.mean(x * x, axis=-1, keepdims=True) + EPS)) * g_ref[...]


def _rmsnorm_call(x, g, tm):
    B, T, D = x.shape
    return pl.pallas_call(
        _rmsnorm_kernel,
        grid=(B, T // tm),
        in_specs=[pl.BlockSpec((1, tm, D), lambda b, i: (b, i, 0)),
                  pl.BlockSpec((1, D), lambda b, i: (0, 0))],
        out_specs=pl.BlockSpec((1, tm, D), lambda b, i: (b, i, 0)),
        out_shape=jax.ShapeDtypeStruct((B, T, D), F32),
        compiler_params=_params("parallel", "parallel"),
    )(x, g.reshape(1, D))


NEG_BIG = -1e30


def _dot_nt(a, b):
    return lax.dot_general(a, b, (((1,), (1,)), ((), ())), preferred_element_type=F32)


def _lane_group():
    return lax.broadcasted_iota(jnp.int32, (1, LANES), 1) // HEAD_DIM


def _attend(score_fns, values, sink):
    scores = [fn() for fn in score_fns]
    yield
    s = scores[0] if len(scores) == 1 else jnp.concatenate(scores, axis=1)
    m = jnp.max(s, axis=-1, keepdims=True)
    if sink is not None:
        m = jnp.maximum(m, sink)
    e = jnp.exp(s - m)
    den = jnp.sum(e, axis=-1, keepdims=True)
    if sink is not None:
        den = den + jnp.exp(sink - m)
    eb = e.astype(BF16)
    out = None
    c0 = 0
    for v in values:
        o = jnp.dot(eb[:, c0:c0 + v.shape[0]], v, preferred_element_type=F32)
        out = o if out is None else out + o
        c0 += v.shape[0]
    yield
    return out / den


def _merge_heads(outs, grp):
    return jnp.where(grp == 0, outs[0], outs[1])


def _dense_attn_kernel(q_ref, k_ref, v_ref, sink_ref, o_ref, *, gqa, use_sink):
    grp = _lane_group()
    q = q_ref[0]
    k = k_ref[0]
    v = v_ref[0]
    if gqa:
        kvg = pl.program_id(1) // 2
        k = jnp.where(grp == kvg, k, pltpu.roll(k, HEAD_DIM, axis=1))
        v = jnp.where(grp == kvg, v, pltpu.roll(v, HEAD_DIM, axis=1))
    kb = k.astype(BF16)
    vb = v.astype(BF16)

    def head(h):
        qh = jnp.where(grp == h, q, 0.0).astype(BF16)
        sink = sink_ref[0, h][:, :1] if use_sink else None
        return _attend([lambda: _dot_nt(qh, kb) * (HEAD_DIM ** -0.5)], [vb], sink)

    o_ref[0] = _merge_heads(_lockstep([head(0), head(1)]), grp)


def _dense_attn_call(p, q_blk, k_blk, v_blk, sink, gqa):
    B, T, _ = p.shape
    n_pair = H_A // 2
    use_sink = sink is not None
    if use_sink:
        sink_arr = jnp.broadcast_to(sink.reshape(n_pair, 2, 1, 1), (n_pair, 2, 1, LANES))
    else:
        sink_arr = jnp.zeros((n_pair, 2, 1, LANES), F32)
    kv_idx = (lambda hp: 0) if gqa else (lambda hp: hp)
    return pl.pallas_call(
        functools.partial(_dense_attn_kernel, gqa=gqa, use_sink=use_sink),
        grid=(B, n_pair),
        in_specs=[pl.BlockSpec((1, T, LANES), lambda b, hp: (b, 0, q_blk + hp)),
                  pl.BlockSpec((1, T, LANES), lambda b, hp: (b, 0, k_blk + kv_idx(hp))),
                  pl.BlockSpec((1, T, LANES), lambda b, hp: (b, 0, v_blk + kv_idx(hp))),
                  pl.BlockSpec((1, 2, 1, LANES), lambda b, hp: (hp, 0, 0, 0))],
        out_specs=pl.BlockSpec((1, T, LANES), lambda b, hp: (b, 0, hp)),
        out_shape=jax.ShapeDtypeStruct((B, T, n_pair * LANES), F32),
        compiler_params=_params("parallel", "parallel"),
    )(p, p, p, sink_arr)


NBR_SPAN = WIN_R * GRID_W
ATTN_BLOCKS_PER_STEP = 4


def _nbr_attn_kernel(q_ref, k_ref, v_ref, ck_ref, cv_ref, bias_ref, o_ref, kb_ref, vb_ref):
    kb_ref[...] = k_ref[0].astype(BF16)
    vb_ref[...] = v_ref[0].astype(BF16)
    ckb = ck_ref[0].astype(BF16)
    cvb = cv_ref[0].astype(BF16)
    grp = _lane_group()
    n_rows = q_ref.shape[1] // GRID_W

    def head(i, h):
        r0 = jnp.clip(i - WIN_R // 2, 0, n_rows - WIN_R)
        q = q_ref[0, pl.ds(pl.multiple_of(i * GRID_W, GRID_W), GRID_W), :]
        start = pl.multiple_of(r0 * GRID_W, GRID_W)
        kw = kb_ref[pl.ds(start, NBR_SPAN), :]
        vw = vb_ref[pl.ds(start, NBR_SPAN), :]
        qh = jnp.where(grp == h, q, 0.0).astype(BF16)
        return _attend([lambda: _dot_nt(qh, kw) * (HEAD_DIM ** -0.5) + bias_ref[i - r0, h],
                        lambda: _dot_nt(qh, ckb) * (HEAD_DIM ** -0.5)], [vw, cvb], None)

    def rows(step, carry):
        ids = [step * ATTN_BLOCKS_PER_STEP + j for j in range(ATTN_BLOCKS_PER_STEP)]
        outs = _lockstep([head(i, h) for i in ids for h in range(2)])
        for j, i in enumerate(ids):
            o_ref[0, pl.ds(pl.multiple_of(i * GRID_W, GRID_W), GRID_W), :] = _merge_heads(outs[2 * j:2 * j + 2], grp)
        return carry

    lax.fori_loop(0, n_rows // ATTN_BLOCKS_PER_STEP, rows, 0)


def _nbr_bias_table(rpb):
    var = np.arange(WIN_R)[:, None, None]
    r = np.arange(WIN_R)[None, :, None]
    row_sel = (r - var + (WIN_R - 1) == np.arange(2 * WIN_R - 1)[None, None, :]).astype(np.float32)
    w = np.arange(GRID_W)[:, None, None]
    cc = np.arange(GRID_W)[None, :, None]
    c0 = np.clip(w - WIN_C // 2, 0, GRID_W - WIN_C)
    valid = (cc >= c0) & (cc < c0 + WIN_C)
    col_sel = (valid & (cc - w + (WIN_C - 1) == np.arange(2 * WIN_C - 1)[None, None, :])).astype(np.float32)
    t = jnp.einsum('vra,hab->vhrb', row_sel, rpb, precision=lax.Precision.HIGHEST)
    tbl = jnp.einsum('vhrb,wcb->vhwrc', t, col_sel, precision=lax.Precision.HIGHEST)
    tbl = jnp.where(valid[None, None, :, None, :, 0], tbl, NEG_BIG)
    return tbl.reshape(WIN_R, rpb.shape[0], GRID_W, NBR_SPAN)


def _nbr_attn_call(p, rpb, ck_cache, cv_cache):
    B, T, _ = p.shape
    n_pair = H_A // 2
    L = ck_cache.shape[2]
    ckp = ck_cache.transpose(0, 2, 1, 3).reshape(B, L, H_A * HEAD_DIM)
    cvp = cv_cache.transpose(0, 2, 1, 3).reshape(B, L, H_A * HEAD_DIM)
    bias = _nbr_bias_table(rpb)
    col = lambda off: pl.BlockSpec((1, T, LANES), lambda b, hp: (b, 0, off + hp))
    ctx = lambda: pl.BlockSpec((1, L, LANES), lambda b, hp: (b, 0, hp))
    return pl.pallas_call(
        _nbr_attn_kernel,
        grid=(B, n_pair),
        in_specs=[col(0), col(n_pair), col(2 * n_pair), ctx(), ctx(),
                  pl.BlockSpec((WIN_R, 2, GRID_W, NBR_SPAN), lambda b, hp: (0, hp, 0, 0))],
        out_specs=pl.BlockSpec((1, T, LANES), lambda b, hp: (b, 0, hp)),
        out_shape=jax.ShapeDtypeStruct((B, T, n_pair * LANES), F32),
        scratch_shapes=[pltpu.VMEM((T, LANES), BF16), pltpu.VMEM((T, LANES), BF16)],
        compiler_params=_params("parallel", "parallel"),
    )(p, p, p, ckp, cvp, bias)


WIN_SPAN = 3 * Q_BLOCK
WIN_Q_TILE = 64
WIN_TILES_PER_STEP = 4
ROPE_SWAP = HEAD_DIM // 4


def _rope(x, cos, sin_signed):
    lane = lax.broadcasted_iota(jnp.int32, (1, LANES), 1)
    first = lane % (2 * ROPE_SWAP) < ROPE_SWAP
    partner = jnp.where(first, pltpu.roll(x, LANES - ROPE_SWAP, axis=1), pltpu.roll(x, ROPE_SWAP, axis=1))
    return x * cos + partner * sin_signed


def _win_attn_kernel(q_ref, k_ref, v_ref, ck_ref, cv_ref, cos_ref, sin_ref, sink_ref, o_ref, kb_ref, vb_ref):
    grp = _lane_group()
    kvg = pl.program_id(1) // 2
    both = lambda a: jnp.where(grp == kvg, a, pltpu.roll(a, HEAD_DIM, axis=1))
    kb_ref[...] = _rope(both(k_ref[0]), cos_ref[...], sin_ref[...]).astype(BF16)
    vb_ref[...] = both(v_ref[0]).astype(BF16)
    ckb = both(ck_ref[0]).astype(BF16)
    cvb = both(cv_ref[0]).astype(BF16)
    T = q_ref.shape[1]
    rel = (lax.broadcasted_iota(jnp.int32, (WIN_Q_TILE, WIN_SPAN), 1)
           - lax.broadcasted_iota(jnp.int32, (WIN_Q_TILE, WIN_SPAN), 0))

    def heads(n):
        q0 = pl.multiple_of(n * WIN_Q_TILE, WIN_Q_TILE)
        start = pl.multiple_of(jnp.clip(q0 - WIN_1D, 0, T - WIN_SPAN), WIN_Q_TILE)
        q = _rope(q_ref[0, pl.ds(q0, WIN_Q_TILE), :], cos_ref[pl.ds(q0, WIN_Q_TILE), :],
                  sin_ref[pl.ds(q0, WIN_Q_TILE), :])
        kw = kb_ref[pl.ds(start, WIN_SPAN), :]
        vw = vb_ref[pl.ds(start, WIN_SPAN), :]
        valid = jnp.abs(rel + (start - q0)) <= WIN_1D

        def head(h):
            qh = jnp.where(grp == h, q, 0.0).astype(BF16)
            return _attend([lambda: jnp.where(valid, _dot_nt(qh, kw) * (HEAD_DIM ** -0.5), NEG_BIG),
                            lambda: _dot_nt(qh, ckb) * (HEAD_DIM ** -0.5)], [vw, cvb], sink_ref[0, h][:, :1])

        return [head(0), head(1)]

    def tiles(step, carry):
        ids = [step * WIN_TILES_PER_STEP + j for j in range(WIN_TILES_PER_STEP)]
        outs = _lockstep([prog for n in ids for prog in heads(n)])
        for j, n in enumerate(ids):
            o_ref[0, pl.ds(pl.multiple_of(n * WIN_Q_TILE, WIN_Q_TILE), WIN_Q_TILE), :] = _merge_heads(
                outs[2 * j:2 * j + 2], grp)
        return carry

    lax.fori_loop(0, T // WIN_Q_TILE // WIN_TILES_PER_STEP, tiles, 0)


def _rope_tables(T):
    t = jnp.arange(T)
    quarter = HEAD_DIM // 4
    inv = 1.0 / (ROPE_BASE ** (jnp.arange(quarter, dtype=F32) / quarter))
    ang_r = (t // GRID_W).astype(F32)[:, None] * inv[None, :]
    ang_c = (t % GRID_W).astype(F32)[:, None] * inv[None, :]
    cos = jnp.concatenate([jnp.cos(ang_r), jnp.cos(ang_r), jnp.cos(ang_c), jnp.cos(ang_c)], axis=-1)
    sin = jnp.concatenate([-jnp.sin(ang_r), jnp.sin(ang_r), -jnp.sin(ang_c), jnp.sin(ang_c)], axis=-1)
    return jnp.tile(cos, (1, 2)), jnp.tile(sin, (1, 2))


def _win_attn_call(p, sink, ck_cache, cv_cache):
    B, T, _ = p.shape
    n_pair = H_C // 2
    L = ck_cache.shape[2]
    ckp = ck_cache.transpose(0, 2, 1, 3).reshape(B, L, KV_C * HEAD_DIM)
    cvp = cv_cache.transpose(0, 2, 1, 3).reshape(B, L, KV_C * HEAD_DIM)
    cos, sin = _rope_tables(T)
    sink_arr = jnp.broadcast_to(sink.reshape(n_pair, 2, 1, 1), (n_pair, 2, 1, LANES))
    col = lambda blk: pl.BlockSpec((1, T, LANES), lambda b, hp: (b, 0, blk))
    ctx = lambda: pl.BlockSpec((1, L, LANES), lambda b, hp: (b, 0, 0))
    tab = lambda: pl.BlockSpec((T, LANES), lambda b, hp: (0, 0))
    return pl.pallas_call(
        _win_attn_kernel,
        grid=(B, n_pair),
        in_specs=[pl.BlockSpec((1, T, LANES), lambda b, hp: (b, 0, hp)), col(n_pair), col(n_pair + 1),
                  ctx(), ctx(), tab(), tab(),
                  pl.BlockSpec((1, 2, 1, LANES), lambda b, hp: (hp, 0, 0, 0))],
        out_specs=pl.BlockSpec((1, T, LANES), lambda b, hp: (b, 0, hp)),
        out_shape=jax.ShapeDtypeStruct((B, T, n_pair * LANES), F32),
        scratch_shapes=[pltpu.VMEM((T, LANES), BF16), pltpu.VMEM((T, LANES), BF16)],
        compiler_params=_params("parallel", "parallel"),
    )(p, p, p, ckp, cvp, cos, sin, sink_arr)


def _dot_tn(a, b):
    return lax.dot_general(a, b, (((0,), (0,)), ((), ())), preferred_element_type=F32)


def _split_bf16(x):
    hi = x.astype(BF16)
    return hi, (x - hi.astype(F32)).astype(BF16)


GLA_CHUNKS_PER_STEP = 4
DELTA_CHUNKS_PER_STEP = 4


def _lockstep(programs):
    results = [None] * len(programs)
    live = list(range(len(programs)))
    while live:
        for idx in list(live):
            try:
                next(programs[idx])
            except StopIteration as stop:
                results[idx] = stop.value
                live.remove(idx)
    return results


def _block_diag_mask():
    return (lax.broadcasted_iota(jnp.int32, (LANES, 1), 0) // HEAD_DIM) == _lane_group()


def _head_rmsnorm(o, grp):
    sq = o * o
    s0 = jnp.sum(jnp.where(grp == 0, sq, 0.0), axis=-1, keepdims=True)
    s1 = jnp.sum(jnp.where(grp == 1, sq, 0.0), axis=-1, keepdims=True)
    ms = jnp.where(grp == 0, s0, s1) * (1.0 / HEAD_DIM)
    return o * lax.rsqrt(ms + EPS)


def _gla_kernel(q_ref, k_ref, v_ref, r_ref, glr_ref, w2_ref, bg_ref, ng_ref, s0_ref, o_ref, sfin_ref,
                ob_ref, st_ref):
    T = q_ref.shape[1]
    n = T // CHUNK
    grp = _lane_group()
    bd = _block_diag_mask()
    ri = lax.broadcasted_iota(jnp.int32, (CHUNK, CHUNK), 0)
    ci = lax.broadcasted_iota(jnp.int32, (CHUNK, CHUNK), 1)
    ones_tok = jnp.ones((CHUNK, LANES), BF16)
    st_ref[...] = s0_ref[0, :, 0]

    def chunk(c0, d):
        sl = pl.ds(c0, CHUNK)
        q = q_ref[0, sl, :] * (DK_B ** -0.5)
        k = k_ref[0, sl, :]
        vb = v_ref[0, sl, :].astype(BF16)
        z = jnp.dot(glr_ref[0, sl, :].astype(BF16), w2_ref[d], preferred_element_type=F32) + bg_ref[d]
        yield
        g = (jnp.minimum(z, 0.0) - jnp.log(1.0 + jnp.exp(-jnp.abs(z)))) * (1.0 / GLA_TAU)
        g_hi, g_lo = _split_bf16(g)
        mask = (ri <= ci) if d else (ri >= ci)
        tri = jnp.where(mask, 1.0, 0.0).astype(BF16)
        G = jnp.dot(tri, g_hi, preferred_element_type=F32) + jnp.dot(tri, g_lo, preferred_element_type=F32)
        g_tot_rows = _dot_tn(g_hi, ones_tok) + _dot_tn(g_lo, ones_tok)
        yield
        g_tot = G[0:1] if d else G[CHUNK - 1:CHUNK]
        q_t = q * jnp.exp(G)
        k_t = (k * jnp.exp(-G)).astype(BF16)
        k_e = (k * jnp.exp(g_tot - G)).astype(BF16)
        S = st_ref[d]
        att = [jnp.where(mask, _dot_nt(jnp.where(grp == h, q_t, 0.0).astype(BF16), k_t), 0.0) for h in range(2)]
        o = jnp.dot(q_t.astype(BF16), S.astype(BF16), preferred_element_type=F32)
        st_ref[d] = jnp.exp(g_tot_rows) * S + jnp.where(bd, _dot_tn(k_e, vb), 0.0)
        yield
        for h in range(2):
            o = o + jnp.where(grp == h, jnp.dot(att[h].astype(BF16), vb, preferred_element_type=F32), 0.0)
        return o

    def body(i, carry):
        c_f = [pl.multiple_of((i * GLA_CHUNKS_PER_STEP + j) * CHUNK, CHUNK) for j in range(GLA_CHUNKS_PER_STEP)]
        c_b = [pl.multiple_of((n - 1 - i * GLA_CHUNKS_PER_STEP - j) * CHUNK, CHUNK)
               for j in range(GLA_CHUNKS_PER_STEP)]
        outs = _lockstep([chunk(c0, 0) for c0 in c_f] + [chunk(c0, 1) for c0 in c_b])
        for j in range(GLA_CHUNKS_PER_STEP):
            o_ref[0, pl.ds(c_f[j], CHUNK), :] = outs[j]
            ob_ref[pl.ds(c_b[j], CHUNK), :] = outs[GLA_CHUNKS_PER_STEP + j]
        return carry

    lax.fori_loop(0, n // GLA_CHUNKS_PER_STEP, body, 0)
    sfin_ref[0, :, 0] = st_ref[...]

    def gate(i, carry):
        sl = pl.ds(pl.multiple_of(i * CHUNK, CHUNK), CHUNK)
        r = r_ref[0, sl, :]
        y = _head_rmsnorm(o_ref[0, sl, :] + ob_ref[sl, :], grp) * ng_ref[...]
        o_ref[0, sl, :] = y * (r * jax.nn.sigmoid(r))
        return carry

    lax.fori_loop(0, n, gate, 0)


def _pair_block_diag(s):
    B, _, H, dk, dv = s.shape
    s = s.reshape(B, 2, H // 2, 2, dk, dv)
    z = jnp.zeros_like(s[:, :, :, 0])
    top = jnp.concatenate([s[:, :, :, 0], z], axis=-1)
    bot = jnp.concatenate([z, s[:, :, :, 1]], axis=-1)
    return jnp.concatenate([top, bot], axis=-2)


def _pair_block_diag_inv(sp):
    a = sp[..., :HEAD_DIM, :HEAD_DIM]
    b = sp[..., HEAD_DIM:, HEAD_DIM:]
    B, _, P = sp.shape[:3]
    return jnp.stack([a, b], axis=3).reshape(B, 2, 2 * P, HEAD_DIM, HEAD_DIM)


EV_BQ, EV_BK, EV_BV, EV_BR, EV_GLR = 12, 16, 20, 24, 28


def _gla_call(p, w_g2, b_g, norm_g, s0):
    B, T, _ = p.shape
    n_pair = H_B // 2
    w2 = jnp.zeros((2, LANES, H_B * DK_B), F32)
    w2 = w2.at[0, :GLA_RANK].set(w_g2[0]).at[1, GLA_RANK:2 * GLA_RANK].set(w_g2[1]).astype(BF16)
    col = lambda off: pl.BlockSpec((1, T, LANES), lambda b, hp: (b, 0, off + hp))
    st = lambda: pl.BlockSpec((1, 2, 1, LANES, LANES), lambda b, hp: (b, 0, hp, 0, 0))
    o, s_fin = pl.pallas_call(
        _gla_kernel,
        grid=(B, n_pair),
        in_specs=[col(EV_BQ), col(EV_BK), col(EV_BV), col(EV_BR),
                  pl.BlockSpec((1, T, LANES), lambda b, hp: (b, 0, EV_GLR)),
                  pl.BlockSpec((2, LANES, LANES), lambda b, hp: (0, 0, hp)),
                  pl.BlockSpec((2, 1, LANES), lambda b, hp: (0, 0, hp)),
                  pl.BlockSpec((1, LANES), lambda b, hp: (0, hp)),
                  st()],
        out_specs=[pl.BlockSpec((1, T, LANES), lambda b, hp: (b, 0, hp)), st()],
        out_shape=[jax.ShapeDtypeStruct((B, T, n_pair * LANES), F32),
                   jax.ShapeDtypeStruct((B, 2, n_pair, LANES, LANES), F32)],
        scratch_shapes=[pltpu.VMEM((T, LANES), F32), pltpu.VMEM((2, LANES, LANES), F32)],
        compiler_params=_params("parallel", "parallel"),
    )(p, p, p, p, p, w2, b_g.reshape(2, 1, H_B * DK_B), norm_g.reshape(1, H_B * DV_B), _pair_block_diag(s0))
    return o, _pair_block_diag_inv(s_fin)


SOLVE_PASSES = 3
RHS_PASSES = 1


def _mm(a, b, passes):
    if passes == 1:
        return jnp.dot(a.astype(BF16), b.astype(BF16), preferred_element_type=F32)
    a_hi, a_lo = _split_bf16(a)
    b_hi, b_lo = _split_bf16(b)
    return (jnp.dot(a_hi, b_hi, preferred_element_type=F32) + jnp.dot(a_lo, b_hi, preferred_element_type=F32)
            + jnp.dot(a_hi, b_lo, preferred_element_type=F32))


def _unit_tri_inverse(a, eye):
    b = -a
    p = eye + b
    for _ in range(5):
        b = _mm(b, b, SOLVE_PASSES)
        p = p + _mm(p, b, SOLVE_PASSES)
    return p


def _softplus(x):
    return jnp.maximum(x, 0.0) + jnp.log(1.0 + jnp.exp(-jnp.abs(x)))


def _delta_kernel(q_ref, k_ref, v_ref, z_ref, dab_ref, cq_ref, ck_ref, cv_ref, nega_ref, dtb_ref, ng_ref, s0_ref,
                  o_ref, sfin_ref, qs_ref, ks_ref, vs_ref, ob_ref, st_ref):
    T = q_ref.shape[1]
    n = T // CHUNK
    hp = pl.program_id(1)
    grp = _lane_group()
    lane = lax.broadcasted_iota(jnp.int32, (1, LANES), 1)
    subl = lax.broadcasted_iota(jnp.int32, (LANES, 1), 0)
    bd = _block_diag_mask()
    ri = lax.broadcasted_iota(jnp.int32, (CHUNK, CHUNK), 0)
    ci = lax.broadcasted_iota(jnp.int32, (CHUNK, CHUNK), 1)
    eye = jnp.where(ri == ci, 1.0, 0.0)
    row = lax.broadcasted_iota(jnp.int32, (CHUNK, 1), 0)
    st_ref[...] = s0_ref[0, :, 0]

    def prep(i, carry):
        c0 = pl.multiple_of(i * CHUNK, CHUNK)
        lo = pl.multiple_of(jnp.maximum(c0 - SUBLANES, 0), SUBLANES)
        hi = pl.multiple_of(jnp.minimum(c0 + CHUNK, T - SUBLANES), SUBLANES)
        for src, cw_ref, dst, scale in ((q_ref, cq_ref, qs_ref, DK_D ** -0.5), (k_ref, ck_ref, ks_ref, 1.0),
                                        (v_ref, cv_ref, vs_ref, None)):
            x = src[0, pl.ds(c0, CHUNK), :]
            before = jnp.where(i > 0, src[0, pl.ds(lo, SUBLANES), :][SUBLANES - 1:], 0.0)
            after = jnp.where(i < n - 1, src[0, pl.ds(hi, SUBLANES), :][:1], 0.0)
            x_prev = jnp.where(row == 0, before, pltpu.roll(x, 1, axis=0))
            x_next = jnp.where(row == CHUNK - 1, after, pltpu.roll(x, CHUNK - 1, axis=0))
            cw = cw_ref[...]
            y = x_prev * cw[0:1] + x * cw[1:2] + x_next * cw[2:3]
            y = y * jax.nn.sigmoid(y)
            if scale is not None:
                sq = y * y
                s0 = jnp.sum(jnp.where(grp == 0, sq, 0.0), axis=-1, keepdims=True)
                s1 = jnp.sum(jnp.where(grp == 1, sq, 0.0), axis=-1, keepdims=True)
                y = y * lax.rsqrt(jnp.where(grp == 0, s0, s1) + EPS) * scale
            dst[pl.ds(c0, CHUNK), :] = y
        return carry

    lax.fori_loop(0, n, prep, 0)

    def chunk(c0, d):
        sl = pl.ds(c0, CHUNK)
        q = qs_ref[sl, :]
        k = ks_ref[sl, :]
        v = vs_ref[sl, :]
        ab = dab_ref[0, sl, :]
        gb = jnp.where(lane < 2 * H_D, nega_ref[...] * _softplus(ab + dtb_ref[...]), jax.nn.sigmoid(ab))
        gb_t = gb.T
        incl = (ri <= ci) if d else (ri >= ci)
        incl_t = (ri >= ci) if d else (ri <= ci)
        strict = (ri < ci) if d else (ri > ci)

        def head(h):
            jg = d * H_D + 2 * hp + h
            g_c = jnp.sum(jnp.where(lane == jg, gb, 0.0), axis=1, keepdims=True)
            beta_c = jnp.sum(jnp.where(lane == 2 * H_D + jg, gb, 0.0), axis=1, keepdims=True)
            g_r = jnp.sum(jnp.where(subl == jg, gb_t, 0.0), axis=0, keepdims=True)
            G_c = jnp.sum(jnp.where(incl, g_r, 0.0), axis=1, keepdims=True)
            G_r = jnp.sum(jnp.where(incl_t, g_c, 0.0), axis=0, keepdims=True)
            tot = jnp.sum(g_r, axis=1, keepdims=True)
            gam = jnp.where(incl, jnp.exp(jnp.where(incl, G_c - G_r, 0.0)), 0.0)
            kh = jnp.where(grp == h, k, 0.0)
            qh = jnp.where(grp == h, q, 0.0)
            vh = jnp.where(grp == h, v, 0.0)
            kb = kh * beta_c
            khb = kh.astype(BF16)
            a = jnp.where(strict, _dot_nt(kb.astype(BF16), khb) * gam, 0.0)
            a_qk = (_dot_nt(qh.astype(BF16), khb) * gam).astype(BF16)
            yield
            wide = jnp.concatenate([-a, eye], axis=1)
            for _ in range(6):
                w_hi, w_lo = _split_bf16(wide)
                prod = (jnp.dot(w_hi[:, :CHUNK], w_hi, preferred_element_type=F32)
                        + jnp.dot(w_lo[:, :CHUNK], w_hi, preferred_element_type=F32)
                        + jnp.dot(w_hi[:, :CHUNK], w_lo, preferred_element_type=F32))
                wide = prod + jnp.where(lane >= CHUNK, wide, 0.0)
                yield
            t_inv = wide[:, CHUNK:]
            e_g = jnp.exp(G_c)
            rhs = vh * beta_c + pltpu.roll(kb * e_g, HEAD_DIM, axis=1)
            return _mm(t_inv, rhs, RHS_PASSES), a_qk, qh * e_g, kh * jnp.exp(tot - G_c), jnp.exp(tot)

        return head

    def scan_step(d, heads):
        (r0, aqk0, qg0, ke0, d0), (r1, aqk1, qg1, ke1, d1) = heads
        S = st_ref[d]
        Sb = S.astype(BF16)
        w_val = jnp.where(grp == 0, r0, r1)
        k_cum = pltpu.roll(jnp.where(grp == 0, r1, r0), HEAD_DIM, axis=1)
        v_new = w_val - jnp.dot(k_cum.astype(BF16), Sb, preferred_element_type=F32)
        o = jnp.dot((qg0 + qg1).astype(BF16), Sb, preferred_element_type=F32)
        yield
        vnb = v_new.astype(BF16)
        o = o + jnp.where(grp == 0, jnp.dot(aqk0, vnb, preferred_element_type=F32), 0.0)
        o = o + jnp.where(grp == 1, jnp.dot(aqk1, vnb, preferred_element_type=F32), 0.0)
        d_rows = jnp.where(subl // HEAD_DIM == 0, d0, d1)
        st_ref[d] = S * d_rows + jnp.where(bd, _dot_tn((ke0 + ke1).astype(BF16), vnb), 0.0)
        return o

    def scan_dir(d, heads):
        outs = []
        for j in range(DELTA_CHUNKS_PER_STEP):
            o = yield from scan_step(d, heads[2 * j:2 * j + 2])
            outs.append(o)
        return outs

    def body(i, carry):
        c_f = [pl.multiple_of((i * DELTA_CHUNKS_PER_STEP + j) * CHUNK, CHUNK) for j in range(DELTA_CHUNKS_PER_STEP)]
        c_b = [pl.multiple_of((n - 1 - i * DELTA_CHUNKS_PER_STEP - j) * CHUNK, CHUNK)
               for j in range(DELTA_CHUNKS_PER_STEP)]
        programs = []
        for d, starts in ((0, c_f), (1, c_b)):
            for c0 in starts:
                head = chunk(c0, d)
                programs += [head(0), head(1)]
        heads = _lockstep(programs)
        half = 2 * DELTA_CHUNKS_PER_STEP
        o_f, o_b = _lockstep([scan_dir(0, heads[:half]), scan_dir(1, heads[half:])])
        for j in range(DELTA_CHUNKS_PER_STEP):
            o_ref[0, pl.ds(c_f[j], CHUNK), :] = o_f[j]
            ob_ref[pl.ds(c_b[j], CHUNK), :] = o_b[j]
        return carry

    lax.fori_loop(0, n // DELTA_CHUNKS_PER_STEP, body, 0)
    sfin_ref[0, :, 0] = st_ref[...]

    def gate(i, carry):
        sl = pl.ds(pl.multiple_of(i * CHUNK, CHUNK), CHUNK)
        z = z_ref[0, sl, :]
        y = _head_rmsnorm(o_ref[0, sl, :] + ob_ref[sl, :], grp) * ng_ref[...]
        o_ref[0, sl, :] = y * (z * jax.nn.sigmoid(z))
        return carry

    lax.fori_loop(0, n, gate, 0)


OD_DQ, OD_DK, OD_DV, OD_DZ, OD_DAB = 6, 10, 14, 18, 22


def _delta_call(p, w_conv, a_log, dt_bias, norm_g, s0):
    B, T, _ = p.shape
    n_pair = H_D // 2
    nega = jnp.zeros((1, LANES), F32).at[0, :2 * H_D].set(-jnp.exp(a_log.reshape(-1)))
    dtb = jnp.zeros((1, LANES), F32).at[0, :2 * H_D].set(dt_bias.reshape(-1))
    ng = jnp.tile(norm_g, 2).reshape(1, LANES)
    col = lambda off: pl.BlockSpec((1, T, LANES), lambda b, hp: (b, 0, off + hp))
    cw = lambda off: pl.BlockSpec((SHORT_CONV, LANES), lambda b, hp: (0, off + hp))
    vec = lambda: pl.BlockSpec((1, LANES), lambda b, hp: (0, 0))
    st = lambda: pl.BlockSpec((1, 2, 1, LANES, LANES), lambda b, hp: (b, 0, hp, 0, 0))
    o, s_fin = pl.pallas_call(
        _delta_kernel,
        grid=(B, n_pair),
        in_specs=[col(OD_DQ), col(OD_DK), col(OD_DV), col(OD_DZ),
                  pl.BlockSpec((1, T, LANES), lambda b, hp: (b, 0, OD_DAB)),
                  cw(0), cw(n_pair), cw(2 * n_pair), vec(), vec(), vec(), st()],
        out_specs=[pl.BlockSpec((1, T, LANES), lambda b, hp: (b, 0, hp)), st()],
        out_shape=[jax.ShapeDtypeStruct((B, T, n_pair * LANES), F32),
                   jax.ShapeDtypeStruct((B, 2, n_pair, LANES, LANES), F32)],
        scratch_shapes=[pltpu.VMEM((T, LANES), F32), pltpu.VMEM((T, LANES), F32), pltpu.VMEM((T, LANES), F32),
                        pltpu.VMEM((T, LANES), F32), pltpu.VMEM((2, LANES, LANES), F32)],
        compiler_params=_params("parallel", "parallel"),
    )(p, p, p, p, p, w_conv, w_conv, w_conv, nega, dtb, ng, _pair_block_diag(s0))
    return o, _pair_block_diag_inv(s_fin)


def _rmsnorm(x, g):
    y = x * lax.rsqrt(jnp.mean(x * x, axis=-1, keepdims=True) + EPS)
    return y * g


def _l2norm(x):
    return x * lax.rsqrt(jnp.sum(x * x, axis=-1, keepdims=True) + EPS)


def _split_cols(p, sizes):
    cuts = [int(s) for s in np.cumsum(sizes)[:-1]]
    return jnp.split(p, cuts, axis=-1)


def _dwconv(x, w):
    K = w.shape[0]
    T = x.shape[1]
    pad = K // 2
    xp = jnp.pad(x, ((0, 0), (pad, pad), (0, 0)))
    out = xp[:, 0:T] * w[0]
    for i in range(1, K):
        out = out + xp[:, i:i + T] * w[i]
    return out


def _softmax_sink(s, sink):
    if sink is None:
        return jax.nn.softmax(s, axis=-1)
    m = jnp.maximum(jnp.max(s, axis=-1, keepdims=True), sink)
    e = jnp.exp(s - m)
    return e / (jnp.sum(e, axis=-1, keepdims=True) + jnp.exp(sink - m))


def _axial_rope(x):
    T = x.shape[1]
    t = jnp.arange(T)
    half = HEAD_DIM // 2
    quarter = half // 2
    inv = 1.0 / (ROPE_BASE ** (jnp.arange(quarter, dtype=F32) / quarter))

    def rot(xa, pos):
        ang = pos.astype(F32)[:, None] * inv[None, :]
        cos = jnp.cos(ang)[None, :, None, :]
        sin = jnp.sin(ang)[None, :, None, :]
        x1, x2 = xa[..., :quarter], xa[..., quarter:]
        return jnp.concatenate([x1 * cos - x2 * sin, x1 * sin + x2 * cos], axis=-1)

    return jnp.concatenate([rot(x[..., :half], t // GRID_W), rot(x[..., half:], t % GRID_W)], axis=-1)


def _dense_attn(q, k, v, sink):
    B, Tq, Hq, Dh = q.shape
    Hk = k.shape[1]
    G = Hq // Hk
    nb = Tq // Q_BLOCK
    qb = q.reshape(B, nb, Q_BLOCK, Hk, G, Dh).swapaxes(0, 1)
    sk = None if sink is None else sink.reshape(Hk, G, 1, 1)
    scale = Dh ** -0.5

    def one(qblk):
        s = jnp.einsum('bqkgd,bksd->bkgqs', qblk, k) * scale
        p = _softmax_sink(s, sk)
        return jnp.einsum('bkgqs,bksd->bqkgd', p, v)

    o = lax.map(one, qb)
    return o.swapaxes(0, 1).reshape(B, Tq, Hq, Dh)


def _neighborhood_attn(q, k, v, rpb, ck, cv):
    B, T, H, Dh = q.shape
    R = T // GRID_W
    kr = min(WIN_R, R)
    kc = WIN_C
    qg = q.reshape(B, R, GRID_W, H, Dh)
    kg = k.reshape(B, R, GRID_W, H, Dh)
    vg = v.reshape(B, R, GRID_W, H, Dh)
    col = jnp.arange(GRID_W)
    col_idx = jnp.clip(col - kc // 2, 0, GRID_W - kc)[:, None] + jnp.arange(kc)[None, :]
    dc = col_idx - col[:, None] + (WIN_C - 1)
    scale = Dh ** -0.5

    def one(i):
        r0 = jnp.clip(i - kr // 2, 0, R - kr)
        kb = lax.dynamic_slice_in_dim(kg, r0, kr, axis=1)[:, :, col_idx]
        vb = lax.dynamic_slice_in_dim(vg, r0, kr, axis=1)[:, :, col_idx]
        qi = lax.dynamic_index_in_dim(qg, i, axis=1, keepdims=False)
        dr = r0 + jnp.arange(kr) - i + (WIN_R - 1)
        bias = rpb[:, dr[None, :, None], dc[:, None, :]]
        s_loc = jnp.einsum('bwhd,brwchd->bhwrc', qi, kb) * scale + bias
        s_loc = s_loc.reshape(B, H, GRID_W, kr * kc)
        s_ctx = jnp.einsum('bwhd,bhld->bhwl', qi, ck) * scale
        p = jax.nn.softmax(jnp.concatenate([s_loc, s_ctx], axis=-1), axis=-1)
        p_loc = p[..., :kr * kc].reshape(B, H, GRID_W, kr, kc)
        return (jnp.einsum('bhwrc,brwchd->bwhd', p_loc, vb)
                + jnp.einsum('bhwl,bhld->bwhd', p[..., kr * kc:], cv))

    o = lax.map(one, jnp.arange(R))
    return o.swapaxes(0, 1).reshape(B, T, H, Dh)


def _window_attn(q, k, v, sink, ck, cv):
    B, T, Hq, Dh = q.shape
    Hk = k.shape[2]
    G = Hq // Hk
    blk = Q_BLOCK
    nb = T // blk
    pad = ((0, 0), (blk, blk), (0, 0), (0, 0))
    kp = jnp.pad(k, pad)
    vp = jnp.pad(v, pad)
    sk = sink.reshape(1, Hk, G, 1, 1)
    scale = Dh ** -0.5

    def one(n):
        q0 = n * blk
        qn = lax.dynamic_slice_in_dim(q, q0, blk, axis=1).reshape(B, blk, Hk, G, Dh)
        kn = lax.dynamic_slice_in_dim(kp, q0, 3 * blk, axis=1)
        vn = lax.dynamic_slice_in_dim(vp, q0, 3 * blk, axis=1)
        qpos = q0 + jnp.arange(blk)
        kpos = q0 - blk + jnp.arange(3 * blk)
        valid = ((jnp.abs(kpos[None, :] - qpos[:, None]) <= WIN_1D)
                 & (kpos >= 0)[None, :] & (kpos < T)[None, :])
        s_loc = jnp.einsum('bqkgd,bskd->bkgqs', qn, kn) * scale
        s_loc = jnp.where(valid, s_loc, -jnp.inf)
        s_ctx = jnp.einsum('bqkgd,bksd->bkgqs', qn, ck) * scale
        p = _softmax_sink(jnp.concatenate([s_loc, s_ctx], axis=-1), sk)
        o = (jnp.einsum('bkgqs,bskd->bqkgd', p[..., :3 * blk], vn)
             + jnp.einsum('bkgqs,bksd->bqkgd', p[..., 3 * blk:], cv))
        return o.reshape(B, blk, Hq, Dh)

    o = lax.map(one, jnp.arange(nb))
    return o.swapaxes(0, 1).reshape(B, T, Hq, Dh)


def _gla_scan(q, k, v, g, s0):
    B, T, H, dk = q.shape
    dv = v.shape[-1]
    n = T // CHUNK
    q, k, v, g = [a.reshape(B, n, CHUNK, H, a.shape[-1]) for a in (q, k, v, g)]
    G = jnp.cumsum(g, axis=2)
    G_last = G[:, :, -1:]
    q_t = q * jnp.exp(G)
    k_t = k * jnp.exp(-G)
    k_end = k * jnp.exp(G_last - G)
    causal = jnp.tril(jnp.ones((CHUNK, CHUNK), bool))
    att = jnp.where(causal, jnp.einsum('bnchd,bnshd->bnhcs', q_t, k_t), 0.0)
    o_intra = jnp.einsum('bnhcs,bnshv->bnchv', att, v)
    u = jnp.einsum('bnshd,bnshv->bnhdv', k_end, v)
    decay = jnp.exp(G_last[:, :, 0])

    def step(S, inp):
        d, du = inp
        return d[..., None] * S + du, S

    S_fin, S_start = lax.scan(step, s0, (decay.swapaxes(0, 1), u.swapaxes(0, 1)))
    o_inter = jnp.einsum('bnchd,bnhdv->bnchv', q_t, S_start.swapaxes(0, 1))
    return (o_intra + o_inter).reshape(B, T, H, dv), S_fin


def _delta_scan(q, k, v, beta, g, s0):
    B, T, H, dk = q.shape
    dv = v.shape[-1]
    n = T // CHUNK
    blk = lambda a: a.reshape((B, n, CHUNK) + a.shape[2:]).swapaxes(2, 3)
    q, k, v, beta, g = [blk(a) for a in (q, k, v, beta, g)]
    G = jnp.cumsum(g, axis=-1)
    lower = jnp.tril(jnp.ones((CHUNK, CHUNK), bool))
    strict = jnp.tril(jnp.ones((CHUNK, CHUNK), bool), -1)
    diff = G[..., :, None] - G[..., None, :]
    gam = jnp.where(lower, jnp.exp(jnp.where(lower, diff, 0.0)), 0.0)
    k_beta = k * beta[..., None]
    a_mat = jnp.where(strict, jnp.einsum('bnhcd,bnhsd->bnhcs', k_beta, k) * gam, 0.0)
    m_mat = a_mat + jnp.eye(CHUNK, dtype=F32)
    rhs = jnp.concatenate([v * beta[..., None], k_beta * jnp.exp(G)[..., None]], axis=-1)
    sol = lax.linalg.triangular_solve(m_mat, rhs, left_side=True, lower=True, unit_diagonal=True)
    w_val, k_cum = sol[..., :dv], sol[..., dv:]
    a_qk = jnp.einsum('bnhcd,bnhsd->bnhcs', q, k) * gam
    q_g = q * jnp.exp(G)[..., None]
    k_end = k * jnp.exp(G[..., -1:] - G)[..., None]
    d_last = jnp.exp(G[..., -1])

    def step(S, inp):
        aqk, wv, kc, qg, ke, d = inp
        v_new = wv - jnp.einsum('bhcd,bhdv->bhcv', kc, S)
        o = jnp.einsum('bhcd,bhdv->bhcv', qg, S) + jnp.einsum('bhcs,bhsv->bhcv', aqk, v_new)
        S = S * d[..., None, None] + jnp.einsum('bhcd,bhcv->bhdv', ke, v_new)
        return S, o

    xs = tuple(a.swapaxes(0, 1) for a in (a_qk, w_val, k_cum, q_g, k_end, d_last))
    S_fin, o = lax.scan(step, s0, xs)
    return o.transpose(1, 0, 3, 2, 4).reshape(B, T, H, dv), S_fin


def _gla_mixer(bq, bk, bv, glr, br, w_g2, b_g, norm_g, s0):
    B, T, _ = bq.shape
    q = bq.reshape(B, T, H_B, DK_B) * DK_B ** -0.5
    k = bk.reshape(B, T, H_B, DK_B)
    v = bv.reshape(B, T, H_B, DV_B)
    z = jnp.einsum('btzr,zrc->btzc', glr.reshape(B, T, 2, GLA_RANK), w_g2) + b_g
    g = (jax.nn.log_sigmoid(z) / GLA_TAU).reshape(B, T, 2, H_B, DK_B)
    fl = lambda a: jnp.flip(a, 1)
    o_f, s_f = _gla_scan(q, k, v, g[:, :, 0], s0[:, 0])
    o_b, s_b = _gla_scan(fl(q), fl(k), fl(v), fl(g[:, :, 1]), s0[:, 1])
    o = o_f + fl(o_b)
    o = _rmsnorm(o, norm_g.reshape(H_B, DV_B)) * jax.nn.silu(br.reshape(B, T, H_B, DV_B))
    return o.reshape(B, T, H_B * DV_B), jnp.stack([s_f, s_b], axis=1)


def _delta_mixer(dq, dk, dv, da, db, dz, w_conv, a_log, dt_bias, norm_g, s0):
    B, T, _ = dq.shape
    qkv = jax.nn.silu(_dwconv(jnp.concatenate([dq, dk, dv], axis=-1), w_conv))
    q, k, v = _split_cols(qkv, (H_D * DK_D, H_D * DK_D, H_D * DV_D))
    q = _l2norm(q.reshape(B, T, H_D, DK_D)) * DK_D ** -0.5
    k = _l2norm(k.reshape(B, T, H_D, DK_D))
    v = v.reshape(B, T, H_D, DV_D)
    beta = jax.nn.sigmoid(db.reshape(B, T, 2, H_D))
    g = -jnp.exp(a_log) * jax.nn.softplus(da.reshape(B, T, 2, H_D) + dt_bias)
    fl = lambda a: jnp.flip(a, 1)
    o_f, s_f = _delta_scan(q, k, v, beta[:, :, 0], g[:, :, 0], s0[:, 0])
    o_b, s_b = _delta_scan(fl(q), fl(k), fl(v), fl(beta[:, :, 1]), fl(g[:, :, 1]), s0[:, 1])
    o = o_f + fl(o_b)
    o = _rmsnorm(o, norm_g) * jax.nn.silu(dz.reshape(B, T, H_D, DV_D))
    return o.reshape(B, T, H_D * DV_D), jnp.stack([s_f, s_b], axis=1)


def _even_mix(p, latent, rpb, w_g2, b_g, norm_g, ak_cache, av_cache, sb):
    B, T, _ = p.shape
    ak = p[..., H_A * HEAD_DIM:2 * H_A * HEAD_DIM]
    av = p[..., 2 * H_A * HEAD_DIM:3 * H_A * HEAD_DIM]
    n_pair = H_A // 2
    if latent:
        o_a = _nbr_attn_call(p, rpb, ak_cache, av_cache)
        k_h = v_h = None
    else:
        k_h = ak.reshape(B, T, H_A, HEAD_DIM).transpose(0, 2, 1, 3)
        v_h = av.reshape(B, T, H_A, HEAD_DIM).transpose(0, 2, 1, 3)
        o_a = _dense_attn_call(p, 0, n_pair, 2 * n_pair, None, False)
    o_b, s_b = _gla_call(p, w_g2, b_g, norm_g, sb)
    return o_a, o_b, k_h, v_h, s_b


def _odd_mix(p, latent, sink, w_conv, a_log, dt_bias, norm_g, ck_cache, cv_cache, sd):
    B, T, _ = p.shape
    ck = p[..., H_C * HEAD_DIM:(H_C + KV_C) * HEAD_DIM]
    cv = p[..., (H_C + KV_C) * HEAD_DIM:(H_C + 2 * KV_C) * HEAD_DIM]
    n_pair = H_C // 2
    if latent:
        o_c = _win_attn_call(p, sink, ck_cache, cv_cache)
        k_h = v_h = None
    else:
        k_h = ck.reshape(B, T, KV_C, HEAD_DIM).transpose(0, 2, 1, 3)
        v_h = cv.reshape(B, T, KV_C, HEAD_DIM).transpose(0, 2, 1, 3)
        o_c = _dense_attn_call(p, 0, n_pair, n_pair + 1, sink, True)
    o_d, s_d = _delta_call(p, w_conv, a_log, dt_bias, norm_g, sd)
    return o_c, o_d, k_h, v_h, s_d


def _reorder_cols(w, sizes, order):
    parts = _split_cols(w, sizes)
    out = jnp.concatenate([parts[i] for i in order], axis=-1)
    pad = -out.shape[-1] % LANES
    return jnp.pad(out, ((0, 0), (0, pad)))


def _reorder_even_w_in(w):
    return _reorder_cols(w, EV_SIZES, (0, 1, 2, 3, 4, 5, 7, 6))


def _reorder_odd_w_in(w):
    return _reorder_cols(w, OD_SIZES, (0, 1, 2, 3, 4, 5, 8, 6, 7))


OD_ORDER_SIZES = tuple(OD_SIZES[i] for i in (0, 1, 2, 3, 4, 5, 8, 6, 7)) + (-sum(OD_SIZES) % LANES,)


TM_LATENT = 512
TM_CONTEXT = 256
TF_FFN = 1408
TM_FFN_LATENT = 512


def kernel(x_prompt, x_sample, cache_a_k, cache_a_v, state_b, cache_c_k, cache_c_v, state_d, c, c_ctx, ada_w, ada_b, norm1_g, norm2_g, ffn_up, ffn_conv, ffn_down, ev_w_in, ev_w_out, a_rpb, b_w_g2, b_b_g, b_norm_g, od_w_in, od_w_out, c_sink, d_conv, d_a_log, d_dt_bias, d_norm_g, final_g):
    Bp, Tp, D = x_prompt.shape
    Bs, Ts, _ = x_sample.shape
    cond = jnp.concatenate([c, c_ctx[None, :], jnp.zeros((2 * SUBLANES - Bs - 1, D), F32)], axis=0)
    mods = _adaln(cond, ada_w, ada_b)
    xp, xs = x_prompt, x_sample
    streams = ((False, Bp, TM_CONTEXT), (True, Bs, TM_LATENT))
    ak_l, av_l, sb_l, ck_l, cv_l, sd_l = [], [], [], [], [], []
    for l in range(DEPTH):
        j = l // 2
        mod_s = [mods[l, :Bs, i * D:(i + 1) * D] for i in range(6)]
        mod_p = [jnp.broadcast_to(mods[l, Bs:Bs + 1, i * D:(i + 1) * D], (Bp, D)) for i in range(6)]
        w_in = (_reorder_even_w_in(ev_w_in[j]) if l % 2 == 0 else _reorder_odd_w_in(od_w_in[j])).astype(BF16)
        w_out = (ev_w_out[j] if l % 2 == 0 else od_w_out[j]).astype(BF16)
        w_up = ffn_up[l].astype(BF16)
        w_down = ffn_down[l].astype(BF16)
        new_x = []
        for (latent, B, tm), x, mod in zip(streams, (xp, xs), (mod_p, mod_s)):
            sh1, sc1, g1, sh2, sc2, g2 = mod
            p = _nm_matmul(x, norm1_g[l], sh1, sc1, w_in, tm)
            if l % 2 == 0:
                s0 = state_b[:, j] if latent else jnp.zeros((B, 2, H_B, DK_B, DV_B), F32)
                o1, o2, k_h, v_h, s_new = _even_mix(p, latent, a_rpb[j], b_w_g2[j], b_b_g[j], b_norm_g[j],
                                                    cache_a_k[:, j], cache_a_v[:, j], s0)
                if not latent:
                    ak_l.append(k_h)
                    av_l.append(v_h)
                    sb_l.append(s_new)
            else:
                s0 = state_d[:, j] if latent else jnp.zeros((B, 2, H_D, DK_D, DV_D), F32)
                o1, o2, k_h, v_h, s_new = _odd_mix(p, latent, c_sink[j], d_conv[j], d_a_log[j], d_dt_bias[j],
                                                   d_norm_g[j], cache_c_k[:, j], cache_c_v[:, j], s0)
                if not latent:
                    ck_l.append(k_h)
                    cv_l.append(v_h)
                    sd_l.append(s_new)
            x = _proj_res(o1, o2, w_out, x, g1, tm)
            x = _conv_ffn_res(x, norm2_g[l], sh2, sc2, g2, w_up, ffn_conv[l], w_down,
                              TM_FFN_LATENT if latent else tm, TF_FFN)
            new_x.append(x)
        xp, xs = new_x
    y_prompt = _rmsnorm_call(xp, final_g, TM_CONTEXT)
    y_sample = _rmsnorm_call(xs, final_g, TM_LATENT)
    return (y_prompt, y_sample, jnp.stack(ak_l, axis=1), jnp.stack(av_l, axis=1), jnp.stack(sb_l, axis=1),
            jnp.stack(ck_l, axis=1), jnp.stack(cv_l, axis=1), jnp.stack(sd_l, axis=1))
```

```python
import functools
import math

import jax
import jax.numpy as jnp
import numpy as np
from jax import lax
from jax.experimental import pallas as pl
from jax.experimental.pallas import tpu as pltpu

D_MODEL = 1024
DEPTH = 4
GRID_W = 64
HEAD_DIM = 64
H_A = D_MODEL // (2 * HEAD_DIM)
WIN_R = 8
WIN_C = 16
H_B = D_MODEL // (2 * HEAD_DIM)
DK_B = HEAD_DIM
DV_B = HEAD_DIM
GLA_RANK = 16
GLA_TAU = 16.0
H_C = D_MODEL // (2 * HEAD_DIM)
KV_C = H_C // 4
WIN_1D = 128
ROPE_BASE = 10000.0
H_D = D_MODEL // (2 * HEAD_DIM)
DK_D = HEAD_DIM
DV_D = HEAD_DIM
SHORT_CONV = 3
CHUNK = 64
Q_BLOCK = 128
D_FF = ((8 * D_MODEL // 3 + 127) // 128) * 128
EPS = 1e-6
EV_SIZES = (H_A * HEAD_DIM, H_A * HEAD_DIM, H_A * HEAD_DIM, H_B * DK_B, H_B * DK_B, H_B * DV_B, 2 * GLA_RANK, H_B * DV_B)
OD_SIZES = (H_C * HEAD_DIM, KV_C * HEAD_DIM, KV_C * HEAD_DIM, H_D * DK_D, H_D * DK_D, H_D * DV_D, 2 * H_D, 2 * H_D, H_D * DV_D)
F32 = jnp.float32
BF16 = jnp.bfloat16

V7X_VMEM_LIMIT_BYTES = 52 * 1024 * 1024
SUBLANES = 8
LANES = 128
COL_CHUNK = 512
FFN_SUB_COLS = 1408


def _params(*sem):
    return pltpu.CompilerParams(dimension_semantics=sem, vmem_limit_bytes=V7X_VMEM_LIMIT_BYTES)


def _norm_mod(x, g, shift, scale):
    y = x * lax.rsqrt(jnp.mean(x * x, axis=-1, keepdims=True) + EPS)
    return (y * g) * (1.0 + scale) + shift


def _adaln_kernel(c_ref, w_ref, b_ref, o_ref):
    cond = c_ref[...]
    a = cond * jax.nn.sigmoid(cond)
    a_hi = a.astype(BF16)
    a_lo = (a - a_hi.astype(F32)).astype(BF16)
    w = w_ref[0]
    w_hi = w.astype(BF16)
    w_lo = (w - w_hi.astype(F32)).astype(BF16)
    acc = jnp.dot(a_hi, w_hi, preferred_element_type=F32)
    acc += jnp.dot(a_lo, w_hi, preferred_element_type=F32)
    acc += jnp.dot(a_hi, w_lo, preferred_element_type=F32)
    o_ref[0] = acc + b_ref[0]


def _adaln(cond, ada_w, ada_b):
    R = cond.shape[0]
    L, D, N = ada_w.shape
    tn = 1024
    return pl.pallas_call(
        _adaln_kernel,
        grid=(L, N // tn),
        in_specs=[pl.BlockSpec((R, D), lambda l, j: (0, 0)),
                  pl.BlockSpec((1, D, tn), lambda l, j: (l, 0, j)),
                  pl.BlockSpec((1, 1, tn), lambda l, j: (l, 0, j))],
        out_specs=pl.BlockSpec((1, R, tn), lambda l, j: (l, 0, j)),
        out_shape=jax.ShapeDtypeStruct((L, R, N), F32),
        compiler_params=_params("parallel", "parallel"),
    )(cond, ada_w, ada_b.reshape(L, 1, N))


def _nm_matmul_kernel(x_ref, g_ref, sh_ref, sc_ref, w_ref, o_ref):
    h = _norm_mod(x_ref[0], g_ref[...], sh_ref[0], sc_ref[0]).astype(BF16)
    n = o_ref.shape[-1]
    for c0 in range(0, n, COL_CHUNK):
        c1 = min(c0 + COL_CHUNK, n)
        o_ref[0, :, c0:c1] = jnp.dot(h, w_ref[:, c0:c1], preferred_element_type=F32)


def _nm_matmul(x, g, shift, scale, w, tm):
    B, T, D = x.shape
    N = w.shape[1]
    return pl.pallas_call(
        _nm_matmul_kernel,
        grid=(B, T // tm),
        in_specs=[pl.BlockSpec((1, tm, D), lambda b, i: (b, i, 0)),
                  pl.BlockSpec((1, D), lambda b, i: (0, 0)),
                  pl.BlockSpec((1, 1, D), lambda b, i: (b, 0, 0)),
                  pl.BlockSpec((1, 1, D), lambda b, i: (b, 0, 0)),
                  pl.BlockSpec((D, N), lambda b, i: (0, 0))],
        out_specs=pl.BlockSpec((1, tm, N), lambda b, i: (b, i, 0)),
        out_shape=jax.ShapeDtypeStruct((B, T, N), F32),
        compiler_params=_params("parallel", "parallel"),
    )(x, g.reshape(1, D), shift.reshape(B, 1, D), scale.reshape(B, 1, D), w)


def _proj_res_kernel(a1_ref, a2_ref, w1_ref, w2_ref, x_ref, gate_ref, o_ref):
    acc = jnp.dot(a1_ref[0].astype(BF16), w1_ref[...], preferred_element_type=F32)
    acc += jnp.dot(a2_ref[0].astype(BF16), w2_ref[...], preferred_element_type=F32)
    o_ref[0] = x_ref[0] + gate_ref[0] * acc


def _proj_res(a1, a2, w, x, gate, tm):
    B, T, K = a1.shape
    D = w.shape[1]
    return pl.pallas_call(
        _proj_res_kernel,
        grid=(B, T // tm),
        in_specs=[pl.BlockSpec((1, tm, K), lambda b, i: (b, i, 0)),
                  pl.BlockSpec((1, tm, K), lambda b, i: (b, i, 0)),
                  pl.BlockSpec((K, D), lambda b, i: (0, 0)),
                  pl.BlockSpec((K, D), lambda b, i: (1, 0)),
                  pl.BlockSpec((1, tm, D), lambda b, i: (b, i, 0)),
                  pl.BlockSpec((1, 1, D), lambda b, i: (b, 0, 0))],
        out_specs=pl.BlockSpec((1, tm, D), lambda b, i: (b, i, 0)),
        out_shape=jax.ShapeDtypeStruct((B, T, D), F32),
        compiler_params=_params("parallel", "parallel"),
    )(a1, a2, w, w, x, gate.reshape(B, 1, D))


def _ffn_kernel(x_ref, xp_ref, xn_ref, g_ref, sh_ref, sc_ref, gate_ref, wa_ref, wg_ref,
                ca_ref, cg_ref, wd_ref, o_ref, h_ref, acc_ref):
    i = pl.program_id(1)
    c = pl.program_id(2)
    tm = x_ref.shape[1]
    tf = wa_ref.shape[1]

    @pl.when(c == 0)
    def _():
        g, sh, sc = g_ref[...], sh_ref[0], sc_ref[0]
        h_ref[0:tm, :] = _norm_mod(x_ref[0], g, sh, sc).astype(BF16)
        hp = jnp.where(i > 0, _norm_mod(xp_ref[0], g, sh, sc), 0.0)
        hn = jnp.where(i < pl.num_programs(1) - 1, _norm_mod(xn_ref[0], g, sh, sc), 0.0)
        h_ref[tm:tm + 2 * SUBLANES, :] = jnp.concatenate([hp, hn], axis=0).astype(BF16)
        acc_ref[...] = jnp.zeros_like(acc_ref)

    h = h_ref[...]
    row = lax.broadcasted_iota(jnp.int32, (tm, 1), 0)

    def conv_branch(w_ref, cw_ref, c0, c1):
        u_all = jnp.dot(h, w_ref[:, c0:c1], preferred_element_type=F32)
        u = u_all[0:tm]
        u_prev = jnp.where(row == 0, u_all[tm + SUBLANES - 1:tm + SUBLANES], pltpu.roll(u, 1, axis=0))
        u_next = jnp.where(row == tm - 1, u_all[tm + SUBLANES:tm + SUBLANES + 1], pltpu.roll(u, tm - 1, axis=0))
        cw = cw_ref[:, c0:c1]
        return u_prev * cw[0:1] + u * cw[1:2] + u_next * cw[2:3]

    for c0 in range(0, tf, FFN_SUB_COLS):
        c1 = min(c0 + FFN_SUB_COLS, tf)
        a = conv_branch(wa_ref, ca_ref, c0, c1)
        gt = conv_branch(wg_ref, cg_ref, c0, c1)
        act = a * (gt * jax.nn.sigmoid(gt))
        acc_ref[...] += jnp.dot(act.astype(BF16), wd_ref[c0:c1, :], preferred_element_type=F32)

    @pl.when(c == pl.num_programs(2) - 1)
    def _():
        o_ref[0] = x_ref[0] + gate_ref[0] * acc_ref[...]


def _conv_ffn_res(x, g, shift, scale, gate, w_up, w_conv, w_down, tm, tf):
    B, T, D = x.shape
    F = w_down.shape[0]
    nf = F // tf
    nt = T // tm
    rows8 = tm // SUBLANES
    vec = lambda: pl.BlockSpec((1, 1, D), lambda b, i, c: (b, 0, 0))
    return pl.pallas_call(
        _ffn_kernel,
        grid=(B, nt, nf),
        in_specs=[pl.BlockSpec((1, tm, D), lambda b, i, c: (b, i, 0)),
                  pl.BlockSpec((1, SUBLANES, D), lambda b, i, c: (b, jnp.maximum(i * rows8 - 1, 0), 0)),
                  pl.BlockSpec((1, SUBLANES, D), lambda b, i, c: (b, jnp.minimum((i + 1) * rows8, T // SUBLANES - 1), 0)),
                  pl.BlockSpec((1, D), lambda b, i, c: (0, 0)),
                  vec(), vec(), vec(),
                  pl.BlockSpec((D, tf), lambda b, i, c: (0, c)),
                  pl.BlockSpec((D, tf), lambda b, i, c: (0, nf + c)),
                  pl.BlockSpec((3, tf), lambda b, i, c: (0, c)),
                  pl.BlockSpec((3, tf), lambda b, i, c: (0, nf + c)),
                  pl.BlockSpec((tf, D), lambda b, i, c: (c, 0))],
        out_specs=pl.BlockSpec((1, tm, D), lambda b, i, c: (b, i, 0)),
        out_shape=jax.ShapeDtypeStruct((B, T, D), F32),
        scratch_shapes=[pltpu.VMEM((tm + 2 * SUBLANES, D), BF16),
                        pltpu.VMEM((tm, D), F32)],
        compiler_params=_params("parallel", "parallel", "arbitrary"),
    )(x, x, x, g.reshape(1, D), shift.reshape(B, 1, D), scale.reshape(B, 1, D), gate.reshape(B, 1, D),
      w_up, w_up, w_conv, w_conv, w_down)


def _rmsnorm_kernel(x_ref, g_ref, o_ref):
    x = x_ref[0]
    o_ref[0] = (x * lax.rsqrt(jnp.mean(x * x, axis=-1, keepdims=True) + EPS)) * g_ref[...]


def _rmsnorm_call(x, g, tm):
    B, T, D = x.shape
    return pl.pallas_call(
        _rmsnorm_kernel,
        grid=(B, T // tm),
        in_specs=[pl.BlockSpec((1, tm, D), lambda b, i: (b, i, 0)),
                  pl.BlockSpec((1, D), lambda b, i: (0, 0))],
        out_specs=pl.BlockSpec((1, tm, D), lambda b, i: (b, i, 0)),
        out_shape=jax.ShapeDtypeStruct((B, T, D), F32),
        compiler_params=_params("parallel", "parallel"),
    )(x, g.reshape(1, D))


NEG_BIG = -1e30


def _dot_nt(a, b):
    return lax.dot_general(a, b, (((1,), (1,)), ((), ())), preferred_element_type=F32)


def _lane_group():
    return lax.broadcasted_iota(jnp.int32, (1, LANES), 1) // HEAD_DIM


def _attend(score_fns, values, sink):
    scores = [fn() for fn in score_fns]
    yield
    s = scores[0] if len(scores) == 1 else jnp.concatenate(scores, axis=1)
    m = jnp.max(s, axis=-1, keepdims=True)
    if sink is not None:
        m = jnp.maximum(m, sink)
    e = jnp.exp(s - m)
    den = jnp.sum(e, axis=-1, keepdims=True)
    if sink is not None:
        den = den + jnp.exp(sink - m)
    eb = e.astype(BF16)
    out = None
    c0 = 0
    for v in values:
        o = jnp.dot(eb[:, c0:c0 + v.shape[0]], v, preferred_element_type=F32)
        out = o if out is None else out + o
        c0 += v.shape[0]
    yield
    return out / den


def _merge_heads(outs, grp):
    return jnp.where(grp == 0, outs[0], outs[1])


def _dense_attn_kernel(q_ref, k_ref, v_ref, sink_ref, o_ref, *, gqa, use_sink):
    grp = _lane_group()
    q = q_ref[0]
    k = k_ref[0]
    v = v_ref[0]
    if gqa:
        kvg = pl.program_id(1) // 2
        k = jnp.where(grp == kvg, k, pltpu.roll(k, HEAD_DIM, axis=1))
        v = jnp.where(grp == kvg, v, pltpu.roll(v, HEAD_DIM, axis=1))
    kb = k.astype(BF16)
    vb = v.astype(BF16)

    def head(h):
        qh = jnp.where(grp == h, q, 0.0).astype(BF16)
        sink = sink_ref[0, h][:, :1] if use_sink else None
        return _attend([lambda: _dot_nt(qh, kb) * (HEAD_DIM ** -0.5)], [vb], sink)

    o_ref[0] = _merge_heads(_lockstep([head(0), head(1)]), grp)


def _dense_attn_call(p, q_blk, k_blk, v_blk, sink, gqa):
    B, T, _ = p.shape
    n_pair = H_A // 2
    use_sink = sink is not None
    if use_sink:
        sink_arr = jnp.broadcast_to(sink.reshape(n_pair, 2, 1, 1), (n_pair, 2, 1, LANES))
    else:
        sink_arr = jnp.zeros((n_pair, 2, 1, LANES), F32)
    kv_idx = (lambda hp: 0) if gqa else (lambda hp: hp)
    return pl.pallas_call(
        functools.partial(_dense_attn_kernel, gqa=gqa, use_sink=use_sink),
        grid=(B, n_pair),
        in_specs=[pl.BlockSpec((1, T, LANES), lambda b, hp: (b, 0, q_blk + hp)),
                  pl.BlockSpec((1, T, LANES), lambda b, hp: (b, 0, k_blk + kv_idx(hp))),
                  pl.BlockSpec((1, T, LANES), lambda b, hp: (b, 0, v_blk + kv_idx(hp))),
                  pl.BlockSpec((1, 2, 1, LANES), lambda b, hp: (hp, 0, 0, 0))],
        out_specs=pl.BlockSpec((1, T, LANES), lambda b, hp: (b, 0, hp)),
        out_shape=jax.ShapeDtypeStruct((B, T, n_pair * LANES), F32),
        compiler_params=_params("parallel", "parallel"),
    )(p, p, p, sink_arr)


NBR_SPAN = WIN_R * GRID_W
ATTN_BLOCKS_PER_STEP = 8


def _nbr_attn_kernel(q_ref, k_ref, v_ref, ck_ref, cv_ref, bias_ref, o_ref, kb_ref, vb_ref):
    kb_ref[...] = k_ref[0].astype(BF16)
    vb_ref[...] = v_ref[0].astype(BF16)
    ckb = ck_ref[0].astype(BF16)
    cvb = cv_ref[0].astype(BF16)
    grp = _lane_group()
    n_rows = q_ref.shape[1] // GRID_W

    def head(i, h):
        r0 = jnp.clip(i - WIN_R // 2, 0, n_rows - WIN_R)
        q = q_ref[0, pl.ds(pl.multiple_of(i * GRID_W, GRID_W), GRID_W), :]
        start = pl.multiple_of(r0 * GRID_W, GRID_W)
        kw = kb_ref[pl.ds(start, NBR_SPAN), :]
        vw = vb_ref[pl.ds(start, NBR_SPAN), :]
        qh = jnp.where(grp == h, q, 0.0).astype(BF16)
        return _attend([lambda: _dot_nt(qh, kw) * (HEAD_DIM ** -0.5) + bias_ref[i - r0, h],
                        lambda: _dot_nt(qh, ckb) * (HEAD_DIM ** -0.5)], [vw, cvb], None)

    def rows(step, carry):
        ids = [step * ATTN_BLOCKS_PER_STEP + j for j in range(ATTN_BLOCKS_PER_STEP)]
        outs = _lockstep([head(i, h) for i in ids for h in range(2)])
        for j, i in enumerate(ids):
            o_ref[0, pl.ds(pl.multiple_of(i * GRID_W, GRID_W), GRID_W), :] = _merge_heads(outs[2 * j:2 * j + 2], grp)
        return carry

    lax.fori_loop(0, n_rows // ATTN_BLOCKS_PER_STEP, rows, 0)


def _nbr_bias_table(rpb):
    var = np.arange(WIN_R)[:, None, None]
    r = np.arange(WIN_R)[None, :, None]
    row_sel = (r - var + (WIN_R - 1) == np.arange(2 * WIN_R - 1)[None, None, :]).astype(np.float32)
    w = np.arange(GRID_W)[:, None, None]
    cc = np.arange(GRID_W)[None, :, None]
    c0 = np.clip(w - WIN_C // 2, 0, GRID_W - WIN_C)
    valid = (cc >= c0) & (cc < c0 + WIN_C)
    col_sel = (valid & (cc - w + (WIN_C - 1) == np.arange(2 * WIN_C - 1)[None, None, :])).astype(np.float32)
    t = jnp.einsum('vra,hab->vhrb', row_sel, rpb, precision=lax.Precision.HIGHEST)
    tbl = jnp.einsum('vhrb,wcb->vhwrc', t, col_sel, precision=lax.Precision.HIGHEST)
    tbl = jnp.where(valid[None, None, :, None, :, 0], tbl, NEG_BIG)
    return tbl.reshape(WIN_R, rpb.shape[0], GRID_W, NBR_SPAN)


def _nbr_attn_call(p, rpb, ck_cache, cv_cache):
    B, T, _ = p.shape
    n_pair = H_A // 2
    L = ck_cache.shape[2]
    ckp = ck_cache.transpose(0, 2, 1, 3).reshape(B, L, H_A * HEAD_DIM)
    cvp = cv_cache.transpose(0, 2, 1, 3).reshape(B, L, H_A * HEAD_DIM)
    bias = _nbr_bias_table(rpb)
    col = lambda off: pl.BlockSpec((1, T, LANES), lambda b, hp: (b, 0, off + hp))
    ctx = lambda: pl.BlockSpec((1, L, LANES), lambda b, hp: (b, 0, hp))
    return pl.pallas_call(
        _nbr_attn_kernel,
        grid=(B, n_pair),
        in_specs=[col(0), col(n_pair), col(2 * n_pair), ctx(), ctx(),
                  pl.BlockSpec((WIN_R, 2, GRID_W, NBR_SPAN), lambda b, hp: (0, hp, 0, 0))],
        out_specs=pl.BlockSpec((1, T, LANES), lambda b, hp: (b, 0, hp)),
        out_shape=jax.ShapeDtypeStruct((B, T, n_pair * LANES), F32),
        scratch_shapes=[pltpu.VMEM((T, LANES), BF16), pltpu.VMEM((T, LANES), BF16)],
        compiler_params=_params("parallel", "parallel"),
    )(p, p, p, ckp, cvp, bias)


WIN_SPAN = 3 * Q_BLOCK
WIN_Q_TILE = 64
WIN_TILES_PER_STEP = 4
ROPE_SWAP = HEAD_DIM // 4


def _rope(x, cos, sin_signed):
    lane = lax.broadcasted_iota(jnp.int32, (1, LANES), 1)
    first = lane % (2 * ROPE_SWAP) < ROPE_SWAP
    partner = jnp.where(first, pltpu.roll(x, LANES - ROPE_SWAP, axis=1), pltpu.roll(x, ROPE_SWAP, axis=1))
    return x * cos + partner * sin_signed


def _win_attn_kernel(q_ref, k_ref, v_ref, ck_ref, cv_ref, cos_ref, sin_ref, sink_ref, o_ref, kb_ref, vb_ref):
    grp = _lane_group()
    kvg = pl.program_id(1) // 2
    both = lambda a: jnp.where(grp == kvg, a, pltpu.roll(a, HEAD_DIM, axis=1))
    kb_ref[...] = _rope(both(k_ref[0]), cos_ref[...], sin_ref[...]).astype(BF16)
    vb_ref[...] = both(v_ref[0]).astype(BF16)
    ckb = both(ck_ref[0]).astype(BF16)
    cvb = both(cv_ref[0]).astype(BF16)
    T = q_ref.shape[1]
    rel = (lax.broadcasted_iota(jnp.int32, (WIN_Q_TILE, WIN_SPAN), 1)
           - lax.broadcasted_iota(jnp.int32, (WIN_Q_TILE, WIN_SPAN), 0))

    def heads(n):
        q0 = pl.multiple_of(n * WIN_Q_TILE, WIN_Q_TILE)
        start = pl.multiple_of(jnp.clip(q0 - WIN_1D, 0, T - WIN_SPAN), WIN_Q_TILE)
        q = _rope(q_ref[0, pl.ds(q0, WIN_Q_TILE), :], cos_ref[pl.ds(q0, WIN_Q_TILE), :],
                  sin_ref[pl.ds(q0, WIN_Q_TILE), :])
        kw = kb_ref[pl.ds(start, WIN_SPAN), :]
        vw = vb_ref[pl.ds(start, WIN_SPAN), :]
        valid = jnp.abs(rel + (start - q0)) <= WIN_1D

        def head(h):
            qh = jnp.where(grp == h, q, 0.0).astype(BF16)
            return _attend([lambda: jnp.where(valid, _dot_nt(qh, kw) * (HEAD_DIM ** -0.5), NEG_BIG),
                            lambda: _dot_nt(qh, ckb) * (HEAD_DIM ** -0.5)], [vw, cvb], sink_ref[0, h][:, :1])

        return [head(0), head(1)]

    def tiles(step, carry):
        ids = [step * WIN_TILES_PER_STEP + j for j in range(WIN_TILES_PER_STEP)]
        outs = _lockstep([prog for n in ids for prog in heads(n)])
        for j, n in enumerate(ids):
            o_ref[0, pl.ds(pl.multiple_of(n * WIN_Q_TILE, WIN_Q_TILE), WIN_Q_TILE), :] = _merge_heads(
                outs[2 * j:2 * j + 2], grp)
        return carry

    lax.fori_loop(0, T // WIN_Q_TILE // WIN_TILES_PER_STEP, tiles, 0)


def _rope_tables(T):
    t = jnp.arange(T)
    quarter = HEAD_DIM // 4
    inv = 1.0 / (ROPE_BASE ** (jnp.arange(quarter, dtype=F32) / quarter))
    ang_r = (t // GRID_W).astype(F32)[:, None] * inv[None, :]
    ang_c = (t % GRID_W).astype(F32)[:, None] * inv[None, :]
    cos = jnp.concatenate([jnp.cos(ang_r), jnp.cos(ang_r), jnp.cos(ang_c), jnp.cos(ang_c)], axis=-1)
    sin = jnp.concatenate([-jnp.sin(ang_r), jnp.sin(ang_r), -jnp.sin(ang_c), jnp.sin(ang_c)], axis=-1)
    return jnp.tile(cos, (1, 2)), jnp.tile(sin, (1, 2))


def _win_attn_call(p, sink, ck_cache, cv_cache):
    B, T, _ = p.shape
    n_pair = H_C // 2
    L = ck_cache.shape[2]
    ckp = ck_cache.transpose(0, 2, 1, 3).reshape(B, L, KV_C * HEAD_DIM)
    cvp = cv_cache.transpose(0, 2, 1, 3).reshape(B, L, KV_C * HEAD_DIM)
    cos, sin = _rope_tables(T)
    sink_arr = jnp.broadcast_to(sink.reshape(n_pair, 2, 1, 1), (n_pair, 2, 1, LANES))
    col = lambda blk: pl.BlockSpec((1, T, LANES), lambda b, hp: (b, 0, blk))
    ctx = lambda: pl.BlockSpec((1, L, LANES), lambda b, hp: (b, 0, 0))
    tab = lambda: pl.BlockSpec((T, LANES), lambda b, hp: (0, 0))
    return pl.pallas_call(
        _win_attn_kernel,
        grid=(B, n_pair),
        in_specs=[pl.BlockSpec((1, T, LANES), lambda b, hp: (b, 0, hp)), col(n_pair), col(n_pair + 1),
                  ctx(), ctx(), tab(), tab(),
                  pl.BlockSpec((1, 2, 1, LANES), lambda b, hp: (hp, 0, 0, 0))],
        out_specs=pl.BlockSpec((1, T, LANES), lambda b, hp: (b, 0, hp)),
        out_shape=jax.ShapeDtypeStruct((B, T, n_pair * LANES), F32),
        scratch_shapes=[pltpu.VMEM((T, LANES), BF16), pltpu.VMEM((T, LANES), BF16)],
        compiler_params=_params("parallel", "parallel"),
    )(p, p, p, ckp, cvp, cos, sin, sink_arr)


def _dot_tn(a, b):
    return lax.dot_general(a, b, (((0,), (0,)), ((), ())), preferred_element_type=F32)


def _split_bf16(x):
    hi = x.astype(BF16)
    return hi, (x - hi.astype(F32)).astype(BF16)


GLA_CHUNKS_PER_STEP = 8
DELTA_CHUNKS_PER_STEP = 4


def _lockstep(programs):
    results = [None] * len(programs)
    live = list(range(len(programs)))
    while live:
        for idx in list(live):
            try:
                next(programs[idx])
            except StopIteration as stop:
                results[idx] = stop.value
                live.remove(idx)
    return results


def _block_diag_mask():
    return (lax.broadcasted_iota(jnp.int32, (LANES, 1), 0) // HEAD_DIM) == _lane_group()


def _head_rmsnorm(o, grp):
    sq = o * o
    s0 = jnp.sum(jnp.where(grp == 0, sq, 0.0), axis=-1, keepdims=True)
    s1 = jnp.sum(jnp.where(grp == 1, sq, 0.0), axis=-1, keepdims=True)
    ms = jnp.where(grp == 0, s0, s1) * (1.0 / HEAD_DIM)
    return o * lax.rsqrt(ms + EPS)


def _gla_kernel(q_ref, k_ref, v_ref, r_ref, glr_ref, w2_ref, bg_ref, ng_ref, s0_ref, o_ref, sfin_ref,
                ob_ref, st_ref):
    T = q_ref.shape[1]
    n = T // CHUNK
    grp = _lane_group()
    bd = _block_diag_mask()
    ri = lax.broadcasted_iota(jnp.int32, (CHUNK, CHUNK), 0)
    ci = lax.broadcasted_iota(jnp.int32, (CHUNK, CHUNK), 1)
    ones_tok = jnp.ones((CHUNK, LANES), BF16)
    st_ref[...] = s0_ref[0, :, 0]

    def chunk(c0, d):
        sl = pl.ds(c0, CHUNK)
        q = q_ref[0, sl, :] * (DK_B ** -0.5)
        k = k_ref[0, sl, :]
        vb = v_ref[0, sl, :].astype(BF16)
        z = jnp.dot(glr_ref[0, sl, :].astype(BF16), w2_ref[d], preferred_element_type=F32) + bg_ref[d]
        yield
        g = (jnp.minimum(z, 0.0) - jnp.log(1.0 + jnp.exp(-jnp.abs(z)))) * (1.0 / GLA_TAU)
        g_hi, g_lo = _split_bf16(g)
        mask = (ri <= ci) if d else (ri >= ci)
        tri = jnp.where(mask, 1.0, 0.0).astype(BF16)
        G = jnp.dot(tri, g_hi, preferred_element_type=F32) + jnp.dot(tri, g_lo, preferred_element_type=F32)
        g_tot_rows = _dot_tn(g_hi, ones_tok) + _dot_tn(g_lo, ones_tok)
        yield
        g_tot = G[0:1] if d else G[CHUNK - 1:CHUNK]
        q_t = q * jnp.exp(G)
        k_t = (k * jnp.exp(-G)).astype(BF16)
        k_e = (k * jnp.exp(g_tot - G)).astype(BF16)
        att = [jnp.where(mask, _dot_nt(jnp.where(grp == h, q_t, 0.0).astype(BF16), k_t), 0.0) for h in range(2)]
        u = jnp.where(bd, _dot_tn(k_e, vb), 0.0)
        decay = jnp.exp(g_tot_rows)
        yield
        S = st_ref[d]
        st_ref[d] = decay * S + u
        o = jnp.dot(q_t.astype(BF16), S.astype(BF16), preferred_element_type=F32)
        for h in range(2):
            o = o + jnp.where(grp == h, jnp.dot(att[h].astype(BF16), vb, preferred_element_type=F32), 0.0)
        return o

    per_step = min(GLA_CHUNKS_PER_STEP, n)

    def body(i, carry):
        c_f = [pl.multiple_of((i * per_step + j) * CHUNK, CHUNK) for j in range(per_step)]
        c_b = [pl.multiple_of((n - 1 - i * per_step - j) * CHUNK, CHUNK) for j in range(per_step)]
        outs = _lockstep([chunk(c0, 0) for c0 in c_f] + [chunk(c0, 1) for c0 in c_b])
        for j in range(per_step):
            o_ref[0, pl.ds(c_f[j], CHUNK), :] = outs[j]
            ob_ref[pl.ds(c_b[j], CHUNK), :] = outs[per_step + j]
        return carry

    lax.fori_loop(0, n // per_step, body, 0)
    sfin_ref[0, :, 0] = st_ref[...]

    def gate(i, carry):
        sl = pl.ds(pl.multiple_of(i * CHUNK, CHUNK), CHUNK)
        r = r_ref[0, sl, :]
        y = _head_rmsnorm(o_ref[0, sl, :] + ob_ref[sl, :], grp) * ng_ref[...]
        o_ref[0, sl, :] = y * (r * jax.nn.sigmoid(r))
        return carry

    lax.fori_loop(0, n, gate, 0)


def _pair_block_diag(s):
    B, _, H, dk, dv = s.shape
    s = s.reshape(B, 2, H // 2, 2, dk, dv)
    z = jnp.zeros_like(s[:, :, :, 0])
    top = jnp.concatenate([s[:, :, :, 0], z], axis=-1)
    bot = jnp.concatenate([z, s[:, :, :, 1]], axis=-1)
    return jnp.concatenate([top, bot], axis=-2)


def _pair_block_diag_inv(sp):
    a = sp[..., :HEAD_DIM, :HEAD_DIM]
    b = sp[..., HEAD_DIM:, HEAD_DIM:]
    B, _, P = sp.shape[:3]
    return jnp.stack([a, b], axis=3).reshape(B, 2, 2 * P, HEAD_DIM, HEAD_DIM)


EV_BQ, EV_BK, EV_BV, EV_BR, EV_GLR = 12, 16, 20, 24, 28


def _gla_call(p, w_g2, b_g, norm_g, s0):
    B, T, _ = p.shape
    n_pair = H_B // 2
    w2 = jnp.zeros((2, LANES, H_B * DK_B), F32)
    w2 = w2.at[0, :GLA_RANK].set(w_g2[0]).at[1, GLA_RANK:2 * GLA_RANK].set(w_g2[1]).astype(BF16)
    col = lambda off: pl.BlockSpec((1, T, LANES), lambda b, hp: (b, 0, off + hp))
    st = lambda: pl.BlockSpec((1, 2, 1, LANES, LANES), lambda b, hp: (b, 0, hp, 0, 0))
    o, s_fin = pl.pallas_call(
        _gla_kernel,
        grid=(B, n_pair),
        in_specs=[col(EV_BQ), col(EV_BK), col(EV_BV), col(EV_BR),
                  pl.BlockSpec((1, T, LANES), lambda b, hp: (b, 0, EV_GLR)),
                  pl.BlockSpec((2, LANES, LANES), lambda b, hp: (0, 0, hp)),
                  pl.BlockSpec((2, 1, LANES), lambda b, hp: (0, 0, hp)),
                  pl.BlockSpec((1, LANES), lambda b, hp: (0, hp)),
                  st()],
        out_specs=[pl.BlockSpec((1, T, LANES), lambda b, hp: (b, 0, hp)), st()],
        out_shape=[jax.ShapeDtypeStruct((B, T, n_pair * LANES), F32),
                   jax.ShapeDtypeStruct((B, 2, n_pair, LANES, LANES), F32)],
        scratch_shapes=[pltpu.VMEM((T, LANES), F32), pltpu.VMEM((2, LANES, LANES), F32)],
        compiler_params=_params("parallel", "parallel"),
    )(p, p, p, p, p, w2, b_g.reshape(2, 1, H_B * DK_B), norm_g.reshape(1, H_B * DV_B), _pair_block_diag(s0))
    return o, _pair_block_diag_inv(s_fin)


SOLVE_PASSES = 3
RHS_PASSES = 1


def _mm(a, b, passes):
    if passes == 1:
        return jnp.dot(a.astype(BF16), b.astype(BF16), preferred_element_type=F32)
    a_hi, a_lo = _split_bf16(a)
    b_hi, b_lo = _split_bf16(b)
    return (jnp.dot(a_hi, b_hi, preferred_element_type=F32) + jnp.dot(a_lo, b_hi, preferred_element_type=F32)
            + jnp.dot(a_hi, b_lo, preferred_element_type=F32))


def _unit_tri_inverse(a, eye):
    b = -a
    p = eye + b
    for _ in range(5):
        b = _mm(b, b, SOLVE_PASSES)
        p = p + _mm(p, b, SOLVE_PASSES)
    return p


def _softplus(x):
    return jnp.maximum(x, 0.0) + jnp.log(1.0 + jnp.exp(-jnp.abs(x)))


def _delta_kernel(q_ref, k_ref, v_ref, z_ref, dab_ref, cq_ref, ck_ref, cv_ref, nega_ref, dtb_ref, ng_ref, s0_ref,
                  o_ref, sfin_ref, qs_ref, ks_ref, vs_ref, ob_ref, st_ref, hw_ref, ha_ref, hd_ref):
    T = q_ref.shape[1]
    n = T // CHUNK
    hp = pl.program_id(1)
    grp = _lane_group()
    lane = lax.broadcasted_iota(jnp.int32, (1, LANES), 1)
    subl = lax.broadcasted_iota(jnp.int32, (LANES, 1), 0)
    bd = _block_diag_mask()
    ri = lax.broadcasted_iota(jnp.int32, (CHUNK, CHUNK), 0)
    ci = lax.broadcasted_iota(jnp.int32, (CHUNK, CHUNK), 1)
    eye = jnp.where(ri == ci, 1.0, 0.0)
    row = lax.broadcasted_iota(jnp.int32, (CHUNK, 1), 0)
    st_ref[...] = s0_ref[0, :, 0]

    def prep(i, carry):
        c0 = pl.multiple_of(i * CHUNK, CHUNK)
        lo = pl.multiple_of(jnp.maximum(c0 - SUBLANES, 0), SUBLANES)
        hi = pl.multiple_of(jnp.minimum(c0 + CHUNK, T - SUBLANES), SUBLANES)
        for src, cw_ref, dst, scale in ((q_ref, cq_ref, qs_ref, DK_D ** -0.5), (k_ref, ck_ref, ks_ref, 1.0),
                                        (v_ref, cv_ref, vs_ref, None)):
            x = src[0, pl.ds(c0, CHUNK), :]
            before = jnp.where(i > 0, src[0, pl.ds(lo, SUBLANES), :][SUBLANES - 1:], 0.0)
            after = jnp.where(i < n - 1, src[0, pl.ds(hi, SUBLANES), :][:1], 0.0)
            x_prev = jnp.where(row == 0, before, pltpu.roll(x, 1, axis=0))
            x_next = jnp.where(row == CHUNK - 1, after, pltpu.roll(x, CHUNK - 1, axis=0))
            cw = cw_ref[...]
            y = x_prev * cw[0:1] + x * cw[1:2] + x_next * cw[2:3]
            y = y * jax.nn.sigmoid(y)
            if scale is not None:
                sq = y * y
                s0 = jnp.sum(jnp.where(grp == 0, sq, 0.0), axis=-1, keepdims=True)
                s1 = jnp.sum(jnp.where(grp == 1, sq, 0.0), axis=-1, keepdims=True)
                y = y * lax.rsqrt(jnp.where(grp == 0, s0, s1) + EPS) * scale
            dst[pl.ds(c0, CHUNK), :] = y
        return carry

    lax.fori_loop(0, n, prep, 0)

    def chunk(c0, d):
        sl = pl.ds(c0, CHUNK)
        q = qs_ref[sl, :]
        k = ks_ref[sl, :]
        v = vs_ref[sl, :]
        ab = dab_ref[0, sl, :]
        gb = jnp.where(lane < 2 * H_D, nega_ref[...] * _softplus(ab + dtb_ref[...]), jax.nn.sigmoid(ab))
        gb_t = gb.T
        incl = (ri <= ci) if d else (ri >= ci)
        incl_t = (ri >= ci) if d else (ri <= ci)
        strict = (ri < ci) if d else (ri > ci)

        def head(h):
            jg = d * H_D + 2 * hp + h
            g_c = jnp.sum(jnp.where(lane == jg, gb, 0.0), axis=1, keepdims=True)
            beta_c = jnp.sum(jnp.where(lane == 2 * H_D + jg, gb, 0.0), axis=1, keepdims=True)
            g_r = jnp.sum(jnp.where(subl == jg, gb_t, 0.0), axis=0, keepdims=True)
            G_c = jnp.sum(jnp.where(incl, g_r, 0.0), axis=1, keepdims=True)
            G_r = jnp.sum(jnp.where(incl_t, g_c, 0.0), axis=0, keepdims=True)
            tot = jnp.sum(g_r, axis=1, keepdims=True)
            gam = jnp.where(incl, jnp.exp(jnp.where(incl, G_c - G_r, 0.0)), 0.0)
            kh = jnp.where(grp == h, k, 0.0)
            qh = jnp.where(grp == h, q, 0.0)
            vh = jnp.where(grp == h, v, 0.0)
            kb = kh * beta_c
            khb = kh.astype(BF16)
            a = jnp.where(strict, _dot_nt(kb.astype(BF16), khb) * gam, 0.0)
            a_qk = (_dot_nt(qh.astype(BF16), khb) * gam).astype(BF16)
            yield
            wide = jnp.concatenate([-a, eye], axis=1)
            for _ in range(6):
                w_hi, w_lo = _split_bf16(wide)
                prod = (jnp.dot(w_hi[:, :CHUNK], w_hi, preferred_element_type=F32)
                        + jnp.dot(w_lo[:, :CHUNK], w_hi, preferred_element_type=F32)
                        + jnp.dot(w_hi[:, :CHUNK], w_lo, preferred_element_type=F32))
                wide = prod + jnp.where(lane >= CHUNK, wide, 0.0)
                yield
            t_inv = wide[:, CHUNK:]
            e_g = jnp.exp(G_c)
            rhs = vh * beta_c + pltpu.roll(kb * e_g, HEAD_DIM, axis=1)
            return _mm(t_inv, rhs, RHS_PASSES), a_qk, qh * e_g, kh * jnp.exp(tot - G_c), jnp.exp(tot)

        return head

    per_step = min(DELTA_CHUNKS_PER_STEP, n)
    n_steps = n // per_step
    n_sys = 4 * per_step

    def starts(i):
        c_f = [pl.multiple_of((i * per_step + j) * CHUNK, CHUNK) for j in range(per_step)]
        c_b = [pl.multiple_of((n - 1 - i * per_step - j) * CHUNK, CHUNK) for j in range(per_step)]
        return c_f, c_b

    def head_programs(i):
        programs = []
        for d, c0s in zip((0, 1), starts(i)):
            for c0 in c0s:
                head = chunk(c0, d)
                programs += [head(0), head(1)]
        return programs

    def store_heads(slot, results):
        for s, (r, a_qk, q_g, k_e, d_last) in enumerate(results):
            hw_ref[slot, s, 0] = r
            hw_ref[slot, s, 1] = q_g
            hw_ref[slot, s, 2] = k_e
            ha_ref[slot, s] = a_qk
            hd_ref[slot, s] = jnp.broadcast_to(d_last, (SUBLANES, LANES))

    def load_head(slot, s):
        return hw_ref[slot, s, 0], ha_ref[slot, s], hw_ref[slot, s, 1], hw_ref[slot, s, 2], hd_ref[slot, s][0:1, 0:1]

    def scan_step(d, slot, s):
        (r0, aqk0, qg0, ke0, d0), (r1, aqk1, qg1, ke1, d1) = load_head(slot, s), load_head(slot, s + 1)
        S = st_ref[d]
        Sb = S.astype(BF16)
        w_val = jnp.where(grp == 0, r0, r1)
        k_cum = pltpu.roll(jnp.where(grp == 0, r1, r0), HEAD_DIM, axis=1)
        v_new = w_val - jnp.dot(k_cum.astype(BF16), Sb, preferred_element_type=F32)
        o = jnp.dot((qg0 + qg1).astype(BF16), Sb, preferred_element_type=F32)
        yield
        vnb = v_new.astype(BF16)
        o = o + jnp.where(grp == 0, jnp.dot(aqk0, vnb, preferred_element_type=F32), 0.0)
        o = o + jnp.where(grp == 1, jnp.dot(aqk1, vnb, preferred_element_type=F32), 0.0)
        d_rows = jnp.where(subl // HEAD_DIM == 0, d0, d1)
        st_ref[d] = S * d_rows + jnp.where(bd, _dot_tn((ke0 + ke1).astype(BF16), vnb), 0.0)
        return o

    def scan_dir(d, i, slot):
        c0s = starts(i)[d]
        for j in range(per_step):
            o = yield from scan_step(d, slot, d * 2 * per_step + 2 * j)
            if d == 0:
                o_ref[0, pl.ds(c0s[j], CHUNK), :] = o
            else:
                ob_ref[pl.ds(c0s[j], CHUNK), :] = o

    store_heads(0, _lockstep(head_programs(0)))

    def body(i, carry):
        slot = i % 2
        results = _lockstep(head_programs(i) + [scan_dir(0, i - 1, 1 - slot), scan_dir(1, i - 1, 1 - slot)])
        store_heads(slot, results[:n_sys])
        return carry

    lax.fori_loop(1, n_steps, body, 0)
    last = n_steps - 1
    _lockstep([scan_dir(0, last, last % 2), scan_dir(1, last, last % 2)])
    sfin_ref[0, :, 0] = st_ref[...]

    def gate(i, carry):
        sl = pl.ds(pl.multiple_of(i * CHUNK, CHUNK), CHUNK)
        z = z_ref[0, sl, :]
        y = _head_rmsnorm(o_ref[0, sl, :] + ob_ref[sl, :], grp) * ng_ref[...]
        o_ref[0, sl, :] = y * (z * jax.nn.sigmoid(z))
        return carry

    lax.fori_loop(0, n, gate, 0)


OD_DQ, OD_DK, OD_DV, OD_DZ, OD_DAB = 6, 10, 14, 18, 22


def _delta_call(p, w_conv, a_log, dt_bias, norm_g, s0):
    B, T, _ = p.shape
    n_pair = H_D // 2
    n_sys = 4 * min(DELTA_CHUNKS_PER_STEP, T // CHUNK)
    nega = jnp.zeros((1, LANES), F32).at[0, :2 * H_D].set(-jnp.exp(a_log.reshape(-1)))
    dtb = jnp.zeros((1, LANES), F32).at[0, :2 * H_D].set(dt_bias.reshape(-1))
    ng = jnp.tile(norm_g, 2).reshape(1, LANES)
    col = lambda off: pl.BlockSpec((1, T, LANES), lambda b, hp: (b, 0, off + hp))
    cw = lambda off: pl.BlockSpec((SHORT_CONV, LANES), lambda b, hp: (0, off + hp))
    vec = lambda: pl.BlockSpec((1, LANES), lambda b, hp: (0, 0))
    st = lambda: pl.BlockSpec((1, 2, 1, LANES, LANES), lambda b, hp: (b, 0, hp, 0, 0))
    o, s_fin = pl.pallas_call(
        _delta_kernel,
        grid=(B, n_pair),
        in_specs=[col(OD_DQ), col(OD_DK), col(OD_DV), col(OD_DZ),
                  pl.BlockSpec((1, T, LANES), lambda b, hp: (b, 0, OD_DAB)),
                  cw(0), cw(n_pair), cw(2 * n_pair), vec(), vec(), vec(), st()],
        out_specs=[pl.BlockSpec((1, T, LANES), lambda b, hp: (b, 0, hp)), st()],
        out_shape=[jax.ShapeDtypeStruct((B, T, n_pair * LANES), F32),
                   jax.ShapeDtypeStruct((B, 2, n_pair, LANES, LANES), F32)],
        scratch_shapes=[pltpu.VMEM((T, LANES), F32), pltpu.VMEM((T, LANES), F32), pltpu.VMEM((T, LANES), F32),
                        pltpu.VMEM((T, LANES), F32), pltpu.VMEM((2, LANES, LANES), F32),
                        pltpu.VMEM((2, n_sys, 3, CHUNK, LANES), F32),
                        pltpu.VMEM((2, n_sys, CHUNK, CHUNK), BF16),
                        pltpu.VMEM((2, n_sys, SUBLANES, LANES), F32)],
        compiler_params=_params("parallel", "parallel"),
    )(p, p, p, p, p, w_conv, w_conv, w_conv, nega, dtb, ng, _pair_block_diag(s0))
    return o, _pair_block_diag_inv(s_fin)


def _rmsnorm(x, g):
    y = x * lax.rsqrt(jnp.mean(x * x, axis=-1, keepdims=True) + EPS)
    return y * g


def _l2norm(x):
    return x * lax.rsqrt(jnp.sum(x * x, axis=-1, keepdims=True) + EPS)


def _split_cols(p, sizes):
    cuts = [int(s) for s in np.cumsum(sizes)[:-1]]
    return jnp.split(p, cuts, axis=-1)


def _dwconv(x, w):
    K = w.shape[0]
    T = x.shape[1]
    pad = K // 2
    xp = jnp.pad(x, ((0, 0), (pad, pad), (0, 0)))
    out = xp[:, 0:T] * w[0]
    for i in range(1, K):
        out = out + xp[:, i:i + T] * w[i]
    return out


def _softmax_sink(s, sink):
    if sink is None:
        return jax.nn.softmax(s, axis=-1)
    m = jnp.maximum(jnp.max(s, axis=-1, keepdims=True), sink)
    e = jnp.exp(s - m)
    return e / (jnp.sum(e, axis=-1, keepdims=True) + jnp.exp(sink - m))


def _axial_rope(x):
    T = x.shape[1]
    t = jnp.arange(T)
    half = HEAD_DIM // 2
    quarter = half // 2
    inv = 1.0 / (ROPE_BASE ** (jnp.arange(quarter, dtype=F32) / quarter))

    def rot(xa, pos):
        ang = pos.astype(F32)[:, None] * inv[None, :]
        cos = jnp.cos(ang)[None, :, None, :]
        sin = jnp.sin(ang)[None, :, None, :]
        x1, x2 = xa[..., :quarter], xa[..., quarter:]
        return jnp.concatenate([x1 * cos - x2 * sin, x1 * sin + x2 * cos], axis=-1)

    return jnp.concatenate([rot(x[..., :half], t // GRID_W), rot(x[..., half:], t % GRID_W)], axis=-1)


def _dense_attn(q, k, v, sink):
    B, Tq, Hq, Dh = q.shape
    Hk = k.shape[1]
    G = Hq // Hk
    nb = Tq // Q_BLOCK
    qb = q.reshape(B, nb, Q_BLOCK, Hk, G, Dh).swapaxes(0, 1)
    sk = None if sink is None else sink.reshape(Hk, G, 1, 1)
    scale = Dh ** -0.5

    def one(qblk):
        s = jnp.einsum('bqkgd,bksd->bkgqs', qblk, k) * scale
        p = _softmax_sink(s, sk)
        return jnp.einsum('bkgqs,bksd->bqkgd', p, v)

    o = lax.map(one, qb)
    return o.swapaxes(0, 1).reshape(B, Tq, Hq, Dh)


def _neighborhood_attn(q, k, v, rpb, ck, cv):
    B, T, H, Dh = q.shape
    R = T // GRID_W
    kr = min(WIN_R, R)
    kc = WIN_C
    qg = q.reshape(B, R, GRID_W, H, Dh)
    kg = k.reshape(B, R, GRID_W, H, Dh)
    vg = v.reshape(B, R, GRID_W, H, Dh)
    col = jnp.arange(GRID_W)
    col_idx = jnp.clip(col - kc // 2, 0, GRID_W - kc)[:, None] + jnp.arange(kc)[None, :]
    dc = col_idx - col[:, None] + (WIN_C - 1)
    scale = Dh ** -0.5

    def one(i):
        r0 = jnp.clip(i - kr // 2, 0, R - kr)
        kb = lax.dynamic_slice_in_dim(kg, r0, kr, axis=1)[:, :, col_idx]
        vb = lax.dynamic_slice_in_dim(vg, r0, kr, axis=1)[:, :, col_idx]
        qi = lax.dynamic_index_in_dim(qg, i, axis=1, keepdims=False)
        dr = r0 + jnp.arange(kr) - i + (WIN_R - 1)
        bias = rpb[:, dr[None, :, None], dc[:, None, :]]
        s_loc = jnp.einsum('bwhd,brwchd->bhwrc', qi, kb) * scale + bias
        s_loc = s_loc.reshape(B, H, GRID_W, kr * kc)
        s_ctx = jnp.einsum('bwhd,bhld->bhwl', qi, ck) * scale
        p = jax.nn.softmax(jnp.concatenate([s_loc, s_ctx], axis=-1), axis=-1)
        p_loc = p[..., :kr * kc].reshape(B, H, GRID_W, kr, kc)
        return (jnp.einsum('bhwrc,brwchd->bwhd', p_loc, vb)
                + jnp.einsum('bhwl,bhld->bwhd', p[..., kr * kc:], cv))

    o = lax.map(one, jnp.arange(R))
    return o.swapaxes(0, 1).reshape(B, T, H, Dh)


def _window_attn(q, k, v, sink, ck, cv):
    B, T, Hq, Dh = q.shape
    Hk = k.shape[2]
    G = Hq // Hk
    blk = Q_BLOCK
    nb = T // blk
    pad = ((0, 0), (blk, blk), (0, 0), (0, 0))
    kp = jnp.pad(k, pad)
    vp = jnp.pad(v, pad)
    sk = sink.reshape(1, Hk, G, 1, 1)
    scale = Dh ** -0.5

    def one(n):
        q0 = n * blk
        qn = lax.dynamic_slice_in_dim(q, q0, blk, axis=1).reshape(B, blk, Hk, G, Dh)
        kn = lax.dynamic_slice_in_dim(kp, q0, 3 * blk, axis=1)
        vn = lax.dynamic_slice_in_dim(vp, q0, 3 * blk, axis=1)
        qpos = q0 + jnp.arange(blk)
        kpos = q0 - blk + jnp.arange(3 * blk)
        valid = ((jnp.abs(kpos[None, :] - qpos[:, None]) <= WIN_1D)
                 & (kpos >= 0)[None, :] & (kpos < T)[None, :])
        s_loc = jnp.einsum('bqkgd,bskd->bkgqs', qn, kn) * scale
        s_loc = jnp.where(valid, s_loc, -jnp.inf)
        s_ctx = jnp.einsum('bqkgd,bksd->bkgqs', qn, ck) * scale
        p = _softmax_sink(jnp.concatenate([s_loc, s_ctx], axis=-1), sk)
        o = (jnp.einsum('bkgqs,bskd->bqkgd', p[..., :3 * blk], vn)
             + jnp.einsum('bkgqs,bksd->bqkgd', p[..., 3 * blk:], cv))
        return o.reshape(B, blk, Hq, Dh)

    o = lax.map(one, jnp.arange(nb))
    return o.swapaxes(0, 1).reshape(B, T, Hq, Dh)


def _gla_scan(q, k, v, g, s0):
    B, T, H, dk = q.shape
    dv = v.shape[-1]
    n = T // CHUNK
    q, k, v, g = [a.reshape(B, n, CHUNK, H, a.shape[-1]) for a in (q, k, v, g)]
    G = jnp.cumsum(g, axis=2)
    G_last = G[:, :, -1:]
    q_t = q * jnp.exp(G)
    k_t = k * jnp.exp(-G)
    k_end = k * jnp.exp(G_last - G)
    causal = jnp.tril(jnp.ones((CHUNK, CHUNK), bool))
    att = jnp.where(causal, jnp.einsum('bnchd,bnshd->bnhcs', q_t, k_t), 0.0)
    o_intra = jnp.einsum('bnhcs,bnshv->bnchv', att, v)
    u = jnp.einsum('bnshd,bnshv->bnhdv', k_end, v)
    decay = jnp.exp(G_last[:, :, 0])

    def step(S, inp):
        d, du = inp
        return d[..., None] * S + du, S

    S_fin, S_start = lax.scan(step, s0, (decay.swapaxes(0, 1), u.swapaxes(0, 1)))
    o_inter = jnp.einsum('bnchd,bnhdv->bnchv', q_t, S_start.swapaxes(0, 1))
    return (o_intra + o_inter).reshape(B, T, H, dv), S_fin


def _delta_scan(q, k, v, beta, g, s0):
    B, T, H, dk = q.shape
    dv = v.shape[-1]
    n = T // CHUNK
    blk = lambda a: a.reshape((B, n, CHUNK) + a.shape[2:]).swapaxes(2, 3)
    q, k, v, beta, g = [blk(a) for a in (q, k, v, beta, g)]
    G = jnp.cumsum(g, axis=-1)
    lower = jnp.tril(jnp.ones((CHUNK, CHUNK), bool))
    strict = jnp.tril(jnp.ones((CHUNK, CHUNK), bool), -1)
    diff = G[..., :, None] - G[..., None, :]
    gam = jnp.where(lower, jnp.exp(jnp.where(lower, diff, 0.0)), 0.0)
    k_beta = k * beta[..., None]
    a_mat = jnp.where(strict, jnp.einsum('bnhcd,bnhsd->bnhcs', k_beta, k) * gam, 0.0)
    m_mat = a_mat + jnp.eye(CHUNK, dtype=F32)
    rhs = jnp.concatenate([v * beta[..., None], k_beta * jnp.exp(G)[..., None]], axis=-1)
    sol = lax.linalg.triangular_solve(m_mat, rhs, left_side=True, lower=True, unit_diagonal=True)
    w_val, k_cum = sol[..., :dv], sol[..., dv:]
    a_qk = jnp.einsum('bnhcd,bnhsd->bnhcs', q, k) * gam
    q_g = q * jnp.exp(G)[..., None]
    k_end = k * jnp.exp(G[..., -1:] - G)[..., None]
    d_last = jnp.exp(G[..., -1])

    def step(S, inp):
        aqk, wv, kc, qg, ke, d = inp
        v_new = wv - jnp.einsum('bhcd,bhdv->bhcv', kc, S)
        o = jnp.einsum('bhcd,bhdv->bhcv', qg, S) + jnp.einsum('bhcs,bhsv->bhcv', aqk, v_new)
        S = S * d[..., None, None] + jnp.einsum('bhcd,bhcv->bhdv', ke, v_new)
        return S, o

    xs = tuple(a.swapaxes(0, 1) for a in (a_qk, w_val, k_cum, q_g, k_end, d_last))
    S_fin, o = lax.scan(step, s0, xs)
    return o.transpose(1, 0, 3, 2, 4).reshape(B, T, H, dv), S_fin


def _gla_mixer(bq, bk, bv, glr, br, w_g2, b_g, norm_g, s0):
    B, T, _ = bq.shape
    q = bq.reshape(B, T, H_B, DK_B) * DK_B ** -0.5
    k = bk.reshape(B, T, H_B, DK_B)
    v = bv.reshape(B, T, H_B, DV_B)
    z = jnp.einsum('btzr,zrc->btzc', glr.reshape(B, T, 2, GLA_RANK), w_g2) + b_g
    g = (jax.nn.log_sigmoid(z) / GLA_TAU).reshape(B, T, 2, H_B, DK_B)
    fl = lambda a: jnp.flip(a, 1)
    o_f, s_f = _gla_scan(q, k, v, g[:, :, 0], s0[:, 0])
    o_b, s_b = _gla_scan(fl(q), fl(k), fl(v), fl(g[:, :, 1]), s0[:, 1])
    o = o_f + fl(o_b)
    o = _rmsnorm(o, norm_g.reshape(H_B, DV_B)) * jax.nn.silu(br.reshape(B, T, H_B, DV_B))
    return o.reshape(B, T, H_B * DV_B), jnp.stack([s_f, s_b], axis=1)


def _delta_mixer(dq, dk, dv, da, db, dz, w_conv, a_log, dt_bias, norm_g, s0):
    B, T, _ = dq.shape
    qkv = jax.nn.silu(_dwconv(jnp.concatenate([dq, dk, dv], axis=-1), w_conv))
    q, k, v = _split_cols(qkv, (H_D * DK_D, H_D * DK_D, H_D * DV_D))
    q = _l2norm(q.reshape(B, T, H_D, DK_D)) * DK_D ** -0.5
    k = _l2norm(k.reshape(B, T, H_D, DK_D))
    v = v.reshape(B, T, H_D, DV_D)
    beta = jax.nn.sigmoid(db.reshape(B, T, 2, H_D))
    g = -jnp.exp(a_log) * jax.nn.softplus(da.reshape(B, T, 2, H_D) + dt_bias)
    fl = lambda a: jnp.flip(a, 1)
    o_f, s_f = _delta_scan(q, k, v, beta[:, :, 0], g[:, :, 0], s0[:, 0])
    o_b, s_b = _delta_scan(fl(q), fl(k), fl(v), fl(beta[:, :, 1]), fl(g[:, :, 1]), s0[:, 1])
    o = o_f + fl(o_b)
    o = _rmsnorm(o, norm_g) * jax.nn.silu(dz.reshape(B, T, H_D, DV_D))
    return o.reshape(B, T, H_D * DV_D), jnp.stack([s_f, s_b], axis=1)


def _even_mix(p, latent, rpb, w_g2, b_g, norm_g, ak_cache, av_cache, sb):
    B, T, _ = p.shape
    ak = p[..., H_A * HEAD_DIM:2 * H_A * HEAD_DIM]
    av = p[..., 2 * H_A * HEAD_DIM:3 * H_A * HEAD_DIM]
    n_pair = H_A // 2
    if latent:
        o_a = _nbr_attn_call(p, rpb, ak_cache, av_cache)
        k_h = v_h = None
    else:
        k_h = ak.reshape(B, T, H_A, HEAD_DIM).transpose(0, 2, 1, 3)
        v_h = av.reshape(B, T, H_A, HEAD_DIM).transpose(0, 2, 1, 3)
        o_a = _dense_attn_call(p, 0, n_pair, 2 * n_pair, None, False)
    o_b, s_b = _gla_call(p, w_g2, b_g, norm_g, sb)
    return o_a, o_b, k_h, v_h, s_b


def _odd_mix(p, latent, sink, w_conv, a_log, dt_bias, norm_g, ck_cache, cv_cache, sd):
    B, T, _ = p.shape
    ck = p[..., H_C * HEAD_DIM:(H_C + KV_C) * HEAD_DIM]
    cv = p[..., (H_C + KV_C) * HEAD_DIM:(H_C + 2 * KV_C) * HEAD_DIM]
    n_pair = H_C // 2
    if latent:
        o_c = _win_attn_call(p, sink, ck_cache, cv_cache)
        k_h = v_h = None
    else:
        k_h = ck.reshape(B, T, KV_C, HEAD_DIM).transpose(0, 2, 1, 3)
        v_h = cv.reshape(B, T, KV_C, HEAD_DIM).transpose(0, 2, 1, 3)
        o_c = _dense_attn_call(p, 0, n_pair, n_pair + 1, sink, True)
    o_d, s_d = _delta_call(p, w_conv, a_log, dt_bias, norm_g, sd)
    return o_c, o_d, k_h, v_h, s_d


def _reorder_cols(w, sizes, order):
    parts = _split_cols(w, sizes)
    out = jnp.concatenate([parts[i] for i in order], axis=-1)
    pad = -out.shape[-1] % LANES
    return jnp.pad(out, ((0, 0), (0, pad)))


def _reorder_even_w_in(w):
    return _reorder_cols(w, EV_SIZES, (0, 1, 2, 3, 4, 5, 7, 6))


def _reorder_odd_w_in(w):
    return _reorder_cols(w, OD_SIZES, (0, 1, 2, 3, 4, 5, 8, 6, 7))


OD_ORDER_SIZES = tuple(OD_SIZES[i] for i in (0, 1, 2, 3, 4, 5, 8, 6, 7)) + (-sum(OD_SIZES) % LANES,)


TM_LATENT = 512
TM_CONTEXT = 256
TF_FFN = 1408
TM_FFN_LATENT = 512


def kernel(x_prompt, x_sample, cache_a_k, cache_a_v, state_b, cache_c_k, cache_c_v, state_d, c, c_ctx, ada_w, ada_b, norm1_g, norm2_g, ffn_up, ffn_conv, ffn_down, ev_w_in, ev_w_out, a_rpb, b_w_g2, b_b_g, b_norm_g, od_w_in, od_w_out, c_sink, d_conv, d_a_log, d_dt_bias, d_norm_g, final_g):
    Bp, Tp, D = x_prompt.shape
    Bs, Ts, _ = x_sample.shape
    cond = jnp.concatenate([c, c_ctx[None, :], jnp.zeros((2 * SUBLANES - Bs - 1, D), F32)], axis=0)
    mods = _adaln(cond, ada_w, ada_b)
    xp, xs = x_prompt, x_sample
    streams = ((False, Bp, TM_CONTEXT), (True, Bs, TM_LATENT))
    ak_l, av_l, sb_l, ck_l, cv_l, sd_l = [], [], [], [], [], []
    for l in range(DEPTH):
        j = l // 2
        mod_s = [mods[l, :Bs, i * D:(i + 1) * D] for i in range(6)]
        mod_p = [jnp.broadcast_to(mods[l, Bs:Bs + 1, i * D:(i + 1) * D], (Bp, D)) for i in range(6)]
        w_in = (_reorder_even_w_in(ev_w_in[j]) if l % 2 == 0 else _reorder_odd_w_in(od_w_in[j])).astype(BF16)
        w_out = (ev_w_out[j] if l % 2 == 0 else od_w_out[j]).astype(BF16)
        w_up = ffn_up[l].astype(BF16)
        w_down = ffn_down[l].astype(BF16)
        new_x = []
        for (latent, B, tm), x, mod in zip(streams, (xp, xs), (mod_p, mod_s)):
            sh1, sc1, g1, sh2, sc2, g2 = mod
            p = _nm_matmul(x, norm1_g[l], sh1, sc1, w_in, tm)
            if l % 2 == 0:
                s0 = state_b[:, j] if latent else jnp.zeros((B, 2, H_B, DK_B, DV_B), F32)
                o1, o2, k_h, v_h, s_new = _even_mix(p, latent, a_rpb[j], b_w_g2[j], b_b_g[j], b_norm_g[j],
                                                    cache_a_k[:, j], cache_a_v[:, j], s0)
                if not latent:
                    ak_l.append(k_h)
                    av_l.append(v_h)
                    sb_l.append(s_new)
            else:
                s0 = state_d[:, j] if latent else jnp.zeros((B, 2, H_D, DK_D, DV_D), F32)
                o1, o2, k_h, v_h, s_new = _odd_mix(p, latent, c_sink[j], d_conv[j], d_a_log[j], d_dt_bias[j],
                                                   d_norm_g[j], cache_c_k[:, j], cache_c_v[:, j], s0)
                if not latent:
                    ck_l.append(k_h)
                    cv_l.append(v_h)
                    sd_l.append(s_new)
            x = _proj_res(o1, o2, w_out, x, g1, tm)
            x = _conv_ffn_res(x, norm2_g[l], sh2, sc2, g2, w_up, ffn_conv[l], w_down,
                              TM_FFN_LATENT if latent else tm, TF_FFN)
            new_x.append(x)
        xp, xs = new_x
    y_prompt = _rmsnorm_call(xp, final_g, TM_CONTEXT)
    y_sample = _rmsnorm_call(xs, final_g, TM_LATENT)
    return (y_prompt, y_sample, jnp.stack(ak_l, axis=1), jnp.stack(av_l, axis=1), jnp.stack(sb_l, axis=1),
            jnp.stack(ck_l, axis=1), jnp.stack(cv_l, axis=1), jnp.stack(sd_l, axis=1))
```

```python
import functools
import math

import jax
import jax.numpy as jnp
import numpy as np
from jax import lax
from jax.experimental import pallas as pl
from jax.experimental.pallas import tpu as pltpu

D_MODEL = 1024
DEPTH = 4
GRID_W = 64
HEAD_DIM = 64
H_A = D_MODEL // (2 * HEAD_DIM)
WIN_R = 8
WIN_C = 16
H_B = D_MODEL // (2 * HEAD_DIM)
DK_B = HEAD_DIM
DV_B = HEAD_DIM
GLA_RANK = 16
GLA_TAU = 16.0
H_C = D_MODEL // (2 * HEAD_DIM)
KV_C = H_C // 4
WIN_1D = 128
ROPE_BASE = 10000.0
H_D = D_MODEL // (2 * HEAD_DIM)
DK_D = HEAD_DIM
DV_D = HEAD_DIM
SHORT_CONV = 3
CHUNK = 64
EPS = 1e-6
EV_SIZES = (H_A * HEAD_DIM, H_A * HEAD_DIM, H_A * HEAD_DIM, H_B * DK_B, H_B * DK_B, H_B * DV_B, 2 * GLA_RANK, H_B * DV_B)
OD_SIZES = (H_C * HEAD_DIM, KV_C * HEAD_DIM, KV_C * HEAD_DIM, H_D * DK_D, H_D * DK_D, H_D * DV_D, 2 * H_D, 2 * H_D, H_D * DV_D)
F32 = jnp.float32
BF16 = jnp.bfloat16

V7X_VMEM_LIMIT_BYTES = 52 * 1024 * 1024
SUBLANES = 8
LANES = 128
COL_CHUNK = 512
FFN_SUB_COLS = 1408


def _params(*sem):
    return pltpu.CompilerParams(dimension_semantics=sem, vmem_limit_bytes=V7X_VMEM_LIMIT_BYTES)


def _norm_mod(x, g, shift, scale):
    y = x * lax.rsqrt(jnp.mean(x * x, axis=-1, keepdims=True) + EPS)
    return (y * g) * (1.0 + scale) + shift


def _adaln_kernel(c_ref, w_ref, b_ref, o_ref):
    cond = c_ref[...]
    a = cond * jax.nn.sigmoid(cond)
    a_hi = a.astype(BF16)
    a_lo = (a - a_hi.astype(F32)).astype(BF16)
    w = w_ref[0]
    w_hi = w.astype(BF16)
    w_lo = (w - w_hi.astype(F32)).astype(BF16)
    acc = jnp.dot(a_hi, w_hi, preferred_element_type=F32)
    acc += jnp.dot(a_lo, w_hi, preferred_element_type=F32)
    acc += jnp.dot(a_hi, w_lo, preferred_element_type=F32)
    o_ref[0] = acc + b_ref[0]


def _adaln(cond, ada_w, ada_b):
    R = cond.shape[0]
    L, D, N = ada_w.shape
    tn = 1024
    return pl.pallas_call(
        _adaln_kernel,
        grid=(L, N // tn),
        in_specs=[pl.BlockSpec((R, D), lambda l, j: (0, 0)),
                  pl.BlockSpec((1, D, tn), lambda l, j: (l, 0, j)),
                  pl.BlockSpec((1, 1, tn), lambda l, j: (l, 0, j))],
        out_specs=pl.BlockSpec((1, R, tn), lambda l, j: (l, 0, j)),
        out_shape=jax.ShapeDtypeStruct((L, R, N), F32),
        compiler_params=_params("parallel", "parallel"),
    )(cond, ada_w, ada_b.reshape(L, 1, N))


def _nm_matmul_kernel(x_ref, g_ref, sh_ref, sc_ref, w_ref, o_ref):
    h = _norm_mod(x_ref[0], g_ref[...], sh_ref[0], sc_ref[0]).astype(BF16)
    n = o_ref.shape[-1]
    for c0 in range(0, n, COL_CHUNK):
        c1 = min(c0 + COL_CHUNK, n)
        o_ref[0, :, c0:c1] = jnp.dot(h, w_ref[:, c0:c1], preferred_element_type=F32)


def _nm_matmul(x, g, shift, scale, w, tm):
    B, T, D = x.shape
    N = w.shape[1]
    return pl.pallas_call(
        _nm_matmul_kernel,
        grid=(B, T // tm),
        in_specs=[pl.BlockSpec((1, tm, D), lambda b, i: (b, i, 0)),
                  pl.BlockSpec((1, D), lambda b, i: (0, 0)),
                  pl.BlockSpec((1, 1, D), lambda b, i: (b, 0, 0)),
                  pl.BlockSpec((1, 1, D), lambda b, i: (b, 0, 0)),
                  pl.BlockSpec((D, N), lambda b, i: (0, 0))],
        out_specs=pl.BlockSpec((1, tm, N), lambda b, i: (b, i, 0)),
        out_shape=jax.ShapeDtypeStruct((B, T, N), F32),
        compiler_params=_params("parallel", "parallel"),
    )(x, g.reshape(1, D), shift.reshape(B, 1, D), scale.reshape(B, 1, D), w)


def _proj_res_kernel(a1_ref, a2_ref, w1_ref, w2_ref, x_ref, gate_ref, o_ref):
    acc = jnp.dot(a1_ref[0].astype(BF16), w1_ref[...], preferred_element_type=F32)
    acc += jnp.dot(a2_ref[0].astype(BF16), w2_ref[...], preferred_element_type=F32)
    o_ref[0] = x_ref[0] + gate_ref[0] * acc


def _proj_res(a1, a2, w, x, gate, tm):
    B, T, K = a1.shape
    D = w.shape[1]
    return pl.pallas_call(
        _proj_res_kernel,
        grid=(B, T // tm),
        in_specs=[pl.BlockSpec((1, tm, K), lambda b, i: (b, i, 0)),
                  pl.BlockSpec((1, tm, K), lambda b, i: (b, i, 0)),
                  pl.BlockSpec((K, D), lambda b, i: (0, 0)),
                  pl.BlockSpec((K, D), lambda b, i: (1, 0)),
                  pl.BlockSpec((1, tm, D), lambda b, i: (b, i, 0)),
                  pl.BlockSpec((1, 1, D), lambda b, i: (b, 0, 0))],
        out_specs=pl.BlockSpec((1, tm, D), lambda b, i: (b, i, 0)),
        out_shape=jax.ShapeDtypeStruct((B, T, D), F32),
        compiler_params=_params("parallel", "parallel"),
    )(a1, a2, w, w, x, gate.reshape(B, 1, D))


def _ffn_kernel(x_ref, xp_ref, xn_ref, g_ref, sh_ref, sc_ref, gate_ref, wa_ref, wg_ref,
                ca_ref, cg_ref, wd_ref, fg_ref, o_ref, h_ref, acc_ref, *, final_norm):
    i = pl.program_id(1)
    c = pl.program_id(2)
    tm = x_ref.shape[1]
    tf = wa_ref.shape[1]

    @pl.when(c == 0)
    def _():
        g, sh, sc = g_ref[...], sh_ref[0], sc_ref[0]
        h_ref[0:tm, :] = _norm_mod(x_ref[0], g, sh, sc).astype(BF16)
        hp = jnp.where(i > 0, _norm_mod(xp_ref[0], g, sh, sc), 0.0)
        hn = jnp.where(i < pl.num_programs(1) - 1, _norm_mod(xn_ref[0], g, sh, sc), 0.0)
        h_ref[tm:tm + 2 * SUBLANES, :] = jnp.concatenate([hp, hn], axis=0).astype(BF16)
        acc_ref[...] = jnp.zeros_like(acc_ref)

    h = h_ref[...]
    row = lax.broadcasted_iota(jnp.int32, (tm, 1), 0)

    def conv_branch(w_ref, cw_ref, c0, c1):
        u_all = jnp.dot(h, w_ref[:, c0:c1], preferred_element_type=F32)
        u = u_all[0:tm]
        u_prev = jnp.where(row == 0, u_all[tm + SUBLANES - 1:tm + SUBLANES], pltpu.roll(u, 1, axis=0))
        u_next = jnp.where(row == tm - 1, u_all[tm + SUBLANES:tm + SUBLANES + 1], pltpu.roll(u, tm - 1, axis=0))
        cw = cw_ref[:, c0:c1]
        return u_prev * cw[0:1] + u * cw[1:2] + u_next * cw[2:3]

    for c0 in range(0, tf, FFN_SUB_COLS):
        c1 = min(c0 + FFN_SUB_COLS, tf)
        a = conv_branch(wa_ref, ca_ref, c0, c1)
        gt = conv_branch(wg_ref, cg_ref, c0, c1)
        act = a * (gt * jax.nn.sigmoid(gt))
        acc_ref[...] += jnp.dot(act.astype(BF16), wd_ref[c0:c1, :], preferred_element_type=F32)

    @pl.when(c == pl.num_programs(2) - 1)
    def _():
        y = x_ref[0] + gate_ref[0] * acc_ref[...]
        if final_norm:
            y = (y * lax.rsqrt(jnp.mean(y * y, axis=-1, keepdims=True) + EPS)) * fg_ref[...]
        o_ref[0] = y


def _conv_ffn_res(x, g, shift, scale, gate, w_up, w_conv, w_down, tm, tf, final_g=None):
    B, T, D = x.shape
    F = w_down.shape[0]
    nf = F // tf
    nt = T // tm
    rows8 = tm // SUBLANES
    vec = lambda: pl.BlockSpec((1, 1, D), lambda b, i, c: (b, 0, 0))
    return pl.pallas_call(
        functools.partial(_ffn_kernel, final_norm=final_g is not None),
        grid=(B, nt, nf),
        in_specs=[pl.BlockSpec((1, tm, D), lambda b, i, c: (b, i, 0)),
                  pl.BlockSpec((1, SUBLANES, D), lambda b, i, c: (b, jnp.maximum(i * rows8 - 1, 0), 0)),
                  pl.BlockSpec((1, SUBLANES, D), lambda b, i, c: (b, jnp.minimum((i + 1) * rows8, T // SUBLANES - 1), 0)),
                  pl.BlockSpec((1, D), lambda b, i, c: (0, 0)),
                  vec(), vec(), vec(),
                  pl.BlockSpec((D, tf), lambda b, i, c: (0, c)),
                  pl.BlockSpec((D, tf), lambda b, i, c: (0, nf + c)),
                  pl.BlockSpec((3, tf), lambda b, i, c: (0, c)),
                  pl.BlockSpec((3, tf), lambda b, i, c: (0, nf + c)),
                  pl.BlockSpec((tf, D), lambda b, i, c: (c, 0)),
                  pl.BlockSpec((1, D), lambda b, i, c: (0, 0))],
        out_specs=pl.BlockSpec((1, tm, D), lambda b, i, c: (b, i, 0)),
        out_shape=jax.ShapeDtypeStruct((B, T, D), F32),
        scratch_shapes=[pltpu.VMEM((tm + 2 * SUBLANES, D), BF16),
                        pltpu.VMEM((tm, D), F32)],
        compiler_params=_params("parallel", "parallel", "arbitrary"),
    )(x, x, x, g.reshape(1, D), shift.reshape(B, 1, D), scale.reshape(B, 1, D), gate.reshape(B, 1, D),
      w_up, w_up, w_conv, w_conv, w_down, (g if final_g is None else final_g).reshape(1, D))


NEG_BIG = -1e30
ATTN_SCALE = HEAD_DIM ** -0.5


def _dot_nt(a, b):
    return lax.dot_general(a, b, (((1,), (1,)), ((), ())), preferred_element_type=F32)


def _lane_group():
    return lax.broadcasted_iota(jnp.int32, (1, LANES), 1) // HEAD_DIM


def _attend(score_fns, values, sink):
    scores = [fn() for fn in score_fns]
    yield
    s = scores[0] if len(scores) == 1 else jnp.concatenate(scores, axis=1)
    m = jnp.max(s, axis=-1, keepdims=True)
    if sink is not None:
        m = jnp.maximum(m, sink)
    e = jnp.exp(s - m)
    den = jnp.sum(e, axis=-1, keepdims=True)
    if sink is not None:
        den = den + jnp.exp(sink - m)
    eb = e.astype(BF16)
    out = None
    c0 = 0
    for v in values:
        o = jnp.dot(eb[:, c0:c0 + v.shape[0]], v, preferred_element_type=F32)
        out = o if out is None else out + o
        c0 += v.shape[0]
    yield
    return out / den


def _merge_heads(outs, grp):
    return jnp.where(grp == 0, outs[0], outs[1])


def _dense_attn_kernel(q_ref, k_ref, v_ref, sink_ref, o_ref, *, gqa, use_sink):
    grp = _lane_group()
    q = q_ref[0] * ATTN_SCALE
    k = k_ref[0]
    v = v_ref[0]
    if gqa:
        kvg = pl.program_id(1) // 2
        k = jnp.where(grp == kvg, k, pltpu.roll(k, HEAD_DIM, axis=1))
        v = jnp.where(grp == kvg, v, pltpu.roll(v, HEAD_DIM, axis=1))
    kb = k.astype(BF16)
    vb = v.astype(BF16)

    def head(h):
        qh = jnp.where(grp == h, q, 0.0).astype(BF16)
        sink = sink_ref[0, h][:, :1] if use_sink else None
        return _attend([lambda: _dot_nt(qh, kb)], [vb], sink)

    o_ref[0] = _merge_heads(_lockstep([head(0), head(1)]), grp)


def _dense_attn_call(p, q_blk, k_blk, v_blk, sink, gqa):
    B, T, _ = p.shape
    n_pair = H_A // 2
    use_sink = sink is not None
    if use_sink:
        sink_arr = jnp.broadcast_to(sink.reshape(n_pair, 2, 1, 1), (n_pair, 2, 1, LANES))
    else:
        sink_arr = jnp.zeros((n_pair, 2, 1, LANES), F32)
    kv_idx = (lambda hp: 0) if gqa else (lambda hp: hp)
    return pl.pallas_call(
        functools.partial(_dense_attn_kernel, gqa=gqa, use_sink=use_sink),
        grid=(B, n_pair),
        in_specs=[pl.BlockSpec((1, T, LANES), lambda b, hp: (b, 0, q_blk + hp)),
                  pl.BlockSpec((1, T, LANES), lambda b, hp: (b, 0, k_blk + kv_idx(hp))),
                  pl.BlockSpec((1, T, LANES), lambda b, hp: (b, 0, v_blk + kv_idx(hp))),
                  pl.BlockSpec((1, 2, 1, LANES), lambda b, hp: (hp, 0, 0, 0))],
        out_specs=pl.BlockSpec((1, T, LANES), lambda b, hp: (b, 0, hp)),
        out_shape=jax.ShapeDtypeStruct((B, T, n_pair * LANES), F32),
        compiler_params=_params("parallel", "parallel"),
    )(p, p, p, sink_arr)


NBR_SPAN = WIN_R * GRID_W
ATTN_BLOCKS_PER_STEP = 8


def _nbr_attn_kernel(q_ref, k_ref, v_ref, ck_ref, cv_ref, bias_ref, o_ref, kb_ref, vb_ref):
    kb_ref[...] = k_ref[0].astype(BF16)
    vb_ref[...] = v_ref[0].astype(BF16)
    ckb = ck_ref[0].astype(BF16)
    cvb = cv_ref[0].astype(BF16)
    grp = _lane_group()
    n_rows = q_ref.shape[1] // GRID_W

    def head(i, h):
        r0 = jnp.clip(i - WIN_R // 2, 0, n_rows - WIN_R)
        q = q_ref[0, pl.ds(pl.multiple_of(i * GRID_W, GRID_W), GRID_W), :] * ATTN_SCALE
        start = pl.multiple_of(r0 * GRID_W, GRID_W)
        kw = kb_ref[pl.ds(start, NBR_SPAN), :]
        vw = vb_ref[pl.ds(start, NBR_SPAN), :]
        qh = jnp.where(grp == h, q, 0.0).astype(BF16)
        return _attend([lambda: _dot_nt(qh, kw) + bias_ref[i - r0, h], lambda: _dot_nt(qh, ckb)], [vw, cvb], None)

    def rows(step, carry):
        ids = [step * ATTN_BLOCKS_PER_STEP + j for j in range(ATTN_BLOCKS_PER_STEP)]
        outs = _lockstep([head(i, h) for i in ids for h in range(2)])
        for j, i in enumerate(ids):
            o_ref[0, pl.ds(pl.multiple_of(i * GRID_W, GRID_W), GRID_W), :] = _merge_heads(outs[2 * j:2 * j + 2], grp)
        return carry

    lax.fori_loop(0, n_rows // ATTN_BLOCKS_PER_STEP, rows, 0)


def _nbr_bias_table(rpb):
    var = np.arange(WIN_R)[:, None, None]
    r = np.arange(WIN_R)[None, :, None]
    row_sel = (r - var + (WIN_R - 1) == np.arange(2 * WIN_R - 1)[None, None, :]).astype(np.float32)
    w = np.arange(GRID_W)[:, None, None]
    cc = np.arange(GRID_W)[None, :, None]
    c0 = np.clip(w - WIN_C // 2, 0, GRID_W - WIN_C)
    valid = (cc >= c0) & (cc < c0 + WIN_C)
    col_sel = (valid & (cc - w + (WIN_C - 1) == np.arange(2 * WIN_C - 1)[None, None, :])).astype(np.float32)
    t = jnp.einsum('vra,hab->vhrb', row_sel, rpb, precision=lax.Precision.HIGHEST)
    tbl = jnp.einsum('vhrb,wcb->vhwrc', t, col_sel, precision=lax.Precision.HIGHEST)
    tbl = jnp.where(valid[None, None, :, None, :, 0], tbl, NEG_BIG)
    return tbl.reshape(WIN_R, rpb.shape[0], GRID_W, NBR_SPAN)


def _nbr_attn_call(p, rpb, ck_cache, cv_cache):
    B, T, _ = p.shape
    n_pair = H_A // 2
    L = ck_cache.shape[2]
    ckp = ck_cache.transpose(0, 2, 1, 3).reshape(B, L, H_A * HEAD_DIM)
    cvp = cv_cache.transpose(0, 2, 1, 3).reshape(B, L, H_A * HEAD_DIM)
    bias = _nbr_bias_table(rpb)
    col = lambda off: pl.BlockSpec((1, T, LANES), lambda b, hp: (b, 0, off + hp))
    ctx = lambda: pl.BlockSpec((1, L, LANES), lambda b, hp: (b, 0, hp))
    return pl.pallas_call(
        _nbr_attn_kernel,
        grid=(B, n_pair),
        in_specs=[col(0), col(n_pair), col(2 * n_pair), ctx(), ctx(),
                  pl.BlockSpec((WIN_R, 2, GRID_W, NBR_SPAN), lambda b, hp: (0, hp, 0, 0))],
        out_specs=pl.BlockSpec((1, T, LANES), lambda b, hp: (b, 0, hp)),
        out_shape=jax.ShapeDtypeStruct((B, T, n_pair * LANES), F32),
        scratch_shapes=[pltpu.VMEM((T, LANES), BF16), pltpu.VMEM((T, LANES), BF16)],
        compiler_params=_params("parallel", "parallel"),
    )(p, p, p, ckp, cvp, bias)


WIN_SPAN = 3 * WIN_1D
WIN_Q_TILE = 64
WIN_TILES_PER_STEP = 4
ROPE_SWAP = HEAD_DIM // 4


def _rope(x, cos, sin_signed):
    lane = lax.broadcasted_iota(jnp.int32, (1, LANES), 1)
    first = lane % (2 * ROPE_SWAP) < ROPE_SWAP
    partner = jnp.where(first, pltpu.roll(x, LANES - ROPE_SWAP, axis=1), pltpu.roll(x, ROPE_SWAP, axis=1))
    return x * cos + partner * sin_signed


def _win_attn_kernel(q_ref, k_ref, v_ref, ck_ref, cv_ref, cos_ref, sin_ref, sink_ref, o_ref, kb_ref, vb_ref):
    grp = _lane_group()
    kvg = pl.program_id(1) // 2
    both = lambda a: jnp.where(grp == kvg, a, pltpu.roll(a, HEAD_DIM, axis=1))
    kb_ref[...] = _rope(both(k_ref[0]), cos_ref[...], sin_ref[...]).astype(BF16)
    vb_ref[...] = both(v_ref[0]).astype(BF16)
    ckb = both(ck_ref[0]).astype(BF16)
    cvb = both(cv_ref[0]).astype(BF16)
    T = q_ref.shape[1]
    rel = (lax.broadcasted_iota(jnp.int32, (WIN_Q_TILE, WIN_SPAN), 1)
           - lax.broadcasted_iota(jnp.int32, (WIN_Q_TILE, WIN_SPAN), 0))

    def heads(n):
        q0 = pl.multiple_of(n * WIN_Q_TILE, WIN_Q_TILE)
        start = pl.multiple_of(jnp.clip(q0 - WIN_1D, 0, T - WIN_SPAN), WIN_Q_TILE)
        q = _rope(q_ref[0, pl.ds(q0, WIN_Q_TILE), :], cos_ref[pl.ds(q0, WIN_Q_TILE), :],
                  sin_ref[pl.ds(q0, WIN_Q_TILE), :]) * ATTN_SCALE
        kw = kb_ref[pl.ds(start, WIN_SPAN), :]
        vw = vb_ref[pl.ds(start, WIN_SPAN), :]
        valid = jnp.abs(rel + (start - q0)) <= WIN_1D

        def head(h):
            qh = jnp.where(grp == h, q, 0.0).astype(BF16)
            return _attend([lambda: jnp.where(valid, _dot_nt(qh, kw), NEG_BIG), lambda: _dot_nt(qh, ckb)],
                           [vw, cvb], sink_ref[0, h][:, :1])

        return [head(0), head(1)]

    def tiles(step, carry):
        ids = [step * WIN_TILES_PER_STEP + j for j in range(WIN_TILES_PER_STEP)]
        outs = _lockstep([prog for n in ids for prog in heads(n)])
        for j, n in enumerate(ids):
            o_ref[0, pl.ds(pl.multiple_of(n * WIN_Q_TILE, WIN_Q_TILE), WIN_Q_TILE), :] = _merge_heads(
                outs[2 * j:2 * j + 2], grp)
        return carry

    lax.fori_loop(0, T // WIN_Q_TILE // WIN_TILES_PER_STEP, tiles, 0)


def _rope_tables(T):
    t = jnp.arange(T)
    quarter = HEAD_DIM // 4
    inv = 1.0 / (ROPE_BASE ** (jnp.arange(quarter, dtype=F32) / quarter))
    ang_r = (t // GRID_W).astype(F32)[:, None] * inv[None, :]
    ang_c = (t % GRID_W).astype(F32)[:, None] * inv[None, :]
    cos = jnp.concatenate([jnp.cos(ang_r), jnp.cos(ang_r), jnp.cos(ang_c), jnp.cos(ang_c)], axis=-1)
    sin = jnp.concatenate([-jnp.sin(ang_r), jnp.sin(ang_r), -jnp.sin(ang_c), jnp.sin(ang_c)], axis=-1)
    return jnp.tile(cos, (1, 2)), jnp.tile(sin, (1, 2))


def _win_attn_call(p, sink, ck_cache, cv_cache):
    B, T, _ = p.shape
    n_pair = H_C // 2
    L = ck_cache.shape[2]
    ckp = ck_cache.transpose(0, 2, 1, 3).reshape(B, L, KV_C * HEAD_DIM)
    cvp = cv_cache.transpose(0, 2, 1, 3).reshape(B, L, KV_C * HEAD_DIM)
    cos, sin = _rope_tables(T)
    sink_arr = jnp.broadcast_to(sink.reshape(n_pair, 2, 1, 1), (n_pair, 2, 1, LANES))
    col = lambda blk: pl.BlockSpec((1, T, LANES), lambda b, hp: (b, 0, blk))
    ctx = lambda: pl.BlockSpec((1, L, LANES), lambda b, hp: (b, 0, 0))
    tab = lambda: pl.BlockSpec((T, LANES), lambda b, hp: (0, 0))
    return pl.pallas_call(
        _win_attn_kernel,
        grid=(B, n_pair),
        in_specs=[pl.BlockSpec((1, T, LANES), lambda b, hp: (b, 0, hp)), col(n_pair), col(n_pair + 1),
                  ctx(), ctx(), tab(), tab(),
                  pl.BlockSpec((1, 2, 1, LANES), lambda b, hp: (hp, 0, 0, 0))],
        out_specs=pl.BlockSpec((1, T, LANES), lambda b, hp: (b, 0, hp)),
        out_shape=jax.ShapeDtypeStruct((B, T, n_pair * LANES), F32),
        scratch_shapes=[pltpu.VMEM((T, LANES), BF16), pltpu.VMEM((T, LANES), BF16)],
        compiler_params=_params("parallel", "parallel"),
    )(p, p, p, ckp, cvp, cos, sin, sink_arr)


def _dot_tn(a, b):
    return lax.dot_general(a, b, (((0,), (0,)), ((), ())), preferred_element_type=F32)


def _split_bf16(x):
    hi = x.astype(BF16)
    return hi, (x - hi.astype(F32)).astype(BF16)


GLA_CHUNKS_PER_STEP = 8
DELTA_CHUNKS_PER_STEP = 4


def _lockstep(programs):
    results = [None] * len(programs)
    live = list(range(len(programs)))
    while live:
        for idx in list(live):
            try:
                next(programs[idx])
            except StopIteration as stop:
                results[idx] = stop.value
                live.remove(idx)
    return results


def _block_diag_mask():
    return (lax.broadcasted_iota(jnp.int32, (LANES, 1), 0) // HEAD_DIM) == _lane_group()


def _head_rmsnorm(o, grp):
    sq = o * o
    s0 = jnp.sum(jnp.where(grp == 0, sq, 0.0), axis=-1, keepdims=True)
    s1 = jnp.sum(jnp.where(grp == 1, sq, 0.0), axis=-1, keepdims=True)
    ms = jnp.where(grp == 0, s0, s1) * (1.0 / HEAD_DIM)
    return o * lax.rsqrt(ms + EPS)


def _gla_kernel(q_ref, k_ref, v_ref, r_ref, glr_ref, w2_ref, bg_ref, ng_ref, s0_ref, o_ref, sfin_ref,
                ob_ref, st_ref):
    T = q_ref.shape[1]
    n = T // CHUNK
    grp = _lane_group()
    bd = _block_diag_mask()
    ri = lax.broadcasted_iota(jnp.int32, (CHUNK, CHUNK), 0)
    ci = lax.broadcasted_iota(jnp.int32, (CHUNK, CHUNK), 1)
    st_ref[...] = s0_ref[0, :, 0]

    def chunk(c0, d):
        sl = pl.ds(c0, CHUNK)
        q = q_ref[0, sl, :] * (DK_B ** -0.5)
        k = k_ref[0, sl, :]
        vb = v_ref[0, sl, :].astype(BF16)
        z = jnp.dot(glr_ref[0, sl, :].astype(BF16), w2_ref[d], preferred_element_type=F32) + bg_ref[d]
        yield
        g = (jnp.minimum(z, 0.0) - jnp.log(1.0 + jnp.exp(-jnp.abs(z)))) * (1.0 / GLA_TAU)
        g_hi, g_lo = _split_bf16(g)
        mask = (ri <= ci) if d else (ri >= ci)
        tri = jnp.where(mask, 1.0, 0.0).astype(BF16)
        G = jnp.dot(tri, g_hi, preferred_element_type=F32) + jnp.dot(tri, g_lo, preferred_element_type=F32)
        yield
        g_tot = G[0:1] if d else G[CHUNK - 1:CHUNK]
        q_t = q * jnp.exp(G)
        k_t = (k * jnp.exp(-G)).astype(BF16)
        k_e = (k * jnp.exp(g_tot - G)).astype(BF16)
        att = [jnp.where(mask, _dot_nt(jnp.where(grp == h, q_t, 0.0).astype(BF16), k_t), 0.0) for h in range(2)]
        u = jnp.where(bd, _dot_tn(k_e, vb), 0.0)
        decay = jnp.exp(jnp.broadcast_to(g_tot, (LANES, LANES)).T)
        yield
        S = st_ref[d]
        st_ref[d] = decay * S + u
        o = jnp.dot(q_t.astype(BF16), S.astype(BF16), preferred_element_type=F32)
        for h in range(2):
            o = o + jnp.where(grp == h, jnp.dot(att[h].astype(BF16), vb, preferred_element_type=F32), 0.0)
        return o

    per_step = min(GLA_CHUNKS_PER_STEP, n)

    def body(i, carry):
        c_f = [pl.multiple_of((i * per_step + j) * CHUNK, CHUNK) for j in range(per_step)]
        c_b = [pl.multiple_of((n - 1 - i * per_step - j) * CHUNK, CHUNK) for j in range(per_step)]
        outs = _lockstep([chunk(c0, 0) for c0 in c_f] + [chunk(c0, 1) for c0 in c_b])
        for j in range(per_step):
            o_ref[0, pl.ds(c_f[j], CHUNK), :] = outs[j]
            ob_ref[pl.ds(c_b[j], CHUNK), :] = outs[per_step + j]
        return carry

    lax.fori_loop(0, n // per_step, body, 0)
    sfin_ref[0, :, 0] = st_ref[...]

    def gate(i, carry):
        sl = pl.ds(pl.multiple_of(i * CHUNK, CHUNK), CHUNK)
        r = r_ref[0, sl, :]
        y = _head_rmsnorm(o_ref[0, sl, :] + ob_ref[sl, :], grp) * ng_ref[...]
        o_ref[0, sl, :] = y * (r * jax.nn.sigmoid(r))
        return carry

    lax.fori_loop(0, n, gate, 0)


def _pair_block_diag(s):
    B, _, H, dk, dv = s.shape
    s = s.reshape(B, 2, H // 2, 2, dk, dv)
    z = jnp.zeros_like(s[:, :, :, 0])
    top = jnp.concatenate([s[:, :, :, 0], z], axis=-1)
    bot = jnp.concatenate([z, s[:, :, :, 1]], axis=-1)
    return jnp.concatenate([top, bot], axis=-2)


def _pair_block_diag_inv(sp):
    a = sp[..., :HEAD_DIM, :HEAD_DIM]
    b = sp[..., HEAD_DIM:, HEAD_DIM:]
    B, _, P = sp.shape[:3]
    return jnp.stack([a, b], axis=3).reshape(B, 2, 2 * P, HEAD_DIM, HEAD_DIM)


EV_BQ, EV_BK, EV_BV, EV_BR, EV_GLR = 12, 16, 20, 24, 28


def _gla_call(p, w_g2, b_g, norm_g, s0):
    B, T, _ = p.shape
    n_pair = H_B // 2
    w2 = jnp.zeros((2, LANES, H_B * DK_B), F32)
    w2 = w2.at[0, :GLA_RANK].set(w_g2[0]).at[1, GLA_RANK:2 * GLA_RANK].set(w_g2[1]).astype(BF16)
    col = lambda off: pl.BlockSpec((1, T, LANES), lambda b, hp: (b, 0, off + hp))
    st = lambda: pl.BlockSpec((1, 2, 1, LANES, LANES), lambda b, hp: (b, 0, hp, 0, 0))
    o, s_fin = pl.pallas_call(
        _gla_kernel,
        grid=(B, n_pair),
        in_specs=[col(EV_BQ), col(EV_BK), col(EV_BV), col(EV_BR),
                  pl.BlockSpec((1, T, LANES), lambda b, hp: (b, 0, EV_GLR)),
                  pl.BlockSpec((2, LANES, LANES), lambda b, hp: (0, 0, hp)),
                  pl.BlockSpec((2, 1, LANES), lambda b, hp: (0, 0, hp)),
                  pl.BlockSpec((1, LANES), lambda b, hp: (0, hp)),
                  st()],
        out_specs=[pl.BlockSpec((1, T, LANES), lambda b, hp: (b, 0, hp)), st()],
        out_shape=[jax.ShapeDtypeStruct((B, T, n_pair * LANES), F32),
                   jax.ShapeDtypeStruct((B, 2, n_pair, LANES, LANES), F32)],
        scratch_shapes=[pltpu.VMEM((T, LANES), F32), pltpu.VMEM((2, LANES, LANES), F32)],
        compiler_params=_params("parallel", "parallel"),
    )(p, p, p, p, p, w2, b_g.reshape(2, 1, H_B * DK_B), norm_g.reshape(1, H_B * DV_B), _pair_block_diag(s0))
    return o, _pair_block_diag_inv(s_fin)


def _softplus(x):
    return jnp.maximum(x, 0.0) + jnp.log(1.0 + jnp.exp(-jnp.abs(x)))


def _delta_kernel(q_ref, k_ref, v_ref, z_ref, dab_ref, cq_ref, ck_ref, cv_ref, nega_ref, dtb_ref, ng_ref, s0_ref,
                  o_ref, sfin_ref, qs_ref, ks_ref, vs_ref, ob_ref, st_ref, hw_ref, ha_ref, hd_ref):
    T = q_ref.shape[1]
    n = T // CHUNK
    hp = pl.program_id(1)
    grp = _lane_group()
    lane = lax.broadcasted_iota(jnp.int32, (1, LANES), 1)
    subl = lax.broadcasted_iota(jnp.int32, (LANES, 1), 0)
    bd = _block_diag_mask()
    ri = lax.broadcasted_iota(jnp.int32, (CHUNK, CHUNK), 0)
    ci = lax.broadcasted_iota(jnp.int32, (CHUNK, CHUNK), 1)
    eye = jnp.where(ri == ci, 1.0, 0.0)
    row = lax.broadcasted_iota(jnp.int32, (CHUNK, 1), 0)
    st_ref[...] = s0_ref[0, :, 0]

    def prep(i, carry):
        c0 = pl.multiple_of(i * CHUNK, CHUNK)
        lo = pl.multiple_of(jnp.maximum(c0 - SUBLANES, 0), SUBLANES)
        hi = pl.multiple_of(jnp.minimum(c0 + CHUNK, T - SUBLANES), SUBLANES)
        for src, cw_ref, dst, scale in ((q_ref, cq_ref, qs_ref, DK_D ** -0.5), (k_ref, ck_ref, ks_ref, 1.0),
                                        (v_ref, cv_ref, vs_ref, None)):
            x = src[0, pl.ds(c0, CHUNK), :]
            before = jnp.where(i > 0, src[0, pl.ds(lo, SUBLANES), :][SUBLANES - 1:], 0.0)
            after = jnp.where(i < n - 1, src[0, pl.ds(hi, SUBLANES), :][:1], 0.0)
            x_prev = jnp.where(row == 0, before, pltpu.roll(x, 1, axis=0))
            x_next = jnp.where(row == CHUNK - 1, after, pltpu.roll(x, CHUNK - 1, axis=0))
            cw = cw_ref[...]
            y = x_prev * cw[0:1] + x * cw[1:2] + x_next * cw[2:3]
            y = y * jax.nn.sigmoid(y)
            if scale is not None:
                sq = y * y
                s0 = jnp.sum(jnp.where(grp == 0, sq, 0.0), axis=-1, keepdims=True)
                s1 = jnp.sum(jnp.where(grp == 1, sq, 0.0), axis=-1, keepdims=True)
                y = y * lax.rsqrt(jnp.where(grp == 0, s0, s1) + EPS) * scale
            dst[pl.ds(c0, CHUNK), :] = y
        return carry

    lax.fori_loop(0, n, prep, 0)

    def chunk(c0, d):
        sl = pl.ds(c0, CHUNK)
        q = qs_ref[sl, :]
        k = ks_ref[sl, :]
        v = vs_ref[sl, :]
        ab = dab_ref[0, sl, :]
        gb = jnp.where(lane < 2 * H_D, nega_ref[...] * _softplus(ab + dtb_ref[...]), jax.nn.sigmoid(ab))
        gb_t = gb.T
        incl = (ri <= ci) if d else (ri >= ci)
        incl_t = (ri >= ci) if d else (ri <= ci)
        strict = (ri < ci) if d else (ri > ci)

        def head(h):
            jg = d * H_D + 2 * hp + h
            g_c = jnp.sum(jnp.where(lane == jg, gb, 0.0), axis=1, keepdims=True)
            beta_c = jnp.sum(jnp.where(lane == 2 * H_D + jg, gb, 0.0), axis=1, keepdims=True)
            g_r = jnp.sum(jnp.where(subl == jg, gb_t, 0.0), axis=0, keepdims=True)
            G_c = jnp.sum(jnp.where(incl, g_r, 0.0), axis=1, keepdims=True)
            G_r = jnp.sum(jnp.where(incl_t, g_c, 0.0), axis=0, keepdims=True)
            tot = jnp.sum(g_r, axis=1, keepdims=True)
            gam = jnp.where(incl, jnp.exp(jnp.where(incl, G_c - G_r, 0.0)), 0.0)
            kh = jnp.where(grp == h, k, 0.0)
            qh = jnp.where(grp == h, q, 0.0)
            vh = jnp.where(grp == h, v, 0.0)
            kb = kh * beta_c
            khb = kh.astype(BF16)
            a = jnp.where(strict, _dot_nt(kb.astype(BF16), khb) * gam, 0.0)
            a_qk = (_dot_nt(qh.astype(BF16), khb) * gam).astype(BF16)
            yield
            wide = jnp.concatenate([-a, eye], axis=1)
            for _ in range(6):
                w_hi, w_lo = _split_bf16(wide)
                prod = (jnp.dot(w_hi[:, :CHUNK], w_hi, preferred_element_type=F32)
                        + jnp.dot(w_lo[:, :CHUNK], w_hi, preferred_element_type=F32)
                        + jnp.dot(w_hi[:, :CHUNK], w_lo, preferred_element_type=F32))
                wide = prod + jnp.where(lane >= CHUNK, wide, 0.0)
                yield
            t_inv = wide[:, CHUNK:]
            e_g = jnp.exp(G_c)
            rhs = vh * beta_c + pltpu.roll(kb * e_g, HEAD_DIM, axis=1)
            sol = jnp.dot(t_inv.astype(BF16), rhs.astype(BF16), preferred_element_type=F32)
            return sol, a_qk, qh * e_g, kh * jnp.exp(tot - G_c), jnp.exp(tot)

        return head

    per_step = min(DELTA_CHUNKS_PER_STEP, n)
    n_steps = n // per_step
    n_sys = 4 * per_step

    def starts(i):
        c_f = [pl.multiple_of((i * per_step + j) * CHUNK, CHUNK) for j in range(per_step)]
        c_b = [pl.multiple_of((n - 1 - i * per_step - j) * CHUNK, CHUNK) for j in range(per_step)]
        return c_f, c_b

    def head_programs(i):
        programs = []
        for d, c0s in zip((0, 1), starts(i)):
            for c0 in c0s:
                head = chunk(c0, d)
                programs += [head(0), head(1)]
        return programs

    def store_heads(slot, results):
        for s, (r, a_qk, q_g, k_e, d_last) in enumerate(results):
            hw_ref[slot, s, 0] = r
            hw_ref[slot, s, 1] = q_g
            hw_ref[slot, s, 2] = k_e
            ha_ref[slot, s] = a_qk
            hd_ref[slot, s] = jnp.broadcast_to(d_last, (SUBLANES, LANES))

    def load_head(slot, s):
        return hw_ref[slot, s, 0], ha_ref[slot, s], hw_ref[slot, s, 1], hw_ref[slot, s, 2], hd_ref[slot, s][0:1, 0:1]

    def scan_step(d, slot, s):
        (r0, aqk0, qg0, ke0, d0), (r1, aqk1, qg1, ke1, d1) = load_head(slot, s), load_head(slot, s + 1)
        S = st_ref[d]
        Sb = S.astype(BF16)
        w_val = jnp.where(grp == 0, r0, r1)
        k_cum = pltpu.roll(jnp.where(grp == 0, r1, r0), HEAD_DIM, axis=1)
        v_new = w_val - jnp.dot(k_cum.astype(BF16), Sb, preferred_element_type=F32)
        o = jnp.dot((qg0 + qg1).astype(BF16), Sb, preferred_element_type=F32)
        yield
        vnb = v_new.astype(BF16)
        o = o + jnp.where(grp == 0, jnp.dot(aqk0, vnb, preferred_element_type=F32), 0.0)
        o = o + jnp.where(grp == 1, jnp.dot(aqk1, vnb, preferred_element_type=F32), 0.0)
        d_rows = jnp.where(subl // HEAD_DIM == 0, d0, d1)
        st_ref[d] = S * d_rows + jnp.where(bd, _dot_tn((ke0 + ke1).astype(BF16), vnb), 0.0)
        return o

    def scan_dir(d, i, slot):
        c0s = starts(i)[d]
        for j in range(per_step):
            o = yield from scan_step(d, slot, d * 2 * per_step + 2 * j)
            if d == 0:
                o_ref[0, pl.ds(c0s[j], CHUNK), :] = o
            else:
                ob_ref[pl.ds(c0s[j], CHUNK), :] = o

    store_heads(0, _lockstep(head_programs(0)))

    def body(i, carry):
        slot = i % 2
        results = _lockstep(head_programs(i) + [scan_dir(0, i - 1, 1 - slot), scan_dir(1, i - 1, 1 - slot)])
        store_heads(slot, results[:n_sys])
        return carry

    lax.fori_loop(1, n_steps, body, 0)
    last = n_steps - 1
    _lockstep([scan_dir(0, last, last % 2), scan_dir(1, last, last % 2)])
    sfin_ref[0, :, 0] = st_ref[...]

    def gate(i, carry):
        sl = pl.ds(pl.multiple_of(i * CHUNK, CHUNK), CHUNK)
        z = z_ref[0, sl, :]
        y = _head_rmsnorm(o_ref[0, sl, :] + ob_ref[sl, :], grp) * ng_ref[...]
        o_ref[0, sl, :] = y * (z * jax.nn.sigmoid(z))
        return carry

    lax.fori_loop(0, n, gate, 0)


OD_DQ, OD_DK, OD_DV, OD_DZ, OD_DAB = 6, 10, 14, 18, 22


def _delta_call(p, w_conv, a_log, dt_bias, norm_g, s0):
    B, T, _ = p.shape
    n_pair = H_D // 2
    n_sys = 4 * min(DELTA_CHUNKS_PER_STEP, T // CHUNK)
    nega = jnp.zeros((1, LANES), F32).at[0, :2 * H_D].set(-jnp.exp(a_log.reshape(-1)))
    dtb = jnp.zeros((1, LANES), F32).at[0, :2 * H_D].set(dt_bias.reshape(-1))
    ng = jnp.tile(norm_g, 2).reshape(1, LANES)
    col = lambda off: pl.BlockSpec((1, T, LANES), lambda b, hp: (b, 0, off + hp))
    cw = lambda off: pl.BlockSpec((SHORT_CONV, LANES), lambda b, hp: (0, off + hp))
    vec = lambda: pl.BlockSpec((1, LANES), lambda b, hp: (0, 0))
    st = lambda: pl.BlockSpec((1, 2, 1, LANES, LANES), lambda b, hp: (b, 0, hp, 0, 0))
    o, s_fin = pl.pallas_call(
        _delta_kernel,
        grid=(B, n_pair),
        in_specs=[col(OD_DQ), col(OD_DK), col(OD_DV), col(OD_DZ),
                  pl.BlockSpec((1, T, LANES), lambda b, hp: (b, 0, OD_DAB)),
                  cw(0), cw(n_pair), cw(2 * n_pair), vec(), vec(), vec(), st()],
        out_specs=[pl.BlockSpec((1, T, LANES), lambda b, hp: (b, 0, hp)), st()],
        out_shape=[jax.ShapeDtypeStruct((B, T, n_pair * LANES), F32),
                   jax.ShapeDtypeStruct((B, 2, n_pair, LANES, LANES), F32)],
        scratch_shapes=[pltpu.VMEM((T, LANES), F32), pltpu.VMEM((T, LANES), F32), pltpu.VMEM((T, LANES), F32),
                        pltpu.VMEM((T, LANES), F32), pltpu.VMEM((2, LANES, LANES), F32),
                        pltpu.VMEM((2, n_sys, 3, CHUNK, LANES), F32),
                        pltpu.VMEM((2, n_sys, CHUNK, CHUNK), BF16),
                        pltpu.VMEM((2, n_sys, SUBLANES, LANES), F32)],
        compiler_params=_params("parallel", "parallel"),
    )(p, p, p, p, p, w_conv, w_conv, w_conv, nega, dtb, ng, _pair_block_diag(s0))
    return o, _pair_block_diag_inv(s_fin)


def _split_cols(p, sizes):
    cuts = [int(s) for s in np.cumsum(sizes)[:-1]]
    return jnp.split(p, cuts, axis=-1)


def _even_mix(p, latent, rpb, w_g2, b_g, norm_g, ak_cache, av_cache, sb):
    B, T, _ = p.shape
    ak = p[..., H_A * HEAD_DIM:2 * H_A * HEAD_DIM]
    av = p[..., 2 * H_A * HEAD_DIM:3 * H_A * HEAD_DIM]
    n_pair = H_A // 2
    if latent:
        o_a = _nbr_attn_call(p, rpb, ak_cache, av_cache)
        k_h = v_h = None
    else:
        k_h = ak.reshape(B, T, H_A, HEAD_DIM).transpose(0, 2, 1, 3)
        v_h = av.reshape(B, T, H_A, HEAD_DIM).transpose(0, 2, 1, 3)
        o_a = _dense_attn_call(p, 0, n_pair, 2 * n_pair, None, False)
    o_b, s_b = _gla_call(p, w_g2, b_g, norm_g, sb)
    return o_a, o_b, k_h, v_h, s_b


def _odd_mix(p, latent, sink, w_conv, a_log, dt_bias, norm_g, ck_cache, cv_cache, sd):
    B, T, _ = p.shape
    ck = p[..., H_C * HEAD_DIM:(H_C + KV_C) * HEAD_DIM]
    cv = p[..., (H_C + KV_C) * HEAD_DIM:(H_C + 2 * KV_C) * HEAD_DIM]
    n_pair = H_C // 2
    if latent:
        o_c = _win_attn_call(p, sink, ck_cache, cv_cache)
        k_h = v_h = None
    else:
        k_h = ck.reshape(B, T, KV_C, HEAD_DIM).transpose(0, 2, 1, 3)
        v_h = cv.reshape(B, T, KV_C, HEAD_DIM).transpose(0, 2, 1, 3)
        o_c = _dense_attn_call(p, 0, n_pair, n_pair + 1, sink, True)
    o_d, s_d = _delta_call(p, w_conv, a_log, dt_bias, norm_g, sd)
    return o_c, o_d, k_h, v_h, s_d


def _reorder_cols(w, sizes, order):
    parts = _split_cols(w, sizes)
    out = jnp.concatenate([parts[i] for i in order], axis=-1)
    pad = -out.shape[-1] % LANES
    return jnp.pad(out, ((0, 0), (0, pad)))


def _reorder_even_w_in(w):
    return _reorder_cols(w, EV_SIZES, (0, 1, 2, 3, 4, 5, 7, 6))


def _reorder_odd_w_in(w):
    return _reorder_cols(w, OD_SIZES, (0, 1, 2, 3, 4, 5, 8, 6, 7))


TM_LATENT = 512
TM_CONTEXT = 256
TF_FFN = 1408
TM_FFN_LATENT = 512


def kernel(x_prompt, x_sample, cache_a_k, cache_a_v, state_b, cache_c_k, cache_c_v, state_d, c, c_ctx, ada_w, ada_b, norm1_g, norm2_g, ffn_up, ffn_conv, ffn_down, ev_w_in, ev_w_out, a_rpb, b_w_g2, b_b_g, b_norm_g, od_w_in, od_w_out, c_sink, d_conv, d_a_log, d_dt_bias, d_norm_g, final_g):
    Bp, Tp, D = x_prompt.shape
    Bs, Ts, _ = x_sample.shape
    cond = jnp.concatenate([c, c_ctx[None, :], jnp.zeros((2 * SUBLANES - Bs - 1, D), F32)], axis=0)
    mods = _adaln(cond, ada_w, ada_b)
    xp, xs = x_prompt, x_sample
    streams = ((False, Bp, TM_CONTEXT), (True, Bs, TM_LATENT))
    ak_l, av_l, sb_l, ck_l, cv_l, sd_l = [], [], [], [], [], []
    for l in range(DEPTH):
        j = l // 2
        mod_s = [mods[l, :Bs, i * D:(i + 1) * D] for i in range(6)]
        mod_p = [jnp.broadcast_to(mods[l, Bs:Bs + 1, i * D:(i + 1) * D], (Bp, D)) for i in range(6)]
        w_in = (_reorder_even_w_in(ev_w_in[j]) if l % 2 == 0 else _reorder_odd_w_in(od_w_in[j])).astype(BF16)
        w_out = (ev_w_out[j] if l % 2 == 0 else od_w_out[j]).astype(BF16)
        w_up = ffn_up[l].astype(BF16)
        w_down = ffn_down[l].astype(BF16)
        new_x = []
        for (latent, B, tm), x, mod in zip(streams, (xp, xs), (mod_p, mod_s)):
            sh1, sc1, g1, sh2, sc2, g2 = mod
            p = _nm_matmul(x, norm1_g[l], sh1, sc1, w_in, tm)
            if l % 2 == 0:
                s0 = state_b[:, j] if latent else jnp.zeros((B, 2, H_B, DK_B, DV_B), F32)
                o1, o2, k_h, v_h, s_new = _even_mix(p, latent, a_rpb[j], b_w_g2[j], b_b_g[j], b_norm_g[j],
                                                    cache_a_k[:, j], cache_a_v[:, j], s0)
                if not latent:
                    ak_l.append(k_h)
                    av_l.append(v_h)
                    sb_l.append(s_new)
            else:
                s0 = state_d[:, j] if latent else jnp.zeros((B, 2, H_D, DK_D, DV_D), F32)
                o1, o2, k_h, v_h, s_new = _odd_mix(p, latent, c_sink[j], d_conv[j], d_a_log[j], d_dt_bias[j],
                                                   d_norm_g[j], cache_c_k[:, j], cache_c_v[:, j], s0)
                if not latent:
                    ck_l.append(k_h)
                    cv_l.append(v_h)
                    sd_l.append(s_new)
            x = _proj_res(o1, o2, w_out, x, g1, tm)
            x = _conv_ffn_res(x, norm2_g[l], sh2, sc2, g2, w_up, ffn_conv[l], w_down,
                              TM_FFN_LATENT if latent else tm, TF_FFN, final_g if l == DEPTH - 1 else None)
            new_x.append(x)
        xp, xs = new_x
    return (xp, xs, jnp.stack(ak_l, axis=1), jnp.stack(av_l, axis=1), jnp.stack(sb_l, axis=1),
            jnp.stack(ck_l, axis=1), jnp.stack(cv_l, axis=1), jnp.stack(sd_l, axis=1))
```

```python
import functools
import math

import jax
import jax.numpy as jnp
import numpy as np
from jax import lax
from jax.experimental import pallas as pl
from jax.experimental.pallas import tpu as pltpu

D_MODEL = 1024
DEPTH = 4
GRID_W = 64
HEAD_DIM = 64
H_A = D_MODEL // (2 * HEAD_DIM)
WIN_R = 8
WIN_C = 16
H_B = D_MODEL // (2 * HEAD_DIM)
DK_B = HEAD_DIM
DV_B = HEAD_DIM
GLA_RANK = 16
GLA_TAU = 16.0
H_C = D_MODEL // (2 * HEAD_DIM)
KV_C = H_C // 4
WIN_1D = 128
ROPE_BASE = 10000.0
H_D = D_MODEL // (2 * HEAD_DIM)
DK_D = HEAD_DIM
DV_D = HEAD_DIM
SHORT_CONV = 3
CHUNK = 64
EPS = 1e-6
EV_SIZES = (H_A * HEAD_DIM, H_A * HEAD_DIM, H_A * HEAD_DIM, H_B * DK_B, H_B * DK_B, H_B * DV_B, 2 * GLA_RANK, H_B * DV_B)
OD_SIZES = (H_C * HEAD_DIM, KV_C * HEAD_DIM, KV_C * HEAD_DIM, H_D * DK_D, H_D * DK_D, H_D * DV_D, 2 * H_D, 2 * H_D, H_D * DV_D)
F32 = jnp.float32
BF16 = jnp.bfloat16

V7X_VMEM_LIMIT_BYTES = 52 * 1024 * 1024
SUBLANES = 8
LANES = 128
COL_CHUNK = 512
FFN_SUB_COLS = 1536


def _params(*sem):
    return pltpu.CompilerParams(dimension_semantics=sem, vmem_limit_bytes=V7X_VMEM_LIMIT_BYTES)


def _norm_mod(x, g, shift, scale):
    y = x * lax.rsqrt(jnp.mean(x * x, axis=-1, keepdims=True) + EPS)
    return (y * g) * (1.0 + scale) + shift


def _adaln_kernel(c_ref, w_ref, b_ref, o_ref):
    cond = c_ref[...]
    a = cond * jax.nn.sigmoid(cond)
    a_hi = a.astype(BF16)
    a_lo = (a - a_hi.astype(F32)).astype(BF16)
    w = w_ref[0]
    w_hi = w.astype(BF16)
    w_lo = (w - w_hi.astype(F32)).astype(BF16)
    acc = jnp.dot(a_hi, w_hi, preferred_element_type=F32)
    acc += jnp.dot(a_lo, w_hi, preferred_element_type=F32)
    acc += jnp.dot(a_hi, w_lo, preferred_element_type=F32)
    o_ref[0] = acc + b_ref[0]


def _adaln(cond, ada_w, ada_b):
    R = cond.shape[0]
    L, D, N = ada_w.shape
    tn = 1024
    return pl.pallas_call(
        _adaln_kernel,
        grid=(L, N // tn),
        in_specs=[pl.BlockSpec((R, D), lambda l, j: (0, 0)),
                  pl.BlockSpec((1, D, tn), lambda l, j: (l, 0, j)),
                  pl.BlockSpec((1, 1, tn), lambda l, j: (l, 0, j))],
        out_specs=pl.BlockSpec((1, R, tn), lambda l, j: (l, 0, j)),
        out_shape=jax.ShapeDtypeStruct((L, R, N), F32),
        compiler_params=_params("parallel", "parallel"),
    )(cond, ada_w, ada_b.reshape(L, 1, N))


def _nm_matmul_kernel(x_ref, g_ref, sh_ref, sc_ref, w_ref, o_ref):
    h = _norm_mod(x_ref[0], g_ref[...], sh_ref[0], sc_ref[0]).astype(BF16)
    n = o_ref.shape[-1]
    for c0 in range(0, n, COL_CHUNK):
        c1 = min(c0 + COL_CHUNK, n)
        o_ref[0, :, c0:c1] = jnp.dot(h, w_ref[:, c0:c1], preferred_element_type=F32)


def _nm_matmul(x, g, shift, scale, w, tm):
    B, T, D = x.shape
    N = w.shape[1]
    return pl.pallas_call(
        _nm_matmul_kernel,
        grid=(B, T // tm),
        in_specs=[pl.BlockSpec((1, tm, D), lambda b, i: (b, i, 0)),
                  pl.BlockSpec((1, D), lambda b, i: (0, 0)),
                  pl.BlockSpec((1, 1, D), lambda b, i: (b, 0, 0)),
                  pl.BlockSpec((1, 1, D), lambda b, i: (b, 0, 0)),
                  pl.BlockSpec((D, N), lambda b, i: (0, 0))],
        out_specs=pl.BlockSpec((1, tm, N), lambda b, i: (b, i, 0)),
        out_shape=jax.ShapeDtypeStruct((B, T, N), F32),
        compiler_params=_params("parallel", "parallel"),
    )(x, g.reshape(1, D), shift.reshape(B, 1, D), scale.reshape(B, 1, D), w)


def _proj_res_kernel(a1_ref, a2_ref, w1_ref, w2_ref, x_ref, gate_ref, o_ref):
    acc = jnp.dot(a1_ref[0].astype(BF16), w1_ref[...], preferred_element_type=F32)
    acc += jnp.dot(a2_ref[0].astype(BF16), w2_ref[...], preferred_element_type=F32)
    o_ref[0] = x_ref[0] + gate_ref[0] * acc


def _proj_res(a1, a2, w, x, gate, tm):
    B, T, K = a1.shape
    D = w.shape[1]
    return pl.pallas_call(
        _proj_res_kernel,
        grid=(B, T // tm),
        in_specs=[pl.BlockSpec((1, tm, K), lambda b, i: (b, i, 0)),
                  pl.BlockSpec((1, tm, K), lambda b, i: (b, i, 0)),
                  pl.BlockSpec((K, D), lambda b, i: (0, 0)),
                  pl.BlockSpec((K, D), lambda b, i: (1, 0)),
                  pl.BlockSpec((1, tm, D), lambda b, i: (b, i, 0)),
                  pl.BlockSpec((1, 1, D), lambda b, i: (b, 0, 0))],
        out_specs=pl.BlockSpec((1, tm, D), lambda b, i: (b, i, 0)),
        out_shape=jax.ShapeDtypeStruct((B, T, D), F32),
        compiler_params=_params("parallel", "parallel"),
    )(a1, a2, w, w, x, gate.reshape(B, 1, D))


def _ffn_kernel(x_ref, xp_ref, xn_ref, g_ref, sh_ref, sc_ref, gate_ref, wu_ref, cw_ref, wd_ref, fg_ref,
                o_ref, h_ref, *, final_norm):
    i = pl.program_id(1)
    tm = x_ref.shape[1]
    F = wd_ref.shape[0]
    g, sh, sc = g_ref[...], sh_ref[0], sc_ref[0]
    h_ref[0:tm, :] = _norm_mod(x_ref[0], g, sh, sc).astype(BF16)
    hp = jnp.where(i > 0, _norm_mod(xp_ref[0], g, sh, sc), 0.0)
    hn = jnp.where(i < pl.num_programs(1) - 1, _norm_mod(xn_ref[0], g, sh, sc), 0.0)
    h_ref[tm:tm + 2 * SUBLANES, :] = jnp.concatenate([hp, hn], axis=0).astype(BF16)
    h = h_ref[...]
    row = lax.broadcasted_iota(jnp.int32, (tm, 1), 0)

    def conv_branch(c0, c1):
        u_all = jnp.dot(h, wu_ref[:, c0:c1], preferred_element_type=F32)
        u = u_all[0:tm]
        u_prev = jnp.where(row == 0, u_all[tm + SUBLANES - 1:tm + SUBLANES], pltpu.roll(u, 1, axis=0))
        u_next = jnp.where(row == tm - 1, u_all[tm + SUBLANES:tm + SUBLANES + 1], pltpu.roll(u, tm - 1, axis=0))
        cw = cw_ref[:, c0:c1]
        return u_prev * cw[0:1] + u * cw[1:2] + u_next * cw[2:3]

    acc = None
    for c0 in range(0, F, FFN_SUB_COLS):
        c1 = min(c0 + FFN_SUB_COLS, F)
        a = conv_branch(c0, c1)
        gt = conv_branch(F + c0, F + c1)
        act = a * (gt * jax.nn.sigmoid(gt))
        part = jnp.dot(act.astype(BF16), wd_ref[c0:c1, :], preferred_element_type=F32)
        acc = part if acc is None else acc + part

    y = x_ref[0] + gate_ref[0] * acc
    if final_norm:
        y = (y * lax.rsqrt(jnp.mean(y * y, axis=-1, keepdims=True) + EPS)) * fg_ref[...]
    o_ref[0] = y


def _conv_ffn_res(x, g, shift, scale, gate, w_up, w_conv, w_down, tm, final_g=None):
    B, T, D = x.shape
    F = w_down.shape[0]
    nt = T // tm
    rows8 = tm // SUBLANES
    vec = lambda: pl.BlockSpec((1, 1, D), lambda b, i: (b, 0, 0))
    resident = lambda shape: pl.BlockSpec(shape, lambda b, i: (0, 0), pipeline_mode=pl.Buffered(1))
    return pl.pallas_call(
        functools.partial(_ffn_kernel, final_norm=final_g is not None),
        grid=(B, nt),
        in_specs=[pl.BlockSpec((1, tm, D), lambda b, i: (b, i, 0)),
                  pl.BlockSpec((1, SUBLANES, D), lambda b, i: (b, jnp.maximum(i * rows8 - 1, 0), 0)),
                  pl.BlockSpec((1, SUBLANES, D), lambda b, i: (b, jnp.minimum((i + 1) * rows8, T // SUBLANES - 1), 0)),
                  pl.BlockSpec((1, D), lambda b, i: (0, 0)),
                  vec(), vec(), vec(),
                  resident((D, 2 * F)), resident((3, 2 * F)), resident((F, D)),
                  pl.BlockSpec((1, D), lambda b, i: (0, 0))],
        out_specs=pl.BlockSpec((1, tm, D), lambda b, i: (b, i, 0)),
        out_shape=jax.ShapeDtypeStruct((B, T, D), F32),
        scratch_shapes=[pltpu.VMEM((tm + 2 * SUBLANES, D), BF16)],
        compiler_params=_params("parallel", "parallel"),
    )(x, x, x, g.reshape(1, D), shift.reshape(B, 1, D), scale.reshape(B, 1, D), gate.reshape(B, 1, D),
      w_up, w_conv, w_down, (g if final_g is None else final_g).reshape(1, D))


NEG_BIG = -1e30
ATTN_SCALE = HEAD_DIM ** -0.5


def _dot_nt(a, b):
    return lax.dot_general(a, b, (((1,), (1,)), ((), ())), preferred_element_type=F32)


def _lane_group():
    return lax.broadcasted_iota(jnp.int32, (1, LANES), 1) // HEAD_DIM


def _attend(score_fns, values, sink):
    scores = [fn() for fn in score_fns]
    yield
    s = scores[0] if len(scores) == 1 else jnp.concatenate(scores, axis=1)
    m = jnp.max(s, axis=-1, keepdims=True)
    if sink is not None:
        m = jnp.maximum(m, sink)
    e = jnp.exp(s - m)
    den = jnp.sum(e, axis=-1, keepdims=True)
    if sink is not None:
        den = den + jnp.exp(sink - m)
    eb = e.astype(BF16)
    out = None
    c0 = 0
    for v in values:
        o = jnp.dot(eb[:, c0:c0 + v.shape[0]], v, preferred_element_type=F32)
        out = o if out is None else out + o
        c0 += v.shape[0]
    yield
    return out / den


def _merge_heads(outs, grp):
    return jnp.where(grp == 0, outs[0], outs[1])


def _dense_attn_kernel(q_ref, k_ref, v_ref, sink_ref, o_ref, *, gqa, use_sink):
    grp = _lane_group()
    q = q_ref[0] * ATTN_SCALE
    k = k_ref[0]
    v = v_ref[0]
    if gqa:
        kvg = pl.program_id(1) // 2
        k = jnp.where(grp == kvg, k, pltpu.roll(k, HEAD_DIM, axis=1))
        v = jnp.where(grp == kvg, v, pltpu.roll(v, HEAD_DIM, axis=1))
    kb = k.astype(BF16)
    vb = v.astype(BF16)

    def head(h):
        qh = jnp.where(grp == h, q, 0.0).astype(BF16)
        sink = sink_ref[0, h][:, :1] if use_sink else None
        return _attend([lambda: _dot_nt(qh, kb)], [vb], sink)

    o_ref[0] = _merge_heads(_lockstep([head(0), head(1)]), grp)


def _dense_attn_call(p, q_blk, k_blk, v_blk, sink, gqa):
    B, T, _ = p.shape
    n_pair = H_A // 2
    use_sink = sink is not None
    if use_sink:
        sink_arr = jnp.broadcast_to(sink.reshape(n_pair, 2, 1, 1), (n_pair, 2, 1, LANES))
    else:
        sink_arr = jnp.zeros((n_pair, 2, 1, LANES), F32)
    kv_idx = (lambda hp: 0) if gqa else (lambda hp: hp)
    return pl.pallas_call(
        functools.partial(_dense_attn_kernel, gqa=gqa, use_sink=use_sink),
        grid=(B, n_pair),
        in_specs=[pl.BlockSpec((1, T, LANES), lambda b, hp: (b, 0, q_blk + hp)),
                  pl.BlockSpec((1, T, LANES), lambda b, hp: (b, 0, k_blk + kv_idx(hp))),
                  pl.BlockSpec((1, T, LANES), lambda b, hp: (b, 0, v_blk + kv_idx(hp))),
                  pl.BlockSpec((1, 2, 1, LANES), lambda b, hp: (hp, 0, 0, 0))],
        out_specs=pl.BlockSpec((1, T, LANES), lambda b, hp: (b, 0, hp)),
        out_shape=jax.ShapeDtypeStruct((B, T, n_pair * LANES), F32),
        compiler_params=_params("parallel", "parallel"),
    )(p, p, p, sink_arr)


NBR_SPAN = WIN_R * GRID_W
ATTN_BLOCKS_PER_STEP = 8


def _nbr_attn_kernel(q_ref, k_ref, v_ref, ck_ref, cv_ref, bias_ref, o_ref, kb_ref, vb_ref):
    kb_ref[...] = k_ref[0].astype(BF16)
    vb_ref[...] = v_ref[0].astype(BF16)
    ckb = ck_ref[0].astype(BF16)
    cvb = cv_ref[0].astype(BF16)
    grp = _lane_group()
    n_rows = q_ref.shape[1] // GRID_W

    def head(i, h):
        r0 = jnp.clip(i - WIN_R // 2, 0, n_rows - WIN_R)
        q = q_ref[0, pl.ds(pl.multiple_of(i * GRID_W, GRID_W), GRID_W), :] * ATTN_SCALE
        start = pl.multiple_of(r0 * GRID_W, GRID_W)
        kw = kb_ref[pl.ds(start, NBR_SPAN), :]
        vw = vb_ref[pl.ds(start, NBR_SPAN), :]
        qh = jnp.where(grp == h, q, 0.0).astype(BF16)
        return _attend([lambda: _dot_nt(qh, kw) + bias_ref[i - r0, h], lambda: _dot_nt(qh, ckb)], [vw, cvb], None)

    def rows(step, carry):
        ids = [step * ATTN_BLOCKS_PER_STEP + j for j in range(ATTN_BLOCKS_PER_STEP)]
        outs = _lockstep([head(i, h) for i in ids for h in range(2)])
        for j, i in enumerate(ids):
            o_ref[0, pl.ds(pl.multiple_of(i * GRID_W, GRID_W), GRID_W), :] = _merge_heads(outs[2 * j:2 * j + 2], grp)
        return carry

    lax.fori_loop(0, n_rows // ATTN_BLOCKS_PER_STEP, rows, 0)


def _nbr_bias_table(rpb):
    var = np.arange(WIN_R)[:, None, None]
    r = np.arange(WIN_R)[None, :, None]
    row_sel = (r - var + (WIN_R - 1) == np.arange(2 * WIN_R - 1)[None, None, :]).astype(np.float32)
    w = np.arange(GRID_W)[:, None, None]
    cc = np.arange(GRID_W)[None, :, None]
    c0 = np.clip(w - WIN_C // 2, 0, GRID_W - WIN_C)
    valid = (cc >= c0) & (cc < c0 + WIN_C)
    col_sel = (valid & (cc - w + (WIN_C - 1) == np.arange(2 * WIN_C - 1)[None, None, :])).astype(np.float32)
    t = jnp.einsum('vra,hab->vhrb', row_sel, rpb, precision=lax.Precision.HIGHEST)
    tbl = jnp.einsum('vhrb,wcb->vhwrc', t, col_sel, precision=lax.Precision.HIGHEST)
    tbl = jnp.where(valid[None, None, :, None, :, 0], tbl, NEG_BIG)
    return tbl.reshape(WIN_R, rpb.shape[0], GRID_W, NBR_SPAN)


def _nbr_attn_call(p, rpb, ck_cache, cv_cache):
    B, T, _ = p.shape
    n_pair = H_A // 2
    L = ck_cache.shape[2]
    ckp = ck_cache.transpose(0, 2, 1, 3).reshape(B, L, H_A * HEAD_DIM)
    cvp = cv_cache.transpose(0, 2, 1, 3).reshape(B, L, H_A * HEAD_DIM)
    bias = _nbr_bias_table(rpb)
    col = lambda off: pl.BlockSpec((1, T, LANES), lambda b, hp: (b, 0, off + hp))
    ctx = lambda: pl.BlockSpec((1, L, LANES), lambda b, hp: (b, 0, hp))
    return pl.pallas_call(
        _nbr_attn_kernel,
        grid=(B, n_pair),
        in_specs=[col(0), col(n_pair), col(2 * n_pair), ctx(), ctx(),
                  pl.BlockSpec((WIN_R, 2, GRID_W, NBR_SPAN), lambda b, hp: (0, hp, 0, 0))],
        out_specs=pl.BlockSpec((1, T, LANES), lambda b, hp: (b, 0, hp)),
        out_shape=jax.ShapeDtypeStruct((B, T, n_pair * LANES), F32),
        scratch_shapes=[pltpu.VMEM((T, LANES), BF16), pltpu.VMEM((T, LANES), BF16)],
        compiler_params=_params("parallel", "parallel"),
    )(p, p, p, ckp, cvp, bias)


WIN_SPAN = 3 * WIN_1D
WIN_Q_TILE = 64
WIN_TILES_PER_STEP = 4
ROPE_SWAP = HEAD_DIM // 4


def _rope(x, cos, sin_signed):
    lane = lax.broadcasted_iota(jnp.int32, (1, LANES), 1)
    first = lane % (2 * ROPE_SWAP) < ROPE_SWAP
    partner = jnp.where(first, pltpu.roll(x, LANES - ROPE_SWAP, axis=1), pltpu.roll(x, ROPE_SWAP, axis=1))
    return x * cos + partner * sin_signed


def _win_attn_kernel(q_ref, k_ref, v_ref, ck_ref, cv_ref, cos_ref, sin_ref, sink_ref, o_ref, kb_ref, vb_ref):
    grp = _lane_group()
    kvg = pl.program_id(1) // 2
    both = lambda a: jnp.where(grp == kvg, a, pltpu.roll(a, HEAD_DIM, axis=1))
    kb_ref[...] = _rope(both(k_ref[0]), cos_ref[...], sin_ref[...]).astype(BF16)
    vb_ref[...] = both(v_ref[0]).astype(BF16)
    ckb = both(ck_ref[0]).astype(BF16)
    cvb = both(cv_ref[0]).astype(BF16)
    T = q_ref.shape[1]
    rel = (lax.broadcasted_iota(jnp.int32, (WIN_Q_TILE, WIN_SPAN), 1)
           - lax.broadcasted_iota(jnp.int32, (WIN_Q_TILE, WIN_SPAN), 0))

    def heads(n):
        q0 = pl.multiple_of(n * WIN_Q_TILE, WIN_Q_TILE)
        start = pl.multiple_of(jnp.clip(q0 - WIN_1D, 0, T - WIN_SPAN), WIN_Q_TILE)
        q = _rope(q_ref[0, pl.ds(q0, WIN_Q_TILE), :], cos_ref[pl.ds(q0, WIN_Q_TILE), :],
                  sin_ref[pl.ds(q0, WIN_Q_TILE), :]) * ATTN_SCALE
        kw = kb_ref[pl.ds(start, WIN_SPAN), :]
        vw = vb_ref[pl.ds(start, WIN_SPAN), :]
        valid = jnp.abs(rel + (start - q0)) <= WIN_1D

        def head(h):
            qh = jnp.where(grp == h, q, 0.0).astype(BF16)
            return _attend([lambda: jnp.where(valid, _dot_nt(qh, kw), NEG_BIG), lambda: _dot_nt(qh, ckb)],
                           [vw, cvb], sink_ref[0, h][:, :1])

        return [head(0), head(1)]

    def tiles(step, carry):
        ids = [step * WIN_TILES_PER_STEP + j for j in range(WIN_TILES_PER_STEP)]
        outs = _lockstep([prog for n in ids for prog in heads(n)])
        for j, n in enumerate(ids):
            o_ref[0, pl.ds(pl.multiple_of(n * WIN_Q_TILE, WIN_Q_TILE), WIN_Q_TILE), :] = _merge_heads(
                outs[2 * j:2 * j + 2], grp)
        return carry

    lax.fori_loop(0, T // WIN_Q_TILE // WIN_TILES_PER_STEP, tiles, 0)


def _rope_tables(T):
    t = jnp.arange(T)
    quarter = HEAD_DIM // 4
    inv = 1.0 / (ROPE_BASE ** (jnp.arange(quarter, dtype=F32) / quarter))
    ang_r = (t // GRID_W).astype(F32)[:, None] * inv[None, :]
    ang_c = (t % GRID_W).astype(F32)[:, None] * inv[None, :]
    cos = jnp.concatenate([jnp.cos(ang_r), jnp.cos(ang_r), jnp.cos(ang_c), jnp.cos(ang_c)], axis=-1)
    sin = jnp.concatenate([-jnp.sin(ang_r), jnp.sin(ang_r), -jnp.sin(ang_c), jnp.sin(ang_c)], axis=-1)
    return jnp.tile(cos, (1, 2)), jnp.tile(sin, (1, 2))


def _win_attn_call(p, sink, ck_cache, cv_cache):
    B, T, _ = p.shape
    n_pair = H_C // 2
    L = ck_cache.shape[2]
    ckp = ck_cache.transpose(0, 2, 1, 3).reshape(B, L, KV_C * HEAD_DIM)
    cvp = cv_cache.transpose(0, 2, 1, 3).reshape(B, L, KV_C * HEAD_DIM)
    cos, sin = _rope_tables(T)
    sink_arr = jnp.broadcast_to(sink.reshape(n_pair, 2, 1, 1), (n_pair, 2, 1, LANES))
    col = lambda blk: pl.BlockSpec((1, T, LANES), lambda b, hp: (b, 0, blk))
    ctx = lambda: pl.BlockSpec((1, L, LANES), lambda b, hp: (b, 0, 0))
    tab = lambda: pl.BlockSpec((T, LANES), lambda b, hp: (0, 0))
    return pl.pallas_call(
        _win_attn_kernel,
        grid=(B, n_pair),
        in_specs=[pl.BlockSpec((1, T, LANES), lambda b, hp: (b, 0, hp)), col(n_pair), col(n_pair + 1),
                  ctx(), ctx(), tab(), tab(),
                  pl.BlockSpec((1, 2, 1, LANES), lambda b, hp: (hp, 0, 0, 0))],
        out_specs=pl.BlockSpec((1, T, LANES), lambda b, hp: (b, 0, hp)),
        out_shape=jax.ShapeDtypeStruct((B, T, n_pair * LANES), F32),
        scratch_shapes=[pltpu.VMEM((T, LANES), BF16), pltpu.VMEM((T, LANES), BF16)],
        compiler_params=_params("parallel", "parallel"),
    )(p, p, p, ckp, cvp, cos, sin, sink_arr)


def _dot_tn(a, b):
    return lax.dot_general(a, b, (((0,), (0,)), ((), ())), preferred_element_type=F32)


def _split_bf16(x):
    hi = x.astype(BF16)
    return hi, (x - hi.astype(F32)).astype(BF16)


GLA_CHUNKS_PER_STEP = 8
DELTA_CHUNKS_PER_STEP = 4


def _lockstep(programs):
    results = [None] * len(programs)
    live = list(range(len(programs)))
    while live:
        for idx in list(live):
            try:
                next(programs[idx])
            except StopIteration as stop:
                results[idx] = stop.value
                live.remove(idx)
    return results


def _block_diag_mask():
    return (lax.broadcasted_iota(jnp.int32, (LANES, 1), 0) // HEAD_DIM) == _lane_group()


def _head_rmsnorm(o, grp):
    sq = o * o
    s0 = jnp.sum(jnp.where(grp == 0, sq, 0.0), axis=-1, keepdims=True)
    s1 = jnp.sum(jnp.where(grp == 1, sq, 0.0), axis=-1, keepdims=True)
    ms = jnp.where(grp == 0, s0, s1) * (1.0 / HEAD_DIM)
    return o * lax.rsqrt(ms + EPS)


def _gla_kernel(q_ref, k_ref, v_ref, r_ref, glr_ref, w2_ref, bg_ref, ng_ref, s0_ref, o_ref, sfin_ref,
                ob_ref, st_ref):
    T = q_ref.shape[1]
    n = T // CHUNK
    grp = _lane_group()
    bd = _block_diag_mask()
    ri = lax.broadcasted_iota(jnp.int32, (CHUNK, CHUNK), 0)
    ci = lax.broadcasted_iota(jnp.int32, (CHUNK, CHUNK), 1)
    st_ref[...] = s0_ref[0, :, 0]

    def chunk(c0, d):
        sl = pl.ds(c0, CHUNK)
        q = q_ref[0, sl, :] * (DK_B ** -0.5)
        k = k_ref[0, sl, :]
        vb = v_ref[0, sl, :].astype(BF16)
        z = jnp.dot(glr_ref[0, sl, :].astype(BF16), w2_ref[d], preferred_element_type=F32) + bg_ref[d]
        yield
        g = (jnp.minimum(z, 0.0) - jnp.log(1.0 + jnp.exp(-jnp.abs(z)))) * (1.0 / GLA_TAU)
        g_hi, g_lo = _split_bf16(g)
        mask = (ri <= ci) if d else (ri >= ci)
        tri = jnp.where(mask, 1.0, 0.0).astype(BF16)
        G = jnp.dot(tri, g_hi, preferred_element_type=F32) + jnp.dot(tri, g_lo, preferred_element_type=F32)
        yield
        g_tot = G[0:1] if d else G[CHUNK - 1:CHUNK]
        q_t = q * jnp.exp(G)
        k_t = (k * jnp.exp(-G)).astype(BF16)
        k_e = (k * jnp.exp(g_tot - G)).astype(BF16)
        att = [jnp.where(mask, _dot_nt(jnp.where(grp == h, q_t, 0.0).astype(BF16), k_t), 0.0) for h in range(2)]
        u = jnp.where(bd, _dot_tn(k_e, vb), 0.0)
        decay = jnp.exp(jnp.broadcast_to(g_tot, (LANES, LANES)).T)
        yield
        S = st_ref[d]
        st_ref[d] = decay * S + u
        o = jnp.dot(q_t.astype(BF16), S.astype(BF16), preferred_element_type=F32)
        for h in range(2):
            o = o + jnp.where(grp == h, jnp.dot(att[h].astype(BF16), vb, preferred_element_type=F32), 0.0)
        return o

    per_step = min(GLA_CHUNKS_PER_STEP, n)

    def body(i, carry):
        c_f = [pl.multiple_of((i * per_step + j) * CHUNK, CHUNK) for j in range(per_step)]
        c_b = [pl.multiple_of((n - 1 - i * per_step - j) * CHUNK, CHUNK) for j in range(per_step)]
        outs = _lockstep([chunk(c0, 0) for c0 in c_f] + [chunk(c0, 1) for c0 in c_b])
        for j in range(per_step):
            o_ref[0, pl.ds(c_f[j], CHUNK), :] = outs[j]
            ob_ref[pl.ds(c_b[j], CHUNK), :] = outs[per_step + j]
        return carry

    lax.fori_loop(0, n // per_step, body, 0)
    sfin_ref[0, :, 0] = st_ref[...]

    def gate(i, carry):
        sl = pl.ds(pl.multiple_of(i * CHUNK, CHUNK), CHUNK)
        r = r_ref[0, sl, :]
        y = _head_rmsnorm(o_ref[0, sl, :] + ob_ref[sl, :], grp) * ng_ref[...]
        o_ref[0, sl, :] = y * (r * jax.nn.sigmoid(r))
        return carry

    lax.fori_loop(0, n, gate, 0)


def _pair_block_diag(s):
    B, _, H, dk, dv = s.shape
    s = s.reshape(B, 2, H // 2, 2, dk, dv)
    z = jnp.zeros_like(s[:, :, :, 0])
    top = jnp.concatenate([s[:, :, :, 0], z], axis=-1)
    bot = jnp.concatenate([z, s[:, :, :, 1]], axis=-1)
    return jnp.concatenate([top, bot], axis=-2)


def _pair_block_diag_inv(sp):
    a = sp[..., :HEAD_DIM, :HEAD_DIM]
    b = sp[..., HEAD_DIM:, HEAD_DIM:]
    B, _, P = sp.shape[:3]
    return jnp.stack([a, b], axis=3).reshape(B, 2, 2 * P, HEAD_DIM, HEAD_DIM)


EV_BQ, EV_BK, EV_BV, EV_BR, EV_GLR = 12, 16, 20, 24, 28


def _gla_call(p, w_g2, b_g, norm_g, s0):
    B, T, _ = p.shape
    n_pair = H_B // 2
    w2 = jnp.zeros((2, LANES, H_B * DK_B), F32)
    w2 = w2.at[0, :GLA_RANK].set(w_g2[0]).at[1, GLA_RANK:2 * GLA_RANK].set(w_g2[1]).astype(BF16)
    col = lambda off: pl.BlockSpec((1, T, LANES), lambda b, hp: (b, 0, off + hp))
    st = lambda: pl.BlockSpec((1, 2, 1, LANES, LANES), lambda b, hp: (b, 0, hp, 0, 0))
    o, s_fin = pl.pallas_call(
        _gla_kernel,
        grid=(B, n_pair),
        in_specs=[col(EV_BQ), col(EV_BK), col(EV_BV), col(EV_BR),
                  pl.BlockSpec((1, T, LANES), lambda b, hp: (b, 0, EV_GLR)),
                  pl.BlockSpec((2, LANES, LANES), lambda b, hp: (0, 0, hp)),
                  pl.BlockSpec((2, 1, LANES), lambda b, hp: (0, 0, hp)),
                  pl.BlockSpec((1, LANES), lambda b, hp: (0, hp)),
                  st()],
        out_specs=[pl.BlockSpec((1, T, LANES), lambda b, hp: (b, 0, hp)), st()],
        out_shape=[jax.ShapeDtypeStruct((B, T, n_pair * LANES), F32),
                   jax.ShapeDtypeStruct((B, 2, n_pair, LANES, LANES), F32)],
        scratch_shapes=[pltpu.VMEM((T, LANES), F32), pltpu.VMEM((2, LANES, LANES), F32)],
        compiler_params=_params("parallel", "parallel"),
    )(p, p, p, p, p, w2, b_g.reshape(2, 1, H_B * DK_B), norm_g.reshape(1, H_B * DV_B), _pair_block_diag(s0))
    return o, _pair_block_diag_inv(s_fin)


def _softplus(x):
    return jnp.maximum(x, 0.0) + jnp.log(1.0 + jnp.exp(-jnp.abs(x)))


def _delta_kernel(q_ref, k_ref, v_ref, z_ref, dab_ref, cq_ref, ck_ref, cv_ref, nega_ref, dtb_ref, ng_ref, s0_ref,
                  o_ref, sfin_ref, qs_ref, ks_ref, vs_ref, ob_ref, st_ref, hw_ref, ha_ref, hd_ref):
    T = q_ref.shape[1]
    n = T // CHUNK
    hp = pl.program_id(1)
    grp = _lane_group()
    lane = lax.broadcasted_iota(jnp.int32, (1, LANES), 1)
    subl = lax.broadcasted_iota(jnp.int32, (LANES, 1), 0)
    bd = _block_diag_mask()
    ri = lax.broadcasted_iota(jnp.int32, (CHUNK, CHUNK), 0)
    ci = lax.broadcasted_iota(jnp.int32, (CHUNK, CHUNK), 1)
    eye = jnp.where(ri == ci, 1.0, 0.0)
    row = lax.broadcasted_iota(jnp.int32, (CHUNK, 1), 0)
    st_ref[...] = s0_ref[0, :, 0]

    def prep(i, carry):
        c0 = pl.multiple_of(i * CHUNK, CHUNK)
        lo = pl.multiple_of(jnp.maximum(c0 - SUBLANES, 0), SUBLANES)
        hi = pl.multiple_of(jnp.minimum(c0 + CHUNK, T - SUBLANES), SUBLANES)
        for src, cw_ref, dst, scale in ((q_ref, cq_ref, qs_ref, DK_D ** -0.5), (k_ref, ck_ref, ks_ref, 1.0),
                                        (v_ref, cv_ref, vs_ref, None)):
            x = src[0, pl.ds(c0, CHUNK), :]
            before = jnp.where(i > 0, src[0, pl.ds(lo, SUBLANES), :][SUBLANES - 1:], 0.0)
            after = jnp.where(i < n - 1, src[0, pl.ds(hi, SUBLANES), :][:1], 0.0)
            x_prev = jnp.where(row == 0, before, pltpu.roll(x, 1, axis=0))
            x_next = jnp.where(row == CHUNK - 1, after, pltpu.roll(x, CHUNK - 1, axis=0))
            cw = cw_ref[...]
            y = x_prev * cw[0:1] + x * cw[1:2] + x_next * cw[2:3]
            y = y * jax.nn.sigmoid(y)
            if scale is not None:
                sq = y * y
                s0 = jnp.sum(jnp.where(grp == 0, sq, 0.0), axis=-1, keepdims=True)
                s1 = jnp.sum(jnp.where(grp == 1, sq, 0.0), axis=-1, keepdims=True)
                y = y * lax.rsqrt(jnp.where(grp == 0, s0, s1) + EPS) * scale
            dst[pl.ds(c0, CHUNK), :] = y
        return carry

    lax.fori_loop(0, n, prep, 0)

    def chunk(c0, d):
        sl = pl.ds(c0, CHUNK)
        q = qs_ref[sl, :]
        k = ks_ref[sl, :]
        v = vs_ref[sl, :]
        ab = dab_ref[0, sl, :]
        gb = jnp.where(lane < 2 * H_D, nega_ref[...] * _softplus(ab + dtb_ref[...]), jax.nn.sigmoid(ab))
        gb_t = gb.T
        incl = (ri <= ci) if d else (ri >= ci)
        incl_t = (ri >= ci) if d else (ri <= ci)
        strict = (ri < ci) if d else (ri > ci)

        def head(h):
            jg = d * H_D + 2 * hp + h
            g_c = jnp.sum(jnp.where(lane == jg, gb, 0.0), axis=1, keepdims=True)
            beta_c = jnp.sum(jnp.where(lane == 2 * H_D + jg, gb, 0.0), axis=1, keepdims=True)
            g_r = jnp.sum(jnp.where(subl == jg, gb_t, 0.0), axis=0, keepdims=True)
            G_c = jnp.sum(jnp.where(incl, g_r, 0.0), axis=1, keepdims=True)
            G_r = jnp.sum(jnp.where(incl_t, g_c, 0.0), axis=0, keepdims=True)
            tot = jnp.sum(g_r, axis=1, keepdims=True)
            gam = jnp.where(incl, jnp.exp(jnp.where(incl, G_c - G_r, 0.0)), 0.0)
            kh = jnp.where(grp == h, k, 0.0)
            qh = jnp.where(grp == h, q, 0.0)
            vh = jnp.where(grp == h, v, 0.0)
            kb = kh * beta_c
            khb = kh.astype(BF16)
            a = jnp.where(strict, _dot_nt(kb.astype(BF16), khb) * gam, 0.0)
            a_qk = (_dot_nt(qh.astype(BF16), khb) * gam).astype(BF16)
            yield
            wide = jnp.concatenate([-a, eye], axis=1)
            for _ in range(6):
                w_hi, w_lo = _split_bf16(wide)
                prod = (jnp.dot(w_hi[:, :CHUNK], w_hi, preferred_element_type=F32)
                        + jnp.dot(w_lo[:, :CHUNK], w_hi, preferred_element_type=F32)
                        + jnp.dot(w_hi[:, :CHUNK], w_lo, preferred_element_type=F32))
                wide = prod + jnp.where(lane >= CHUNK, wide, 0.0)
                yield
            t_inv = wide[:, CHUNK:]
            e_g = jnp.exp(G_c)
            rhs = vh * beta_c + pltpu.roll(kb * e_g, HEAD_DIM, axis=1)
            sol = jnp.dot(t_inv.astype(BF16), rhs.astype(BF16), preferred_element_type=F32)
            return sol, a_qk, qh * e_g, kh * jnp.exp(tot - G_c), jnp.exp(tot)

        return head

    per_step = min(DELTA_CHUNKS_PER_STEP, n)
    n_steps = n // per_step
    n_sys = 4 * per_step

    def starts(i):
        c_f = [pl.multiple_of((i * per_step + j) * CHUNK, CHUNK) for j in range(per_step)]
        c_b = [pl.multiple_of((n - 1 - i * per_step - j) * CHUNK, CHUNK) for j in range(per_step)]
        return c_f, c_b

    def head_programs(i):
        programs = []
        for d, c0s in zip((0, 1), starts(i)):
            for c0 in c0s:
                head = chunk(c0, d)
                programs += [head(0), head(1)]
        return programs

    def store_heads(slot, results):
        for s, (r, a_qk, q_g, k_e, d_last) in enumerate(results):
            hw_ref[slot, s, 0] = r
            hw_ref[slot, s, 1] = q_g
            hw_ref[slot, s, 2] = k_e
            ha_ref[slot, s] = a_qk
            hd_ref[slot, s] = jnp.broadcast_to(d_last, (SUBLANES, LANES))

    def load_head(slot, s):
        return hw_ref[slot, s, 0], ha_ref[slot, s], hw_ref[slot, s, 1], hw_ref[slot, s, 2], hd_ref[slot, s][0:1, 0:1]

    def scan_step(d, slot, s):
        (r0, aqk0, qg0, ke0, d0), (r1, aqk1, qg1, ke1, d1) = load_head(slot, s), load_head(slot, s + 1)
        S = st_ref[d]
        Sb = S.astype(BF16)
        w_val = jnp.where(grp == 0, r0, r1)
        k_cum = pltpu.roll(jnp.where(grp == 0, r1, r0), HEAD_DIM, axis=1)
        v_new = w_val - jnp.dot(k_cum.astype(BF16), Sb, preferred_element_type=F32)
        o = jnp.dot((qg0 + qg1).astype(BF16), Sb, preferred_element_type=F32)
        yield
        vnb = v_new.astype(BF16)
        o = o + jnp.where(grp == 0, jnp.dot(aqk0, vnb, preferred_element_type=F32), 0.0)
        o = o + jnp.where(grp == 1, jnp.dot(aqk1, vnb, preferred_element_type=F32), 0.0)
        d_rows = jnp.where(subl // HEAD_DIM == 0, d0, d1)
        st_ref[d] = S * d_rows + jnp.where(bd, _dot_tn((ke0 + ke1).astype(BF16), vnb), 0.0)
        return o

    def scan_dir(d, i, slot):
        c0s = starts(i)[d]
        for j in range(per_step):
            o = yield from scan_step(d, slot, d * 2 * per_step + 2 * j)
            if d == 0:
                o_ref[0, pl.ds(c0s[j], CHUNK), :] = o
            else:
                ob_ref[pl.ds(c0s[j], CHUNK), :] = o

    store_heads(0, _lockstep(head_programs(0)))

    def body(i, carry):
        slot = i % 2
        results = _lockstep(head_programs(i) + [scan_dir(0, i - 1, 1 - slot), scan_dir(1, i - 1, 1 - slot)])
        store_heads(slot, results[:n_sys])
        return carry

    lax.fori_loop(1, n_steps, body, 0)
    last = n_steps - 1
    _lockstep([scan_dir(0, last, last % 2), scan_dir(1, last, last % 2)])
    sfin_ref[0, :, 0] = st_ref[...]

    def gate(i, carry):
        sl = pl.ds(pl.multiple_of(i * CHUNK, CHUNK), CHUNK)
        z = z_ref[0, sl, :]
        y = _head_rmsnorm(o_ref[0, sl, :] + ob_ref[sl, :], grp) * ng_ref[...]
        o_ref[0, sl, :] = y * (z * jax.nn.sigmoid(z))
        return carry

    lax.fori_loop(0, n, gate, 0)


OD_DQ, OD_DK, OD_DV, OD_DZ, OD_DAB = 6, 10, 14, 18, 22


def _delta_call(p, w_conv, a_log, dt_bias, norm_g, s0):
    B, T, _ = p.shape
    n_pair = H_D // 2
    n_sys = 4 * min(DELTA_CHUNKS_PER_STEP, T // CHUNK)
    nega = jnp.zeros((1, LANES), F32).at[0, :2 * H_D].set(-jnp.exp(a_log.reshape(-1)))
    dtb = jnp.zeros((1, LANES), F32).at[0, :2 * H_D].set(dt_bias.reshape(-1))
    ng = jnp.tile(norm_g, 2).reshape(1, LANES)
    col = lambda off: pl.BlockSpec((1, T, LANES), lambda b, hp: (b, 0, off + hp))
    cw = lambda off: pl.BlockSpec((SHORT_CONV, LANES), lambda b, hp: (0, off + hp))
    vec = lambda: pl.BlockSpec((1, LANES), lambda b, hp: (0, 0))
    st = lambda: pl.BlockSpec((1, 2, 1, LANES, LANES), lambda b, hp: (b, 0, hp, 0, 0))
    o, s_fin = pl.pallas_call(
        _delta_kernel,
        grid=(B, n_pair),
        in_specs=[col(OD_DQ), col(OD_DK), col(OD_DV), col(OD_DZ),
                  pl.BlockSpec((1, T, LANES), lambda b, hp: (b, 0, OD_DAB)),
                  cw(0), cw(n_pair), cw(2 * n_pair), vec(), vec(), vec(), st()],
        out_specs=[pl.BlockSpec((1, T, LANES), lambda b, hp: (b, 0, hp)), st()],
        out_shape=[jax.ShapeDtypeStruct((B, T, n_pair * LANES), F32),
                   jax.ShapeDtypeStruct((B, 2, n_pair, LANES, LANES), F32)],
        scratch_shapes=[pltpu.VMEM((T, LANES), F32), pltpu.VMEM((T, LANES), F32), pltpu.VMEM((T, LANES), F32),
                        pltpu.VMEM((T, LANES), F32), pltpu.VMEM((2, LANES, LANES), F32),
                        pltpu.VMEM((2, n_sys, 3, CHUNK, LANES), F32),
                        pltpu.VMEM((2, n_sys, CHUNK, CHUNK), BF16),
                        pltpu.VMEM((2, n_sys, SUBLANES, LANES), F32)],
        compiler_params=_params("parallel", "parallel"),
    )(p, p, p, p, p, w_conv, w_conv, w_conv, nega, dtb, ng, _pair_block_diag(s0))
    return o, _pair_block_diag_inv(s_fin)


def _split_cols(p, sizes):
    cuts = [int(s) for s in np.cumsum(sizes)[:-1]]
    return jnp.split(p, cuts, axis=-1)


def _even_mix(p, latent, rpb, w_g2, b_g, norm_g, ak_cache, av_cache, sb):
    B, T, _ = p.shape
    ak = p[..., H_A * HEAD_DIM:2 * H_A * HEAD_DIM]
    av = p[..., 2 * H_A * HEAD_DIM:3 * H_A * HEAD_DIM]
    n_pair = H_A // 2
    if latent:
        o_a = _nbr_attn_call(p, rpb, ak_cache, av_cache)
        k_h = v_h = None
    else:
        k_h = ak.reshape(B, T, H_A, HEAD_DIM).transpose(0, 2, 1, 3)
        v_h = av.reshape(B, T, H_A, HEAD_DIM).transpose(0, 2, 1, 3)
        o_a = _dense_attn_call(p, 0, n_pair, 2 * n_pair, None, False)
    o_b, s_b = _gla_call(p, w_g2, b_g, norm_g, sb)
    return o_a, o_b, k_h, v_h, s_b


def _odd_mix(p, latent, sink, w_conv, a_log, dt_bias, norm_g, ck_cache, cv_cache, sd):
    B, T, _ = p.shape
    ck = p[..., H_C * HEAD_DIM:(H_C + KV_C) * HEAD_DIM]
    cv = p[..., (H_C + KV_C) * HEAD_DIM:(H_C + 2 * KV_C) * HEAD_DIM]
    n_pair = H_C // 2
    if latent:
        o_c = _win_attn_call(p, sink, ck_cache, cv_cache)
        k_h = v_h = None
    else:
        k_h = ck.reshape(B, T, KV_C, HEAD_DIM).transpose(0, 2, 1, 3)
        v_h = cv.reshape(B, T, KV_C, HEAD_DIM).transpose(0, 2, 1, 3)
        o_c = _dense_attn_call(p, 0, n_pair, n_pair + 1, sink, True)
    o_d, s_d = _delta_call(p, w_conv, a_log, dt_bias, norm_g, sd)
    return o_c, o_d, k_h, v_h, s_d


def _reorder_cols(w, sizes, order):
    parts = _split_cols(w, sizes)
    out = jnp.concatenate([parts[i] for i in order], axis=-1)
    pad = -out.shape[-1] % LANES
    return jnp.pad(out, ((0, 0), (0, pad)))


def _reorder_even_w_in(w):
    return _reorder_cols(w, EV_SIZES, (0, 1, 2, 3, 4, 5, 7, 6))


def _reorder_odd_w_in(w):
    return _reorder_cols(w, OD_SIZES, (0, 1, 2, 3, 4, 5, 8, 6, 7))


TM_LATENT = 512
TM_CONTEXT = 256
TM_FFN_LATENT = 512


def kernel(x_prompt, x_sample, cache_a_k, cache_a_v, state_b, cache_c_k, cache_c_v, state_d, c, c_ctx, ada_w, ada_b, norm1_g, norm2_g, ffn_up, ffn_conv, ffn_down, ev_w_in, ev_w_out, a_rpb, b_w_g2, b_b_g, b_norm_g, od_w_in, od_w_out, c_sink, d_conv, d_a_log, d_dt_bias, d_norm_g, final_g):
    Bp, Tp, D = x_prompt.shape
    Bs, Ts, _ = x_sample.shape
    cond = jnp.concatenate([c, c_ctx[None, :], jnp.zeros((2 * SUBLANES - Bs - 1, D), F32)], axis=0)
    mods = _adaln(cond, ada_w, ada_b)
    xp, xs = x_prompt, x_sample
    streams = ((False, Bp, TM_CONTEXT), (True, Bs, TM_LATENT))
    ak_l, av_l, sb_l, ck_l, cv_l, sd_l = [], [], [], [], [], []
    for l in range(DEPTH):
        j = l // 2
        mod_s = [mods[l, :Bs, i * D:(i + 1) * D] for i in range(6)]
        mod_p = [jnp.broadcast_to(mods[l, Bs:Bs + 1, i * D:(i + 1) * D], (Bp, D)) for i in range(6)]
        w_in = (_reorder_even_w_in(ev_w_in[j]) if l % 2 == 0 else _reorder_odd_w_in(od_w_in[j])).astype(BF16)
        w_out = (ev_w_out[j] if l % 2 == 0 else od_w_out[j]).astype(BF16)
        w_up = ffn_up[l].astype(BF16)
        w_down = ffn_down[l].astype(BF16)
        new_x = []
        for (latent, B, tm), x, mod in zip(streams, (xp, xs), (mod_p, mod_s)):
            sh1, sc1, g1, sh2, sc2, g2 = mod
            p = _nm_matmul(x, norm1_g[l], sh1, sc1, w_in, tm)
            if l % 2 == 0:
                s0 = state_b[:, j] if latent else jnp.zeros((B, 2, H_B, DK_B, DV_B), F32)
                o1, o2, k_h, v_h, s_new = _even_mix(p, latent, a_rpb[j], b_w_g2[j], b_b_g[j], b_norm_g[j],
                                                    cache_a_k[:, j], cache_a_v[:, j], s0)
                if not latent:
                    ak_l.append(k_h)
                    av_l.append(v_h)
                    sb_l.append(s_new)
            else:
                s0 = state_d[:, j] if latent else jnp.zeros((B, 2, H_D, DK_D, DV_D), F32)
                o1, o2, k_h, v_h, s_new = _odd_mix(p, latent, c_sink[j], d_conv[j], d_a_log[j], d_dt_bias[j],
                                                   d_norm_g[j], cache_c_k[:, j], cache_c_v[:, j], s0)
                if not latent:
                    ck_l.append(k_h)
                    cv_l.append(v_h)
                    sd_l.append(s_new)
            x = _proj_res(o1, o2, w_out, x, g1, tm)
            x = _conv_ffn_res(x, norm2_g[l], sh2, sc2, g2, w_up, ffn_conv[l], w_down,
                              TM_FFN_LATENT if latent else tm, final_g if l == DEPTH - 1 else None)
            new_x.append(x)
        xp, xs = new_x
    return (xp, xs, jnp.stack(ak_l, axis=1), jnp.stack(av_l, axis=1), jnp.stack(sb_l, axis=1),
            jnp.stack(ck_l, axis=1), jnp.stack(cv_l, axis=1), jnp.stack(sd_l, axis=1))
```

```python
import functools
import math

import jax
import jax.numpy as jnp
import numpy as np
from jax import lax
from jax.experimental import pallas as pl
from jax.experimental.pallas import tpu as pltpu

D_MODEL = 1024
DEPTH = 4
GRID_W = 64
HEAD_DIM = 64
H_A = D_MODEL // (2 * HEAD_DIM)
WIN_R = 8
WIN_C = 16
H_B = D_MODEL // (2 * HEAD_DIM)
DK_B = HEAD_DIM
DV_B = HEAD_DIM
GLA_RANK = 16
GLA_TAU = 16.0
H_C = D_MODEL // (2 * HEAD_DIM)
KV_C = H_C // 4
WIN_1D = 128
ROPE_BASE = 10000.0
H_D = D_MODEL // (2 * HEAD_DIM)
DK_D = HEAD_DIM
DV_D = HEAD_DIM
SHORT_CONV = 3
CHUNK = 64
EPS = 1e-6
EV_SIZES = (H_A * HEAD_DIM, H_A * HEAD_DIM, H_A * HEAD_DIM, H_B * DK_B, H_B * DK_B, H_B * DV_B, 2 * GLA_RANK, H_B * DV_B)
OD_SIZES = (H_C * HEAD_DIM, KV_C * HEAD_DIM, KV_C * HEAD_DIM, H_D * DK_D, H_D * DK_D, H_D * DV_D, 2 * H_D, 2 * H_D, H_D * DV_D)
F32 = jnp.float32
BF16 = jnp.bfloat16

V7X_VMEM_LIMIT_BYTES = 52 * 1024 * 1024
SUBLANES = 8
LANES = 128
COL_CHUNK = 512
FFN_SUB_COLS = 1536


def _params(*sem):
    return pltpu.CompilerParams(dimension_semantics=sem, vmem_limit_bytes=V7X_VMEM_LIMIT_BYTES)


def _norm_mod(x, g, shift, scale):
    y = x * lax.rsqrt(jnp.mean(x * x, axis=-1, keepdims=True) + EPS)
    return (y * g) * (1.0 + scale) + shift


def _adaln_kernel(c_ref, w_ref, b_ref, o_ref):
    cond = c_ref[...]
    a = cond * jax.nn.sigmoid(cond)
    a_hi = a.astype(BF16)
    a_lo = (a - a_hi.astype(F32)).astype(BF16)
    w = w_ref[0]
    w_hi = w.astype(BF16)
    w_lo = (w - w_hi.astype(F32)).astype(BF16)
    acc = jnp.dot(a_hi, w_hi, preferred_element_type=F32)
    acc += jnp.dot(a_lo, w_hi, preferred_element_type=F32)
    acc += jnp.dot(a_hi, w_lo, preferred_element_type=F32)
    o_ref[0] = acc + b_ref[0]


def _adaln(cond, ada_w, ada_b):
    R = cond.shape[0]
    L, D, N = ada_w.shape
    tn = 1024
    return pl.pallas_call(
        _adaln_kernel,
        grid=(L, N // tn),
        in_specs=[pl.BlockSpec((R, D), lambda l, j: (0, 0)),
                  pl.BlockSpec((1, D, tn), lambda l, j: (l, 0, j)),
                  pl.BlockSpec((1, 1, tn), lambda l, j: (l, 0, j))],
        out_specs=pl.BlockSpec((1, R, tn), lambda l, j: (l, 0, j)),
        out_shape=jax.ShapeDtypeStruct((L, R, N), F32),
        compiler_params=_params("parallel", "parallel"),
    )(cond, ada_w, ada_b.reshape(L, 1, N))


def _nm_matmul_kernel(x_ref, g_ref, sh_ref, sc_ref, w_ref, o_ref):
    h = _norm_mod(x_ref[0], g_ref[...], sh_ref[0], sc_ref[0]).astype(BF16)
    n = o_ref.shape[-1]
    for c0 in range(0, n, COL_CHUNK):
        c1 = min(c0 + COL_CHUNK, n)
        o_ref[0, :, c0:c1] = jnp.dot(h, w_ref[:, c0:c1], preferred_element_type=F32)


def _nm_matmul(x, g, shift, scale, w, tm):
    B, T, D = x.shape
    N = w.shape[1]
    return pl.pallas_call(
        _nm_matmul_kernel,
        grid=(B, T // tm),
        in_specs=[pl.BlockSpec((1, tm, D), lambda b, i: (b, i, 0)),
                  pl.BlockSpec((1, D), lambda b, i: (0, 0)),
                  pl.BlockSpec((1, 1, D), lambda b, i: (b, 0, 0)),
                  pl.BlockSpec((1, 1, D), lambda b, i: (b, 0, 0)),
                  pl.BlockSpec((D, N), lambda b, i: (0, 0))],
        out_specs=pl.BlockSpec((1, tm, N), lambda b, i: (b, i, 0)),
        out_shape=jax.ShapeDtypeStruct((B, T, N), F32),
        compiler_params=_params("parallel", "parallel"),
    )(x, g.reshape(1, D), shift.reshape(B, 1, D), scale.reshape(B, 1, D), w)


def _mix_ffn_kernel(x_ref, xp_ref, xn_ref, a1_ref, a1p_ref, a1n_ref, a2_ref, a2p_ref, a2n_ref, wo1_ref, wo2_ref,
                    gate1_ref, g_ref, sh_ref, sc_ref, gate_ref, wu_ref, cw_ref, wd_ref, fg_ref,
                    o_ref, h_ref, *, final_norm):
    i = pl.program_id(1)
    tm = x_ref.shape[1]
    F = wd_ref.shape[0]
    ext = lambda m, p, n: jnp.concatenate([m[0], p[0], n[0]], axis=0).astype(BF16)
    mix = (jnp.dot(ext(a1_ref, a1p_ref, a1n_ref), wo1_ref[...], preferred_element_type=F32)
           + jnp.dot(ext(a2_ref, a2p_ref, a2n_ref), wo2_ref[...], preferred_element_type=F32))
    gate1 = gate1_ref[0]
    x1 = x_ref[0] + gate1 * mix[0:tm]
    x1p = xp_ref[0] + gate1 * mix[tm:tm + SUBLANES]
    x1n = xn_ref[0] + gate1 * mix[tm + SUBLANES:]
    g, sh, sc = g_ref[...], sh_ref[0], sc_ref[0]
    h_ref[0:tm, :] = _norm_mod(x1, g, sh, sc).astype(BF16)
    hp = jnp.where(i > 0, _norm_mod(x1p, g, sh, sc), 0.0)
    hn = jnp.where(i < pl.num_programs(1) - 1, _norm_mod(x1n, g, sh, sc), 0.0)
    h_ref[tm:tm + 2 * SUBLANES, :] = jnp.concatenate([hp, hn], axis=0).astype(BF16)
    h = h_ref[...]
    row = lax.broadcasted_iota(jnp.int32, (tm, 1), 0)

    def conv_branch(c0, c1):
        u_all = jnp.dot(h, wu_ref[:, c0:c1], preferred_element_type=F32)
        u = u_all[0:tm]
        u_prev = jnp.where(row == 0, u_all[tm + SUBLANES - 1:tm + SUBLANES], pltpu.roll(u, 1, axis=0))
        u_next = jnp.where(row == tm - 1, u_all[tm + SUBLANES:tm + SUBLANES + 1], pltpu.roll(u, tm - 1, axis=0))
        cw = cw_ref[:, c0:c1]
        return u_prev * cw[0:1] + u * cw[1:2] + u_next * cw[2:3]

    acc = None
    for c0 in range(0, F, FFN_SUB_COLS):
        c1 = min(c0 + FFN_SUB_COLS, F)
        a = conv_branch(c0, c1)
        gt = conv_branch(F + c0, F + c1)
        act = a * (gt * jax.nn.sigmoid(gt))
        part = jnp.dot(act.astype(BF16), wd_ref[c0:c1, :], preferred_element_type=F32)
        acc = part if acc is None else acc + part

    y = x1 + gate_ref[0] * acc
    if final_norm:
        y = (y * lax.rsqrt(jnp.mean(y * y, axis=-1, keepdims=True) + EPS)) * fg_ref[...]
    o_ref[0] = y


def _mix_ffn_res(x, a1, a2, w_out, gate1, g, shift, scale, gate, w_up, w_conv, w_down, tm, final_g=None):
    B, T, D = x.shape
    K = a1.shape[-1]
    F = w_down.shape[0]
    nt = T // tm
    rows8 = tm // SUBLANES
    vec = lambda: pl.BlockSpec((1, 1, D), lambda b, i: (b, 0, 0))
    resident = lambda shape, blk=0: pl.BlockSpec(shape, lambda b, i: (blk, 0), pipeline_mode=pl.Buffered(1))
    tile = lambda w: pl.BlockSpec((1, tm, w), lambda b, i: (b, i, 0))
    before = lambda w: pl.BlockSpec((1, SUBLANES, w), lambda b, i: (b, jnp.maximum(i * rows8 - 1, 0), 0))
    after = lambda w: pl.BlockSpec((1, SUBLANES, w),
                                   lambda b, i: (b, jnp.minimum((i + 1) * rows8, T // SUBLANES - 1), 0))
    return pl.pallas_call(
        functools.partial(_mix_ffn_kernel, final_norm=final_g is not None),
        grid=(B, nt),
        in_specs=[tile(D), before(D), after(D), tile(K), before(K), after(K), tile(K), before(K), after(K),
                  resident((K, D), 0), resident((K, D), 1), vec(),
                  pl.BlockSpec((1, D), lambda b, i: (0, 0)),
                  vec(), vec(), vec(),
                  resident((D, 2 * F)), resident((3, 2 * F)), resident((F, D)),
                  pl.BlockSpec((1, D), lambda b, i: (0, 0))],
        out_specs=pl.BlockSpec((1, tm, D), lambda b, i: (b, i, 0)),
        out_shape=jax.ShapeDtypeStruct((B, T, D), F32),
        scratch_shapes=[pltpu.VMEM((tm + 2 * SUBLANES, D), BF16)],
        compiler_params=_params("parallel", "parallel"),
    )(x, x, x, a1, a1, a1, a2, a2, a2, w_out, w_out, gate1.reshape(B, 1, D),
      g.reshape(1, D), shift.reshape(B, 1, D), scale.reshape(B, 1, D), gate.reshape(B, 1, D),
      w_up, w_conv, w_down, (g if final_g is None else final_g).reshape(1, D))


NEG_BIG = -1e30
ATTN_SCALE = HEAD_DIM ** -0.5


def _dot_nt(a, b):
    return lax.dot_general(a, b, (((1,), (1,)), ((), ())), preferred_element_type=F32)


def _lane_group():
    return lax.broadcasted_iota(jnp.int32, (1, LANES), 1) // HEAD_DIM


def _attend(score_fns, values, sink):
    scores = [fn() for fn in score_fns]
    yield
    s = scores[0] if len(scores) == 1 else jnp.concatenate(scores, axis=1)
    m = jnp.max(s, axis=-1, keepdims=True)
    if sink is not None:
        m = jnp.maximum(m, sink)
    e = jnp.exp(s - m)
    den = jnp.sum(e, axis=-1, keepdims=True)
    if sink is not None:
        den = den + jnp.exp(sink - m)
    eb = e.astype(BF16)
    out = None
    c0 = 0
    for v in values:
        o = jnp.dot(eb[:, c0:c0 + v.shape[0]], v, preferred_element_type=F32)
        out = o if out is None else out + o
        c0 += v.shape[0]
    yield
    return out / den


def _merge_heads(outs, grp):
    return jnp.where(grp == 0, outs[0], outs[1])


def _dense_attn_kernel(q_ref, k_ref, v_ref, sink_ref, o_ref, *, gqa, use_sink):
    grp = _lane_group()
    q = q_ref[0] * ATTN_SCALE
    k = k_ref[0]
    v = v_ref[0]
    if gqa:
        kvg = pl.program_id(1) // 2
        k = jnp.where(grp == kvg, k, pltpu.roll(k, HEAD_DIM, axis=1))
        v = jnp.where(grp == kvg, v, pltpu.roll(v, HEAD_DIM, axis=1))
    kb = k.astype(BF16)
    vb = v.astype(BF16)

    def head(h):
        qh = jnp.where(grp == h, q, 0.0).astype(BF16)
        sink = sink_ref[0, h][:, :1] if use_sink else None
        return _attend([lambda: _dot_nt(qh, kb)], [vb], sink)

    o_ref[0] = _merge_heads(_lockstep([head(0), head(1)]), grp)


def _dense_attn_call(p, q_blk, k_blk, v_blk, sink, gqa):
    B, T, _ = p.shape
    n_pair = H_A // 2
    use_sink = sink is not None
    if use_sink:
        sink_arr = jnp.broadcast_to(sink.reshape(n_pair, 2, 1, 1), (n_pair, 2, 1, LANES))
    else:
        sink_arr = jnp.zeros((n_pair, 2, 1, LANES), F32)
    kv_idx = (lambda hp: 0) if gqa else (lambda hp: hp)
    return pl.pallas_call(
        functools.partial(_dense_attn_kernel, gqa=gqa, use_sink=use_sink),
        grid=(B, n_pair),
        in_specs=[pl.BlockSpec((1, T, LANES), lambda b, hp: (b, 0, q_blk + hp)),
                  pl.BlockSpec((1, T, LANES), lambda b, hp: (b, 0, k_blk + kv_idx(hp))),
                  pl.BlockSpec((1, T, LANES), lambda b, hp: (b, 0, v_blk + kv_idx(hp))),
                  pl.BlockSpec((1, 2, 1, LANES), lambda b, hp: (hp, 0, 0, 0))],
        out_specs=pl.BlockSpec((1, T, LANES), lambda b, hp: (b, 0, hp)),
        out_shape=jax.ShapeDtypeStruct((B, T, n_pair * LANES), F32),
        compiler_params=_params("parallel", "parallel"),
    )(p, p, p, sink_arr)


NBR_SPAN = WIN_R * GRID_W
ATTN_BLOCKS_PER_STEP = 8


def _nbr_attn_kernel(q_ref, k_ref, v_ref, ck_ref, cv_ref, bias_ref, o_ref, kb_ref, vb_ref):
    kb_ref[...] = k_ref[0].astype(BF16)
    vb_ref[...] = v_ref[0].astype(BF16)
    ckb = ck_ref[0].astype(BF16)
    cvb = cv_ref[0].astype(BF16)
    grp = _lane_group()
    n_rows = q_ref.shape[1] // GRID_W

    def head(i, h):
        r0 = jnp.clip(i - WIN_R // 2, 0, n_rows - WIN_R)
        q = q_ref[0, pl.ds(pl.multiple_of(i * GRID_W, GRID_W), GRID_W), :] * ATTN_SCALE
        start = pl.multiple_of(r0 * GRID_W, GRID_W)
        kw = kb_ref[pl.ds(start, NBR_SPAN), :]
        vw = vb_ref[pl.ds(start, NBR_SPAN), :]
        qh = jnp.where(grp == h, q, 0.0).astype(BF16)
        return _attend([lambda: _dot_nt(qh, kw) + bias_ref[i - r0, h], lambda: _dot_nt(qh, ckb)], [vw, cvb], None)

    def rows(step, carry):
        ids = [step * ATTN_BLOCKS_PER_STEP + j for j in range(ATTN_BLOCKS_PER_STEP)]
        outs = _lockstep([head(i, h) for i in ids for h in range(2)])
        for j, i in enumerate(ids):
            o_ref[0, pl.ds(pl.multiple_of(i * GRID_W, GRID_W), GRID_W), :] = _merge_heads(outs[2 * j:2 * j + 2], grp)
        return carry

    lax.fori_loop(0, n_rows // ATTN_BLOCKS_PER_STEP, rows, 0)


def _nbr_bias_table(rpb):
    var = np.arange(WIN_R)[:, None, None]
    r = np.arange(WIN_R)[None, :, None]
    row_sel = (r - var + (WIN_R - 1) == np.arange(2 * WIN_R - 1)[None, None, :]).astype(np.float32)
    w = np.arange(GRID_W)[:, None, None]
    cc = np.arange(GRID_W)[None, :, None]
    c0 = np.clip(w - WIN_C // 2, 0, GRID_W - WIN_C)
    valid = (cc >= c0) & (cc < c0 + WIN_C)
    col_sel = (valid & (cc - w + (WIN_C - 1) == np.arange(2 * WIN_C - 1)[None, None, :])).astype(np.float32)
    t = jnp.einsum('vra,hab->vhrb', row_sel, rpb, precision=lax.Precision.HIGHEST)
    tbl = jnp.einsum('vhrb,wcb->vhwrc', t, col_sel, precision=lax.Precision.HIGHEST)
    tbl = jnp.where(valid[None, None, :, None, :, 0], tbl, NEG_BIG)
    return tbl.reshape(WIN_R, rpb.shape[0], GRID_W, NBR_SPAN)


def _nbr_attn_call(p, rpb, ck_cache, cv_cache):
    B, T, _ = p.shape
    n_pair = H_A // 2
    L = ck_cache.shape[2]
    ckp = ck_cache.transpose(0, 2, 1, 3).reshape(B, L, H_A * HEAD_DIM)
    cvp = cv_cache.transpose(0, 2, 1, 3).reshape(B, L, H_A * HEAD_DIM)
    bias = _nbr_bias_table(rpb)
    col = lambda off: pl.BlockSpec((1, T, LANES), lambda b, hp: (b, 0, off + hp))
    ctx = lambda: pl.BlockSpec((1, L, LANES), lambda b, hp: (b, 0, hp))
    return pl.pallas_call(
        _nbr_attn_kernel,
        grid=(B, n_pair),
        in_specs=[col(0), col(n_pair), col(2 * n_pair), ctx(), ctx(),
                  pl.BlockSpec((WIN_R, 2, GRID_W, NBR_SPAN), lambda b, hp: (0, hp, 0, 0))],
        out_specs=pl.BlockSpec((1, T, LANES), lambda b, hp: (b, 0, hp)),
        out_shape=jax.ShapeDtypeStruct((B, T, n_pair * LANES), F32),
        scratch_shapes=[pltpu.VMEM((T, LANES), BF16), pltpu.VMEM((T, LANES), BF16)],
        compiler_params=_params("parallel", "parallel"),
    )(p, p, p, ckp, cvp, bias)


WIN_SPAN = 3 * WIN_1D
WIN_Q_TILE = 64
WIN_TILES_PER_STEP = 4
ROPE_SWAP = HEAD_DIM // 4


def _rope(x, cos, sin_signed):
    lane = lax.broadcasted_iota(jnp.int32, (1, LANES), 1)
    first = lane % (2 * ROPE_SWAP) < ROPE_SWAP
    partner = jnp.where(first, pltpu.roll(x, LANES - ROPE_SWAP, axis=1), pltpu.roll(x, ROPE_SWAP, axis=1))
    return x * cos + partner * sin_signed


def _win_attn_kernel(q_ref, k_ref, v_ref, ck_ref, cv_ref, cos_ref, sin_ref, sink_ref, o_ref, kb_ref, vb_ref):
    grp = _lane_group()
    kvg = pl.program_id(1) // 2
    both = lambda a: jnp.where(grp == kvg, a, pltpu.roll(a, HEAD_DIM, axis=1))
    kb_ref[...] = _rope(both(k_ref[0]), cos_ref[...], sin_ref[...]).astype(BF16)
    vb_ref[...] = both(v_ref[0]).astype(BF16)
    ckb = both(ck_ref[0]).astype(BF16)
    cvb = both(cv_ref[0]).astype(BF16)
    T = q_ref.shape[1]
    rel = (lax.broadcasted_iota(jnp.int32, (WIN_Q_TILE, WIN_SPAN), 1)
           - lax.broadcasted_iota(jnp.int32, (WIN_Q_TILE, WIN_SPAN), 0))

    def heads(n):
        q0 = pl.multiple_of(n * WIN_Q_TILE, WIN_Q_TILE)
        start = pl.multiple_of(jnp.clip(q0 - WIN_1D, 0, T - WIN_SPAN), WIN_Q_TILE)
        q = _rope(q_ref[0, pl.ds(q0, WIN_Q_TILE), :], cos_ref[pl.ds(q0, WIN_Q_TILE), :],
                  sin_ref[pl.ds(q0, WIN_Q_TILE), :]) * ATTN_SCALE
        kw = kb_ref[pl.ds(start, WIN_SPAN), :]
        vw = vb_ref[pl.ds(start, WIN_SPAN), :]
        valid = jnp.abs(rel + (start - q0)) <= WIN_1D

        def head(h):
            qh = jnp.where(grp == h, q, 0.0).astype(BF16)
            return _attend([lambda: jnp.where(valid, _dot_nt(qh, kw), NEG_BIG), lambda: _dot_nt(qh, ckb)],
                           [vw, cvb], sink_ref[0, h][:, :1])

        return [head(0), head(1)]

    def tiles(step, carry):
        ids = [step * WIN_TILES_PER_STEP + j for j in range(WIN_TILES_PER_STEP)]
        outs = _lockstep([prog for n in ids for prog in heads(n)])
        for j, n in enumerate(ids):
            o_ref[0, pl.ds(pl.multiple_of(n * WIN_Q_TILE, WIN_Q_TILE), WIN_Q_TILE), :] = _merge_heads(
                outs[2 * j:2 * j + 2], grp)
        return carry

    lax.fori_loop(0, T // WIN_Q_TILE // WIN_TILES_PER_STEP, tiles, 0)


def _rope_tables(T):
    t = jnp.arange(T)
    quarter = HEAD_DIM // 4
    inv = 1.0 / (ROPE_BASE ** (jnp.arange(quarter, dtype=F32) / quarter))
    ang_r = (t // GRID_W).astype(F32)[:, None] * inv[None, :]
    ang_c = (t % GRID_W).astype(F32)[:, None] * inv[None, :]
    cos = jnp.concatenate([jnp.cos(ang_r), jnp.cos(ang_r), jnp.cos(ang_c), jnp.cos(ang_c)], axis=-1)
    sin = jnp.concatenate([-jnp.sin(ang_r), jnp.sin(ang_r), -jnp.sin(ang_c), jnp.sin(ang_c)], axis=-1)
    return jnp.tile(cos, (1, 2)), jnp.tile(sin, (1, 2))


def _win_attn_call(p, sink, ck_cache, cv_cache):
    B, T, _ = p.shape
    n_pair = H_C // 2
    L = ck_cache.shape[2]
    ckp = ck_cache.transpose(0, 2, 1, 3).reshape(B, L, KV_C * HEAD_DIM)
    cvp = cv_cache.transpose(0, 2, 1, 3).reshape(B, L, KV_C * HEAD_DIM)
    cos, sin = _rope_tables(T)
    sink_arr = jnp.broadcast_to(sink.reshape(n_pair, 2, 1, 1), (n_pair, 2, 1, LANES))
    col = lambda blk: pl.BlockSpec((1, T, LANES), lambda b, hp: (b, 0, blk))
    ctx = lambda: pl.BlockSpec((1, L, LANES), lambda b, hp: (b, 0, 0))
    tab = lambda: pl.BlockSpec((T, LANES), lambda b, hp: (0, 0))
    return pl.pallas_call(
        _win_attn_kernel,
        grid=(B, n_pair),
        in_specs=[pl.BlockSpec((1, T, LANES), lambda b, hp: (b, 0, hp)), col(n_pair), col(n_pair + 1),
                  ctx(), ctx(), tab(), tab(),
                  pl.BlockSpec((1, 2, 1, LANES), lambda b, hp: (hp, 0, 0, 0))],
        out_specs=pl.BlockSpec((1, T, LANES), lambda b, hp: (b, 0, hp)),
        out_shape=jax.ShapeDtypeStruct((B, T, n_pair * LANES), F32),
        scratch_shapes=[pltpu.VMEM((T, LANES), BF16), pltpu.VMEM((T, LANES), BF16)],
        compiler_params=_params("parallel", "parallel"),
    )(p, p, p, ckp, cvp, cos, sin, sink_arr)


def _dot_tn(a, b):
    return lax.dot_general(a, b, (((0,), (0,)), ((), ())), preferred_element_type=F32)


def _split_bf16(x):
    hi = x.astype(BF16)
    return hi, (x - hi.astype(F32)).astype(BF16)


GLA_CHUNKS_PER_STEP = 8
DELTA_CHUNKS_PER_STEP = 4


def _lockstep(programs):
    results = [None] * len(programs)
    live = list(range(len(programs)))
    while live:
        for idx in list(live):
            try:
                next(programs[idx])
            except StopIteration as stop:
                results[idx] = stop.value
                live.remove(idx)
    return results


def _block_diag_mask():
    return (lax.broadcasted_iota(jnp.int32, (LANES, 1), 0) // HEAD_DIM) == _lane_group()


def _head_rmsnorm(o, grp):
    sq = o * o
    s0 = jnp.sum(jnp.where(grp == 0, sq, 0.0), axis=-1, keepdims=True)
    s1 = jnp.sum(jnp.where(grp == 1, sq, 0.0), axis=-1, keepdims=True)
    ms = jnp.where(grp == 0, s0, s1) * (1.0 / HEAD_DIM)
    return o * lax.rsqrt(ms + EPS)


def _gla_kernel(q_ref, k_ref, v_ref, r_ref, glr_ref, w2_ref, bg_ref, ng_ref, s0_ref, o_ref, sfin_ref,
                ob_ref, st_ref):
    T = q_ref.shape[1]
    n = T // CHUNK
    grp = _lane_group()
    bd = _block_diag_mask()
    ri = lax.broadcasted_iota(jnp.int32, (CHUNK, CHUNK), 0)
    ci = lax.broadcasted_iota(jnp.int32, (CHUNK, CHUNK), 1)
    st_ref[...] = s0_ref[0, :, 0]

    def chunk(c0, d):
        sl = pl.ds(c0, CHUNK)
        q = q_ref[0, sl, :] * (DK_B ** -0.5)
        k = k_ref[0, sl, :]
        vb = v_ref[0, sl, :].astype(BF16)
        z = jnp.dot(glr_ref[0, sl, :].astype(BF16), w2_ref[d], preferred_element_type=F32) + bg_ref[d]
        yield
        g = (jnp.minimum(z, 0.0) - jnp.log(1.0 + jnp.exp(-jnp.abs(z)))) * (1.0 / GLA_TAU)
        g_hi, g_lo = _split_bf16(g)
        mask = (ri <= ci) if d else (ri >= ci)
        tri = jnp.where(mask, 1.0, 0.0).astype(BF16)
        G = jnp.dot(tri, g_hi, preferred_element_type=F32) + jnp.dot(tri, g_lo, preferred_element_type=F32)
        yield
        g_tot = G[0:1] if d else G[CHUNK - 1:CHUNK]
        q_t = q * jnp.exp(G)
        k_t = (k * jnp.exp(-G)).astype(BF16)
        k_e = (k * jnp.exp(g_tot - G)).astype(BF16)
        att = [jnp.where(mask, _dot_nt(jnp.where(grp == h, q_t, 0.0).astype(BF16), k_t), 0.0) for h in range(2)]
        u = jnp.where(bd, _dot_tn(k_e, vb), 0.0)
        decay = jnp.exp(jnp.broadcast_to(g_tot, (LANES, LANES)).T)
        yield
        S = st_ref[d]
        st_ref[d] = decay * S + u
        o = jnp.dot(q_t.astype(BF16), S.astype(BF16), preferred_element_type=F32)
        for h in range(2):
            o = o + jnp.where(grp == h, jnp.dot(att[h].astype(BF16), vb, preferred_element_type=F32), 0.0)
        return o

    per_step = min(GLA_CHUNKS_PER_STEP, n)

    def body(i, carry):
        c_f = [pl.multiple_of((i * per_step + j) * CHUNK, CHUNK) for j in range(per_step)]
        c_b = [pl.multiple_of((n - 1 - i * per_step - j) * CHUNK, CHUNK) for j in range(per_step)]
        outs = _lockstep([chunk(c0, 0) for c0 in c_f] + [chunk(c0, 1) for c0 in c_b])
        for j in range(per_step):
            o_ref[0, pl.ds(c_f[j], CHUNK), :] = outs[j]
            ob_ref[pl.ds(c_b[j], CHUNK), :] = outs[per_step + j]
        return carry

    lax.fori_loop(0, n // per_step, body, 0)
    sfin_ref[0, :, 0] = st_ref[...]

    def gate(i, carry):
        sl = pl.ds(pl.multiple_of(i * CHUNK, CHUNK), CHUNK)
        r = r_ref[0, sl, :]
        y = _head_rmsnorm(o_ref[0, sl, :] + ob_ref[sl, :], grp) * ng_ref[...]
        o_ref[0, sl, :] = y * (r * jax.nn.sigmoid(r))
        return carry

    lax.fori_loop(0, n, gate, 0)


def _pair_block_diag(s):
    B, _, H, dk, dv = s.shape
    s = s.reshape(B, 2, H // 2, 2, dk, dv)
    z = jnp.zeros_like(s[:, :, :, 0])
    top = jnp.concatenate([s[:, :, :, 0], z], axis=-1)
    bot = jnp.concatenate([z, s[:, :, :, 1]], axis=-1)
    return jnp.concatenate([top, bot], axis=-2)


def _pair_block_diag_inv(sp):
    a = sp[..., :HEAD_DIM, :HEAD_DIM]
    b = sp[..., HEAD_DIM:, HEAD_DIM:]
    B, _, P = sp.shape[:3]
    return jnp.stack([a, b], axis=3).reshape(B, 2, 2 * P, HEAD_DIM, HEAD_DIM)


EV_BQ, EV_BK, EV_BV, EV_BR, EV_GLR = 12, 16, 20, 24, 28


def _gla_call(p, w_g2, b_g, norm_g, s0):
    B, T, _ = p.shape
    n_pair = H_B // 2
    w2 = jnp.zeros((2, LANES, H_B * DK_B), F32)
    w2 = w2.at[0, :GLA_RANK].set(w_g2[0]).at[1, GLA_RANK:2 * GLA_RANK].set(w_g2[1]).astype(BF16)
    col = lambda off: pl.BlockSpec((1, T, LANES), lambda b, hp: (b, 0, off + hp))
    st = lambda: pl.BlockSpec((1, 2, 1, LANES, LANES), lambda b, hp: (b, 0, hp, 0, 0))
    o, s_fin = pl.pallas_call(
        _gla_kernel,
        grid=(B, n_pair),
        in_specs=[col(EV_BQ), col(EV_BK), col(EV_BV), col(EV_BR),
                  pl.BlockSpec((1, T, LANES), lambda b, hp: (b, 0, EV_GLR)),
                  pl.BlockSpec((2, LANES, LANES), lambda b, hp: (0, 0, hp)),
                  pl.BlockSpec((2, 1, LANES), lambda b, hp: (0, 0, hp)),
                  pl.BlockSpec((1, LANES), lambda b, hp: (0, hp)),
                  st()],
        out_specs=[pl.BlockSpec((1, T, LANES), lambda b, hp: (b, 0, hp)), st()],
        out_shape=[jax.ShapeDtypeStruct((B, T, n_pair * LANES), F32),
                   jax.ShapeDtypeStruct((B, 2, n_pair, LANES, LANES), F32)],
        scratch_shapes=[pltpu.VMEM((T, LANES), F32), pltpu.VMEM((2, LANES, LANES), F32)],
        compiler_params=_params("parallel", "parallel"),
    )(p, p, p, p, p, w2, b_g.reshape(2, 1, H_B * DK_B), norm_g.reshape(1, H_B * DV_B), _pair_block_diag(s0))
    return o, _pair_block_diag_inv(s_fin)


def _softplus(x):
    return jnp.maximum(x, 0.0) + jnp.log(1.0 + jnp.exp(-jnp.abs(x)))


def _delta_kernel(q_ref, k_ref, v_ref, z_ref, dab_ref, cq_ref, ck_ref, cv_ref, nega_ref, dtb_ref, ng_ref, s0_ref,
                  o_ref, sfin_ref, qs_ref, ks_ref, vs_ref, ob_ref, st_ref, hw_ref, ha_ref, hd_ref):
    T = q_ref.shape[1]
    n = T // CHUNK
    hp = pl.program_id(1)
    grp = _lane_group()
    lane = lax.broadcasted_iota(jnp.int32, (1, LANES), 1)
    subl = lax.broadcasted_iota(jnp.int32, (LANES, 1), 0)
    bd = _block_diag_mask()
    ri = lax.broadcasted_iota(jnp.int32, (CHUNK, CHUNK), 0)
    ci = lax.broadcasted_iota(jnp.int32, (CHUNK, CHUNK), 1)
    eye = jnp.where(ri == ci, 1.0, 0.0)
    row = lax.broadcasted_iota(jnp.int32, (CHUNK, 1), 0)
    st_ref[...] = s0_ref[0, :, 0]

    def prep(i, carry):
        c0 = pl.multiple_of(i * CHUNK, CHUNK)
        lo = pl.multiple_of(jnp.maximum(c0 - SUBLANES, 0), SUBLANES)
        hi = pl.multiple_of(jnp.minimum(c0 + CHUNK, T - SUBLANES), SUBLANES)
        for src, cw_ref, dst, scale in ((q_ref, cq_ref, qs_ref, DK_D ** -0.5), (k_ref, ck_ref, ks_ref, 1.0),
                                        (v_ref, cv_ref, vs_ref, None)):
            x = src[0, pl.ds(c0, CHUNK), :]
            before = jnp.where(i > 0, src[0, pl.ds(lo, SUBLANES), :][SUBLANES - 1:], 0.0)
            after = jnp.where(i < n - 1, src[0, pl.ds(hi, SUBLANES), :][:1], 0.0)
            x_prev = jnp.where(row == 0, before, pltpu.roll(x, 1, axis=0))
            x_next = jnp.where(row == CHUNK - 1, after, pltpu.roll(x, CHUNK - 1, axis=0))
            cw = cw_ref[...]
            y = x_prev * cw[0:1] + x * cw[1:2] + x_next * cw[2:3]
            y = y * jax.nn.sigmoid(y)
            if scale is not None:
                sq = y * y
                s0 = jnp.sum(jnp.where(grp == 0, sq, 0.0), axis=-1, keepdims=True)
                s1 = jnp.sum(jnp.where(grp == 1, sq, 0.0), axis=-1, keepdims=True)
                y = y * lax.rsqrt(jnp.where(grp == 0, s0, s1) + EPS) * scale
            dst[pl.ds(c0, CHUNK), :] = y
        return carry

    lax.fori_loop(0, n, prep, 0)

    def chunk(c0, d):
        sl = pl.ds(c0, CHUNK)
        q = qs_ref[sl, :]
        k = ks_ref[sl, :]
        v = vs_ref[sl, :]
        ab = dab_ref[0, sl, :]
        gb = jnp.where(lane < 2 * H_D, nega_ref[...] * _softplus(ab + dtb_ref[...]), jax.nn.sigmoid(ab))
        gb_t = gb.T
        incl = (ri <= ci) if d else (ri >= ci)
        incl_t = (ri >= ci) if d else (ri <= ci)
        strict = (ri < ci) if d else (ri > ci)

        def head(h):
            jg = d * H_D + 2 * hp + h
            g_c = jnp.sum(jnp.where(lane == jg, gb, 0.0), axis=1, keepdims=True)
            beta_c = jnp.sum(jnp.where(lane == 2 * H_D + jg, gb, 0.0), axis=1, keepdims=True)
            g_r = jnp.sum(jnp.where(subl == jg, gb_t, 0.0), axis=0, keepdims=True)
            G_c = jnp.sum(jnp.where(incl, g_r, 0.0), axis=1, keepdims=True)
            G_r = jnp.sum(jnp.where(incl_t, g_c, 0.0), axis=0, keepdims=True)
            tot = jnp.sum(g_r, axis=1, keepdims=True)
            gam = jnp.where(incl, jnp.exp(jnp.where(incl, G_c - G_r, 0.0)), 0.0)
            kh = jnp.where(grp == h, k, 0.0)
            qh = jnp.where(grp == h, q, 0.0)
            vh = jnp.where(grp == h, v, 0.0)
            kb = kh * beta_c
            khb = kh.astype(BF16)
            a = jnp.where(strict, _dot_nt(kb.astype(BF16), khb) * gam, 0.0)
            a_qk = (_dot_nt(qh.astype(BF16), khb) * gam).astype(BF16)
            yield
            wide = jnp.concatenate([-a, eye], axis=1)
            for _ in range(6):
                w_hi, w_lo = _split_bf16(wide)
                prod = (jnp.dot(w_hi[:, :CHUNK], w_hi, preferred_element_type=F32)
                        + jnp.dot(w_lo[:, :CHUNK], w_hi, preferred_element_type=F32)
                        + jnp.dot(w_hi[:, :CHUNK], w_lo, preferred_element_type=F32))
                wide = prod + jnp.where(lane >= CHUNK, wide, 0.0)
                yield
            t_inv = wide[:, CHUNK:]
            e_g = jnp.exp(G_c)
            rhs = vh * beta_c + pltpu.roll(kb * e_g, HEAD_DIM, axis=1)
            sol = jnp.dot(t_inv.astype(BF16), rhs.astype(BF16), preferred_element_type=F32)
            return sol, a_qk, qh * e_g, kh * jnp.exp(tot - G_c), jnp.exp(tot)

        return head

    per_step = min(DELTA_CHUNKS_PER_STEP, n)
    n_steps = n // per_step
    n_sys = 4 * per_step

    def starts(i):
        c_f = [pl.multiple_of((i * per_step + j) * CHUNK, CHUNK) for j in range(per_step)]
        c_b = [pl.multiple_of((n - 1 - i * per_step - j) * CHUNK, CHUNK) for j in range(per_step)]
        return c_f, c_b

    def head_programs(i):
        programs = []
        for d, c0s in zip((0, 1), starts(i)):
            for c0 in c0s:
                head = chunk(c0, d)
                programs += [head(0), head(1)]
        return programs

    def store_heads(slot, results):
        for s, (r, a_qk, q_g, k_e, d_last) in enumerate(results):
            hw_ref[slot, s, 0] = r
            hw_ref[slot, s, 1] = q_g
            hw_ref[slot, s, 2] = k_e
            ha_ref[slot, s] = a_qk
            hd_ref[slot, s] = jnp.broadcast_to(d_last, (SUBLANES, LANES))

    def load_head(slot, s):
        return hw_ref[slot, s, 0], ha_ref[slot, s], hw_ref[slot, s, 1], hw_ref[slot, s, 2], hd_ref[slot, s][0:1, 0:1]

    def scan_step(d, slot, s):
        (r0, aqk0, qg0, ke0, d0), (r1, aqk1, qg1, ke1, d1) = load_head(slot, s), load_head(slot, s + 1)
        S = st_ref[d]
        Sb = S.astype(BF16)
        w_val = jnp.where(grp == 0, r0, r1)
        k_cum = pltpu.roll(jnp.where(grp == 0, r1, r0), HEAD_DIM, axis=1)
        v_new = w_val - jnp.dot(k_cum.astype(BF16), Sb, preferred_element_type=F32)
        o = jnp.dot((qg0 + qg1).astype(BF16), Sb, preferred_element_type=F32)
        yield
        vnb = v_new.astype(BF16)
        o = o + jnp.where(grp == 0, jnp.dot(aqk0, vnb, preferred_element_type=F32), 0.0)
        o = o + jnp.where(grp == 1, jnp.dot(aqk1, vnb, preferred_element_type=F32), 0.0)
        d_rows = jnp.where(subl // HEAD_DIM == 0, d0, d1)
        st_ref[d] = S * d_rows + jnp.where(bd, _dot_tn((ke0 + ke1).astype(BF16), vnb), 0.0)
        return o

    def scan_dir(d, i, slot):
        c0s = starts(i)[d]
        for j in range(per_step):
            o = yield from scan_step(d, slot, d * 2 * per_step + 2 * j)
            if d == 0:
                o_ref[0, pl.ds(c0s[j], CHUNK), :] = o
            else:
                ob_ref[pl.ds(c0s[j], CHUNK), :] = o

    store_heads(0, _lockstep(head_programs(0)))

    def body(i, carry):
        slot = i % 2
        results = _lockstep(head_programs(i) + [scan_dir(0, i - 1, 1 - slot), scan_dir(1, i - 1, 1 - slot)])
        store_heads(slot, results[:n_sys])
        return carry

    lax.fori_loop(1, n_steps, body, 0)
    last = n_steps - 1
    _lockstep([scan_dir(0, last, last % 2), scan_dir(1, last, last % 2)])
    sfin_ref[0, :, 0] = st_ref[...]

    def gate(i, carry):
        sl = pl.ds(pl.multiple_of(i * CHUNK, CHUNK), CHUNK)
        z = z_ref[0, sl, :]
        y = _head_rmsnorm(o_ref[0, sl, :] + ob_ref[sl, :], grp) * ng_ref[...]
        o_ref[0, sl, :] = y * (z * jax.nn.sigmoid(z))
        return carry

    lax.fori_loop(0, n, gate, 0)


OD_DQ, OD_DK, OD_DV, OD_DZ, OD_DAB = 6, 10, 14, 18, 22


def _delta_call(p, w_conv, a_log, dt_bias, norm_g, s0):
    B, T, _ = p.shape
    n_pair = H_D // 2
    n_sys = 4 * min(DELTA_CHUNKS_PER_STEP, T // CHUNK)
    nega = jnp.zeros((1, LANES), F32).at[0, :2 * H_D].set(-jnp.exp(a_log.reshape(-1)))
    dtb = jnp.zeros((1, LANES), F32).at[0, :2 * H_D].set(dt_bias.reshape(-1))
    ng = jnp.tile(norm_g, 2).reshape(1, LANES)
    col = lambda off: pl.BlockSpec((1, T, LANES), lambda b, hp: (b, 0, off + hp))
    cw = lambda off: pl.BlockSpec((SHORT_CONV, LANES), lambda b, hp: (0, off + hp))
    vec = lambda: pl.BlockSpec((1, LANES), lambda b, hp: (0, 0))
    st = lambda: pl.BlockSpec((1, 2, 1, LANES, LANES), lambda b, hp: (b, 0, hp, 0, 0))
    o, s_fin = pl.pallas_call(
        _delta_kernel,
        grid=(B, n_pair),
        in_specs=[col(OD_DQ), col(OD_DK), col(OD_DV), col(OD_DZ),
                  pl.BlockSpec((1, T, LANES), lambda b, hp: (b, 0, OD_DAB)),
                  cw(0), cw(n_pair), cw(2 * n_pair), vec(), vec(), vec(), st()],
        out_specs=[pl.BlockSpec((1, T, LANES), lambda b, hp: (b, 0, hp)), st()],
        out_shape=[jax.ShapeDtypeStruct((B, T, n_pair * LANES), F32),
                   jax.ShapeDtypeStruct((B, 2, n_pair, LANES, LANES), F32)],
        scratch_shapes=[pltpu.VMEM((T, LANES), F32), pltpu.VMEM((T, LANES), F32), pltpu.VMEM((T, LANES), F32),
                        pltpu.VMEM((T, LANES), F32), pltpu.VMEM((2, LANES, LANES), F32),
                        pltpu.VMEM((2, n_sys, 3, CHUNK, LANES), F32),
                        pltpu.VMEM((2, n_sys, CHUNK, CHUNK), BF16),
                        pltpu.VMEM((2, n_sys, SUBLANES, LANES), F32)],
        compiler_params=_params("parallel", "parallel"),
    )(p, p, p, p, p, w_conv, w_conv, w_conv, nega, dtb, ng, _pair_block_diag(s0))
    return o, _pair_block_diag_inv(s_fin)


def _split_cols(p, sizes):
    cuts = [int(s) for s in np.cumsum(sizes)[:-1]]
    return jnp.split(p, cuts, axis=-1)


def _even_mix(p, latent, rpb, w_g2, b_g, norm_g, ak_cache, av_cache, sb):
    B, T, _ = p.shape
    ak = p[..., H_A * HEAD_DIM:2 * H_A * HEAD_DIM]
    av = p[..., 2 * H_A * HEAD_DIM:3 * H_A * HEAD_DIM]
    n_pair = H_A // 2
    if latent:
        o_a = _nbr_attn_call(p, rpb, ak_cache, av_cache)
        k_h = v_h = None
    else:
        k_h = ak.reshape(B, T, H_A, HEAD_DIM).transpose(0, 2, 1, 3)
        v_h = av.reshape(B, T, H_A, HEAD_DIM).transpose(0, 2, 1, 3)
        o_a = _dense_attn_call(p, 0, n_pair, 2 * n_pair, None, False)
    o_b, s_b = _gla_call(p, w_g2, b_g, norm_g, sb)
    return o_a, o_b, k_h, v_h, s_b


def _odd_mix(p, latent, sink, w_conv, a_log, dt_bias, norm_g, ck_cache, cv_cache, sd):
    B, T, _ = p.shape
    ck = p[..., H_C * HEAD_DIM:(H_C + KV_C) * HEAD_DIM]
    cv = p[..., (H_C + KV_C) * HEAD_DIM:(H_C + 2 * KV_C) * HEAD_DIM]
    n_pair = H_C // 2
    if latent:
        o_c = _win_attn_call(p, sink, ck_cache, cv_cache)
        k_h = v_h = None
    else:
        k_h = ck.reshape(B, T, KV_C, HEAD_DIM).transpose(0, 2, 1, 3)
        v_h = cv.reshape(B, T, KV_C, HEAD_DIM).transpose(0, 2, 1, 3)
        o_c = _dense_attn_call(p, 0, n_pair, n_pair + 1, sink, True)
    o_d, s_d = _delta_call(p, w_conv, a_log, dt_bias, norm_g, sd)
    return o_c, o_d, k_h, v_h, s_d


def _reorder_cols(w, sizes, order):
    parts = _split_cols(w, sizes)
    out = jnp.concatenate([parts[i] for i in order], axis=-1)
    pad = -out.shape[-1] % LANES
    return jnp.pad(out, ((0, 0), (0, pad)))


def _reorder_even_w_in(w):
    return _reorder_cols(w, EV_SIZES, (0, 1, 2, 3, 4, 5, 7, 6))


def _reorder_odd_w_in(w):
    return _reorder_cols(w, OD_SIZES, (0, 1, 2, 3, 4, 5, 8, 6, 7))


TM_LATENT = 512
TM_CONTEXT = 256
TM_FFN_LATENT = 512


def kernel(x_prompt, x_sample, cache_a_k, cache_a_v, state_b, cache_c_k, cache_c_v, state_d, c, c_ctx, ada_w, ada_b, norm1_g, norm2_g, ffn_up, ffn_conv, ffn_down, ev_w_in, ev_w_out, a_rpb, b_w_g2, b_b_g, b_norm_g, od_w_in, od_w_out, c_sink, d_conv, d_a_log, d_dt_bias, d_norm_g, final_g):
    Bp, Tp, D = x_prompt.shape
    Bs, Ts, _ = x_sample.shape
    cond = jnp.concatenate([c, c_ctx[None, :], jnp.zeros((2 * SUBLANES - Bs - 1, D), F32)], axis=0)
    mods = _adaln(cond, ada_w, ada_b)
    xp, xs = x_prompt, x_sample
    streams = ((False, Bp, TM_CONTEXT), (True, Bs, TM_LATENT))
    ak_l, av_l, sb_l, ck_l, cv_l, sd_l = [], [], [], [], [], []
    for l in range(DEPTH):
        j = l // 2
        mod_s = [mods[l, :Bs, i * D:(i + 1) * D] for i in range(6)]
        mod_p = [jnp.broadcast_to(mods[l, Bs:Bs + 1, i * D:(i + 1) * D], (Bp, D)) for i in range(6)]
        w_in = (_reorder_even_w_in(ev_w_in[j]) if l % 2 == 0 else _reorder_odd_w_in(od_w_in[j])).astype(BF16)
        w_out = (ev_w_out[j] if l % 2 == 0 else od_w_out[j]).astype(BF16)
        w_up = ffn_up[l].astype(BF16)
        w_down = ffn_down[l].astype(BF16)
        new_x = []
        for (latent, B, tm), x, mod in zip(streams, (xp, xs), (mod_p, mod_s)):
            sh1, sc1, g1, sh2, sc2, g2 = mod
            p = _nm_matmul(x, norm1_g[l], sh1, sc1, w_in, tm)
            if l % 2 == 0:
                s0 = state_b[:, j] if latent else jnp.zeros((B, 2, H_B, DK_B, DV_B), F32)
                o1, o2, k_h, v_h, s_new = _even_mix(p, latent, a_rpb[j], b_w_g2[j], b_b_g[j], b_norm_g[j],
                                                    cache_a_k[:, j], cache_a_v[:, j], s0)
                if not latent:
                    ak_l.append(k_h)
                    av_l.append(v_h)
                    sb_l.append(s_new)
            else:
                s0 = state_d[:, j] if latent else jnp.zeros((B, 2, H_D, DK_D, DV_D), F32)
                o1, o2, k_h, v_h, s_new = _odd_mix(p, latent, c_sink[j], d_conv[j], d_a_log[j], d_dt_bias[j],
                                                   d_norm_g[j], cache_c_k[:, j], cache_c_v[:, j], s0)
                if not latent:
                    ck_l.append(k_h)
                    cv_l.append(v_h)
                    sd_l.append(s_new)
            x = _mix_ffn_res(x, o1, o2, w_out, g1, norm2_g[l], sh2, sc2, g2, w_up, ffn_conv[l], w_down,
                             TM_FFN_LATENT if latent else tm, final_g if l == DEPTH - 1 else None)
            new_x.append(x)
        xp, xs = new_x
    return (xp, xs, jnp.stack(ak_l, axis=1), jnp.stack(av_l, axis=1), jnp.stack(sb_l, axis=1),
            jnp.stack(ck_l, axis=1), jnp.stack(cv_l, axis=1), jnp.stack(sd_l, axis=1))
```

```python
import functools
import math

import jax
import jax.numpy as jnp
import numpy as np
from jax import lax
from jax.experimental import pallas as pl
from jax.experimental.pallas import tpu as pltpu

D_MODEL = 1024
DEPTH = 4
GRID_W = 64
HEAD_DIM = 64
H_A = D_MODEL // (2 * HEAD_DIM)
WIN_R = 8
WIN_C = 16
H_B = D_MODEL // (2 * HEAD_DIM)
DK_B = HEAD_DIM
DV_B = HEAD_DIM
GLA_RANK = 16
GLA_TAU = 16.0
H_C = D_MODEL // (2 * HEAD_DIM)
KV_C = H_C // 4
WIN_1D = 128
ROPE_BASE = 10000.0
H_D = D_MODEL // (2 * HEAD_DIM)
DK_D = HEAD_DIM
DV_D = HEAD_DIM
SHORT_CONV = 3
CHUNK = 64
EPS = 1e-6
EV_SIZES = (H_A * HEAD_DIM, H_A * HEAD_DIM, H_A * HEAD_DIM, H_B * DK_B, H_B * DK_B, H_B * DV_B, 2 * GLA_RANK, H_B * DV_B)
OD_SIZES = (H_C * HEAD_DIM, KV_C * HEAD_DIM, KV_C * HEAD_DIM, H_D * DK_D, H_D * DK_D, H_D * DV_D, 2 * H_D, 2 * H_D, H_D * DV_D)
F32 = jnp.float32
BF16 = jnp.bfloat16

V7X_VMEM_LIMIT_BYTES = 52 * 1024 * 1024
SUBLANES = 8
LANES = 128
COL_CHUNK = 512
FFN_SUB_COLS = 1536


def _params(*sem):
    return pltpu.CompilerParams(dimension_semantics=sem, vmem_limit_bytes=V7X_VMEM_LIMIT_BYTES)


def _norm_mod(x, g, shift, scale):
    y = x * lax.rsqrt(jnp.mean(x * x, axis=-1, keepdims=True) + EPS)
    return (y * g) * (1.0 + scale) + shift


def _adaln_kernel(c_ref, w_ref, b_ref, o_ref):
    cond = c_ref[...]
    a = cond * jax.nn.sigmoid(cond)
    a_hi = a.astype(BF16)
    a_lo = (a - a_hi.astype(F32)).astype(BF16)
    w = w_ref[0]
    w_hi = w.astype(BF16)
    w_lo = (w - w_hi.astype(F32)).astype(BF16)
    acc = jnp.dot(a_hi, w_hi, preferred_element_type=F32)
    acc += jnp.dot(a_lo, w_hi, preferred_element_type=F32)
    acc += jnp.dot(a_hi, w_lo, preferred_element_type=F32)
    o_ref[0] = acc + b_ref[0]


def _adaln(cond, ada_w, ada_b):
    R = cond.shape[0]
    L, D, N = ada_w.shape
    tn = 1024
    return pl.pallas_call(
        _adaln_kernel,
        grid=(L, N // tn),
        in_specs=[pl.BlockSpec((R, D), lambda l, j: (0, 0)),
                  pl.BlockSpec((1, D, tn), lambda l, j: (l, 0, j)),
                  pl.BlockSpec((1, 1, tn), lambda l, j: (l, 0, j))],
        out_specs=pl.BlockSpec((1, R, tn), lambda l, j: (l, 0, j)),
        out_shape=jax.ShapeDtypeStruct((L, R, N), F32),
        compiler_params=_params("parallel", "parallel"),
    )(cond, ada_w, ada_b.reshape(L, 1, N))


def _nm_matmul_kernel(x_ref, g_ref, sh_ref, sc_ref, w_ref, o_ref):
    h = _norm_mod(x_ref[0], g_ref[...], sh_ref[0], sc_ref[0]).astype(BF16)
    n = o_ref.shape[-1]
    for c0 in range(0, n, COL_CHUNK):
        c1 = min(c0 + COL_CHUNK, n)
        o_ref[0, :, c0:c1] = jnp.dot(h, w_ref[:, c0:c1], preferred_element_type=F32)


def _nm_matmul(x, g, shift, scale, w, tm):
    B, T, D = x.shape
    N = w.shape[1]
    return pl.pallas_call(
        _nm_matmul_kernel,
        grid=(B, T // tm),
        in_specs=[pl.BlockSpec((1, tm, D), lambda b, i: (b, i, 0)),
                  pl.BlockSpec((1, D), lambda b, i: (0, 0)),
                  pl.BlockSpec((1, 1, D), lambda b, i: (b, 0, 0)),
                  pl.BlockSpec((1, 1, D), lambda b, i: (b, 0, 0)),
                  pl.BlockSpec((D, N), lambda b, i: (0, 0))],
        out_specs=pl.BlockSpec((1, tm, N), lambda b, i: (b, i, 0)),
        out_shape=jax.ShapeDtypeStruct((B, T, N), F32),
        compiler_params=_params("parallel", "parallel"),
    )(x, g.reshape(1, D), shift.reshape(B, 1, D), scale.reshape(B, 1, D), w)


def _mix_ffn_kernel(x_ref, xp_ref, xn_ref, a1_ref, a1p_ref, a1n_ref, a2_ref, a2p_ref, a2n_ref, wo1_ref, wo2_ref,
                    gate1_ref, g_ref, sh_ref, sc_ref, gate_ref, wu_ref, cw_ref, wd_ref, fg_ref,
                    o_ref, h_ref, *, final_norm):
    i = pl.program_id(1)
    tm = x_ref.shape[1]
    F = wd_ref.shape[0]
    ext = lambda m, p, n: jnp.concatenate([m[0], p[0], n[0]], axis=0).astype(BF16)
    mix = (jnp.dot(ext(a1_ref, a1p_ref, a1n_ref), wo1_ref[...], preferred_element_type=F32)
           + jnp.dot(ext(a2_ref, a2p_ref, a2n_ref), wo2_ref[...], preferred_element_type=F32))
    gate1 = gate1_ref[0]
    x1 = x_ref[0] + gate1 * mix[0:tm]
    x1p = xp_ref[0] + gate1 * mix[tm:tm + SUBLANES]
    x1n = xn_ref[0] + gate1 * mix[tm + SUBLANES:]
    g, sh, sc = g_ref[...], sh_ref[0], sc_ref[0]
    h_ref[0:tm, :] = _norm_mod(x1, g, sh, sc).astype(BF16)
    hp = jnp.where(i > 0, _norm_mod(x1p, g, sh, sc), 0.0)
    hn = jnp.where(i < pl.num_programs(1) - 1, _norm_mod(x1n, g, sh, sc), 0.0)
    h_ref[tm:tm + 2 * SUBLANES, :] = jnp.concatenate([hp, hn], axis=0).astype(BF16)
    h = h_ref[...]
    row = lax.broadcasted_iota(jnp.int32, (tm, 1), 0)

    def conv_branch(c0, c1):
        u_all = jnp.dot(h, wu_ref[:, c0:c1], preferred_element_type=F32)
        u = u_all[0:tm]
        u_prev = jnp.where(row == 0, u_all[tm + SUBLANES - 1:tm + SUBLANES], pltpu.roll(u, 1, axis=0))
        u_next = jnp.where(row == tm - 1, u_all[tm + SUBLANES:tm + SUBLANES + 1], pltpu.roll(u, tm - 1, axis=0))
        cw = cw_ref[:, c0:c1]
        return u_prev * cw[0:1] + u * cw[1:2] + u_next * cw[2:3]

    acc = None
    for c0 in range(0, F, FFN_SUB_COLS):
        c1 = min(c0 + FFN_SUB_COLS, F)
        a = conv_branch(c0, c1)
        gt = conv_branch(F + c0, F + c1)
        act = a * (gt * jax.nn.sigmoid(gt))
        part = jnp.dot(act.astype(BF16), wd_ref[c0:c1, :], preferred_element_type=F32)
        acc = part if acc is None else acc + part

    y = x1 + gate_ref[0] * acc
    if final_norm:
        y = (y * lax.rsqrt(jnp.mean(y * y, axis=-1, keepdims=True) + EPS)) * fg_ref[...]
    o_ref[0] = y


def _mix_ffn_res(x, a1, a2, w_out, gate1, g, shift, scale, gate, w_up, w_conv, w_down, tm, final_g=None):
    B, T, D = x.shape
    K = a1.shape[-1]
    F = w_down.shape[0]
    nt = T // tm
    rows8 = tm // SUBLANES
    vec = lambda: pl.BlockSpec((1, 1, D), lambda b, i: (b, 0, 0))
    resident = lambda shape, blk=0: pl.BlockSpec(shape, lambda b, i: (blk, 0), pipeline_mode=pl.Buffered(1))
    tile = lambda w: pl.BlockSpec((1, tm, w), lambda b, i: (b, i, 0))
    before = lambda w: pl.BlockSpec((1, SUBLANES, w), lambda b, i: (b, jnp.maximum(i * rows8 - 1, 0), 0))
    after = lambda w: pl.BlockSpec((1, SUBLANES, w),
                                   lambda b, i: (b, jnp.minimum((i + 1) * rows8, T // SUBLANES - 1), 0))
    return pl.pallas_call(
        functools.partial(_mix_ffn_kernel, final_norm=final_g is not None),
        grid=(B, nt),
        in_specs=[tile(D), before(D), after(D), tile(K), before(K), after(K), tile(K), before(K), after(K),
                  resident((K, D), 0), resident((K, D), 1), vec(),
                  pl.BlockSpec((1, D), lambda b, i: (0, 0)),
                  vec(), vec(), vec(),
                  resident((D, 2 * F)), resident((3, 2 * F)), resident((F, D)),
                  pl.BlockSpec((1, D), lambda b, i: (0, 0))],
        out_specs=pl.BlockSpec((1, tm, D), lambda b, i: (b, i, 0)),
        out_shape=jax.ShapeDtypeStruct((B, T, D), F32),
        scratch_shapes=[pltpu.VMEM((tm + 2 * SUBLANES, D), BF16)],
        compiler_params=_params("parallel", "parallel"),
    )(x, x, x, a1, a1, a1, a2, a2, a2, w_out, w_out, gate1.reshape(B, 1, D),
      g.reshape(1, D), shift.reshape(B, 1, D), scale.reshape(B, 1, D), gate.reshape(B, 1, D),
      w_up, w_conv, w_down, (g if final_g is None else final_g).reshape(1, D))


NEG_BIG = -1e30
ATTN_SCALE = HEAD_DIM ** -0.5


def _dot_nt(a, b):
    return lax.dot_general(a, b, (((1,), (1,)), ((), ())), preferred_element_type=F32)


def _lane_group():
    return lax.broadcasted_iota(jnp.int32, (1, LANES), 1) // HEAD_DIM


def _attend(score_fns, values, sink):
    scores = [fn() for fn in score_fns]
    yield
    s = scores[0] if len(scores) == 1 else jnp.concatenate(scores, axis=1)
    m = jnp.max(s, axis=-1, keepdims=True)
    if sink is not None:
        m = jnp.maximum(m, sink)
    e = jnp.exp(s - m)
    den = jnp.sum(e, axis=-1, keepdims=True)
    if sink is not None:
        den = den + jnp.exp(sink - m)
    eb = e.astype(BF16)
    out = None
    c0 = 0
    for v in values:
        o = jnp.dot(eb[:, c0:c0 + v.shape[0]], v, preferred_element_type=F32)
        out = o if out is None else out + o
        c0 += v.shape[0]
    yield
    return out / den


def _merge_heads(outs, grp):
    return jnp.where(grp == 0, outs[0], outs[1])


def _dense_attn_kernel(q_ref, k_ref, v_ref, sink_ref, o_ref, *, gqa, use_sink):
    grp = _lane_group()
    q = q_ref[0] * ATTN_SCALE
    k = k_ref[0]
    v = v_ref[0]
    if gqa:
        kvg = pl.program_id(1) // 2
        k = jnp.where(grp == kvg, k, pltpu.roll(k, HEAD_DIM, axis=1))
        v = jnp.where(grp == kvg, v, pltpu.roll(v, HEAD_DIM, axis=1))
    kb = k.astype(BF16)
    vb = v.astype(BF16)

    def head(h):
        qh = jnp.where(grp == h, q, 0.0).astype(BF16)
        sink = sink_ref[0, h][:, :1] if use_sink else None
        return _attend([lambda: _dot_nt(qh, kb)], [vb], sink)

    o_ref[0] = _merge_heads(_lockstep([head(0), head(1)]), grp)


def _dense_attn_call(p, q_blk, k_blk, v_blk, sink, gqa):
    B, T, _ = p.shape
    n_pair = H_A // 2
    use_sink = sink is not None
    if use_sink:
        sink_arr = jnp.broadcast_to(sink.reshape(n_pair, 2, 1, 1), (n_pair, 2, 1, LANES))
    else:
        sink_arr = jnp.zeros((n_pair, 2, 1, LANES), F32)
    kv_idx = (lambda hp: 0) if gqa else (lambda hp: hp)
    return pl.pallas_call(
        functools.partial(_dense_attn_kernel, gqa=gqa, use_sink=use_sink),
        grid=(B, n_pair),
        in_specs=[pl.BlockSpec((1, T, LANES), lambda b, hp: (b, 0, q_blk + hp)),
                  pl.BlockSpec((1, T, LANES), lambda b, hp: (b, 0, k_blk + kv_idx(hp))),
                  pl.BlockSpec((1, T, LANES), lambda b, hp: (b, 0, v_blk + kv_idx(hp))),
                  pl.BlockSpec((1, 2, 1, LANES), lambda b, hp: (hp, 0, 0, 0))],
        out_specs=pl.BlockSpec((1, T, LANES), lambda b, hp: (b, 0, hp)),
        out_shape=jax.ShapeDtypeStruct((B, T, n_pair * LANES), F32),
        compiler_params=_params("parallel", "parallel"),
    )(p, p, p, sink_arr)


NBR_SPAN = WIN_R * GRID_W
ATTN_BLOCKS_PER_STEP = 16


def _nbr_attn_kernel(q_ref, k_ref, v_ref, ck_ref, cv_ref, bias_ref, o_ref, kb_ref, vb_ref):
    kb_ref[...] = k_ref[0].astype(BF16)
    vb_ref[...] = v_ref[0].astype(BF16)
    ckb = ck_ref[0].astype(BF16)
    cvb = cv_ref[0].astype(BF16)
    grp = _lane_group()
    n_rows = q_ref.shape[1] // GRID_W

    def head(i, h):
        r0 = jnp.clip(i - WIN_R // 2, 0, n_rows - WIN_R)
        q = q_ref[0, pl.ds(pl.multiple_of(i * GRID_W, GRID_W), GRID_W), :] * ATTN_SCALE
        start = pl.multiple_of(r0 * GRID_W, GRID_W)
        kw = kb_ref[pl.ds(start, NBR_SPAN), :]
        vw = vb_ref[pl.ds(start, NBR_SPAN), :]
        qh = jnp.where(grp == h, q, 0.0).astype(BF16)
        return _attend([lambda: _dot_nt(qh, kw) + bias_ref[i - r0, h], lambda: _dot_nt(qh, ckb)], [vw, cvb], None)

    def rows(step, carry):
        ids = [step * ATTN_BLOCKS_PER_STEP + j for j in range(ATTN_BLOCKS_PER_STEP)]
        outs = _lockstep([head(i, h) for i in ids for h in range(2)])
        for j, i in enumerate(ids):
            o_ref[0, pl.ds(pl.multiple_of(i * GRID_W, GRID_W), GRID_W), :] = _merge_heads(outs[2 * j:2 * j + 2], grp)
        return carry

    lax.fori_loop(0, n_rows // ATTN_BLOCKS_PER_STEP, rows, 0)


def _nbr_bias_table(rpb):
    var = np.arange(WIN_R)[:, None, None]
    r = np.arange(WIN_R)[None, :, None]
    row_sel = (r - var + (WIN_R - 1) == np.arange(2 * WIN_R - 1)[None, None, :]).astype(np.float32)
    w = np.arange(GRID_W)[:, None, None]
    cc = np.arange(GRID_W)[None, :, None]
    c0 = np.clip(w - WIN_C // 2, 0, GRID_W - WIN_C)
    valid = (cc >= c0) & (cc < c0 + WIN_C)
    col_sel = (valid & (cc - w + (WIN_C - 1) == np.arange(2 * WIN_C - 1)[None, None, :])).astype(np.float32)
    t = jnp.einsum('vra,hab->vhrb', row_sel, rpb, precision=lax.Precision.HIGHEST)
    tbl = jnp.einsum('vhrb,wcb->vhwrc', t, col_sel, precision=lax.Precision.HIGHEST)
    tbl = jnp.where(valid[None, None, :, None, :, 0], tbl, NEG_BIG)
    return tbl.reshape(WIN_R, rpb.shape[0], GRID_W, NBR_SPAN)


def _nbr_attn_call(p, rpb, ck_cache, cv_cache):
    B, T, _ = p.shape
    n_pair = H_A // 2
    L = ck_cache.shape[2]
    ckp = ck_cache.transpose(0, 2, 1, 3).reshape(B, L, H_A * HEAD_DIM)
    cvp = cv_cache.transpose(0, 2, 1, 3).reshape(B, L, H_A * HEAD_DIM)
    bias = _nbr_bias_table(rpb)
    col = lambda off: pl.BlockSpec((1, T, LANES), lambda b, hp: (b, 0, off + hp))
    ctx = lambda: pl.BlockSpec((1, L, LANES), lambda b, hp: (b, 0, hp))
    return pl.pallas_call(
        _nbr_attn_kernel,
        grid=(B, n_pair),
        in_specs=[col(0), col(n_pair), col(2 * n_pair), ctx(), ctx(),
                  pl.BlockSpec((WIN_R, 2, GRID_W, NBR_SPAN), lambda b, hp: (0, hp, 0, 0))],
        out_specs=pl.BlockSpec((1, T, LANES), lambda b, hp: (b, 0, hp)),
        out_shape=jax.ShapeDtypeStruct((B, T, n_pair * LANES), F32),
        scratch_shapes=[pltpu.VMEM((T, LANES), BF16), pltpu.VMEM((T, LANES), BF16)],
        compiler_params=_params("parallel", "parallel"),
    )(p, p, p, ckp, cvp, bias)


WIN_SPAN = 3 * WIN_1D
WIN_Q_TILE = 64
WIN_TILES_PER_STEP = 16
ROPE_SWAP = HEAD_DIM // 4


def _rope(x, cos, sin_signed):
    lane = lax.broadcasted_iota(jnp.int32, (1, LANES), 1)
    first = lane % (2 * ROPE_SWAP) < ROPE_SWAP
    partner = jnp.where(first, pltpu.roll(x, LANES - ROPE_SWAP, axis=1), pltpu.roll(x, ROPE_SWAP, axis=1))
    return x * cos + partner * sin_signed


def _win_attn_kernel(q_ref, k_ref, v_ref, ck_ref, cv_ref, cos_ref, sin_ref, sink_ref, o_ref, kb_ref, vb_ref):
    grp = _lane_group()
    kvg = pl.program_id(1) // 2
    both = lambda a: jnp.where(grp == kvg, a, pltpu.roll(a, HEAD_DIM, axis=1))
    kb_ref[...] = _rope(both(k_ref[0]), cos_ref[...], sin_ref[...]).astype(BF16)
    vb_ref[...] = both(v_ref[0]).astype(BF16)
    ckb = both(ck_ref[0]).astype(BF16)
    cvb = both(cv_ref[0]).astype(BF16)
    T = q_ref.shape[1]
    rel = (lax.broadcasted_iota(jnp.int32, (WIN_Q_TILE, WIN_SPAN), 1)
           - lax.broadcasted_iota(jnp.int32, (WIN_Q_TILE, WIN_SPAN), 0))

    def heads(n):
        q0 = pl.multiple_of(n * WIN_Q_TILE, WIN_Q_TILE)
        start = pl.multiple_of(jnp.clip(q0 - WIN_1D, 0, T - WIN_SPAN), WIN_Q_TILE)
        q = _rope(q_ref[0, pl.ds(q0, WIN_Q_TILE), :], cos_ref[pl.ds(q0, WIN_Q_TILE), :],
                  sin_ref[pl.ds(q0, WIN_Q_TILE), :]) * ATTN_SCALE
        kw = kb_ref[pl.ds(start, WIN_SPAN), :]
        vw = vb_ref[pl.ds(start, WIN_SPAN), :]
        valid = jnp.abs(rel + (start - q0)) <= WIN_1D

        def head(h):
            qh = jnp.where(grp == h, q, 0.0).astype(BF16)
            return _attend([lambda: jnp.where(valid, _dot_nt(qh, kw), NEG_BIG), lambda: _dot_nt(qh, ckb)],
                           [vw, cvb], sink_ref[0, h][:, :1])

        return [head(0), head(1)]

    def tiles(step, carry):
        ids = [step * WIN_TILES_PER_STEP + j for j in range(WIN_TILES_PER_STEP)]
        outs = _lockstep([prog for n in ids for prog in heads(n)])
        for j, n in enumerate(ids):
            o_ref[0, pl.ds(pl.multiple_of(n * WIN_Q_TILE, WIN_Q_TILE), WIN_Q_TILE), :] = _merge_heads(
                outs[2 * j:2 * j + 2], grp)
        return carry

    lax.fori_loop(0, T // WIN_Q_TILE // WIN_TILES_PER_STEP, tiles, 0)


def _rope_tables(T):
    t = jnp.arange(T)
    quarter = HEAD_DIM // 4
    inv = 1.0 / (ROPE_BASE ** (jnp.arange(quarter, dtype=F32) / quarter))
    ang_r = (t // GRID_W).astype(F32)[:, None] * inv[None, :]
    ang_c = (t % GRID_W).astype(F32)[:, None] * inv[None, :]
    cos = jnp.concatenate([jnp.cos(ang_r), jnp.cos(ang_r), jnp.cos(ang_c), jnp.cos(ang_c)], axis=-1)
    sin = jnp.concatenate([-jnp.sin(ang_r), jnp.sin(ang_r), -jnp.sin(ang_c), jnp.sin(ang_c)], axis=-1)
    return jnp.tile(cos, (1, 2)), jnp.tile(sin, (1, 2))


def _win_attn_call(p, sink, ck_cache, cv_cache):
    B, T, _ = p.shape
    n_pair = H_C // 2
    L = ck_cache.shape[2]
    ckp = ck_cache.transpose(0, 2, 1, 3).reshape(B, L, KV_C * HEAD_DIM)
    cvp = cv_cache.transpose(0, 2, 1, 3).reshape(B, L, KV_C * HEAD_DIM)
    cos, sin = _rope_tables(T)
    sink_arr = jnp.broadcast_to(sink.reshape(n_pair, 2, 1, 1), (n_pair, 2, 1, LANES))
    col = lambda blk: pl.BlockSpec((1, T, LANES), lambda b, hp: (b, 0, blk))
    ctx = lambda: pl.BlockSpec((1, L, LANES), lambda b, hp: (b, 0, 0))
    tab = lambda: pl.BlockSpec((T, LANES), lambda b, hp: (0, 0))
    return pl.pallas_call(
        _win_attn_kernel,
        grid=(B, n_pair),
        in_specs=[pl.BlockSpec((1, T, LANES), lambda b, hp: (b, 0, hp)), col(n_pair), col(n_pair + 1),
                  ctx(), ctx(), tab(), tab(),
                  pl.BlockSpec((1, 2, 1, LANES), lambda b, hp: (hp, 0, 0, 0))],
        out_specs=pl.BlockSpec((1, T, LANES), lambda b, hp: (b, 0, hp)),
        out_shape=jax.ShapeDtypeStruct((B, T, n_pair * LANES), F32),
        scratch_shapes=[pltpu.VMEM((T, LANES), BF16), pltpu.VMEM((T, LANES), BF16)],
        compiler_params=_params("parallel", "parallel"),
    )(p, p, p, ckp, cvp, cos, sin, sink_arr)


def _dot_tn(a, b):
    return lax.dot_general(a, b, (((0,), (0,)), ((), ())), preferred_element_type=F32)


def _split_bf16(x):
    hi = x.astype(BF16)
    return hi, (x - hi.astype(F32)).astype(BF16)


GLA_CHUNKS_PER_STEP = 8
DELTA_CHUNKS_PER_STEP = 4


def _lockstep(programs):
    results = [None] * len(programs)
    live = list(range(len(programs)))
    while live:
        for idx in list(live):
            try:
                next(programs[idx])
            except StopIteration as stop:
                results[idx] = stop.value
                live.remove(idx)
    return results


def _stack_heads(x, grp):
    zero = jnp.zeros_like(x)
    return jnp.concatenate([jnp.where(grp == 0, x, zero), jnp.where(grp == 1, x, zero)], axis=0)


def _block_diag_mask():
    return (lax.broadcasted_iota(jnp.int32, (LANES, 1), 0) // HEAD_DIM) == _lane_group()


def _head_rmsnorm(o, grp):
    sq = o * o
    s0 = jnp.sum(jnp.where(grp == 0, sq, 0.0), axis=-1, keepdims=True)
    s1 = jnp.sum(jnp.where(grp == 1, sq, 0.0), axis=-1, keepdims=True)
    ms = jnp.where(grp == 0, s0, s1) * (1.0 / HEAD_DIM)
    return o * lax.rsqrt(ms + EPS)


def _gla_kernel(q_ref, k_ref, v_ref, r_ref, glr_ref, w2_ref, bg_ref, ng_ref, s0_ref, o_ref, sfin_ref,
                ob_ref, st_ref):
    T = q_ref.shape[1]
    n = T // CHUNK
    grp = _lane_group()
    bd = _block_diag_mask()
    ri = lax.broadcasted_iota(jnp.int32, (CHUNK, CHUNK), 0)
    ci = lax.broadcasted_iota(jnp.int32, (CHUNK, CHUNK), 1)
    ri_pair = lax.broadcasted_iota(jnp.int32, (CHUNK, LANES), 0)
    ci_pair = lax.broadcasted_iota(jnp.int32, (CHUNK, LANES), 1) % CHUNK
    st_ref[...] = s0_ref[0, :, 0]

    def chunk(c0, d):
        sl = pl.ds(c0, CHUNK)
        q = q_ref[0, sl, :] * (DK_B ** -0.5)
        k = k_ref[0, sl, :]
        vb = v_ref[0, sl, :].astype(BF16)
        z = jnp.dot(glr_ref[0, sl, :].astype(BF16), w2_ref[d], preferred_element_type=F32) + bg_ref[d]
        yield
        g = (jnp.minimum(z, 0.0) - jnp.log(1.0 + jnp.exp(-jnp.abs(z)))) * (1.0 / GLA_TAU)
        g_hi, g_lo = _split_bf16(g)
        mask = (ri <= ci) if d else (ri >= ci)
        mask_pair = (ri_pair <= ci_pair) if d else (ri_pair >= ci_pair)
        tri = jnp.where(mask, 1.0, 0.0).astype(BF16)
        G = jnp.dot(tri, g_hi, preferred_element_type=F32) + jnp.dot(tri, g_lo, preferred_element_type=F32)
        yield
        g_tot = G[0:1] if d else G[CHUNK - 1:CHUNK]
        q_t = q * jnp.exp(G)
        k_t = (k * jnp.exp(-G)).astype(BF16)
        k_e = (k * jnp.exp(g_tot - G)).astype(BF16)
        att = jnp.where(mask_pair, _dot_nt(q_t.astype(BF16), _stack_heads(k_t, grp)), 0.0)
        u = jnp.where(bd, _dot_tn(k_e, vb), 0.0)
        decay = jnp.exp(jnp.broadcast_to(g_tot, (LANES, LANES)).T)
        yield
        S = st_ref[d]
        st_ref[d] = decay * S + u
        return (jnp.dot(q_t.astype(BF16), S.astype(BF16), preferred_element_type=F32)
                + jnp.dot(att.astype(BF16), _stack_heads(vb, grp), preferred_element_type=F32))

    per_step = min(GLA_CHUNKS_PER_STEP, n)

    def body(i, carry):
        c_f = [pl.multiple_of((i * per_step + j) * CHUNK, CHUNK) for j in range(per_step)]
        c_b = [pl.multiple_of((n - 1 - i * per_step - j) * CHUNK, CHUNK) for j in range(per_step)]
        outs = _lockstep([chunk(c0, 0) for c0 in c_f] + [chunk(c0, 1) for c0 in c_b])
        for j in range(per_step):
            o_ref[0, pl.ds(c_f[j], CHUNK), :] = outs[j]
            ob_ref[pl.ds(c_b[j], CHUNK), :] = outs[per_step + j]
        return carry

    lax.fori_loop(0, n // per_step, body, 0)
    sfin_ref[0, :, 0] = st_ref[...]

    def gate(i, carry):
        sl = pl.ds(pl.multiple_of(i * CHUNK, CHUNK), CHUNK)
        r = r_ref[0, sl, :]
        y = _head_rmsnorm(o_ref[0, sl, :] + ob_ref[sl, :], grp) * ng_ref[...]
        o_ref[0, sl, :] = y * (r * jax.nn.sigmoid(r))
        return carry

    lax.fori_loop(0, n, gate, 0)


def _pair_block_diag(s):
    B, _, H, dk, dv = s.shape
    s = s.reshape(B, 2, H // 2, 2, dk, dv)
    z = jnp.zeros_like(s[:, :, :, 0])
    top = jnp.concatenate([s[:, :, :, 0], z], axis=-1)
    bot = jnp.concatenate([z, s[:, :, :, 1]], axis=-1)
    return jnp.concatenate([top, bot], axis=-2)


def _pair_block_diag_inv(sp):
    a = sp[..., :HEAD_DIM, :HEAD_DIM]
    b = sp[..., HEAD_DIM:, HEAD_DIM:]
    B, _, P = sp.shape[:3]
    return jnp.stack([a, b], axis=3).reshape(B, 2, 2 * P, HEAD_DIM, HEAD_DIM)


EV_BQ, EV_BK, EV_BV, EV_BR, EV_GLR = 12, 16, 20, 24, 28


def _gla_call(p, w_g2, b_g, norm_g, s0):
    B, T, _ = p.shape
    n_pair = H_B // 2
    w2 = jnp.zeros((2, LANES, H_B * DK_B), F32)
    w2 = w2.at[0, :GLA_RANK].set(w_g2[0]).at[1, GLA_RANK:2 * GLA_RANK].set(w_g2[1]).astype(BF16)
    col = lambda off: pl.BlockSpec((1, T, LANES), lambda b, hp: (b, 0, off + hp))
    st = lambda: pl.BlockSpec((1, 2, 1, LANES, LANES), lambda b, hp: (b, 0, hp, 0, 0))
    o, s_fin = pl.pallas_call(
        _gla_kernel,
        grid=(B, n_pair),
        in_specs=[col(EV_BQ), col(EV_BK), col(EV_BV), col(EV_BR),
                  pl.BlockSpec((1, T, LANES), lambda b, hp: (b, 0, EV_GLR)),
                  pl.BlockSpec((2, LANES, LANES), lambda b, hp: (0, 0, hp)),
                  pl.BlockSpec((2, 1, LANES), lambda b, hp: (0, 0, hp)),
                  pl.BlockSpec((1, LANES), lambda b, hp: (0, hp)),
                  st()],
        out_specs=[pl.BlockSpec((1, T, LANES), lambda b, hp: (b, 0, hp)), st()],
        out_shape=[jax.ShapeDtypeStruct((B, T, n_pair * LANES), F32),
                   jax.ShapeDtypeStruct((B, 2, n_pair, LANES, LANES), F32)],
        scratch_shapes=[pltpu.VMEM((T, LANES), F32), pltpu.VMEM((2, LANES, LANES), F32)],
        compiler_params=_params("parallel", "parallel"),
    )(p, p, p, p, p, w2, b_g.reshape(2, 1, H_B * DK_B), norm_g.reshape(1, H_B * DV_B), _pair_block_diag(s0))
    return o, _pair_block_diag_inv(s_fin)


def _softplus(x):
    return jnp.maximum(x, 0.0) + jnp.log(1.0 + jnp.exp(-jnp.abs(x)))


def _delta_kernel(q_ref, k_ref, v_ref, z_ref, dab_ref, cq_ref, ck_ref, cv_ref, nega_ref, dtb_ref, ng_ref, s0_ref,
                  o_ref, sfin_ref, qs_ref, ks_ref, vs_ref, ob_ref, st_ref, hw_ref, ha_ref, hd_ref):
    T = q_ref.shape[1]
    n = T // CHUNK
    hp = pl.program_id(1)
    grp = _lane_group()
    lane = lax.broadcasted_iota(jnp.int32, (1, LANES), 1)
    subl = lax.broadcasted_iota(jnp.int32, (LANES, 1), 0)
    bd = _block_diag_mask()
    ri = lax.broadcasted_iota(jnp.int32, (CHUNK, CHUNK), 0)
    ci = lax.broadcasted_iota(jnp.int32, (CHUNK, CHUNK), 1)
    eye = jnp.where(ri == ci, 1.0, 0.0)
    row = lax.broadcasted_iota(jnp.int32, (CHUNK, 1), 0)
    st_ref[...] = s0_ref[0, :, 0]

    def prep(i, carry):
        c0 = pl.multiple_of(i * CHUNK, CHUNK)
        lo = pl.multiple_of(jnp.maximum(c0 - SUBLANES, 0), SUBLANES)
        hi = pl.multiple_of(jnp.minimum(c0 + CHUNK, T - SUBLANES), SUBLANES)
        for src, cw_ref, dst, scale in ((q_ref, cq_ref, qs_ref, DK_D ** -0.5), (k_ref, ck_ref, ks_ref, 1.0),
                                        (v_ref, cv_ref, vs_ref, None)):
            x = src[0, pl.ds(c0, CHUNK), :]
            before = jnp.where(i > 0, src[0, pl.ds(lo, SUBLANES), :][SUBLANES - 1:], 0.0)
            after = jnp.where(i < n - 1, src[0, pl.ds(hi, SUBLANES), :][:1], 0.0)
            x_prev = jnp.where(row == 0, before, pltpu.roll(x, 1, axis=0))
            x_next = jnp.where(row == CHUNK - 1, after, pltpu.roll(x, CHUNK - 1, axis=0))
            cw = cw_ref[...]
            y = x_prev * cw[0:1] + x * cw[1:2] + x_next * cw[2:3]
            y = y * jax.nn.sigmoid(y)
            if scale is not None:
                sq = y * y
                s0 = jnp.sum(jnp.where(grp == 0, sq, 0.0), axis=-1, keepdims=True)
                s1 = jnp.sum(jnp.where(grp == 1, sq, 0.0), axis=-1, keepdims=True)
                y = y * lax.rsqrt(jnp.where(grp == 0, s0, s1) + EPS) * scale
            dst[pl.ds(c0, CHUNK), :] = y
        return carry

    lax.fori_loop(0, n, prep, 0)

    def chunk(c0, d):
        sl = pl.ds(c0, CHUNK)
        q = qs_ref[sl, :]
        k = ks_ref[sl, :]
        v = vs_ref[sl, :]
        ab = dab_ref[0, sl, :]
        gb = jnp.where(lane < 2 * H_D, nega_ref[...] * _softplus(ab + dtb_ref[...]), jax.nn.sigmoid(ab))
        gb_t = gb.T
        incl = (ri <= ci) if d else (ri >= ci)
        incl_t = (ri >= ci) if d else (ri <= ci)
        strict = (ri < ci) if d else (ri > ci)

        def head(h):
            jg = d * H_D + 2 * hp + h
            g_c = jnp.sum(jnp.where(lane == jg, gb, 0.0), axis=1, keepdims=True)
            beta_c = jnp.sum(jnp.where(lane == 2 * H_D + jg, gb, 0.0), axis=1, keepdims=True)
            g_r = jnp.sum(jnp.where(subl == jg, gb_t, 0.0), axis=0, keepdims=True)
            G_c = jnp.sum(jnp.where(incl, g_r, 0.0), axis=1, keepdims=True)
            G_r = jnp.sum(jnp.where(incl_t, g_c, 0.0), axis=0, keepdims=True)
            tot = jnp.sum(g_r, axis=1, keepdims=True)
            gam = jnp.where(incl, jnp.exp(jnp.where(incl, G_c - G_r, 0.0)), 0.0)
            kh = jnp.where(grp == h, k, 0.0)
            qh = jnp.where(grp == h, q, 0.0)
            vh = jnp.where(grp == h, v, 0.0)
            kb = kh * beta_c
            khb = kh.astype(BF16)
            a = jnp.where(strict, _dot_nt(kb.astype(BF16), khb) * gam, 0.0)
            a_qk = (_dot_nt(qh.astype(BF16), khb) * gam).astype(BF16)
            yield
            wide = jnp.concatenate([-a, eye], axis=1)
            for _ in range(6):
                w_hi, w_lo = _split_bf16(wide)
                prod = (jnp.dot(w_hi[:, :CHUNK], w_hi, preferred_element_type=F32)
                        + jnp.dot(w_lo[:, :CHUNK], w_hi, preferred_element_type=F32)
                        + jnp.dot(w_hi[:, :CHUNK], w_lo, preferred_element_type=F32))
                wide = prod + jnp.where(lane >= CHUNK, wide, 0.0)
                yield
            t_inv = wide[:, CHUNK:]
            e_g = jnp.exp(G_c)
            rhs = vh * beta_c + pltpu.roll(kb * e_g, HEAD_DIM, axis=1)
            sol = jnp.dot(t_inv.astype(BF16), rhs.astype(BF16), preferred_element_type=F32)
            return sol, a_qk, qh * e_g, kh * jnp.exp(tot - G_c), jnp.exp(tot)

        return head

    per_step = min(DELTA_CHUNKS_PER_STEP, n)
    n_steps = n // per_step
    n_sys = 4 * per_step

    def starts(i):
        c_f = [pl.multiple_of((i * per_step + j) * CHUNK, CHUNK) for j in range(per_step)]
        c_b = [pl.multiple_of((n - 1 - i * per_step - j) * CHUNK, CHUNK) for j in range(per_step)]
        return c_f, c_b

    def head_programs(i):
        programs = []
        for d, c0s in zip((0, 1), starts(i)):
            for c0 in c0s:
                head = chunk(c0, d)
                programs += [head(0), head(1)]
        return programs

    def store_heads(slot, results):
        for s, (r, a_qk, q_g, k_e, d_last) in enumerate(results):
            hw_ref[slot, s, 0] = r
            hw_ref[slot, s, 1] = q_g
            hw_ref[slot, s, 2] = k_e
            ha_ref[slot, s] = a_qk
            hd_ref[slot, s] = jnp.broadcast_to(d_last, (SUBLANES, LANES))

    def load_head(slot, s):
        return hw_ref[slot, s, 0], ha_ref[slot, s], hw_ref[slot, s, 1], hw_ref[slot, s, 2], hd_ref[slot, s][0:1, 0:1]

    def scan_step(d, slot, s):
        (r0, aqk0, qg0, ke0, d0), (r1, aqk1, qg1, ke1, d1) = load_head(slot, s), load_head(slot, s + 1)
        S = st_ref[d]
        Sb = S.astype(BF16)
        w_val = jnp.where(grp == 0, r0, r1)
        k_cum = pltpu.roll(jnp.where(grp == 0, r1, r0), HEAD_DIM, axis=1)
        v_new = w_val - jnp.dot(k_cum.astype(BF16), Sb, preferred_element_type=F32)
        o = jnp.dot((qg0 + qg1).astype(BF16), Sb, preferred_element_type=F32)
        yield
        vnb = v_new.astype(BF16)
        o = o + jnp.where(grp == 0, jnp.dot(aqk0, vnb, preferred_element_type=F32), 0.0)
        o = o + jnp.where(grp == 1, jnp.dot(aqk1, vnb, preferred_element_type=F32), 0.0)
        d_rows = jnp.where(subl // HEAD_DIM == 0, d0, d1)
        st_ref[d] = S * d_rows + jnp.where(bd, _dot_tn((ke0 + ke1).astype(BF16), vnb), 0.0)
        return o

    def scan_dir(d, i, slot):
        c0s = starts(i)[d]
        for j in range(per_step):
            o = yield from scan_step(d, slot, d * 2 * per_step + 2 * j)
            if d == 0:
                o_ref[0, pl.ds(c0s[j], CHUNK), :] = o
            else:
                ob_ref[pl.ds(c0s[j], CHUNK), :] = o

    store_heads(0, _lockstep(head_programs(0)))

    def body(i, carry):
        slot = i % 2
        results = _lockstep(head_programs(i) + [scan_dir(0, i - 1, 1 - slot), scan_dir(1, i - 1, 1 - slot)])
        store_heads(slot, results[:n_sys])
        return carry

    lax.fori_loop(1, n_steps, body, 0)
    last = n_steps - 1
    _lockstep([scan_dir(0, last, last % 2), scan_dir(1, last, last % 2)])
    sfin_ref[0, :, 0] = st_ref[...]

    def gate(i, carry):
        sl = pl.ds(pl.multiple_of(i * CHUNK, CHUNK), CHUNK)
        z = z_ref[0, sl, :]
        y = _head_rmsnorm(o_ref[0, sl, :] + ob_ref[sl, :], grp) * ng_ref[...]
        o_ref[0, sl, :] = y * (z * jax.nn.sigmoid(z))
        return carry

    lax.fori_loop(0, n, gate, 0)


OD_DQ, OD_DK, OD_DV, OD_DZ, OD_DAB = 6, 10, 14, 18, 22


def _delta_call(p, w_conv, a_log, dt_bias, norm_g, s0):
    B, T, _ = p.shape
    n_pair = H_D // 2
    n_sys = 4 * min(DELTA_CHUNKS_PER_STEP, T // CHUNK)
    nega = jnp.zeros((1, LANES), F32).at[0, :2 * H_D].set(-jnp.exp(a_log.reshape(-1)))
    dtb = jnp.zeros((1, LANES), F32).at[0, :2 * H_D].set(dt_bias.reshape(-1))
    ng = jnp.tile(norm_g, 2).reshape(1, LANES)
    col = lambda off: pl.BlockSpec((1, T, LANES), lambda b, hp: (b, 0, off + hp))
    cw = lambda off: pl.BlockSpec((SHORT_CONV, LANES), lambda b, hp: (0, off + hp))
    vec = lambda: pl.BlockSpec((1, LANES), lambda b, hp: (0, 0))
    st = lambda: pl.BlockSpec((1, 2, 1, LANES, LANES), lambda b, hp: (b, 0, hp, 0, 0))
    o, s_fin = pl.pallas_call(
        _delta_kernel,
        grid=(B, n_pair),
        in_specs=[col(OD_DQ), col(OD_DK), col(OD_DV), col(OD_DZ),
                  pl.BlockSpec((1, T, LANES), lambda b, hp: (b, 0, OD_DAB)),
                  cw(0), cw(n_pair), cw(2 * n_pair), vec(), vec(), vec(), st()],
        out_specs=[pl.BlockSpec((1, T, LANES), lambda b, hp: (b, 0, hp)), st()],
        out_shape=[jax.ShapeDtypeStruct((B, T, n_pair * LANES), F32),
                   jax.ShapeDtypeStruct((B, 2, n_pair, LANES, LANES), F32)],
        scratch_shapes=[pltpu.VMEM((T, LANES), F32), pltpu.VMEM((T, LANES), F32), pltpu.VMEM((T, LANES), F32),
                        pltpu.VMEM((T, LANES), F32), pltpu.VMEM((2, LANES, LANES), F32),
                        pltpu.VMEM((2, n_sys, 3, CHUNK, LANES), F32),
                        pltpu.VMEM((2, n_sys, CHUNK, CHUNK), BF16),
                        pltpu.VMEM((2, n_sys, SUBLANES, LANES), F32)],
        compiler_params=_params("parallel", "parallel"),
    )(p, p, p, p, p, w_conv, w_conv, w_conv, nega, dtb, ng, _pair_block_diag(s0))
    return o, _pair_block_diag_inv(s_fin)


def _split_cols(p, sizes):
    cuts = [int(s) for s in np.cumsum(sizes)[:-1]]
    return jnp.split(p, cuts, axis=-1)


def _even_mix(p, latent, rpb, w_g2, b_g, norm_g, ak_cache, av_cache, sb):
    B, T, _ = p.shape
    ak = p[..., H_A * HEAD_DIM:2 * H_A * HEAD_DIM]
    av = p[..., 2 * H_A * HEAD_DIM:3 * H_A * HEAD_DIM]
    n_pair = H_A // 2
    if latent:
        o_a = _nbr_attn_call(p, rpb, ak_cache, av_cache)
        k_h = v_h = None
    else:
        k_h = ak.reshape(B, T, H_A, HEAD_DIM).transpose(0, 2, 1, 3)
        v_h = av.reshape(B, T, H_A, HEAD_DIM).transpose(0, 2, 1, 3)
        o_a = _dense_attn_call(p, 0, n_pair, 2 * n_pair, None, False)
    o_b, s_b = _gla_call(p, w_g2, b_g, norm_g, sb)
    return o_a, o_b, k_h, v_h, s_b


def _odd_mix(p, latent, sink, w_conv, a_log, dt_bias, norm_g, ck_cache, cv_cache, sd):
    B, T, _ = p.shape
    ck = p[..., H_C * HEAD_DIM:(H_C + KV_C) * HEAD_DIM]
    cv = p[..., (H_C + KV_C) * HEAD_DIM:(H_C + 2 * KV_C) * HEAD_DIM]
    n_pair = H_C // 2
    if latent:
        o_c = _win_attn_call(p, sink, ck_cache, cv_cache)
        k_h = v_h = None
    else:
        k_h = ck.reshape(B, T, KV_C, HEAD_DIM).transpose(0, 2, 1, 3)
        v_h = cv.reshape(B, T, KV_C, HEAD_DIM).transpose(0, 2, 1, 3)
        o_c = _dense_attn_call(p, 0, n_pair, n_pair + 1, sink, True)
    o_d, s_d = _delta_call(p, w_conv, a_log, dt_bias, norm_g, sd)
    return o_c, o_d, k_h, v_h, s_d


def _reorder_cols(w, sizes, order):
    parts = _split_cols(w, sizes)
    out = jnp.concatenate([parts[i] for i in order], axis=-1)
    pad = -out.shape[-1] % LANES
    return jnp.pad(out, ((0, 0), (0, pad)))


def _reorder_even_w_in(w):
    return _reorder_cols(w, EV_SIZES, (0, 1, 2, 3, 4, 5, 7, 6))


def _reorder_odd_w_in(w):
    return _reorder_cols(w, OD_SIZES, (0, 1, 2, 3, 4, 5, 8, 6, 7))


TM_LATENT = 512
TM_CONTEXT = 256
TM_FFN_LATENT = 512


def kernel(x_prompt, x_sample, cache_a_k, cache_a_v, state_b, cache_c_k, cache_c_v, state_d, c, c_ctx, ada_w, ada_b, norm1_g, norm2_g, ffn_up, ffn_conv, ffn_down, ev_w_in, ev_w_out, a_rpb, b_w_g2, b_b_g, b_norm_g, od_w_in, od_w_out, c_sink, d_conv, d_a_log, d_dt_bias, d_norm_g, final_g):
    Bp, Tp, D = x_prompt.shape
    Bs, Ts, _ = x_sample.shape
    cond = jnp.concatenate([c, c_ctx[None, :], jnp.zeros((2 * SUBLANES - Bs - 1, D), F32)], axis=0)
    mods = _adaln(cond, ada_w, ada_b)
    xp, xs = x_prompt, x_sample
    streams = ((False, Bp, TM_CONTEXT), (True, Bs, TM_LATENT))
    ak_l, av_l, sb_l, ck_l, cv_l, sd_l = [], [], [], [], [], []
    for l in range(DEPTH):
        j = l // 2
        mod_s = [mods[l, :Bs, i * D:(i + 1) * D] for i in range(6)]
        mod_p = [jnp.broadcast_to(mods[l, Bs:Bs + 1, i * D:(i + 1) * D], (Bp, D)) for i in range(6)]
        w_in = (_reorder_even_w_in(ev_w_in[j]) if l % 2 == 0 else _reorder_odd_w_in(od_w_in[j])).astype(BF16)
        w_out = (ev_w_out[j] if l % 2 == 0 else od_w_out[j]).astype(BF16)
        w_up = ffn_up[l].astype(BF16)
        w_down = ffn_down[l].astype(BF16)
        new_x = []
        for (latent, B, tm), x, mod in zip(streams, (xp, xs), (mod_p, mod_s)):
            sh1, sc1, g1, sh2, sc2, g2 = mod
            p = _nm_matmul(x, norm1_g[l], sh1, sc1, w_in, tm)
            if l % 2 == 0:
                s0 = state_b[:, j] if latent else jnp.zeros((B, 2, H_B, DK_B, DV_B), F32)
                o1, o2, k_h, v_h, s_new = _even_mix(p, latent, a_rpb[j], b_w_g2[j], b_b_g[j], b_norm_g[j],
                                                    cache_a_k[:, j], cache_a_v[:, j], s0)
                if not latent:
                    ak_l.append(k_h)
                    av_l.append(v_h)
                    sb_l.append(s_new)
            else:
                s0 = state_d[:, j] if latent else jnp.zeros((B, 2, H_D, DK_D, DV_D), F32)
                o1, o2, k_h, v_h, s_new = _odd_mix(p, latent, c_sink[j], d_conv[j], d_a_log[j], d_dt_bias[j],
                                                   d_norm_g[j], cache_c_k[:, j], cache_c_v[:, j], s0)
                if not latent:
                    ck_l.append(k_h)
                    cv_l.append(v_h)
                    sd_l.append(s_new)
            x = _mix_ffn_res(x, o1, o2, w_out, g1, norm2_g[l], sh2, sc2, g2, w_up, ffn_conv[l], w_down,
                             TM_FFN_LATENT if latent else tm, final_g if l == DEPTH - 1 else None)
            new_x.append(x)
        xp, xs = new_x
    return (xp, xs, jnp.stack(ak_l, axis=1), jnp.stack(av_l, axis=1), jnp.stack(sb_l, axis=1),
            jnp.stack(ck_l, axis=1), jnp.stack(cv_l, axis=1), jnp.stack(sd_l, axis=1))
```

```python
import functools
import math

import jax
import jax.numpy as jnp
import numpy as np
from jax import lax
from jax.experimental import pallas as pl
from jax.experimental.pallas import tpu as pltpu

D_MODEL = 1024
DEPTH = 4
GRID_W = 64
HEAD_DIM = 64
H_A = D_MODEL // (2 * HEAD_DIM)
WIN_R = 8
WIN_C = 16
H_B = D_MODEL // (2 * HEAD_DIM)
DK_B = HEAD_DIM
DV_B = HEAD_DIM
GLA_RANK = 16
GLA_TAU = 16.0
H_C = D_MODEL // (2 * HEAD_DIM)
KV_C = H_C // 4
WIN_1D = 128
ROPE_BASE = 10000.0
H_D = D_MODEL // (2 * HEAD_DIM)
DK_D = HEAD_DIM
DV_D = HEAD_DIM
SHORT_CONV = 3
CHUNK = 64
EPS = 1e-6
EV_SIZES = (H_A * HEAD_DIM, H_A * HEAD_DIM, H_A * HEAD_DIM, H_B * DK_B, H_B * DK_B, H_B * DV_B, 2 * GLA_RANK, H_B * DV_B)
OD_SIZES = (H_C * HEAD_DIM, KV_C * HEAD_DIM, KV_C * HEAD_DIM, H_D * DK_D, H_D * DK_D, H_D * DV_D, 2 * H_D, 2 * H_D, H_D * DV_D)
F32 = jnp.float32
BF16 = jnp.bfloat16

V7X_VMEM_LIMIT_BYTES = 52 * 1024 * 1024
SUBLANES = 8
LANES = 128
COL_CHUNK = 512
ELEMENTWISE_ROWS = 256
FFN_SUB_COLS = 1536


def _params(*sem):
    return pltpu.CompilerParams(dimension_semantics=sem, vmem_limit_bytes=V7X_VMEM_LIMIT_BYTES)


def _norm_mod(x, g, shift, scale):
    y = x * lax.rsqrt(jnp.mean(x * x, axis=-1, keepdims=True) + EPS)
    return (y * g) * (1.0 + scale) + shift


def _adaln_kernel(c_ref, w_ref, b_ref, o_ref):
    cond = c_ref[...]
    a = cond * jax.nn.sigmoid(cond)
    a_hi = a.astype(BF16)
    a_lo = (a - a_hi.astype(F32)).astype(BF16)
    w = w_ref[0]
    w_hi = w.astype(BF16)
    w_lo = (w - w_hi.astype(F32)).astype(BF16)
    acc = jnp.dot(a_hi, w_hi, preferred_element_type=F32)
    acc += jnp.dot(a_lo, w_hi, preferred_element_type=F32)
    acc += jnp.dot(a_hi, w_lo, preferred_element_type=F32)
    o_ref[0] = acc + b_ref[0]


def _adaln(cond, ada_w, ada_b):
    R = cond.shape[0]
    L, D, N = ada_w.shape
    tn = 1024
    return pl.pallas_call(
        _adaln_kernel,
        grid=(L, N // tn),
        in_specs=[pl.BlockSpec((R, D), lambda l, j: (0, 0)),
                  pl.BlockSpec((1, D, tn), lambda l, j: (l, 0, j)),
                  pl.BlockSpec((1, 1, tn), lambda l, j: (l, 0, j))],
        out_specs=pl.BlockSpec((1, R, tn), lambda l, j: (l, 0, j)),
        out_shape=jax.ShapeDtypeStruct((L, R, N), F32),
        compiler_params=_params("parallel", "parallel"),
    )(cond, ada_w, ada_b.reshape(L, 1, N))


def _nm_matmul_kernel(x_ref, g_ref, sh_ref, sc_ref, w_ref, o_ref):
    h = _norm_mod(x_ref[0], g_ref[...], sh_ref[0], sc_ref[0]).astype(BF16)
    n = o_ref.shape[-1]
    for c0 in range(0, n, COL_CHUNK):
        c1 = min(c0 + COL_CHUNK, n)
        o_ref[0, :, c0:c1] = jnp.dot(h, w_ref[:, c0:c1], preferred_element_type=F32)


def _nm_matmul(x, g, shift, scale, w, tm):
    B, T, D = x.shape
    N = w.shape[1]
    return pl.pallas_call(
        _nm_matmul_kernel,
        grid=(B, T // tm),
        in_specs=[pl.BlockSpec((1, tm, D), lambda b, i: (b, i, 0)),
                  pl.BlockSpec((1, D), lambda b, i: (0, 0)),
                  pl.BlockSpec((1, 1, D), lambda b, i: (b, 0, 0)),
                  pl.BlockSpec((1, 1, D), lambda b, i: (b, 0, 0)),
                  pl.BlockSpec((D, N), lambda b, i: (0, 0))],
        out_specs=pl.BlockSpec((1, tm, N), lambda b, i: (b, i, 0)),
        out_shape=jax.ShapeDtypeStruct((B, T, N), F32),
        compiler_params=_params("parallel", "parallel"),
    )(x, g.reshape(1, D), shift.reshape(B, 1, D), scale.reshape(B, 1, D), w)


def _mix_ffn_kernel(x_ref, xp_ref, xn_ref, a1_ref, a1p_ref, a1n_ref, a2_ref, a2p_ref, a2n_ref, wo1_ref, wo2_ref,
                    gate1_ref, g_ref, sh_ref, sc_ref, gate_ref, wu_ref, cw_ref, wd_ref, fg_ref,
                    o_ref, h_ref, *, final_norm):
    i = pl.program_id(1)
    tm = x_ref.shape[1]
    F = wd_ref.shape[0]
    ext = lambda m, p, n: jnp.concatenate([m[0], p[0], n[0]], axis=0).astype(BF16)
    mix = (jnp.dot(ext(a1_ref, a1p_ref, a1n_ref), wo1_ref[...], preferred_element_type=F32)
           + jnp.dot(ext(a2_ref, a2p_ref, a2n_ref), wo2_ref[...], preferred_element_type=F32))
    gate1 = gate1_ref[0]
    x1 = x_ref[0] + gate1 * mix[0:tm]
    x1p = xp_ref[0] + gate1 * mix[tm:tm + SUBLANES]
    x1n = xn_ref[0] + gate1 * mix[tm + SUBLANES:]
    g, sh, sc = g_ref[...], sh_ref[0], sc_ref[0]
    h_ref[0:tm, :] = _norm_mod(x1, g, sh, sc).astype(BF16)
    hp = jnp.where(i > 0, _norm_mod(x1p, g, sh, sc), 0.0)
    hn = jnp.where(i < pl.num_programs(1) - 1, _norm_mod(x1n, g, sh, sc), 0.0)
    h_ref[tm:tm + 2 * SUBLANES, :] = jnp.concatenate([hp, hn], axis=0).astype(BF16)
    h = h_ref[...]
    row = lax.broadcasted_iota(jnp.int32, (tm, 1), 0)

    def conv_branch(c0, c1):
        u_all = jnp.dot(h, wu_ref[:, c0:c1], preferred_element_type=F32)
        u = u_all[0:tm]
        u_prev = jnp.where(row == 0, u_all[tm + SUBLANES - 1:tm + SUBLANES], pltpu.roll(u, 1, axis=0))
        u_next = jnp.where(row == tm - 1, u_all[tm + SUBLANES:tm + SUBLANES + 1], pltpu.roll(u, tm - 1, axis=0))
        cw = cw_ref[:, c0:c1]
        return u_prev * cw[0:1] + u * cw[1:2] + u_next * cw[2:3]

    acc = None
    for c0 in range(0, F, FFN_SUB_COLS):
        c1 = min(c0 + FFN_SUB_COLS, F)
        a = conv_branch(c0, c1)
        gt = conv_branch(F + c0, F + c1)
        act = a * (gt * jax.nn.sigmoid(gt))
        part = jnp.dot(act.astype(BF16), wd_ref[c0:c1, :], preferred_element_type=F32)
        acc = part if acc is None else acc + part

    y = x1 + gate_ref[0] * acc
    if final_norm:
        y = (y * lax.rsqrt(jnp.mean(y * y, axis=-1, keepdims=True) + EPS)) * fg_ref[...]
    o_ref[0] = y


def _mix_ffn_res(x, a1, a2, w_out, gate1, g, shift, scale, gate, w_up, w_conv, w_down, tm, final_g=None):
    B, T, D = x.shape
    K = a1.shape[-1]
    F = w_down.shape[0]
    nt = T // tm
    rows8 = tm // SUBLANES
    vec = lambda: pl.BlockSpec((1, 1, D), lambda b, i: (b, 0, 0))
    resident = lambda shape, blk=0: pl.BlockSpec(shape, lambda b, i: (blk, 0), pipeline_mode=pl.Buffered(1))
    tile = lambda w: pl.BlockSpec((1, tm, w), lambda b, i: (b, i, 0))
    before = lambda w: pl.BlockSpec((1, SUBLANES, w), lambda b, i: (b, jnp.maximum(i * rows8 - 1, 0), 0))
    after = lambda w: pl.BlockSpec((1, SUBLANES, w),
                                   lambda b, i: (b, jnp.minimum((i + 1) * rows8, T // SUBLANES - 1), 0))
    return pl.pallas_call(
        functools.partial(_mix_ffn_kernel, final_norm=final_g is not None),
        grid=(B, nt),
        in_specs=[tile(D), before(D), after(D), tile(K), before(K), after(K), tile(K), before(K), after(K),
                  resident((K, D), 0), resident((K, D), 1), vec(),
                  pl.BlockSpec((1, D), lambda b, i: (0, 0)),
                  vec(), vec(), vec(),
                  resident((D, 2 * F)), resident((3, 2 * F)), resident((F, D)),
                  pl.BlockSpec((1, D), lambda b, i: (0, 0))],
        out_specs=pl.BlockSpec((1, tm, D), lambda b, i: (b, i, 0)),
        out_shape=jax.ShapeDtypeStruct((B, T, D), F32),
        scratch_shapes=[pltpu.VMEM((tm + 2 * SUBLANES, D), BF16)],
        compiler_params=_params("parallel", "parallel"),
    )(x, x, x, a1, a1, a1, a2, a2, a2, w_out, w_out, gate1.reshape(B, 1, D),
      g.reshape(1, D), shift.reshape(B, 1, D), scale.reshape(B, 1, D), gate.reshape(B, 1, D),
      w_up, w_conv, w_down, (g if final_g is None else final_g).reshape(1, D))


NEG_BIG = -1e30
ATTN_SCALE = HEAD_DIM ** -0.5


def _dot_nt(a, b):
    return lax.dot_general(a, b, (((1,), (1,)), ((), ())), preferred_element_type=F32)


def _lane_group():
    return lax.broadcasted_iota(jnp.int32, (1, LANES), 1) // HEAD_DIM


def _attend(score_fns, values, sink):
    scores = [fn() for fn in score_fns]
    yield
    s = scores[0] if len(scores) == 1 else jnp.concatenate(scores, axis=1)
    m = jnp.max(s, axis=-1, keepdims=True)
    if sink is not None:
        m = jnp.maximum(m, sink)
    e = jnp.exp(s - m)
    den = jnp.sum(e, axis=-1, keepdims=True)
    if sink is not None:
        den = den + jnp.exp(sink - m)
    eb = e.astype(BF16)
    out = None
    c0 = 0
    for v in values:
        o = jnp.dot(eb[:, c0:c0 + v.shape[0]], v, preferred_element_type=F32)
        out = o if out is None else out + o
        c0 += v.shape[0]
    yield
    return out / den


def _merge_heads(outs, grp):
    return jnp.where(grp == 0, outs[0], outs[1])


def _dense_attn_kernel(q_ref, k_ref, v_ref, sink_ref, o_ref, *, gqa, use_sink):
    grp = _lane_group()
    q = q_ref[0] * ATTN_SCALE
    k = k_ref[0]
    v = v_ref[0]
    if gqa:
        kvg = pl.program_id(1) // 2
        k = jnp.where(grp == kvg, k, pltpu.roll(k, HEAD_DIM, axis=1))
        v = jnp.where(grp == kvg, v, pltpu.roll(v, HEAD_DIM, axis=1))
    kb = k.astype(BF16)
    vb = v.astype(BF16)

    def head(h):
        qh = jnp.where(grp == h, q, 0.0).astype(BF16)
        sink = sink_ref[0, h][:, :1] if use_sink else None
        return _attend([lambda: _dot_nt(qh, kb)], [vb], sink)

    o_ref[0] = _merge_heads(_lockstep([head(0), head(1)]), grp)


def _dense_attn_call(p, q_blk, k_blk, v_blk, sink, gqa):
    B, T, _ = p.shape
    n_pair = H_A // 2
    use_sink = sink is not None
    if use_sink:
        sink_arr = jnp.broadcast_to(sink.reshape(n_pair, 2, 1, 1), (n_pair, 2, 1, LANES))
    else:
        sink_arr = jnp.zeros((n_pair, 2, 1, LANES), F32)
    kv_idx = (lambda hp: 0) if gqa else (lambda hp: hp)
    return pl.pallas_call(
        functools.partial(_dense_attn_kernel, gqa=gqa, use_sink=use_sink),
        grid=(B, n_pair),
        in_specs=[pl.BlockSpec((1, T, LANES), lambda b, hp: (b, 0, q_blk + hp)),
                  pl.BlockSpec((1, T, LANES), lambda b, hp: (b, 0, k_blk + kv_idx(hp))),
                  pl.BlockSpec((1, T, LANES), lambda b, hp: (b, 0, v_blk + kv_idx(hp))),
                  pl.BlockSpec((1, 2, 1, LANES), lambda b, hp: (hp, 0, 0, 0))],
        out_specs=pl.BlockSpec((1, T, LANES), lambda b, hp: (b, 0, hp)),
        out_shape=jax.ShapeDtypeStruct((B, T, n_pair * LANES), F32),
        compiler_params=_params("parallel", "parallel"),
    )(p, p, p, sink_arr)


NBR_SPAN = WIN_R * GRID_W
ATTN_BLOCKS_PER_STEP = 16


def _nbr_attn_kernel(q_ref, k_ref, v_ref, ck_ref, cv_ref, bias_ref, o_ref, kb_ref, vb_ref):
    kb_ref[...] = k_ref[0].astype(BF16)
    vb_ref[...] = v_ref[0].astype(BF16)
    ckb = ck_ref[0].astype(BF16)
    cvb = cv_ref[0].astype(BF16)
    grp = _lane_group()
    n_rows = q_ref.shape[1] // GRID_W

    def head(i, h):
        r0 = jnp.clip(i - WIN_R // 2, 0, n_rows - WIN_R)
        q = q_ref[0, pl.ds(pl.multiple_of(i * GRID_W, GRID_W), GRID_W), :] * ATTN_SCALE
        start = pl.multiple_of(r0 * GRID_W, GRID_W)
        kw = kb_ref[pl.ds(start, NBR_SPAN), :]
        vw = vb_ref[pl.ds(start, NBR_SPAN), :]
        qh = jnp.where(grp == h, q, 0.0).astype(BF16)
        return _attend([lambda: _dot_nt(qh, kw) + bias_ref[i - r0, h], lambda: _dot_nt(qh, ckb)], [vw, cvb], None)

    def rows(step, carry):
        ids = [step * ATTN_BLOCKS_PER_STEP + j for j in range(ATTN_BLOCKS_PER_STEP)]
        outs = _lockstep([head(i, h) for i in ids for h in range(2)])
        for j, i in enumerate(ids):
            o_ref[0, pl.ds(pl.multiple_of(i * GRID_W, GRID_W), GRID_W), :] = _merge_heads(outs[2 * j:2 * j + 2], grp)
        return carry

    lax.fori_loop(0, n_rows // ATTN_BLOCKS_PER_STEP, rows, 0)


def _nbr_bias_table(rpb):
    var = np.arange(WIN_R)[:, None, None]
    r = np.arange(WIN_R)[None, :, None]
    row_sel = (r - var + (WIN_R - 1) == np.arange(2 * WIN_R - 1)[None, None, :]).astype(np.float32)
    w = np.arange(GRID_W)[:, None, None]
    cc = np.arange(GRID_W)[None, :, None]
    c0 = np.clip(w - WIN_C // 2, 0, GRID_W - WIN_C)
    valid = (cc >= c0) & (cc < c0 + WIN_C)
    col_sel = (valid & (cc - w + (WIN_C - 1) == np.arange(2 * WIN_C - 1)[None, None, :])).astype(np.float32)
    t = jnp.einsum('vra,hab->vhrb', row_sel, rpb, precision=lax.Precision.HIGHEST)
    tbl = jnp.einsum('vhrb,wcb->vhwrc', t, col_sel, precision=lax.Precision.HIGHEST)
    tbl = jnp.where(valid[None, None, :, None, :, 0], tbl, NEG_BIG)
    return tbl.reshape(WIN_R, rpb.shape[0], GRID_W, NBR_SPAN)


def _nbr_attn_call(p, rpb, ck_cache, cv_cache):
    B, T, _ = p.shape
    n_pair = H_A // 2
    L = ck_cache.shape[2]
    ckp = ck_cache.transpose(0, 2, 1, 3).reshape(B, L, H_A * HEAD_DIM)
    cvp = cv_cache.transpose(0, 2, 1, 3).reshape(B, L, H_A * HEAD_DIM)
    bias = _nbr_bias_table(rpb)
    col = lambda off: pl.BlockSpec((1, T, LANES), lambda b, hp: (b, 0, off + hp))
    ctx = lambda: pl.BlockSpec((1, L, LANES), lambda b, hp: (b, 0, hp))
    return pl.pallas_call(
        _nbr_attn_kernel,
        grid=(B, n_pair),
        in_specs=[col(0), col(n_pair), col(2 * n_pair), ctx(), ctx(),
                  pl.BlockSpec((WIN_R, 2, GRID_W, NBR_SPAN), lambda b, hp: (0, hp, 0, 0))],
        out_specs=pl.BlockSpec((1, T, LANES), lambda b, hp: (b, 0, hp)),
        out_shape=jax.ShapeDtypeStruct((B, T, n_pair * LANES), F32),
        scratch_shapes=[pltpu.VMEM((T, LANES), BF16), pltpu.VMEM((T, LANES), BF16)],
        compiler_params=_params("parallel", "parallel"),
    )(p, p, p, ckp, cvp, bias)


WIN_SPAN = 3 * WIN_1D
WIN_Q_TILE = 64
WIN_TILES_PER_STEP = 16
ROPE_SWAP = HEAD_DIM // 4


def _rope(x, cos, sin_signed):
    lane = lax.broadcasted_iota(jnp.int32, (1, LANES), 1)
    first = lane % (2 * ROPE_SWAP) < ROPE_SWAP
    partner = jnp.where(first, pltpu.roll(x, LANES - ROPE_SWAP, axis=1), pltpu.roll(x, ROPE_SWAP, axis=1))
    return x * cos + partner * sin_signed


def _win_attn_kernel(q_ref, k_ref, v_ref, ck_ref, cv_ref, cos_ref, sin_ref, sink_ref, o_ref, kb_ref, vb_ref):
    grp = _lane_group()
    kvg = pl.program_id(1) // 2
    both = lambda a: jnp.where(grp == kvg, a, pltpu.roll(a, HEAD_DIM, axis=1))
    kb_ref[...] = _rope(both(k_ref[0]), cos_ref[...], sin_ref[...]).astype(BF16)
    vb_ref[...] = both(v_ref[0]).astype(BF16)
    ckb = both(ck_ref[0]).astype(BF16)
    cvb = both(cv_ref[0]).astype(BF16)
    T = q_ref.shape[1]
    rel = (lax.broadcasted_iota(jnp.int32, (WIN_Q_TILE, WIN_SPAN), 1)
           - lax.broadcasted_iota(jnp.int32, (WIN_Q_TILE, WIN_SPAN), 0))

    def heads(n):
        q0 = pl.multiple_of(n * WIN_Q_TILE, WIN_Q_TILE)
        start = pl.multiple_of(jnp.clip(q0 - WIN_1D, 0, T - WIN_SPAN), WIN_Q_TILE)
        q = _rope(q_ref[0, pl.ds(q0, WIN_Q_TILE), :], cos_ref[pl.ds(q0, WIN_Q_TILE), :],
                  sin_ref[pl.ds(q0, WIN_Q_TILE), :]) * ATTN_SCALE
        kw = kb_ref[pl.ds(start, WIN_SPAN), :]
        vw = vb_ref[pl.ds(start, WIN_SPAN), :]
        valid = jnp.abs(rel + (start - q0)) <= WIN_1D

        def head(h):
            qh = jnp.where(grp == h, q, 0.0).astype(BF16)
            return _attend([lambda: jnp.where(valid, _dot_nt(qh, kw), NEG_BIG), lambda: _dot_nt(qh, ckb)],
                           [vw, cvb], sink_ref[0, h][:, :1])

        return [head(0), head(1)]

    def tiles(step, carry):
        ids = [step * WIN_TILES_PER_STEP + j for j in range(WIN_TILES_PER_STEP)]
        outs = _lockstep([prog for n in ids for prog in heads(n)])
        for j, n in enumerate(ids):
            o_ref[0, pl.ds(pl.multiple_of(n * WIN_Q_TILE, WIN_Q_TILE), WIN_Q_TILE), :] = _merge_heads(
                outs[2 * j:2 * j + 2], grp)
        return carry

    lax.fori_loop(0, T // WIN_Q_TILE // WIN_TILES_PER_STEP, tiles, 0)


def _rope_tables(T):
    t = jnp.arange(T)
    quarter = HEAD_DIM // 4
    inv = 1.0 / (ROPE_BASE ** (jnp.arange(quarter, dtype=F32) / quarter))
    ang_r = (t // GRID_W).astype(F32)[:, None] * inv[None, :]
    ang_c = (t % GRID_W).astype(F32)[:, None] * inv[None, :]
    cos = jnp.concatenate([jnp.cos(ang_r), jnp.cos(ang_r), jnp.cos(ang_c), jnp.cos(ang_c)], axis=-1)
    sin = jnp.concatenate([-jnp.sin(ang_r), jnp.sin(ang_r), -jnp.sin(ang_c), jnp.sin(ang_c)], axis=-1)
    return jnp.tile(cos, (1, 2)), jnp.tile(sin, (1, 2))


def _win_attn_call(p, sink, ck_cache, cv_cache):
    B, T, _ = p.shape
    n_pair = H_C // 2
    L = ck_cache.shape[2]
    ckp = ck_cache.transpose(0, 2, 1, 3).reshape(B, L, KV_C * HEAD_DIM)
    cvp = cv_cache.transpose(0, 2, 1, 3).reshape(B, L, KV_C * HEAD_DIM)
    cos, sin = _rope_tables(T)
    sink_arr = jnp.broadcast_to(sink.reshape(n_pair, 2, 1, 1), (n_pair, 2, 1, LANES))
    col = lambda blk: pl.BlockSpec((1, T, LANES), lambda b, hp: (b, 0, blk))
    ctx = lambda: pl.BlockSpec((1, L, LANES), lambda b, hp: (b, 0, 0))
    tab = lambda: pl.BlockSpec((T, LANES), lambda b, hp: (0, 0))
    return pl.pallas_call(
        _win_attn_kernel,
        grid=(B, n_pair),
        in_specs=[pl.BlockSpec((1, T, LANES), lambda b, hp: (b, 0, hp)), col(n_pair), col(n_pair + 1),
                  ctx(), ctx(), tab(), tab(),
                  pl.BlockSpec((1, 2, 1, LANES), lambda b, hp: (hp, 0, 0, 0))],
        out_specs=pl.BlockSpec((1, T, LANES), lambda b, hp: (b, 0, hp)),
        out_shape=jax.ShapeDtypeStruct((B, T, n_pair * LANES), F32),
        scratch_shapes=[pltpu.VMEM((T, LANES), BF16), pltpu.VMEM((T, LANES), BF16)],
        compiler_params=_params("parallel", "parallel"),
    )(p, p, p, ckp, cvp, cos, sin, sink_arr)


def _dot_tn(a, b):
    return lax.dot_general(a, b, (((0,), (0,)), ((), ())), preferred_element_type=F32)


def _split_bf16(x):
    hi = x.astype(BF16)
    return hi, (x - hi.astype(F32)).astype(BF16)


GLA_CHUNKS_PER_STEP = 8
DELTA_CHUNKS_PER_STEP = 4


def _lockstep(programs):
    results = [None] * len(programs)
    live = list(range(len(programs)))
    while live:
        for idx in list(live):
            try:
                next(programs[idx])
            except StopIteration as stop:
                results[idx] = stop.value
                live.remove(idx)
    return results


def _stack_heads(x, grp):
    zero = jnp.zeros_like(x)
    return jnp.concatenate([jnp.where(grp == 0, x, zero), jnp.where(grp == 1, x, zero)], axis=0)


def _block_diag_mask():
    return (lax.broadcasted_iota(jnp.int32, (LANES, 1), 0) // HEAD_DIM) == _lane_group()


def _head_rmsnorm(o, grp):
    sq = o * o
    s0 = jnp.sum(jnp.where(grp == 0, sq, 0.0), axis=-1, keepdims=True)
    s1 = jnp.sum(jnp.where(grp == 1, sq, 0.0), axis=-1, keepdims=True)
    ms = jnp.where(grp == 0, s0, s1) * (1.0 / HEAD_DIM)
    return o * lax.rsqrt(ms + EPS)


def _gla_kernel(q_ref, k_ref, v_ref, r_ref, glr_ref, w2_ref, bg_ref, ng_ref, s0_ref, o_ref, sfin_ref,
                ob_ref, st_ref):
    T = q_ref.shape[1]
    n = T // CHUNK
    grp = _lane_group()
    bd = _block_diag_mask()
    ri = lax.broadcasted_iota(jnp.int32, (CHUNK, CHUNK), 0)
    ci = lax.broadcasted_iota(jnp.int32, (CHUNK, CHUNK), 1)
    ri_pair = lax.broadcasted_iota(jnp.int32, (CHUNK, LANES), 0)
    ci_pair = lax.broadcasted_iota(jnp.int32, (CHUNK, LANES), 1) % CHUNK
    st_ref[...] = s0_ref[0, :, 0]

    def chunk(c0, d):
        sl = pl.ds(c0, CHUNK)
        q = q_ref[0, sl, :] * (DK_B ** -0.5)
        k = k_ref[0, sl, :]
        vb = v_ref[0, sl, :].astype(BF16)
        z = jnp.dot(glr_ref[0, sl, :].astype(BF16), w2_ref[d], preferred_element_type=F32) + bg_ref[d]
        yield
        g = (jnp.minimum(z, 0.0) - jnp.log(1.0 + jnp.exp(-jnp.abs(z)))) * (1.0 / GLA_TAU)
        g_hi, g_lo = _split_bf16(g)
        mask = (ri <= ci) if d else (ri >= ci)
        mask_pair = (ri_pair <= ci_pair) if d else (ri_pair >= ci_pair)
        tri = jnp.where(mask, 1.0, 0.0).astype(BF16)
        G = jnp.dot(tri, g_hi, preferred_element_type=F32) + jnp.dot(tri, g_lo, preferred_element_type=F32)
        yield
        g_tot = G[0:1] if d else G[CHUNK - 1:CHUNK]
        q_t = q * jnp.exp(G)
        k_t = (k * jnp.exp(-G)).astype(BF16)
        k_e = (k * jnp.exp(g_tot - G)).astype(BF16)
        att = jnp.where(mask_pair, _dot_nt(q_t.astype(BF16), _stack_heads(k_t, grp)), 0.0)
        u = jnp.where(bd, _dot_tn(k_e, vb), 0.0)
        decay = jnp.exp(jnp.broadcast_to(g_tot, (LANES, LANES)).T)
        yield
        S = st_ref[d]
        st_ref[d] = decay * S + u
        return (jnp.dot(q_t.astype(BF16), S.astype(BF16), preferred_element_type=F32)
                + jnp.dot(att.astype(BF16), _stack_heads(vb, grp), preferred_element_type=F32))

    per_step = min(GLA_CHUNKS_PER_STEP, n)

    def body(i, carry):
        c_f = [pl.multiple_of((i * per_step + j) * CHUNK, CHUNK) for j in range(per_step)]
        c_b = [pl.multiple_of((n - 1 - i * per_step - j) * CHUNK, CHUNK) for j in range(per_step)]
        outs = _lockstep([chunk(c0, 0) for c0 in c_f] + [chunk(c0, 1) for c0 in c_b])
        for j in range(per_step):
            o_ref[0, pl.ds(c_f[j], CHUNK), :] = outs[j]
            ob_ref[pl.ds(c_b[j], CHUNK), :] = outs[per_step + j]
        return carry

    lax.fori_loop(0, n // per_step, body, 0)
    sfin_ref[0, :, 0] = st_ref[...]

    def gate(i, carry):
        sl = pl.ds(pl.multiple_of(i * ELEMENTWISE_ROWS, ELEMENTWISE_ROWS), ELEMENTWISE_ROWS)
        r = r_ref[0, sl, :]
        y = _head_rmsnorm(o_ref[0, sl, :] + ob_ref[sl, :], grp) * ng_ref[...]
        o_ref[0, sl, :] = y * (r * jax.nn.sigmoid(r))
        return carry

    lax.fori_loop(0, T // ELEMENTWISE_ROWS, gate, 0)


def _pair_block_diag(s):
    B, _, H, dk, dv = s.shape
    s = s.reshape(B, 2, H // 2, 2, dk, dv)
    z = jnp.zeros_like(s[:, :, :, 0])
    top = jnp.concatenate([s[:, :, :, 0], z], axis=-1)
    bot = jnp.concatenate([z, s[:, :, :, 1]], axis=-1)
    return jnp.concatenate([top, bot], axis=-2)


def _pair_block_diag_inv(sp):
    a = sp[..., :HEAD_DIM, :HEAD_DIM]
    b = sp[..., HEAD_DIM:, HEAD_DIM:]
    B, _, P = sp.shape[:3]
    return jnp.stack([a, b], axis=3).reshape(B, 2, 2 * P, HEAD_DIM, HEAD_DIM)


EV_BQ, EV_BK, EV_BV, EV_BR, EV_GLR = 12, 16, 20, 24, 28


def _gla_call(p, w_g2, b_g, norm_g, s0):
    B, T, _ = p.shape
    n_pair = H_B // 2
    w2 = jnp.zeros((2, LANES, H_B * DK_B), F32)
    w2 = w2.at[0, :GLA_RANK].set(w_g2[0]).at[1, GLA_RANK:2 * GLA_RANK].set(w_g2[1]).astype(BF16)
    col = lambda off: pl.BlockSpec((1, T, LANES), lambda b, hp: (b, 0, off + hp))
    st = lambda: pl.BlockSpec((1, 2, 1, LANES, LANES), lambda b, hp: (b, 0, hp, 0, 0))
    o, s_fin = pl.pallas_call(
        _gla_kernel,
        grid=(B, n_pair),
        in_specs=[col(EV_BQ), col(EV_BK), col(EV_BV), col(EV_BR),
                  pl.BlockSpec((1, T, LANES), lambda b, hp: (b, 0, EV_GLR)),
                  pl.BlockSpec((2, LANES, LANES), lambda b, hp: (0, 0, hp)),
                  pl.BlockSpec((2, 1, LANES), lambda b, hp: (0, 0, hp)),
                  pl.BlockSpec((1, LANES), lambda b, hp: (0, hp)),
                  st()],
        out_specs=[pl.BlockSpec((1, T, LANES), lambda b, hp: (b, 0, hp)), st()],
        out_shape=[jax.ShapeDtypeStruct((B, T, n_pair * LANES), F32),
                   jax.ShapeDtypeStruct((B, 2, n_pair, LANES, LANES), F32)],
        scratch_shapes=[pltpu.VMEM((T, LANES), F32), pltpu.VMEM((2, LANES, LANES), F32)],
        compiler_params=_params("parallel", "parallel"),
    )(p, p, p, p, p, w2, b_g.reshape(2, 1, H_B * DK_B), norm_g.reshape(1, H_B * DV_B), _pair_block_diag(s0))
    return o, _pair_block_diag_inv(s_fin)


def _softplus(x):
    return jnp.maximum(x, 0.0) + jnp.log(1.0 + jnp.exp(-jnp.abs(x)))


def _delta_kernel(q_ref, k_ref, v_ref, z_ref, dab_ref, cq_ref, ck_ref, cv_ref, nega_ref, dtb_ref, ng_ref, s0_ref,
                  o_ref, sfin_ref, qs_ref, ks_ref, vs_ref, ob_ref, st_ref, hw_ref, ha_ref, hd_ref):
    T = q_ref.shape[1]
    n = T // CHUNK
    hp = pl.program_id(1)
    grp = _lane_group()
    lane = lax.broadcasted_iota(jnp.int32, (1, LANES), 1)
    subl = lax.broadcasted_iota(jnp.int32, (LANES, 1), 0)
    bd = _block_diag_mask()
    ri = lax.broadcasted_iota(jnp.int32, (CHUNK, CHUNK), 0)
    ci = lax.broadcasted_iota(jnp.int32, (CHUNK, CHUNK), 1)
    eye = jnp.where(ri == ci, 1.0, 0.0)
    row = lax.broadcasted_iota(jnp.int32, (ELEMENTWISE_ROWS, 1), 0)
    st_ref[...] = s0_ref[0, :, 0]

    def prep(i, carry):
        c0 = pl.multiple_of(i * ELEMENTWISE_ROWS, ELEMENTWISE_ROWS)
        lo = pl.multiple_of(jnp.maximum(c0 - SUBLANES, 0), SUBLANES)
        hi = pl.multiple_of(jnp.minimum(c0 + ELEMENTWISE_ROWS, T - SUBLANES), SUBLANES)
        for src, cw_ref, dst, scale in ((q_ref, cq_ref, qs_ref, DK_D ** -0.5), (k_ref, ck_ref, ks_ref, 1.0),
                                        (v_ref, cv_ref, vs_ref, None)):
            x = src[0, pl.ds(c0, ELEMENTWISE_ROWS), :]
            before = jnp.where(i > 0, src[0, pl.ds(lo, SUBLANES), :][SUBLANES - 1:], 0.0)
            after = jnp.where(i < T // ELEMENTWISE_ROWS - 1, src[0, pl.ds(hi, SUBLANES), :][:1], 0.0)
            x_prev = jnp.where(row == 0, before, pltpu.roll(x, 1, axis=0))
            x_next = jnp.where(row == ELEMENTWISE_ROWS - 1, after, pltpu.roll(x, ELEMENTWISE_ROWS - 1, axis=0))
            cw = cw_ref[...]
            y = x_prev * cw[0:1] + x * cw[1:2] + x_next * cw[2:3]
            y = y * jax.nn.sigmoid(y)
            if scale is not None:
                sq = y * y
                s0 = jnp.sum(jnp.where(grp == 0, sq, 0.0), axis=-1, keepdims=True)
                s1 = jnp.sum(jnp.where(grp == 1, sq, 0.0), axis=-1, keepdims=True)
                y = y * lax.rsqrt(jnp.where(grp == 0, s0, s1) + EPS) * scale
            dst[pl.ds(c0, ELEMENTWISE_ROWS), :] = y
        return carry

    lax.fori_loop(0, T // ELEMENTWISE_ROWS, prep, 0)

    def chunk(c0, d):
        sl = pl.ds(c0, CHUNK)
        q = qs_ref[sl, :]
        k = ks_ref[sl, :]
        v = vs_ref[sl, :]
        ab = dab_ref[0, sl, :]
        gb = jnp.where(lane < 2 * H_D, nega_ref[...] * _softplus(ab + dtb_ref[...]), jax.nn.sigmoid(ab))
        gb_t = gb.T
        incl = (ri <= ci) if d else (ri >= ci)
        incl_t = (ri >= ci) if d else (ri <= ci)
        strict = (ri < ci) if d else (ri > ci)

        def head(h):
            jg = d * H_D + 2 * hp + h
            g_c = jnp.sum(jnp.where(lane == jg, gb, 0.0), axis=1, keepdims=True)
            beta_c = jnp.sum(jnp.where(lane == 2 * H_D + jg, gb, 0.0), axis=1, keepdims=True)
            g_r = jnp.sum(jnp.where(subl == jg, gb_t, 0.0), axis=0, keepdims=True)
            G_c = jnp.sum(jnp.where(incl, g_r, 0.0), axis=1, keepdims=True)
            G_r = jnp.sum(jnp.where(incl_t, g_c, 0.0), axis=0, keepdims=True)
            tot = jnp.sum(g_r, axis=1, keepdims=True)
            gam = jnp.where(incl, jnp.exp(jnp.where(incl, G_c - G_r, 0.0)), 0.0)
            kh = jnp.where(grp == h, k, 0.0)
            qh = jnp.where(grp == h, q, 0.0)
            vh = jnp.where(grp == h, v, 0.0)
            kb = kh * beta_c
            khb = kh.astype(BF16)
            a = jnp.where(strict, _dot_nt(kb.astype(BF16), khb) * gam, 0.0)
            a_qk = (_dot_nt(qh.astype(BF16), khb) * gam).astype(BF16)
            yield
            wide = jnp.concatenate([-a, eye], axis=1)
            for _ in range(6):
                w_hi, w_lo = _split_bf16(wide)
                prod = (jnp.dot(w_hi[:, :CHUNK], w_hi, preferred_element_type=F32)
                        + jnp.dot(w_lo[:, :CHUNK], w_hi, preferred_element_type=F32)
                        + jnp.dot(w_hi[:, :CHUNK], w_lo, preferred_element_type=F32))
                wide = prod + jnp.where(lane >= CHUNK, wide, 0.0)
                yield
            t_inv = wide[:, CHUNK:]
            e_g = jnp.exp(G_c)
            rhs = vh * beta_c + pltpu.roll(kb * e_g, HEAD_DIM, axis=1)
            sol = jnp.dot(t_inv.astype(BF16), rhs.astype(BF16), preferred_element_type=F32)
            return sol, a_qk, qh * e_g, kh * jnp.exp(tot - G_c), jnp.exp(tot)

        return head

    per_step = min(DELTA_CHUNKS_PER_STEP, n)
    n_steps = n // per_step
    n_sys = 4 * per_step

    def starts(i):
        c_f = [pl.multiple_of((i * per_step + j) * CHUNK, CHUNK) for j in range(per_step)]
        c_b = [pl.multiple_of((n - 1 - i * per_step - j) * CHUNK, CHUNK) for j in range(per_step)]
        return c_f, c_b

    def head_programs(i):
        programs = []
        for d, c0s in zip((0, 1), starts(i)):
            for c0 in c0s:
                head = chunk(c0, d)
                programs += [head(0), head(1)]
        return programs

    def store_heads(slot, results):
        for s, (r, a_qk, q_g, k_e, d_last) in enumerate(results):
            hw_ref[slot, s, 0] = r
            hw_ref[slot, s, 1] = q_g
            hw_ref[slot, s, 2] = k_e
            ha_ref[slot, s] = a_qk
            hd_ref[slot, s] = jnp.broadcast_to(d_last, (SUBLANES, LANES))

    def load_head(slot, s):
        return hw_ref[slot, s, 0], ha_ref[slot, s], hw_ref[slot, s, 1], hw_ref[slot, s, 2], hd_ref[slot, s][0:1, 0:1]

    def scan_step(d, slot, s):
        (r0, aqk0, qg0, ke0, d0), (r1, aqk1, qg1, ke1, d1) = load_head(slot, s), load_head(slot, s + 1)
        S = st_ref[d]
        Sb = S.astype(BF16)
        w_val = jnp.where(grp == 0, r0, r1)
        k_cum = pltpu.roll(jnp.where(grp == 0, r1, r0), HEAD_DIM, axis=1)
        v_new = w_val - jnp.dot(k_cum.astype(BF16), Sb, preferred_element_type=F32)
        o = jnp.dot((qg0 + qg1).astype(BF16), Sb, preferred_element_type=F32)
        yield
        vnb = v_new.astype(BF16)
        o = o + jnp.where(grp == 0, jnp.dot(aqk0, vnb, preferred_element_type=F32), 0.0)
        o = o + jnp.where(grp == 1, jnp.dot(aqk1, vnb, preferred_element_type=F32), 0.0)
        d_rows = jnp.where(subl // HEAD_DIM == 0, d0, d1)
        st_ref[d] = S * d_rows + jnp.where(bd, _dot_tn((ke0 + ke1).astype(BF16), vnb), 0.0)
        return o

    def scan_dir(d, i, slot):
        c0s = starts(i)[d]
        for j in range(per_step):
            o = yield from scan_step(d, slot, d * 2 * per_step + 2 * j)
            if d == 0:
                o_ref[0, pl.ds(c0s[j], CHUNK), :] = o
            else:
                ob_ref[pl.ds(c0s[j], CHUNK), :] = o

    store_heads(0, _lockstep(head_programs(0)))

    def body(i, carry):
        slot = i % 2
        results = _lockstep(head_programs(i) + [scan_dir(0, i - 1, 1 - slot), scan_dir(1, i - 1, 1 - slot)])
        store_heads(slot, results[:n_sys])
        return carry

    lax.fori_loop(1, n_steps, body, 0)
    last = n_steps - 1
    _lockstep([scan_dir(0, last, last % 2), scan_dir(1, last, last % 2)])
    sfin_ref[0, :, 0] = st_ref[...]

    def gate(i, carry):
        sl = pl.ds(pl.multiple_of(i * ELEMENTWISE_ROWS, ELEMENTWISE_ROWS), ELEMENTWISE_ROWS)
        z = z_ref[0, sl, :]
        y = _head_rmsnorm(o_ref[0, sl, :] + ob_ref[sl, :], grp) * ng_ref[...]
        o_ref[0, sl, :] = y * (z * jax.nn.sigmoid(z))
        return carry

    lax.fori_loop(0, T // ELEMENTWISE_ROWS, gate, 0)


OD_DQ, OD_DK, OD_DV, OD_DZ, OD_DAB = 6, 10, 14, 18, 22


def _delta_call(p, w_conv, a_log, dt_bias, norm_g, s0):
    B, T, _ = p.shape
    n_pair = H_D // 2
    n_sys = 4 * min(DELTA_CHUNKS_PER_STEP, T // CHUNK)
    nega = jnp.zeros((1, LANES), F32).at[0, :2 * H_D].set(-jnp.exp(a_log.reshape(-1)))
    dtb = jnp.zeros((1, LANES), F32).at[0, :2 * H_D].set(dt_bias.reshape(-1))
    ng = jnp.tile(norm_g, 2).reshape(1, LANES)
    col = lambda off: pl.BlockSpec((1, T, LANES), lambda b, hp: (b, 0, off + hp))
    cw = lambda off: pl.BlockSpec((SHORT_CONV, LANES), lambda b, hp: (0, off + hp))
    vec = lambda: pl.BlockSpec((1, LANES), lambda b, hp: (0, 0))
    st = lambda: pl.BlockSpec((1, 2, 1, LANES, LANES), lambda b, hp: (b, 0, hp, 0, 0))
    o, s_fin = pl.pallas_call(
        _delta_kernel,
        grid=(B, n_pair),
        in_specs=[col(OD_DQ), col(OD_DK), col(OD_DV), col(OD_DZ),
                  pl.BlockSpec((1, T, LANES), lambda b, hp: (b, 0, OD_DAB)),
                  cw(0), cw(n_pair), cw(2 * n_pair), vec(), vec(), vec(), st()],
        out_specs=[pl.BlockSpec((1, T, LANES), lambda b, hp: (b, 0, hp)), st()],
        out_shape=[jax.ShapeDtypeStruct((B, T, n_pair * LANES), F32),
                   jax.ShapeDtypeStruct((B, 2, n_pair, LANES, LANES), F32)],
        scratch_shapes=[pltpu.VMEM((T, LANES), F32), pltpu.VMEM((T, LANES), F32), pltpu.VMEM((T, LANES), F32),
                        pltpu.VMEM((T, LANES), F32), pltpu.VMEM((2, LANES, LANES), F32),
                        pltpu.VMEM((2, n_sys, 3, CHUNK, LANES), F32),
                        pltpu.VMEM((2, n_sys, CHUNK, CHUNK), BF16),
                        pltpu.VMEM((2, n_sys, SUBLANES, LANES), F32)],
        compiler_params=_params("parallel", "parallel"),
    )(p, p, p, p, p, w_conv, w_conv, w_conv, nega, dtb, ng, _pair_block_diag(s0))
    return o, _pair_block_diag_inv(s_fin)


def _split_cols(p, sizes):
    cuts = [int(s) for s in np.cumsum(sizes)[:-1]]
    return jnp.split(p, cuts, axis=-1)


def _even_mix(p, latent, rpb, w_g2, b_g, norm_g, ak_cache, av_cache, sb):
    B, T, _ = p.shape
    ak = p[..., H_A * HEAD_DIM:2 * H_A * HEAD_DIM]
    av = p[..., 2 * H_A * HEAD_DIM:3 * H_A * HEAD_DIM]
    n_pair = H_A // 2
    if latent:
        o_a = _nbr_attn_call(p, rpb, ak_cache, av_cache)
        k_h = v_h = None
    else:
        k_h = ak.reshape(B, T, H_A, HEAD_DIM).transpose(0, 2, 1, 3)
        v_h = av.reshape(B, T, H_A, HEAD_DIM).transpose(0, 2, 1, 3)
        o_a = _dense_attn_call(p, 0, n_pair, 2 * n_pair, None, False)
    o_b, s_b = _gla_call(p, w_g2, b_g, norm_g, sb)
    return o_a, o_b, k_h, v_h, s_b


def _odd_mix(p, latent, sink, w_conv, a_log, dt_bias, norm_g, ck_cache, cv_cache, sd):
    B, T, _ = p.shape
    ck = p[..., H_C * HEAD_DIM:(H_C + KV_C) * HEAD_DIM]
    cv = p[..., (H_C + KV_C) * HEAD_DIM:(H_C + 2 * KV_C) * HEAD_DIM]
    n_pair = H_C // 2
    if latent:
        o_c = _win_attn_call(p, sink, ck_cache, cv_cache)
        k_h = v_h = None
    else:
        k_h = ck.reshape(B, T, KV_C, HEAD_DIM).transpose(0, 2, 1, 3)
        v_h = cv.reshape(B, T, KV_C, HEAD_DIM).transpose(0, 2, 1, 3)
        o_c = _dense_attn_call(p, 0, n_pair, n_pair + 1, sink, True)
    o_d, s_d = _delta_call(p, w_conv, a_log, dt_bias, norm_g, sd)
    return o_c, o_d, k_h, v_h, s_d


def _reorder_cols(w, sizes, order):
    parts = _split_cols(w, sizes)
    out = jnp.concatenate([parts[i] for i in order], axis=-1)
    pad = -out.shape[-1] % LANES
    return jnp.pad(out, ((0, 0), (0, pad)))


def _reorder_even_w_in(w):
    return _reorder_cols(w, EV_SIZES, (0, 1, 2, 3, 4, 5, 7, 6))


def _reorder_odd_w_in(w):
    return _reorder_cols(w, OD_SIZES, (0, 1, 2, 3, 4, 5, 8, 6, 7))


TM_LATENT = 512
TM_CONTEXT = 256
TM_FFN_LATENT = 512


def kernel(x_prompt, x_sample, cache_a_k, cache_a_v, state_b, cache_c_k, cache_c_v, state_d, c, c_ctx, ada_w, ada_b, norm1_g, norm2_g, ffn_up, ffn_conv, ffn_down, ev_w_in, ev_w_out, a_rpb, b_w_g2, b_b_g, b_norm_g, od_w_in, od_w_out, c_sink, d_conv, d_a_log, d_dt_bias, d_norm_g, final_g):
    Bp, Tp, D = x_prompt.shape
    Bs, Ts, _ = x_sample.shape
    cond = jnp.concatenate([c, c_ctx[None, :], jnp.zeros((2 * SUBLANES - Bs - 1, D), F32)], axis=0)
    mods = _adaln(cond, ada_w, ada_b)
    xp, xs = x_prompt, x_sample
    streams = ((False, Bp, TM_CONTEXT), (True, Bs, TM_LATENT))
    ak_l, av_l, sb_l, ck_l, cv_l, sd_l = [], [], [], [], [], []
    for l in range(DEPTH):
        j = l // 2
        mod_s = [mods[l, :Bs, i * D:(i + 1) * D] for i in range(6)]
        mod_p = [jnp.broadcast_to(mods[l, Bs:Bs + 1, i * D:(i + 1) * D], (Bp, D)) for i in range(6)]
        w_in = (_reorder_even_w_in(ev_w_in[j]) if l % 2 == 0 else _reorder_odd_w_in(od_w_in[j])).astype(BF16)
        w_out = (ev_w_out[j] if l % 2 == 0 else od_w_out[j]).astype(BF16)
        w_up = ffn_up[l].astype(BF16)
        w_down = ffn_down[l].astype(BF16)
        new_x = []
        for (latent, B, tm), x, mod in zip(streams, (xp, xs), (mod_p, mod_s)):
            sh1, sc1, g1, sh2, sc2, g2 = mod
            p = _nm_matmul(x, norm1_g[l], sh1, sc1, w_in, tm)
            if l % 2 == 0:
                s0 = state_b[:, j] if latent else jnp.zeros((B, 2, H_B, DK_B, DV_B), F32)
                o1, o2, k_h, v_h, s_new = _even_mix(p, latent, a_rpb[j], b_w_g2[j], b_b_g[j], b_norm_g[j],
                                                    cache_a_k[:, j], cache_a_v[:, j], s0)
                if not latent:
                    ak_l.append(k_h)
                    av_l.append(v_h)
                    sb_l.append(s_new)
            else:
                s0 = state_d[:, j] if latent else jnp.zeros((B, 2, H_D, DK_D, DV_D), F32)
                o1, o2, k_h, v_h, s_new = _odd_mix(p, latent, c_sink[j], d_conv[j], d_a_log[j], d_dt_bias[j],
                                                   d_norm_g[j], cache_c_k[:, j], cache_c_v[:, j], s0)
                if not latent:
                    ck_l.append(k_h)
                    cv_l.append(v_h)
                    sd_l.append(s_new)
            x = _mix_ffn_res(x, o1, o2, w_out, g1, norm2_g[l], sh2, sc2, g2, w_up, ffn_conv[l], w_down,
                             TM_FFN_LATENT if latent else tm, final_g if l == DEPTH - 1 else None)
            new_x.append(x)
        xp, xs = new_x
    return (xp, xs, jnp.stack(ak_l, axis=1), jnp.stack(av_l, axis=1), jnp.stack(sb_l, axis=1),
            jnp.stack(ck_l, axis=1), jnp.stack(cv_l, axis=1), jnp.stack(sd_l, axis=1))
```

```python
import functools
import math

import jax
import jax.numpy as jnp
import numpy as np
from jax import lax
from jax.experimental import pallas as pl
from jax.experimental.pallas import tpu as pltpu

D_MODEL = 1024
DEPTH = 4
GRID_W = 64
HEAD_DIM = 64
H_A = D_MODEL // (2 * HEAD_DIM)
WIN_R = 8
WIN_C = 16
H_B = D_MODEL // (2 * HEAD_DIM)
DK_B = HEAD_DIM
DV_B = HEAD_DIM
GLA_RANK = 16
GLA_TAU = 16.0
H_C = D_MODEL // (2 * HEAD_DIM)
KV_C = H_C // 4
WIN_1D = 128
ROPE_BASE = 10000.0
H_D = D_MODEL // (2 * HEAD_DIM)
DK_D = HEAD_DIM
DV_D = HEAD_DIM
SHORT_CONV = 3
CHUNK = 64
EPS = 1e-6
EV_SIZES = (H_A * HEAD_DIM, H_A * HEAD_DIM, H_A * HEAD_DIM, H_B * DK_B, H_B * DK_B, H_B * DV_B, 2 * GLA_RANK, H_B * DV_B)
OD_SIZES = (H_C * HEAD_DIM, KV_C * HEAD_DIM, KV_C * HEAD_DIM, H_D * DK_D, H_D * DK_D, H_D * DV_D, 2 * H_D, 2 * H_D, H_D * DV_D)
F32 = jnp.float32
BF16 = jnp.bfloat16

V7X_VMEM_LIMIT_BYTES = 52 * 1024 * 1024
SUBLANES = 8
LANES = 128
COL_CHUNK = 512
ELEMENTWISE_ROWS = 256
FFN_SUB_COLS = 1536


def _params(*sem):
    return pltpu.CompilerParams(dimension_semantics=sem, vmem_limit_bytes=V7X_VMEM_LIMIT_BYTES)


def _norm_mod(x, g, shift, scale):
    y = x * lax.rsqrt(jnp.mean(x * x, axis=-1, keepdims=True) + EPS)
    return (y * g) * (1.0 + scale) + shift


def _adaln_kernel(c_ref, w_ref, b_ref, o_ref):
    cond = c_ref[...]
    a = cond * jax.nn.sigmoid(cond)
    a_hi = a.astype(BF16)
    a_lo = (a - a_hi.astype(F32)).astype(BF16)
    w = w_ref[0]
    w_hi = w.astype(BF16)
    w_lo = (w - w_hi.astype(F32)).astype(BF16)
    acc = jnp.dot(a_hi, w_hi, preferred_element_type=F32)
    acc += jnp.dot(a_lo, w_hi, preferred_element_type=F32)
    acc += jnp.dot(a_hi, w_lo, preferred_element_type=F32)
    o_ref[0] = acc + b_ref[0]


def _adaln(cond, ada_w, ada_b):
    R = cond.shape[0]
    L, D, N = ada_w.shape
    tn = 1024
    return pl.pallas_call(
        _adaln_kernel,
        grid=(L, N // tn),
        in_specs=[pl.BlockSpec((R, D), lambda l, j: (0, 0)),
                  pl.BlockSpec((1, D, tn), lambda l, j: (l, 0, j)),
                  pl.BlockSpec((1, 1, tn), lambda l, j: (l, 0, j))],
        out_specs=pl.BlockSpec((1, R, tn), lambda l, j: (l, 0, j)),
        out_shape=jax.ShapeDtypeStruct((L, R, N), F32),
        compiler_params=_params("parallel", "parallel"),
    )(cond, ada_w, ada_b.reshape(L, 1, N))


def _nm_matmul_kernel(x_ref, g_ref, sh_ref, sc_ref, w_ref, o_ref):
    h = _norm_mod(x_ref[0], g_ref[...], sh_ref[0], sc_ref[0]).astype(BF16)
    n = o_ref.shape[-1]
    for c0 in range(0, n, COL_CHUNK):
        c1 = min(c0 + COL_CHUNK, n)
        o_ref[0, :, c0:c1] = jnp.dot(h, w_ref[:, c0:c1], preferred_element_type=F32)


def _nm_matmul(x, g, shift, scale, w, tm):
    B, T, D = x.shape
    N = w.shape[1]
    return pl.pallas_call(
        _nm_matmul_kernel,
        grid=(B, T // tm),
        in_specs=[pl.BlockSpec((1, tm, D), lambda b, i: (b, i, 0)),
                  pl.BlockSpec((1, D), lambda b, i: (0, 0)),
                  pl.BlockSpec((1, 1, D), lambda b, i: (b, 0, 0)),
                  pl.BlockSpec((1, 1, D), lambda b, i: (b, 0, 0)),
                  pl.BlockSpec((D, N), lambda b, i: (0, 0))],
        out_specs=pl.BlockSpec((1, tm, N), lambda b, i: (b, i, 0)),
        out_shape=jax.ShapeDtypeStruct((B, T, N), F32),
        compiler_params=_params("parallel", "parallel"),
    )(x, g.reshape(1, D), shift.reshape(B, 1, D), scale.reshape(B, 1, D), w)


def _mix_ffn_kernel(x_ref, xp_ref, xn_ref, a1_ref, a1p_ref, a1n_ref, a2_ref, a2p_ref, a2n_ref, wo1_ref, wo2_ref,
                    gate1_ref, g_ref, sh_ref, sc_ref, gate_ref, wu_ref, cw_ref, wd_ref, fg_ref,
                    o_ref, h_ref, *, final_norm):
    i = pl.program_id(1)
    tm = x_ref.shape[1]
    F = wd_ref.shape[0]
    ext = lambda m, p, n: jnp.concatenate([m[0], p[0], n[0]], axis=0).astype(BF16)
    mix = (jnp.dot(ext(a1_ref, a1p_ref, a1n_ref), wo1_ref[...], preferred_element_type=F32)
           + jnp.dot(ext(a2_ref, a2p_ref, a2n_ref), wo2_ref[...], preferred_element_type=F32))
    gate1 = gate1_ref[0]
    x1 = x_ref[0] + gate1 * mix[0:tm]
    x1p = xp_ref[0] + gate1 * mix[tm:tm + SUBLANES]
    x1n = xn_ref[0] + gate1 * mix[tm + SUBLANES:]
    g, sh, sc = g_ref[...], sh_ref[0], sc_ref[0]
    h_ref[0:tm, :] = _norm_mod(x1, g, sh, sc).astype(BF16)
    hp = jnp.where(i > 0, _norm_mod(x1p, g, sh, sc), 0.0)
    hn = jnp.where(i < pl.num_programs(1) - 1, _norm_mod(x1n, g, sh, sc), 0.0)
    h_ref[tm:tm + 2 * SUBLANES, :] = jnp.concatenate([hp, hn], axis=0).astype(BF16)
    h = h_ref[...]
    row = lax.broadcasted_iota(jnp.int32, (tm, 1), 0)

    def conv_branch(c0, c1):
        u_all = jnp.dot(h, wu_ref[:, c0:c1], preferred_element_type=F32)
        u = u_all[0:tm]
        u_prev = jnp.where(row == 0, u_all[tm + SUBLANES - 1:tm + SUBLANES], pltpu.roll(u, 1, axis=0))
        u_next = jnp.where(row == tm - 1, u_all[tm + SUBLANES:tm + SUBLANES + 1], pltpu.roll(u, tm - 1, axis=0))
        cw = cw_ref[:, c0:c1]
        return u_prev * cw[0:1] + u * cw[1:2] + u_next * cw[2:3]

    acc = None
    for c0 in range(0, F, FFN_SUB_COLS):
        c1 = min(c0 + FFN_SUB_COLS, F)
        a = conv_branch(c0, c1)
        gt = conv_branch(F + c0, F + c1)
        act = a * (gt * jax.nn.sigmoid(gt))
        part = jnp.dot(act.astype(BF16), wd_ref[c0:c1, :], preferred_element_type=F32)
        acc = part if acc is None else acc + part

    y = x1 + gate_ref[0] * acc
    if final_norm:
        y = (y * lax.rsqrt(jnp.mean(y * y, axis=-1, keepdims=True) + EPS)) * fg_ref[...]
    o_ref[0] = y


def _mix_ffn_res(x, a1, a2, w_out, gate1, g, shift, scale, gate, w_up, w_conv, w_down, tm, final_g=None):
    B, T, D = x.shape
    K = a1.shape[-1]
    F = w_down.shape[0]
    nt = T // tm
    rows8 = tm // SUBLANES
    vec = lambda: pl.BlockSpec((1, 1, D), lambda b, i: (b, 0, 0))
    resident = lambda shape, blk=0: pl.BlockSpec(shape, lambda b, i: (blk, 0), pipeline_mode=pl.Buffered(1))
    tile = lambda w: pl.BlockSpec((1, tm, w), lambda b, i: (b, i, 0))
    before = lambda w: pl.BlockSpec((1, SUBLANES, w), lambda b, i: (b, jnp.maximum(i * rows8 - 1, 0), 0))
    after = lambda w: pl.BlockSpec((1, SUBLANES, w),
                                   lambda b, i: (b, jnp.minimum((i + 1) * rows8, T // SUBLANES - 1), 0))
    return pl.pallas_call(
        functools.partial(_mix_ffn_kernel, final_norm=final_g is not None),
        grid=(B, nt),
        in_specs=[tile(D), before(D), after(D), tile(K), before(K), after(K), tile(K), before(K), after(K),
                  resident((K, D), 0), resident((K, D), 1), vec(),
                  pl.BlockSpec((1, D), lambda b, i: (0, 0)),
                  vec(), vec(), vec(),
                  resident((D, 2 * F)), resident((3, 2 * F)), resident((F, D)),
                  pl.BlockSpec((1, D), lambda b, i: (0, 0))],
        out_specs=pl.BlockSpec((1, tm, D), lambda b, i: (b, i, 0)),
        out_shape=jax.ShapeDtypeStruct((B, T, D), F32),
        scratch_shapes=[pltpu.VMEM((tm + 2 * SUBLANES, D), BF16)],
        compiler_params=_params("parallel", "parallel"),
    )(x, x, x, a1, a1, a1, a2, a2, a2, w_out, w_out, gate1.reshape(B, 1, D),
      g.reshape(1, D), shift.reshape(B, 1, D), scale.reshape(B, 1, D), gate.reshape(B, 1, D),
      w_up, w_conv, w_down, (g if final_g is None else final_g).reshape(1, D))


NEG_BIG = -1e30
ATTN_SCALE = HEAD_DIM ** -0.5


def _dot_nt(a, b):
    return lax.dot_general(a, b, (((1,), (1,)), ((), ())), preferred_element_type=F32)


def _lane_group():
    return lax.broadcasted_iota(jnp.int32, (1, LANES), 1) // HEAD_DIM


def _attend(score_fns, values, sink):
    scores = [fn() for fn in score_fns]
    yield
    s = scores[0] if len(scores) == 1 else jnp.concatenate(scores, axis=1)
    m = jnp.max(s, axis=-1, keepdims=True)
    if sink is not None:
        m = jnp.maximum(m, sink)
    e = jnp.exp(s - m)
    den = jnp.sum(e, axis=-1, keepdims=True)
    if sink is not None:
        den = den + jnp.exp(sink - m)
    eb = e.astype(BF16)
    out = None
    c0 = 0
    for v in values:
        o = jnp.dot(eb[:, c0:c0 + v.shape[0]], v, preferred_element_type=F32)
        out = o if out is None else out + o
        c0 += v.shape[0]
    yield
    return out / den


def _merge_heads(outs, grp):
    return jnp.where(grp == 0, outs[0], outs[1])


def _dense_attn_kernel(q_ref, k_ref, v_ref, sink_ref, o_ref, *, gqa, use_sink):
    grp = _lane_group()
    q = q_ref[0] * ATTN_SCALE
    k = k_ref[0]
    v = v_ref[0]
    if gqa:
        kvg = pl.program_id(1) // 2
        k = jnp.where(grp == kvg, k, pltpu.roll(k, HEAD_DIM, axis=1))
        v = jnp.where(grp == kvg, v, pltpu.roll(v, HEAD_DIM, axis=1))
    kb = k.astype(BF16)
    vb = v.astype(BF16)

    def head(h):
        qh = jnp.where(grp == h, q, 0.0).astype(BF16)
        sink = sink_ref[0, h][:, :1] if use_sink else None
        return _attend([lambda: _dot_nt(qh, kb)], [vb], sink)

    o_ref[0] = _merge_heads(_lockstep([head(0), head(1)]), grp)


def _dense_attn_call(p, q_blk, k_blk, v_blk, sink, gqa):
    B, T, _ = p.shape
    n_pair = H_A // 2
    use_sink = sink is not None
    if use_sink:
        sink_arr = jnp.broadcast_to(sink.reshape(n_pair, 2, 1, 1), (n_pair, 2, 1, LANES))
    else:
        sink_arr = jnp.zeros((n_pair, 2, 1, LANES), F32)
    kv_idx = (lambda hp: 0) if gqa else (lambda hp: hp)
    return pl.pallas_call(
        functools.partial(_dense_attn_kernel, gqa=gqa, use_sink=use_sink),
        grid=(B, n_pair),
        in_specs=[pl.BlockSpec((1, T, LANES), lambda b, hp: (b, 0, q_blk + hp)),
                  pl.BlockSpec((1, T, LANES), lambda b, hp: (b, 0, k_blk + kv_idx(hp))),
                  pl.BlockSpec((1, T, LANES), lambda b, hp: (b, 0, v_blk + kv_idx(hp))),
                  pl.BlockSpec((1, 2, 1, LANES), lambda b, hp: (hp, 0, 0, 0))],
        out_specs=pl.BlockSpec((1, T, LANES), lambda b, hp: (b, 0, hp)),
        out_shape=jax.ShapeDtypeStruct((B, T, n_pair * LANES), F32),
        compiler_params=_params("parallel", "parallel"),
    )(p, p, p, sink_arr)


NBR_SPAN = WIN_R * GRID_W
ATTN_BLOCKS_PER_STEP = 16


def _nbr_attn_kernel(q_ref, k_ref, v_ref, ck_ref, cv_ref, bias_ref, o_ref, kb_ref, vb_ref):
    kb_ref[...] = k_ref[0].astype(BF16)
    vb_ref[...] = v_ref[0].astype(BF16)
    ckb = ck_ref[0].astype(BF16)
    cvb = cv_ref[0].astype(BF16)
    grp = _lane_group()
    n_rows = q_ref.shape[1] // GRID_W

    def head(i, h):
        r0 = jnp.clip(i - WIN_R // 2, 0, n_rows - WIN_R)
        q = q_ref[0, pl.ds(pl.multiple_of(i * GRID_W, GRID_W), GRID_W), :] * ATTN_SCALE
        start = pl.multiple_of(r0 * GRID_W, GRID_W)
        kw = kb_ref[pl.ds(start, NBR_SPAN), :]
        vw = vb_ref[pl.ds(start, NBR_SPAN), :]
        qh = jnp.where(grp == h, q, 0.0).astype(BF16)
        return _attend([lambda: _dot_nt(qh, kw) + bias_ref[i - r0, h], lambda: _dot_nt(qh, ckb)], [vw, cvb], None)

    def rows(step, carry):
        ids = [step * ATTN_BLOCKS_PER_STEP + j for j in range(ATTN_BLOCKS_PER_STEP)]
        outs = _lockstep([head(i, h) for i in ids for h in range(2)])
        for j, i in enumerate(ids):
            o_ref[0, pl.ds(pl.multiple_of(i * GRID_W, GRID_W), GRID_W), :] = _merge_heads(outs[2 * j:2 * j + 2], grp)
        return carry

    lax.fori_loop(0, n_rows // ATTN_BLOCKS_PER_STEP, rows, 0)


def _nbr_bias_table(rpb):
    var = np.arange(WIN_R)[:, None, None]
    r = np.arange(WIN_R)[None, :, None]
    row_sel = (r - var + (WIN_R - 1) == np.arange(2 * WIN_R - 1)[None, None, :]).astype(np.float32)
    w = np.arange(GRID_W)[:, None, None]
    cc = np.arange(GRID_W)[None, :, None]
    c0 = np.clip(w - WIN_C // 2, 0, GRID_W - WIN_C)
    valid = (cc >= c0) & (cc < c0 + WIN_C)
    col_sel = (valid & (cc - w + (WIN_C - 1) == np.arange(2 * WIN_C - 1)[None, None, :])).astype(np.float32)
    t = jnp.einsum('vra,hab->vhrb', row_sel, rpb, precision=lax.Precision.HIGHEST)
    tbl = jnp.einsum('vhrb,wcb->vhwrc', t, col_sel, precision=lax.Precision.HIGHEST)
    tbl = jnp.where(valid[None, None, :, None, :, 0], tbl, NEG_BIG)
    return tbl.reshape(WIN_R, rpb.shape[0], GRID_W, NBR_SPAN)


def _nbr_attn_call(p, rpb, ck_cache, cv_cache):
    B, T, _ = p.shape
    n_pair = H_A // 2
    L = ck_cache.shape[2]
    ckp = ck_cache.transpose(0, 2, 1, 3).reshape(B, L, H_A * HEAD_DIM)
    cvp = cv_cache.transpose(0, 2, 1, 3).reshape(B, L, H_A * HEAD_DIM)
    bias = _nbr_bias_table(rpb)
    col = lambda off: pl.BlockSpec((1, T, LANES), lambda b, hp: (b, 0, off + hp))
    ctx = lambda: pl.BlockSpec((1, L, LANES), lambda b, hp: (b, 0, hp))
    return pl.pallas_call(
        _nbr_attn_kernel,
        grid=(B, n_pair),
        in_specs=[col(0), col(n_pair), col(2 * n_pair), ctx(), ctx(),
                  pl.BlockSpec((WIN_R, 2, GRID_W, NBR_SPAN), lambda b, hp: (0, hp, 0, 0))],
        out_specs=pl.BlockSpec((1, T, LANES), lambda b, hp: (b, 0, hp)),
        out_shape=jax.ShapeDtypeStruct((B, T, n_pair * LANES), F32),
        scratch_shapes=[pltpu.VMEM((T, LANES), BF16), pltpu.VMEM((T, LANES), BF16)],
        compiler_params=_params("parallel", "parallel"),
    )(p, p, p, ckp, cvp, bias)


WIN_SPAN = 3 * WIN_1D
WIN_Q_TILE = 64
WIN_TILES_PER_STEP = 16
ROPE_SWAP = HEAD_DIM // 4


def _rope(x, cos, sin_signed):
    lane = lax.broadcasted_iota(jnp.int32, (1, LANES), 1)
    first = lane % (2 * ROPE_SWAP) < ROPE_SWAP
    partner = jnp.where(first, pltpu.roll(x, LANES - ROPE_SWAP, axis=1), pltpu.roll(x, ROPE_SWAP, axis=1))
    return x * cos + partner * sin_signed


def _win_attn_kernel(q_ref, k_ref, v_ref, ck_ref, cv_ref, cos_ref, sin_ref, sink_ref, o_ref, kb_ref, vb_ref):
    grp = _lane_group()
    kvg = pl.program_id(1) // 2
    both = lambda a: jnp.where(grp == kvg, a, pltpu.roll(a, HEAD_DIM, axis=1))

    @pl.when(pl.program_id(1) % 2 == 0)
    def _():
        kb_ref[...] = _rope(both(k_ref[0]), cos_ref[...], sin_ref[...]).astype(BF16)
        vb_ref[...] = both(v_ref[0]).astype(BF16)

    ckb = both(ck_ref[0]).astype(BF16)
    cvb = both(cv_ref[0]).astype(BF16)
    T = q_ref.shape[1]
    rel = (lax.broadcasted_iota(jnp.int32, (WIN_Q_TILE, WIN_SPAN), 1)
           - lax.broadcasted_iota(jnp.int32, (WIN_Q_TILE, WIN_SPAN), 0))

    def heads(n):
        q0 = pl.multiple_of(n * WIN_Q_TILE, WIN_Q_TILE)
        start = pl.multiple_of(jnp.clip(q0 - WIN_1D, 0, T - WIN_SPAN), WIN_Q_TILE)
        q = _rope(q_ref[0, pl.ds(q0, WIN_Q_TILE), :], cos_ref[pl.ds(q0, WIN_Q_TILE), :],
                  sin_ref[pl.ds(q0, WIN_Q_TILE), :]) * ATTN_SCALE
        kw = kb_ref[pl.ds(start, WIN_SPAN), :]
        vw = vb_ref[pl.ds(start, WIN_SPAN), :]
        valid = jnp.abs(rel + (start - q0)) <= WIN_1D

        def head(h):
            qh = jnp.where(grp == h, q, 0.0).astype(BF16)
            return _attend([lambda: jnp.where(valid, _dot_nt(qh, kw), NEG_BIG), lambda: _dot_nt(qh, ckb)],
                           [vw, cvb], sink_ref[0, h][:, :1])

        return [head(0), head(1)]

    def tiles(step, carry):
        ids = [step * WIN_TILES_PER_STEP + j for j in range(WIN_TILES_PER_STEP)]
        outs = _lockstep([prog for n in ids for prog in heads(n)])
        for j, n in enumerate(ids):
            o_ref[0, pl.ds(pl.multiple_of(n * WIN_Q_TILE, WIN_Q_TILE), WIN_Q_TILE), :] = _merge_heads(
                outs[2 * j:2 * j + 2], grp)
        return carry

    lax.fori_loop(0, T // WIN_Q_TILE // WIN_TILES_PER_STEP, tiles, 0)


def _rope_tables(T):
    t = jnp.arange(T)
    quarter = HEAD_DIM // 4
    inv = 1.0 / (ROPE_BASE ** (jnp.arange(quarter, dtype=F32) / quarter))
    ang_r = (t // GRID_W).astype(F32)[:, None] * inv[None, :]
    ang_c = (t % GRID_W).astype(F32)[:, None] * inv[None, :]
    cos = jnp.concatenate([jnp.cos(ang_r), jnp.cos(ang_r), jnp.cos(ang_c), jnp.cos(ang_c)], axis=-1)
    sin = jnp.concatenate([-jnp.sin(ang_r), jnp.sin(ang_r), -jnp.sin(ang_c), jnp.sin(ang_c)], axis=-1)
    return jnp.tile(cos, (1, 2)), jnp.tile(sin, (1, 2))


def _win_attn_call(p, sink, ck_cache, cv_cache):
    B, T, _ = p.shape
    n_pair = H_C // 2
    L = ck_cache.shape[2]
    ckp = ck_cache.transpose(0, 2, 1, 3).reshape(B, L, KV_C * HEAD_DIM)
    cvp = cv_cache.transpose(0, 2, 1, 3).reshape(B, L, KV_C * HEAD_DIM)
    cos, sin = _rope_tables(T)
    sink_arr = jnp.broadcast_to(sink.reshape(n_pair, 2, 1, 1), (n_pair, 2, 1, LANES))
    col = lambda blk: pl.BlockSpec((1, T, LANES), lambda b, hp: (b, 0, blk))
    ctx = lambda: pl.BlockSpec((1, L, LANES), lambda b, hp: (b, 0, 0))
    tab = lambda: pl.BlockSpec((T, LANES), lambda b, hp: (0, 0))
    return pl.pallas_call(
        _win_attn_kernel,
        grid=(B, n_pair),
        in_specs=[pl.BlockSpec((1, T, LANES), lambda b, hp: (b, 0, hp)), col(n_pair), col(n_pair + 1),
                  ctx(), ctx(), tab(), tab(),
                  pl.BlockSpec((1, 2, 1, LANES), lambda b, hp: (hp, 0, 0, 0))],
        out_specs=pl.BlockSpec((1, T, LANES), lambda b, hp: (b, 0, hp)),
        out_shape=jax.ShapeDtypeStruct((B, T, n_pair * LANES), F32),
        scratch_shapes=[pltpu.VMEM((T, LANES), BF16), pltpu.VMEM((T, LANES), BF16)],
        compiler_params=_params("parallel", "arbitrary"),
    )(p, p, p, ckp, cvp, cos, sin, sink_arr)


def _dot_tn(a, b):
    return lax.dot_general(a, b, (((0,), (0,)), ((), ())), preferred_element_type=F32)


def _split_bf16(x):
    hi = x.astype(BF16)
    return hi, (x - hi.astype(F32)).astype(BF16)


GLA_CHUNKS_PER_STEP = 8
DELTA_CHUNKS_PER_STEP = 4


def _lockstep(programs):
    results = [None] * len(programs)
    live = list(range(len(programs)))
    while live:
        for idx in list(live):
            try:
                next(programs[idx])
            except StopIteration as stop:
                results[idx] = stop.value
                live.remove(idx)
    return results


def _stack_heads(x, grp):
    zero = jnp.zeros_like(x)
    return jnp.concatenate([jnp.where(grp == 0, x, zero), jnp.where(grp == 1, x, zero)], axis=0)


def _block_diag_mask():
    return (lax.broadcasted_iota(jnp.int32, (LANES, 1), 0) // HEAD_DIM) == _lane_group()


def _head_rmsnorm(o, grp):
    sq = o * o
    s0 = jnp.sum(jnp.where(grp == 0, sq, 0.0), axis=-1, keepdims=True)
    s1 = jnp.sum(jnp.where(grp == 1, sq, 0.0), axis=-1, keepdims=True)
    ms = jnp.where(grp == 0, s0, s1) * (1.0 / HEAD_DIM)
    return o * lax.rsqrt(ms + EPS)


def _gla_kernel(q_ref, k_ref, v_ref, r_ref, glr_ref, w2_ref, bg_ref, ng_ref, s0_ref, o_ref, sfin_ref,
                ob_ref, st_ref):
    T = q_ref.shape[1]
    n = T // CHUNK
    grp = _lane_group()
    bd = _block_diag_mask()
    ri = lax.broadcasted_iota(jnp.int32, (CHUNK, CHUNK), 0)
    ci = lax.broadcasted_iota(jnp.int32, (CHUNK, CHUNK), 1)
    ri_pair = lax.broadcasted_iota(jnp.int32, (CHUNK, LANES), 0)
    ci_pair = lax.broadcasted_iota(jnp.int32, (CHUNK, LANES), 1) % CHUNK
    st_ref[...] = s0_ref[0, :, 0]

    def chunk(c0, d):
        sl = pl.ds(c0, CHUNK)
        q = q_ref[0, sl, :] * (DK_B ** -0.5)
        k = k_ref[0, sl, :]
        vb = v_ref[0, sl, :].astype(BF16)
        z = jnp.dot(glr_ref[0, sl, :].astype(BF16), w2_ref[d], preferred_element_type=F32) + bg_ref[d]
        yield
        g = (jnp.minimum(z, 0.0) - jnp.log(1.0 + jnp.exp(-jnp.abs(z)))) * (1.0 / GLA_TAU)
        g_hi, g_lo = _split_bf16(g)
        mask = (ri <= ci) if d else (ri >= ci)
        mask_pair = (ri_pair <= ci_pair) if d else (ri_pair >= ci_pair)
        tri = jnp.where(mask, 1.0, 0.0).astype(BF16)
        G = jnp.dot(tri, g_hi, preferred_element_type=F32) + jnp.dot(tri, g_lo, preferred_element_type=F32)
        yield
        g_tot = G[0:1] if d else G[CHUNK - 1:CHUNK]
        q_t = q * jnp.exp(G)
        k_t = (k * jnp.exp(-G)).astype(BF16)
        k_e = (k * jnp.exp(g_tot - G)).astype(BF16)
        att = jnp.where(mask_pair, _dot_nt(q_t.astype(BF16), _stack_heads(k_t, grp)), 0.0)
        u = jnp.where(bd, _dot_tn(k_e, vb), 0.0)
        decay = jnp.exp(jnp.broadcast_to(g_tot, (LANES, LANES)).T)
        yield
        S = st_ref[d]
        st_ref[d] = decay * S + u
        return (jnp.dot(q_t.astype(BF16), S.astype(BF16), preferred_element_type=F32)
                + jnp.dot(att.astype(BF16), _stack_heads(vb, grp), preferred_element_type=F32))

    per_step = min(GLA_CHUNKS_PER_STEP, n)

    def body(i, carry):
        c_f = [pl.multiple_of((i * per_step + j) * CHUNK, CHUNK) for j in range(per_step)]
        c_b = [pl.multiple_of((n - 1 - i * per_step - j) * CHUNK, CHUNK) for j in range(per_step)]
        outs = _lockstep([chunk(c0, 0) for c0 in c_f] + [chunk(c0, 1) for c0 in c_b])
        for j in range(per_step):
            o_ref[0, pl.ds(c_f[j], CHUNK), :] = outs[j]
            ob_ref[pl.ds(c_b[j], CHUNK), :] = outs[per_step + j]
        return carry

    lax.fori_loop(0, n // per_step, body, 0)
    sfin_ref[0, :, 0] = st_ref[...]

    def gate(i, carry):
        sl = pl.ds(pl.multiple_of(i * ELEMENTWISE_ROWS, ELEMENTWISE_ROWS), ELEMENTWISE_ROWS)
        r = r_ref[0, sl, :]
        y = _head_rmsnorm(o_ref[0, sl, :] + ob_ref[sl, :], grp) * ng_ref[...]
        o_ref[0, sl, :] = y * (r * jax.nn.sigmoid(r))
        return carry

    lax.fori_loop(0, T // ELEMENTWISE_ROWS, gate, 0)


def _pair_block_diag(s):
    B, _, H, dk, dv = s.shape
    s = s.reshape(B, 2, H // 2, 2, dk, dv)
    z = jnp.zeros_like(s[:, :, :, 0])
    top = jnp.concatenate([s[:, :, :, 0], z], axis=-1)
    bot = jnp.concatenate([z, s[:, :, :, 1]], axis=-1)
    return jnp.concatenate([top, bot], axis=-2)


def _pair_block_diag_inv(sp):
    a = sp[..., :HEAD_DIM, :HEAD_DIM]
    b = sp[..., HEAD_DIM:, HEAD_DIM:]
    B, _, P = sp.shape[:3]
    return jnp.stack([a, b], axis=3).reshape(B, 2, 2 * P, HEAD_DIM, HEAD_DIM)


EV_BQ, EV_BK, EV_BV, EV_BR, EV_GLR = 12, 16, 20, 24, 28


def _gla_call(p, w_g2, b_g, norm_g, s0):
    B, T, _ = p.shape
    n_pair = H_B // 2
    w2 = jnp.zeros((2, LANES, H_B * DK_B), F32)
    w2 = w2.at[0, :GLA_RANK].set(w_g2[0]).at[1, GLA_RANK:2 * GLA_RANK].set(w_g2[1]).astype(BF16)
    col = lambda off: pl.BlockSpec((1, T, LANES), lambda b, hp: (b, 0, off + hp))
    st = lambda: pl.BlockSpec((1, 2, 1, LANES, LANES), lambda b, hp: (b, 0, hp, 0, 0))
    o, s_fin = pl.pallas_call(
        _gla_kernel,
        grid=(B, n_pair),
        in_specs=[col(EV_BQ), col(EV_BK), col(EV_BV), col(EV_BR),
                  pl.BlockSpec((1, T, LANES), lambda b, hp: (b, 0, EV_GLR)),
                  pl.BlockSpec((2, LANES, LANES), lambda b, hp: (0, 0, hp)),
                  pl.BlockSpec((2, 1, LANES), lambda b, hp: (0, 0, hp)),
                  pl.BlockSpec((1, LANES), lambda b, hp: (0, hp)),
                  st()],
        out_specs=[pl.BlockSpec((1, T, LANES), lambda b, hp: (b, 0, hp)), st()],
        out_shape=[jax.ShapeDtypeStruct((B, T, n_pair * LANES), F32),
                   jax.ShapeDtypeStruct((B, 2, n_pair, LANES, LANES), F32)],
        scratch_shapes=[pltpu.VMEM((T, LANES), F32), pltpu.VMEM((2, LANES, LANES), F32)],
        compiler_params=_params("parallel", "parallel"),
    )(p, p, p, p, p, w2, b_g.reshape(2, 1, H_B * DK_B), norm_g.reshape(1, H_B * DV_B), _pair_block_diag(s0))
    return o, _pair_block_diag_inv(s_fin)


def _softplus(x):
    return jnp.maximum(x, 0.0) + jnp.log(1.0 + jnp.exp(-jnp.abs(x)))


def _delta_kernel(q_ref, k_ref, v_ref, z_ref, dab_ref, cq_ref, ck_ref, cv_ref, nega_ref, dtb_ref, ng_ref, s0_ref,
                  o_ref, sfin_ref, qs_ref, ks_ref, vs_ref, ob_ref, st_ref, hw_ref, ha_ref, hd_ref):
    T = q_ref.shape[1]
    n = T // CHUNK
    hp = pl.program_id(1)
    grp = _lane_group()
    lane = lax.broadcasted_iota(jnp.int32, (1, LANES), 1)
    subl = lax.broadcasted_iota(jnp.int32, (LANES, 1), 0)
    bd = _block_diag_mask()
    ri = lax.broadcasted_iota(jnp.int32, (CHUNK, CHUNK), 0)
    ci = lax.broadcasted_iota(jnp.int32, (CHUNK, CHUNK), 1)
    eye = jnp.where(ri == ci, 1.0, 0.0)
    row = lax.broadcasted_iota(jnp.int32, (ELEMENTWISE_ROWS, 1), 0)
    st_ref[...] = s0_ref[0, :, 0]

    def prep(i, carry):
        c0 = pl.multiple_of(i * ELEMENTWISE_ROWS, ELEMENTWISE_ROWS)
        lo = pl.multiple_of(jnp.maximum(c0 - SUBLANES, 0), SUBLANES)
        hi = pl.multiple_of(jnp.minimum(c0 + ELEMENTWISE_ROWS, T - SUBLANES), SUBLANES)
        for src, cw_ref, dst, scale in ((q_ref, cq_ref, qs_ref, DK_D ** -0.5), (k_ref, ck_ref, ks_ref, 1.0),
                                        (v_ref, cv_ref, vs_ref, None)):
            x = src[0, pl.ds(c0, ELEMENTWISE_ROWS), :]
            before = jnp.where(i > 0, src[0, pl.ds(lo, SUBLANES), :][SUBLANES - 1:], 0.0)
            after = jnp.where(i < T // ELEMENTWISE_ROWS - 1, src[0, pl.ds(hi, SUBLANES), :][:1], 0.0)
            x_prev = jnp.where(row == 0, before, pltpu.roll(x, 1, axis=0))
            x_next = jnp.where(row == ELEMENTWISE_ROWS - 1, after, pltpu.roll(x, ELEMENTWISE_ROWS - 1, axis=0))
            cw = cw_ref[...]
            y = x_prev * cw[0:1] + x * cw[1:2] + x_next * cw[2:3]
            y = y * jax.nn.sigmoid(y)
            if scale is not None:
                sq = y * y
                s0 = jnp.sum(jnp.where(grp == 0, sq, 0.0), axis=-1, keepdims=True)
                s1 = jnp.sum(jnp.where(grp == 1, sq, 0.0), axis=-1, keepdims=True)
                y = y * lax.rsqrt(jnp.where(grp == 0, s0, s1) + EPS) * scale
            dst[pl.ds(c0, ELEMENTWISE_ROWS), :] = y
        return carry

    lax.fori_loop(0, T // ELEMENTWISE_ROWS, prep, 0)

    def chunk(c0, d):
        sl = pl.ds(c0, CHUNK)
        q = qs_ref[sl, :]
        k = ks_ref[sl, :]
        v = vs_ref[sl, :]
        ab = dab_ref[0, sl, :]
        gb = jnp.where(lane < 2 * H_D, nega_ref[...] * _softplus(ab + dtb_ref[...]), jax.nn.sigmoid(ab))
        gb_t = gb.T
        incl = (ri <= ci) if d else (ri >= ci)
        incl_t = (ri >= ci) if d else (ri <= ci)
        strict = (ri < ci) if d else (ri > ci)

        def head(h):
            jg = d * H_D + 2 * hp + h
            g_c = jnp.sum(jnp.where(lane == jg, gb, 0.0), axis=1, keepdims=True)
            beta_c = jnp.sum(jnp.where(lane == 2 * H_D + jg, gb, 0.0), axis=1, keepdims=True)
            g_r = jnp.sum(jnp.where(subl == jg, gb_t, 0.0), axis=0, keepdims=True)
            G_c = jnp.sum(jnp.where(incl, g_r, 0.0), axis=1, keepdims=True)
            G_r = jnp.sum(jnp.where(incl_t, g_c, 0.0), axis=0, keepdims=True)
            tot = jnp.sum(g_r, axis=1, keepdims=True)
            gam = jnp.where(incl, jnp.exp(jnp.where(incl, G_c - G_r, 0.0)), 0.0)
            kh = jnp.where(grp == h, k, 0.0)
            qh = jnp.where(grp == h, q, 0.0)
            vh = jnp.where(grp == h, v, 0.0)
            kb = kh * beta_c
            khb = kh.astype(BF16)
            a = jnp.where(strict, _dot_nt(kb.astype(BF16), khb) * gam, 0.0)
            a_qk = (_dot_nt(qh.astype(BF16), khb) * gam).astype(BF16)
            yield
            wide = jnp.concatenate([-a, eye], axis=1)
            for _ in range(6):
                w_hi, w_lo = _split_bf16(wide)
                prod = (jnp.dot(w_hi[:, :CHUNK], w_hi, preferred_element_type=F32)
                        + jnp.dot(w_lo[:, :CHUNK], w_hi, preferred_element_type=F32)
                        + jnp.dot(w_hi[:, :CHUNK], w_lo, preferred_element_type=F32))
                wide = prod + jnp.where(lane >= CHUNK, wide, 0.0)
                yield
            t_inv = wide[:, CHUNK:]
            e_g = jnp.exp(G_c)
            rhs = vh * beta_c + pltpu.roll(kb * e_g, HEAD_DIM, axis=1)
            sol = jnp.dot(t_inv.astype(BF16), rhs.astype(BF16), preferred_element_type=F32)
            return sol, a_qk, qh * e_g, kh * jnp.exp(tot - G_c), jnp.exp(tot)

        return head

    per_step = min(DELTA_CHUNKS_PER_STEP, n)
    n_steps = n // per_step
    n_sys = 4 * per_step

    def starts(i):
        c_f = [pl.multiple_of((i * per_step + j) * CHUNK, CHUNK) for j in range(per_step)]
        c_b = [pl.multiple_of((n - 1 - i * per_step - j) * CHUNK, CHUNK) for j in range(per_step)]
        return c_f, c_b

    def head_programs(i):
        programs = []
        for d, c0s in zip((0, 1), starts(i)):
            for c0 in c0s:
                head = chunk(c0, d)
                programs += [head(0), head(1)]
        return programs

    def store_heads(slot, results):
        for s, (r, a_qk, q_g, k_e, d_last) in enumerate(results):
            hw_ref[slot, s, 0] = r
            hw_ref[slot, s, 1] = q_g
            hw_ref[slot, s, 2] = k_e
            ha_ref[slot, s] = a_qk
            hd_ref[slot, s] = jnp.broadcast_to(d_last, (SUBLANES, LANES))

    def load_head(slot, s):
        return hw_ref[slot, s, 0], ha_ref[slot, s], hw_ref[slot, s, 1], hw_ref[slot, s, 2], hd_ref[slot, s][0:1, 0:1]

    def scan_step(d, slot, s):
        (r0, aqk0, qg0, ke0, d0), (r1, aqk1, qg1, ke1, d1) = load_head(slot, s), load_head(slot, s + 1)
        S = st_ref[d]
        Sb = S.astype(BF16)
        w_val = jnp.where(grp == 0, r0, r1)
        k_cum = pltpu.roll(jnp.where(grp == 0, r1, r0), HEAD_DIM, axis=1)
        v_new = w_val - jnp.dot(k_cum.astype(BF16), Sb, preferred_element_type=F32)
        o = jnp.dot((qg0 + qg1).astype(BF16), Sb, preferred_element_type=F32)
        yield
        vnb = v_new.astype(BF16)
        o = o + jnp.where(grp == 0, jnp.dot(aqk0, vnb, preferred_element_type=F32), 0.0)
        o = o + jnp.where(grp == 1, jnp.dot(aqk1, vnb, preferred_element_type=F32), 0.0)
        d_rows = jnp.where(subl // HEAD_DIM == 0, d0, d1)
        st_ref[d] = S * d_rows + jnp.where(bd, _dot_tn((ke0 + ke1).astype(BF16), vnb), 0.0)
        return o

    def scan_dir(d, i, slot):
        c0s = starts(i)[d]
        for j in range(per_step):
            o = yield from scan_step(d, slot, d * 2 * per_step + 2 * j)
            if d == 0:
                o_ref[0, pl.ds(c0s[j], CHUNK), :] = o
            else:
                ob_ref[pl.ds(c0s[j], CHUNK), :] = o

    store_heads(0, _lockstep(head_programs(0)))

    def body(i, carry):
        slot = i % 2
        results = _lockstep(head_programs(i) + [scan_dir(0, i - 1, 1 - slot), scan_dir(1, i - 1, 1 - slot)])
        store_heads(slot, results[:n_sys])
        return carry

    lax.fori_loop(1, n_steps, body, 0)
    last = n_steps - 1
    _lockstep([scan_dir(0, last, last % 2), scan_dir(1, last, last % 2)])
    sfin_ref[0, :, 0] = st_ref[...]

    def gate(i, carry):
        sl = pl.ds(pl.multiple_of(i * ELEMENTWISE_ROWS, ELEMENTWISE_ROWS), ELEMENTWISE_ROWS)
        z = z_ref[0, sl, :]
        y = _head_rmsnorm(o_ref[0, sl, :] + ob_ref[sl, :], grp) * ng_ref[...]
        o_ref[0, sl, :] = y * (z * jax.nn.sigmoid(z))
        return carry

    lax.fori_loop(0, T // ELEMENTWISE_ROWS, gate, 0)


OD_DQ, OD_DK, OD_DV, OD_DZ, OD_DAB = 6, 10, 14, 18, 22


def _delta_call(p, w_conv, a_log, dt_bias, norm_g, s0):
    B, T, _ = p.shape
    n_pair = H_D // 2
    n_sys = 4 * min(DELTA_CHUNKS_PER_STEP, T // CHUNK)
    nega = jnp.zeros((1, LANES), F32).at[0, :2 * H_D].set(-jnp.exp(a_log.reshape(-1)))
    dtb = jnp.zeros((1, LANES), F32).at[0, :2 * H_D].set(dt_bias.reshape(-1))
    ng = jnp.tile(norm_g, 2).reshape(1, LANES)
    col = lambda off: pl.BlockSpec((1, T, LANES), lambda b, hp: (b, 0, off + hp))
    cw = lambda off: pl.BlockSpec((SHORT_CONV, LANES), lambda b, hp: (0, off + hp))
    vec = lambda: pl.BlockSpec((1, LANES), lambda b, hp: (0, 0))
    st = lambda: pl.BlockSpec((1, 2, 1, LANES, LANES), lambda b, hp: (b, 0, hp, 0, 0))
    o, s_fin = pl.pallas_call(
        _delta_kernel,
        grid=(B, n_pair),
        in_specs=[col(OD_DQ), col(OD_DK), col(OD_DV), col(OD_DZ),
                  pl.BlockSpec((1, T, LANES), lambda b, hp: (b, 0, OD_DAB)),
                  cw(0), cw(n_pair), cw(2 * n_pair), vec(), vec(), vec(), st()],
        out_specs=[pl.BlockSpec((1, T, LANES), lambda b, hp: (b, 0, hp)), st()],
        out_shape=[jax.ShapeDtypeStruct((B, T, n_pair * LANES), F32),
                   jax.ShapeDtypeStruct((B, 2, n_pair, LANES, LANES), F32)],
        scratch_shapes=[pltpu.VMEM((T, LANES), F32), pltpu.VMEM((T, LANES), F32), pltpu.VMEM((T, LANES), F32),
                        pltpu.VMEM((T, LANES), F32), pltpu.VMEM((2, LANES, LANES), F32),
                        pltpu.VMEM((2, n_sys, 3, CHUNK, LANES), F32),
                        pltpu.VMEM((2, n_sys, CHUNK, CHUNK), BF16),
                        pltpu.VMEM((2, n_sys, SUBLANES, LANES), F32)],
        compiler_params=_params("parallel", "parallel"),
    )(p, p, p, p, p, w_conv, w_conv, w_conv, nega, dtb, ng, _pair_block_diag(s0))
    return o, _pair_block_diag_inv(s_fin)


def _split_cols(p, sizes):
    cuts = [int(s) for s in np.cumsum(sizes)[:-1]]
    return jnp.split(p, cuts, axis=-1)


def _even_mix(p, latent, rpb, w_g2, b_g, norm_g, ak_cache, av_cache, sb):
    B, T, _ = p.shape
    ak = p[..., H_A * HEAD_DIM:2 * H_A * HEAD_DIM]
    av = p[..., 2 * H_A * HEAD_DIM:3 * H_A * HEAD_DIM]
    n_pair = H_A // 2
    if latent:
        o_a = _nbr_attn_call(p, rpb, ak_cache, av_cache)
        k_h = v_h = None
    else:
        k_h = ak.reshape(B, T, H_A, HEAD_DIM).transpose(0, 2, 1, 3)
        v_h = av.reshape(B, T, H_A, HEAD_DIM).transpose(0, 2, 1, 3)
        o_a = _dense_attn_call(p, 0, n_pair, 2 * n_pair, None, False)
    o_b, s_b = _gla_call(p, w_g2, b_g, norm_g, sb)
    return o_a, o_b, k_h, v_h, s_b


def _odd_mix(p, latent, sink, w_conv, a_log, dt_bias, norm_g, ck_cache, cv_cache, sd):
    B, T, _ = p.shape
    ck = p[..., H_C * HEAD_DIM:(H_C + KV_C) * HEAD_DIM]
    cv = p[..., (H_C + KV_C) * HEAD_DIM:(H_C + 2 * KV_C) * HEAD_DIM]
    n_pair = H_C // 2
    if latent:
        o_c = _win_attn_call(p, sink, ck_cache, cv_cache)
        k_h = v_h = None
    else:
        k_h = ck.reshape(B, T, KV_C, HEAD_DIM).transpose(0, 2, 1, 3)
        v_h = cv.reshape(B, T, KV_C, HEAD_DIM).transpose(0, 2, 1, 3)
        o_c = _dense_attn_call(p, 0, n_pair, n_pair + 1, sink, True)
    o_d, s_d = _delta_call(p, w_conv, a_log, dt_bias, norm_g, sd)
    return o_c, o_d, k_h, v_h, s_d


def _reorder_cols(w, sizes, order):
    parts = _split_cols(w, sizes)
    out = jnp.concatenate([parts[i] for i in order], axis=-1)
    pad = -out.shape[-1] % LANES
    return jnp.pad(out, ((0, 0), (0, pad)))


def _reorder_even_w_in(w):
    return _reorder_cols(w, EV_SIZES, (0, 1, 2, 3, 4, 5, 7, 6))


def _reorder_odd_w_in(w):
    return _reorder_cols(w, OD_SIZES, (0, 1, 2, 3, 4, 5, 8, 6, 7))


TM_LATENT = 512
TM_CONTEXT = 256
TM_FFN_LATENT = 512


def kernel(x_prompt, x_sample, cache_a_k, cache_a_v, state_b, cache_c_k, cache_c_v, state_d, c, c_ctx, ada_w, ada_b, norm1_g, norm2_g, ffn_up, ffn_conv, ffn_down, ev_w_in, ev_w_out, a_rpb, b_w_g2, b_b_g, b_norm_g, od_w_in, od_w_out, c_sink, d_conv, d_a_log, d_dt_bias, d_norm_g, final_g):
    Bp, Tp, D = x_prompt.shape
    Bs, Ts, _ = x_sample.shape
    cond = jnp.concatenate([c, c_ctx[None, :], jnp.zeros((2 * SUBLANES - Bs - 1, D), F32)], axis=0)
    mods = _adaln(cond, ada_w, ada_b)
    xp, xs = x_prompt, x_sample
    streams = ((False, Bp, TM_CONTEXT), (True, Bs, TM_LATENT))
    ak_l, av_l, sb_l, ck_l, cv_l, sd_l = [], [], [], [], [], []
    for l in range(DEPTH):
        j = l // 2
        mod_s = [mods[l, :Bs, i * D:(i + 1) * D] for i in range(6)]
        mod_p = [jnp.broadcast_to(mods[l, Bs:Bs + 1, i * D:(i + 1) * D], (Bp, D)) for i in range(6)]
        w_in = (_reorder_even_w_in(ev_w_in[j]) if l % 2 == 0 else _reorder_odd_w_in(od_w_in[j])).astype(BF16)
        w_out = (ev_w_out[j] if l % 2 == 0 else od_w_out[j]).astype(BF16)
        w_up = ffn_up[l].astype(BF16)
        w_down = ffn_down[l].astype(BF16)
        new_x = []
        for (latent, B, tm), x, mod in zip(streams, (xp, xs), (mod_p, mod_s)):
            sh1, sc1, g1, sh2, sc2, g2 = mod
            p = _nm_matmul(x, norm1_g[l], sh1, sc1, w_in, tm)
            if l % 2 == 0:
                s0 = state_b[:, j] if latent else jnp.zeros((B, 2, H_B, DK_B, DV_B), F32)
                o1, o2, k_h, v_h, s_new = _even_mix(p, latent, a_rpb[j], b_w_g2[j], b_b_g[j], b_norm_g[j],
                                                    cache_a_k[:, j], cache_a_v[:, j], s0)
                if not latent:
                    ak_l.append(k_h)
                    av_l.append(v_h)
                    sb_l.append(s_new)
            else:
                s0 = state_d[:, j] if latent else jnp.zeros((B, 2, H_D, DK_D, DV_D), F32)
                o1, o2, k_h, v_h, s_new = _odd_mix(p, latent, c_sink[j], d_conv[j], d_a_log[j], d_dt_bias[j],
                                                   d_norm_g[j], cache_c_k[:, j], cache_c_v[:, j], s0)
                if not latent:
                    ck_l.append(k_h)
                    cv_l.append(v_h)
                    sd_l.append(s_new)
            x = _mix_ffn_res(x, o1, o2, w_out, g1, norm2_g[l], sh2, sc2, g2, w_up, ffn_conv[l], w_down,
                             TM_FFN_LATENT if latent else tm, final_g if l == DEPTH - 1 else None)
            new_x.append(x)
        xp, xs = new_x
    return (xp, xs, jnp.stack(ak_l, axis=1), jnp.stack(av_l, axis=1), jnp.stack(sb_l, axis=1),
            jnp.stack(ck_l, axis=1), jnp.stack(cv_l, axis=1), jnp.stack(sd_l, axis=1))
```

```python
import functools
import math

import jax
import jax.numpy as jnp
import numpy as np
from jax import lax
from jax.experimental import pallas as pl
from jax.experimental.pallas import tpu as pltpu

D_MODEL = 1024
DEPTH = 4
GRID_W = 64
HEAD_DIM = 64
H_A = D_MODEL // (2 * HEAD_DIM)
WIN_R = 8
WIN_C = 16
H_B = D_MODEL // (2 * HEAD_DIM)
DK_B = HEAD_DIM
DV_B = HEAD_DIM
GLA_RANK = 16
GLA_TAU = 16.0
H_C = D_MODEL // (2 * HEAD_DIM)
KV_C = H_C // 4
WIN_1D = 128
ROPE_BASE = 10000.0
H_D = D_MODEL // (2 * HEAD_DIM)
DK_D = HEAD_DIM
DV_D = HEAD_DIM
SHORT_CONV = 3
CHUNK = 64
EPS = 1e-6
EV_SIZES = (H_A * HEAD_DIM, H_A * HEAD_DIM, H_A * HEAD_DIM, H_B * DK_B, H_B * DK_B, H_B * DV_B, 2 * GLA_RANK, H_B * DV_B)
OD_SIZES = (H_C * HEAD_DIM, KV_C * HEAD_DIM, KV_C * HEAD_DIM, H_D * DK_D, H_D * DK_D, H_D * DV_D, 2 * H_D, 2 * H_D, H_D * DV_D)
F32 = jnp.float32
BF16 = jnp.bfloat16

V7X_VMEM_LIMIT_BYTES = 52 * 1024 * 1024
SUBLANES = 8
LANES = 128
COL_CHUNK = 512
ELEMENTWISE_ROWS = 512
FFN_SUB_COLS = 1536


def _params(*sem):
    return pltpu.CompilerParams(dimension_semantics=sem, vmem_limit_bytes=V7X_VMEM_LIMIT_BYTES)


def _norm_mod(x, g, shift, scale):
    y = x * lax.rsqrt(jnp.mean(x * x, axis=-1, keepdims=True) + EPS)
    return (y * g) * (1.0 + scale) + shift


def _adaln_kernel(c_ref, w_ref, b_ref, o_ref):
    cond = c_ref[...]
    a = cond * jax.nn.sigmoid(cond)
    a_hi = a.astype(BF16)
    a_lo = (a - a_hi.astype(F32)).astype(BF16)
    w = w_ref[0]
    w_hi = w.astype(BF16)
    w_lo = (w - w_hi.astype(F32)).astype(BF16)
    acc = jnp.dot(a_hi, w_hi, preferred_element_type=F32)
    acc += jnp.dot(a_lo, w_hi, preferred_element_type=F32)
    acc += jnp.dot(a_hi, w_lo, preferred_element_type=F32)
    o_ref[0] = acc + b_ref[0]


def _adaln(cond, ada_w, ada_b):
    R = cond.shape[0]
    L, D, N = ada_w.shape
    tn = 1024
    return pl.pallas_call(
        _adaln_kernel,
        grid=(L, N // tn),
        in_specs=[pl.BlockSpec((R, D), lambda l, j: (0, 0)),
                  pl.BlockSpec((1, D, tn), lambda l, j: (l, 0, j)),
                  pl.BlockSpec((1, 1, tn), lambda l, j: (l, 0, j))],
        out_specs=pl.BlockSpec((1, R, tn), lambda l, j: (l, 0, j)),
        out_shape=jax.ShapeDtypeStruct((L, R, N), F32),
        compiler_params=_params("parallel", "parallel"),
    )(cond, ada_w, ada_b.reshape(L, 1, N))


def _nm_matmul_kernel(x_ref, g_ref, sh_ref, sc_ref, w_ref, o_ref):
    h = _norm_mod(x_ref[0], g_ref[...], sh_ref[0], sc_ref[0]).astype(BF16)
    n = o_ref.shape[-1]
    for c0 in range(0, n, COL_CHUNK):
        c1 = min(c0 + COL_CHUNK, n)
        o_ref[0, :, c0:c1] = jnp.dot(h, w_ref[:, c0:c1], preferred_element_type=F32)


def _nm_matmul(x, g, shift, scale, w, tm):
    B, T, D = x.shape
    N = w.shape[1]
    return pl.pallas_call(
        _nm_matmul_kernel,
        grid=(B, T // tm),
        in_specs=[pl.BlockSpec((1, tm, D), lambda b, i: (b, i, 0)),
                  pl.BlockSpec((1, D), lambda b, i: (0, 0)),
                  pl.BlockSpec((1, 1, D), lambda b, i: (b, 0, 0)),
                  pl.BlockSpec((1, 1, D), lambda b, i: (b, 0, 0)),
                  pl.BlockSpec((D, N), lambda b, i: (0, 0))],
        out_specs=pl.BlockSpec((1, tm, N), lambda b, i: (b, i, 0)),
        out_shape=jax.ShapeDtypeStruct((B, T, N), F32),
        compiler_params=_params("parallel", "parallel"),
    )(x, g.reshape(1, D), shift.reshape(B, 1, D), scale.reshape(B, 1, D), w)


def _mix_ffn_kernel(x_ref, xp_ref, xn_ref, a1_ref, a1p_ref, a1n_ref, a2_ref, a2p_ref, a2n_ref, wo1_ref, wo2_ref,
                    gate1_ref, g_ref, sh_ref, sc_ref, gate_ref, wu_ref, cw_ref, wd_ref, fg_ref,
                    o_ref, h_ref, *, final_norm):
    i = pl.program_id(1)
    tm = x_ref.shape[1]
    F = wd_ref.shape[0]
    ext = lambda m, p, n: jnp.concatenate([m[0], p[0], n[0]], axis=0).astype(BF16)
    mix = (jnp.dot(ext(a1_ref, a1p_ref, a1n_ref), wo1_ref[...], preferred_element_type=F32)
           + jnp.dot(ext(a2_ref, a2p_ref, a2n_ref), wo2_ref[...], preferred_element_type=F32))
    gate1 = gate1_ref[0]
    x1 = x_ref[0] + gate1 * mix[0:tm]
    x1p = xp_ref[0] + gate1 * mix[tm:tm + SUBLANES]
    x1n = xn_ref[0] + gate1 * mix[tm + SUBLANES:]
    g, sh, sc = g_ref[...], sh_ref[0], sc_ref[0]
    h_ref[0:tm, :] = _norm_mod(x1, g, sh, sc).astype(BF16)
    hp = jnp.where(i > 0, _norm_mod(x1p, g, sh, sc), 0.0)
    hn = jnp.where(i < pl.num_programs(1) - 1, _norm_mod(x1n, g, sh, sc), 0.0)
    h_ref[tm:tm + 2 * SUBLANES, :] = jnp.concatenate([hp, hn], axis=0).astype(BF16)
    h = h_ref[...]
    row = lax.broadcasted_iota(jnp.int32, (tm, 1), 0)

    def conv_branch(c0, c1):
        u_all = jnp.dot(h, wu_ref[:, c0:c1], preferred_element_type=F32)
        u = u_all[0:tm]
        u_prev = jnp.where(row == 0, u_all[tm + SUBLANES - 1:tm + SUBLANES], pltpu.roll(u, 1, axis=0))
        u_next = jnp.where(row == tm - 1, u_all[tm + SUBLANES:tm + SUBLANES + 1], pltpu.roll(u, tm - 1, axis=0))
        cw = cw_ref[:, c0:c1]
        return u_prev * cw[0:1] + u * cw[1:2] + u_next * cw[2:3]

    acc = None
    for c0 in range(0, F, FFN_SUB_COLS):
        c1 = min(c0 + FFN_SUB_COLS, F)
        a = conv_branch(c0, c1)
        gt = conv_branch(F + c0, F + c1)
        act = a * (gt * jax.nn.sigmoid(gt))
        part = jnp.dot(act.astype(BF16), wd_ref[c0:c1, :], preferred_element_type=F32)
        acc = part if acc is None else acc + part

    y = x1 + gate_ref[0] * acc
    if final_norm:
        y = (y * lax.rsqrt(jnp.mean(y * y, axis=-1, keepdims=True) + EPS)) * fg_ref[...]
    o_ref[0] = y


def _mix_ffn_res(x, a1, a2, w_out, gate1, g, shift, scale, gate, w_up, w_conv, w_down, tm, final_g=None):
    B, T, D = x.shape
    K = a1.shape[-1]
    F = w_down.shape[0]
    nt = T // tm
    rows8 = tm // SUBLANES
    vec = lambda: pl.BlockSpec((1, 1, D), lambda b, i: (b, 0, 0))
    resident = lambda shape, blk=0: pl.BlockSpec(shape, lambda b, i: (blk, 0), pipeline_mode=pl.Buffered(1))
    tile = lambda w: pl.BlockSpec((1, tm, w), lambda b, i: (b, i, 0))
    before = lambda w: pl.BlockSpec((1, SUBLANES, w), lambda b, i: (b, jnp.maximum(i * rows8 - 1, 0), 0))
    after = lambda w: pl.BlockSpec((1, SUBLANES, w),
                                   lambda b, i: (b, jnp.minimum((i + 1) * rows8, T // SUBLANES - 1), 0))
    return pl.pallas_call(
        functools.partial(_mix_ffn_kernel, final_norm=final_g is not None),
        grid=(B, nt),
        in_specs=[tile(D), before(D), after(D), tile(K), before(K), after(K), tile(K), before(K), after(K),
                  resident((K, D), 0), resident((K, D), 1), vec(),
                  pl.BlockSpec((1, D), lambda b, i: (0, 0)),
                  vec(), vec(), vec(),
                  resident((D, 2 * F)), resident((3, 2 * F)), resident((F, D)),
                  pl.BlockSpec((1, D), lambda b, i: (0, 0))],
        out_specs=pl.BlockSpec((1, tm, D), lambda b, i: (b, i, 0)),
        out_shape=jax.ShapeDtypeStruct((B, T, D), F32),
        scratch_shapes=[pltpu.VMEM((tm + 2 * SUBLANES, D), BF16)],
        compiler_params=_params("parallel", "parallel"),
    )(x, x, x, a1, a1, a1, a2, a2, a2, w_out, w_out, gate1.reshape(B, 1, D),
      g.reshape(1, D), shift.reshape(B, 1, D), scale.reshape(B, 1, D), gate.reshape(B, 1, D),
      w_up, w_conv, w_down, (g if final_g is None else final_g).reshape(1, D))


NEG_BIG = -1e30
ATTN_SCALE = HEAD_DIM ** -0.5


def _dot_nt(a, b):
    return lax.dot_general(a, b, (((1,), (1,)), ((), ())), preferred_element_type=F32)


def _lane_group():
    return lax.broadcasted_iota(jnp.int32, (1, LANES), 1) // HEAD_DIM


def _attend(score_fns, values, sink):
    scores = [fn() for fn in score_fns]
    yield
    s = scores[0] if len(scores) == 1 else jnp.concatenate(scores, axis=1)
    m = jnp.max(s, axis=-1, keepdims=True)
    if sink is not None:
        m = jnp.maximum(m, sink)
    e = jnp.exp(s - m)
    den = jnp.sum(e, axis=-1, keepdims=True)
    if sink is not None:
        den = den + jnp.exp(sink - m)
    eb = e.astype(BF16)
    out = None
    c0 = 0
    for v in values:
        o = jnp.dot(eb[:, c0:c0 + v.shape[0]], v, preferred_element_type=F32)
        out = o if out is None else out + o
        c0 += v.shape[0]
    yield
    return out / den


def _merge_heads(outs, grp):
    return jnp.where(grp == 0, outs[0], outs[1])


def _dense_attn_kernel(q_ref, k_ref, v_ref, sink_ref, o_ref, *, gqa, use_sink):
    grp = _lane_group()
    q = q_ref[0] * ATTN_SCALE
    k = k_ref[0]
    v = v_ref[0]
    if gqa:
        kvg = pl.program_id(1) // 2
        k = jnp.where(grp == kvg, k, pltpu.roll(k, HEAD_DIM, axis=1))
        v = jnp.where(grp == kvg, v, pltpu.roll(v, HEAD_DIM, axis=1))
    kb = k.astype(BF16)
    vb = v.astype(BF16)

    def head(h):
        qh = jnp.where(grp == h, q, 0.0).astype(BF16)
        sink = sink_ref[0, h][:, :1] if use_sink else None
        return _attend([lambda: _dot_nt(qh, kb)], [vb], sink)

    o_ref[0] = _merge_heads(_lockstep([head(0), head(1)]), grp)


def _dense_attn_call(p, q_blk, k_blk, v_blk, sink, gqa):
    B, T, _ = p.shape
    n_pair = H_A // 2
    use_sink = sink is not None
    if use_sink:
        sink_arr = jnp.broadcast_to(sink.reshape(n_pair, 2, 1, 1), (n_pair, 2, 1, LANES))
    else:
        sink_arr = jnp.zeros((n_pair, 2, 1, LANES), F32)
    kv_idx = (lambda hp: 0) if gqa else (lambda hp: hp)
    return pl.pallas_call(
        functools.partial(_dense_attn_kernel, gqa=gqa, use_sink=use_sink),
        grid=(B, n_pair),
        in_specs=[pl.BlockSpec((1, T, LANES), lambda b, hp: (b, 0, q_blk + hp)),
                  pl.BlockSpec((1, T, LANES), lambda b, hp: (b, 0, k_blk + kv_idx(hp))),
                  pl.BlockSpec((1, T, LANES), lambda b, hp: (b, 0, v_blk + kv_idx(hp))),
                  pl.BlockSpec((1, 2, 1, LANES), lambda b, hp: (hp, 0, 0, 0))],
        out_specs=pl.BlockSpec((1, T, LANES), lambda b, hp: (b, 0, hp)),
        out_shape=jax.ShapeDtypeStruct((B, T, n_pair * LANES), F32),
        compiler_params=_params("parallel", "parallel"),
    )(p, p, p, sink_arr)


NBR_SPAN = WIN_R * GRID_W
ATTN_BLOCKS_PER_STEP = 16


def _nbr_attn_kernel(q_ref, k_ref, v_ref, ck_ref, cv_ref, bias_ref, o_ref, kb_ref, vb_ref):
    kb_ref[...] = k_ref[0].astype(BF16)
    vb_ref[...] = v_ref[0].astype(BF16)
    ckb = ck_ref[0].astype(BF16)
    cvb = cv_ref[0].astype(BF16)
    grp = _lane_group()
    n_rows = q_ref.shape[1] // GRID_W

    def head(i, h):
        r0 = jnp.clip(i - WIN_R // 2, 0, n_rows - WIN_R)
        q = q_ref[0, pl.ds(pl.multiple_of(i * GRID_W, GRID_W), GRID_W), :] * ATTN_SCALE
        start = pl.multiple_of(r0 * GRID_W, GRID_W)
        kw = kb_ref[pl.ds(start, NBR_SPAN), :]
        vw = vb_ref[pl.ds(start, NBR_SPAN), :]
        qh = jnp.where(grp == h, q, 0.0).astype(BF16)
        return _attend([lambda: _dot_nt(qh, kw) + bias_ref[i - r0, h], lambda: _dot_nt(qh, ckb)], [vw, cvb], None)

    def rows(step, carry):
        ids = [step * ATTN_BLOCKS_PER_STEP + j for j in range(ATTN_BLOCKS_PER_STEP)]
        outs = _lockstep([head(i, h) for i in ids for h in range(2)])
        for j, i in enumerate(ids):
            o_ref[0, pl.ds(pl.multiple_of(i * GRID_W, GRID_W), GRID_W), :] = _merge_heads(outs[2 * j:2 * j + 2], grp)
        return carry

    lax.fori_loop(0, n_rows // ATTN_BLOCKS_PER_STEP, rows, 0)


def _nbr_bias_table(rpb):
    var = np.arange(WIN_R)[:, None, None]
    r = np.arange(WIN_R)[None, :, None]
    row_sel = (r - var + (WIN_R - 1) == np.arange(2 * WIN_R - 1)[None, None, :]).astype(np.float32)
    w = np.arange(GRID_W)[:, None, None]
    cc = np.arange(GRID_W)[None, :, None]
    c0 = np.clip(w - WIN_C // 2, 0, GRID_W - WIN_C)
    valid = (cc >= c0) & (cc < c0 + WIN_C)
    col_sel = (valid & (cc - w + (WIN_C - 1) == np.arange(2 * WIN_C - 1)[None, None, :])).astype(np.float32)
    t = jnp.einsum('vra,hab->vhrb', row_sel, rpb, precision=lax.Precision.HIGHEST)
    tbl = jnp.einsum('vhrb,wcb->vhwrc', t, col_sel, precision=lax.Precision.HIGHEST)
    tbl = jnp.where(valid[None, None, :, None, :, 0], tbl, NEG_BIG)
    return tbl.reshape(WIN_R, rpb.shape[0], GRID_W, NBR_SPAN)


def _nbr_attn_call(p, rpb, ck_cache, cv_cache):
    B, T, _ = p.shape
    n_pair = H_A // 2
    L = ck_cache.shape[2]
    ckp = ck_cache.transpose(0, 2, 1, 3).reshape(B, L, H_A * HEAD_DIM)
    cvp = cv_cache.transpose(0, 2, 1, 3).reshape(B, L, H_A * HEAD_DIM)
    bias = _nbr_bias_table(rpb)
    col = lambda off: pl.BlockSpec((1, T, LANES), lambda b, hp: (b, 0, off + hp))
    ctx = lambda: pl.BlockSpec((1, L, LANES), lambda b, hp: (b, 0, hp))
    return pl.pallas_call(
        _nbr_attn_kernel,
        grid=(B, n_pair),
        in_specs=[col(0), col(n_pair), col(2 * n_pair), ctx(), ctx(),
                  pl.BlockSpec((WIN_R, 2, GRID_W, NBR_SPAN), lambda b, hp: (0, hp, 0, 0))],
        out_specs=pl.BlockSpec((1, T, LANES), lambda b, hp: (b, 0, hp)),
        out_shape=jax.ShapeDtypeStruct((B, T, n_pair * LANES), F32),
        scratch_shapes=[pltpu.VMEM((T, LANES), BF16), pltpu.VMEM((T, LANES), BF16)],
        compiler_params=_params("parallel", "parallel"),
    )(p, p, p, ckp, cvp, bias)


WIN_SPAN = 3 * WIN_1D
WIN_Q_TILE = 64
WIN_TILES_PER_STEP = 16
ROPE_SWAP = HEAD_DIM // 4


def _rope(x, cos, sin_signed):
    lane = lax.broadcasted_iota(jnp.int32, (1, LANES), 1)
    first = lane % (2 * ROPE_SWAP) < ROPE_SWAP
    partner = jnp.where(first, pltpu.roll(x, LANES - ROPE_SWAP, axis=1), pltpu.roll(x, ROPE_SWAP, axis=1))
    return x * cos + partner * sin_signed


def _win_attn_kernel(q_ref, k_ref, v_ref, ck_ref, cv_ref, cos_ref, sin_ref, sink_ref, o_ref, kb_ref, vb_ref):
    grp = _lane_group()
    kvg = pl.program_id(1) // 2
    both = lambda a: jnp.where(grp == kvg, a, pltpu.roll(a, HEAD_DIM, axis=1))

    @pl.when(pl.program_id(1) % 2 == 0)
    def _():
        kb_ref[...] = _rope(both(k_ref[0]), cos_ref[...], sin_ref[...]).astype(BF16)
        vb_ref[...] = both(v_ref[0]).astype(BF16)

    ckb = both(ck_ref[0]).astype(BF16)
    cvb = both(cv_ref[0]).astype(BF16)
    T = q_ref.shape[1]
    rel = (lax.broadcasted_iota(jnp.int32, (WIN_Q_TILE, WIN_SPAN), 1)
           - lax.broadcasted_iota(jnp.int32, (WIN_Q_TILE, WIN_SPAN), 0))

    def heads(n):
        q0 = pl.multiple_of(n * WIN_Q_TILE, WIN_Q_TILE)
        start = pl.multiple_of(jnp.clip(q0 - WIN_1D, 0, T - WIN_SPAN), WIN_Q_TILE)
        q = _rope(q_ref[0, pl.ds(q0, WIN_Q_TILE), :], cos_ref[pl.ds(q0, WIN_Q_TILE), :],
                  sin_ref[pl.ds(q0, WIN_Q_TILE), :]) * ATTN_SCALE
        kw = kb_ref[pl.ds(start, WIN_SPAN), :]
        vw = vb_ref[pl.ds(start, WIN_SPAN), :]
        valid = jnp.abs(rel + (start - q0)) <= WIN_1D

        def head(h):
            qh = jnp.where(grp == h, q, 0.0).astype(BF16)
            return _attend([lambda: jnp.where(valid, _dot_nt(qh, kw), NEG_BIG), lambda: _dot_nt(qh, ckb)],
                           [vw, cvb], sink_ref[0, h][:, :1])

        return [head(0), head(1)]

    def tiles(step, carry):
        ids = [step * WIN_TILES_PER_STEP + j for j in range(WIN_TILES_PER_STEP)]
        outs = _lockstep([prog for n in ids for prog in heads(n)])
        for j, n in enumerate(ids):
            o_ref[0, pl.ds(pl.multiple_of(n * WIN_Q_TILE, WIN_Q_TILE), WIN_Q_TILE), :] = _merge_heads(
                outs[2 * j:2 * j + 2], grp)
        return carry

    lax.fori_loop(0, T // WIN_Q_TILE // WIN_TILES_PER_STEP, tiles, 0)


def _rope_tables(T):
    t = jnp.arange(T)
    quarter = HEAD_DIM // 4
    inv = 1.0 / (ROPE_BASE ** (jnp.arange(quarter, dtype=F32) / quarter))
    ang_r = (t // GRID_W).astype(F32)[:, None] * inv[None, :]
    ang_c = (t % GRID_W).astype(F32)[:, None] * inv[None, :]
    cos = jnp.concatenate([jnp.cos(ang_r), jnp.cos(ang_r), jnp.cos(ang_c), jnp.cos(ang_c)], axis=-1)
    sin = jnp.concatenate([-jnp.sin(ang_r), jnp.sin(ang_r), -jnp.sin(ang_c), jnp.sin(ang_c)], axis=-1)
    return jnp.tile(cos, (1, 2)), jnp.tile(sin, (1, 2))


def _win_attn_call(p, sink, ck_cache, cv_cache):
    B, T, _ = p.shape
    n_pair = H_C // 2
    L = ck_cache.shape[2]
    ckp = ck_cache.transpose(0, 2, 1, 3).reshape(B, L, KV_C * HEAD_DIM)
    cvp = cv_cache.transpose(0, 2, 1, 3).reshape(B, L, KV_C * HEAD_DIM)
    cos, sin = _rope_tables(T)
    sink_arr = jnp.broadcast_to(sink.reshape(n_pair, 2, 1, 1), (n_pair, 2, 1, LANES))
    col = lambda blk: pl.BlockSpec((1, T, LANES), lambda b, hp: (b, 0, blk))
    ctx = lambda: pl.BlockSpec((1, L, LANES), lambda b, hp: (b, 0, 0))
    tab = lambda: pl.BlockSpec((T, LANES), lambda b, hp: (0, 0))
    return pl.pallas_call(
        _win_attn_kernel,
        grid=(B, n_pair),
        in_specs=[pl.BlockSpec((1, T, LANES), lambda b, hp: (b, 0, hp)), col(n_pair), col(n_pair + 1),
                  ctx(), ctx(), tab(), tab(),
                  pl.BlockSpec((1, 2, 1, LANES), lambda b, hp: (hp, 0, 0, 0))],
        out_specs=pl.BlockSpec((1, T, LANES), lambda b, hp: (b, 0, hp)),
        out_shape=jax.ShapeDtypeStruct((B, T, n_pair * LANES), F32),
        scratch_shapes=[pltpu.VMEM((T, LANES), BF16), pltpu.VMEM((T, LANES), BF16)],
        compiler_params=_params("parallel", "arbitrary"),
    )(p, p, p, ckp, cvp, cos, sin, sink_arr)


def _dot_tn(a, b):
    return lax.dot_general(a, b, (((0,), (0,)), ((), ())), preferred_element_type=F32)


def _split_bf16(x):
    hi = x.astype(BF16)
    return hi, (x - hi.astype(F32)).astype(BF16)


GLA_CHUNKS_PER_STEP = 16
DELTA_CHUNKS_PER_STEP = 4


def _lockstep(programs):
    results = [None] * len(programs)
    live = list(range(len(programs)))
    while live:
        for idx in list(live):
            try:
                next(programs[idx])
            except StopIteration as stop:
                results[idx] = stop.value
                live.remove(idx)
    return results


def _stack_heads(x, grp):
    zero = jnp.zeros_like(x)
    return jnp.concatenate([jnp.where(grp == 0, x, zero), jnp.where(grp == 1, x, zero)], axis=0)


def _block_diag_mask():
    return (lax.broadcasted_iota(jnp.int32, (LANES, 1), 0) // HEAD_DIM) == _lane_group()


def _head_rmsnorm(o, grp):
    sq = o * o
    s0 = jnp.sum(jnp.where(grp == 0, sq, 0.0), axis=-1, keepdims=True)
    s1 = jnp.sum(jnp.where(grp == 1, sq, 0.0), axis=-1, keepdims=True)
    ms = jnp.where(grp == 0, s0, s1) * (1.0 / HEAD_DIM)
    return o * lax.rsqrt(ms + EPS)


def _gla_kernel(q_ref, k_ref, v_ref, r_ref, glr_ref, w2_ref, bg_ref, ng_ref, s0_ref, o_ref, sfin_ref,
                ob_ref, st_ref):
    T = q_ref.shape[1]
    n = T // CHUNK
    ew_rows = min(ELEMENTWISE_ROWS, T)
    grp = _lane_group()
    bd = _block_diag_mask()
    ri = lax.broadcasted_iota(jnp.int32, (CHUNK, CHUNK), 0)
    ci = lax.broadcasted_iota(jnp.int32, (CHUNK, CHUNK), 1)
    ri_pair = lax.broadcasted_iota(jnp.int32, (CHUNK, LANES), 0)
    ci_pair = lax.broadcasted_iota(jnp.int32, (CHUNK, LANES), 1) % CHUNK
    st_ref[...] = s0_ref[0, :, 0]

    def chunk(c0, d):
        sl = pl.ds(c0, CHUNK)
        q = q_ref[0, sl, :] * (DK_B ** -0.5)
        k = k_ref[0, sl, :]
        vb = v_ref[0, sl, :].astype(BF16)
        z = jnp.dot(glr_ref[0, sl, :].astype(BF16), w2_ref[d], preferred_element_type=F32) + bg_ref[d]
        yield
        g = (jnp.minimum(z, 0.0) - jnp.log(1.0 + jnp.exp(-jnp.abs(z)))) * (1.0 / GLA_TAU)
        g_hi, g_lo = _split_bf16(g)
        mask = (ri <= ci) if d else (ri >= ci)
        mask_pair = (ri_pair <= ci_pair) if d else (ri_pair >= ci_pair)
        tri = jnp.where(mask, 1.0, 0.0).astype(BF16)
        G = jnp.dot(tri, g_hi, preferred_element_type=F32) + jnp.dot(tri, g_lo, preferred_element_type=F32)
        yield
        g_tot = G[0:1] if d else G[CHUNK - 1:CHUNK]
        q_t = q * jnp.exp(G)
        k_t = (k * jnp.exp(-G)).astype(BF16)
        k_e = (k * jnp.exp(g_tot - G)).astype(BF16)
        att = jnp.where(mask_pair, _dot_nt(q_t.astype(BF16), _stack_heads(k_t, grp)), 0.0)
        u = jnp.where(bd, _dot_tn(k_e, vb), 0.0)
        decay = jnp.exp(jnp.broadcast_to(g_tot, (LANES, LANES)).T)
        yield
        S = st_ref[d]
        st_ref[d] = decay * S + u
        return (jnp.dot(q_t.astype(BF16), S.astype(BF16), preferred_element_type=F32)
                + jnp.dot(att.astype(BF16), _stack_heads(vb, grp), preferred_element_type=F32))

    per_step = min(GLA_CHUNKS_PER_STEP, n)

    def body(i, carry):
        c_f = [pl.multiple_of((i * per_step + j) * CHUNK, CHUNK) for j in range(per_step)]
        c_b = [pl.multiple_of((n - 1 - i * per_step - j) * CHUNK, CHUNK) for j in range(per_step)]
        outs = _lockstep([chunk(c0, 0) for c0 in c_f] + [chunk(c0, 1) for c0 in c_b])
        for j in range(per_step):
            o_ref[0, pl.ds(c_f[j], CHUNK), :] = outs[j]
            ob_ref[pl.ds(c_b[j], CHUNK), :] = outs[per_step + j]
        return carry

    lax.fori_loop(0, n // per_step, body, 0)
    sfin_ref[0, :, 0] = st_ref[...]

    def gate(i, carry):
        sl = pl.ds(pl.multiple_of(i * ew_rows, ew_rows), ew_rows)
        r = r_ref[0, sl, :]
        y = _head_rmsnorm(o_ref[0, sl, :] + ob_ref[sl, :], grp) * ng_ref[...]
        o_ref[0, sl, :] = y * (r * jax.nn.sigmoid(r))
        return carry

    lax.fori_loop(0, T // ew_rows, gate, 0)


def _pair_block_diag(s):
    B, _, H, dk, dv = s.shape
    s = s.reshape(B, 2, H // 2, 2, dk, dv)
    z = jnp.zeros_like(s[:, :, :, 0])
    top = jnp.concatenate([s[:, :, :, 0], z], axis=-1)
    bot = jnp.concatenate([z, s[:, :, :, 1]], axis=-1)
    return jnp.concatenate([top, bot], axis=-2)


def _pair_block_diag_inv(sp):
    a = sp[..., :HEAD_DIM, :HEAD_DIM]
    b = sp[..., HEAD_DIM:, HEAD_DIM:]
    B, _, P = sp.shape[:3]
    return jnp.stack([a, b], axis=3).reshape(B, 2, 2 * P, HEAD_DIM, HEAD_DIM)


EV_BQ, EV_BK, EV_BV, EV_BR, EV_GLR = 12, 16, 20, 24, 28


def _gla_call(p, w_g2, b_g, norm_g, s0):
    B, T, _ = p.shape
    n_pair = H_B // 2
    w2 = jnp.zeros((2, LANES, H_B * DK_B), F32)
    w2 = w2.at[0, :GLA_RANK].set(w_g2[0]).at[1, GLA_RANK:2 * GLA_RANK].set(w_g2[1]).astype(BF16)
    col = lambda off: pl.BlockSpec((1, T, LANES), lambda b, hp: (b, 0, off + hp))
    st = lambda: pl.BlockSpec((1, 2, 1, LANES, LANES), lambda b, hp: (b, 0, hp, 0, 0))
    o, s_fin = pl.pallas_call(
        _gla_kernel,
        grid=(B, n_pair),
        in_specs=[col(EV_BQ), col(EV_BK), col(EV_BV), col(EV_BR),
                  pl.BlockSpec((1, T, LANES), lambda b, hp: (b, 0, EV_GLR)),
                  pl.BlockSpec((2, LANES, LANES), lambda b, hp: (0, 0, hp)),
                  pl.BlockSpec((2, 1, LANES), lambda b, hp: (0, 0, hp)),
                  pl.BlockSpec((1, LANES), lambda b, hp: (0, hp)),
                  st()],
        out_specs=[pl.BlockSpec((1, T, LANES), lambda b, hp: (b, 0, hp)), st()],
        out_shape=[jax.ShapeDtypeStruct((B, T, n_pair * LANES), F32),
                   jax.ShapeDtypeStruct((B, 2, n_pair, LANES, LANES), F32)],
        scratch_shapes=[pltpu.VMEM((T, LANES), F32), pltpu.VMEM((2, LANES, LANES), F32)],
        compiler_params=_params("parallel", "parallel"),
    )(p, p, p, p, p, w2, b_g.reshape(2, 1, H_B * DK_B), norm_g.reshape(1, H_B * DV_B), _pair_block_diag(s0))
    return o, _pair_block_diag_inv(s_fin)


def _softplus(x):
    return jnp.maximum(x, 0.0) + jnp.log(1.0 + jnp.exp(-jnp.abs(x)))


def _delta_kernel(q_ref, k_ref, v_ref, z_ref, dab_ref, cq_ref, ck_ref, cv_ref, nega_ref, dtb_ref, ng_ref, s0_ref,
                  o_ref, sfin_ref, qs_ref, ks_ref, vs_ref, ob_ref, st_ref, hw_ref, ha_ref, hd_ref):
    T = q_ref.shape[1]
    n = T // CHUNK
    ew_rows = min(ELEMENTWISE_ROWS, T)
    hp = pl.program_id(1)
    grp = _lane_group()
    lane = lax.broadcasted_iota(jnp.int32, (1, LANES), 1)
    subl = lax.broadcasted_iota(jnp.int32, (LANES, 1), 0)
    bd = _block_diag_mask()
    ri = lax.broadcasted_iota(jnp.int32, (CHUNK, CHUNK), 0)
    ci = lax.broadcasted_iota(jnp.int32, (CHUNK, CHUNK), 1)
    eye = jnp.where(ri == ci, 1.0, 0.0)
    row = lax.broadcasted_iota(jnp.int32, (ew_rows, 1), 0)
    st_ref[...] = s0_ref[0, :, 0]

    def prep(i, carry):
        c0 = pl.multiple_of(i * ew_rows, ew_rows)
        lo = pl.multiple_of(jnp.maximum(c0 - SUBLANES, 0), SUBLANES)
        hi = pl.multiple_of(jnp.minimum(c0 + ew_rows, T - SUBLANES), SUBLANES)
        for src, cw_ref, dst, scale in ((q_ref, cq_ref, qs_ref, DK_D ** -0.5), (k_ref, ck_ref, ks_ref, 1.0),
                                        (v_ref, cv_ref, vs_ref, None)):
            x = src[0, pl.ds(c0, ew_rows), :]
            before = jnp.where(i > 0, src[0, pl.ds(lo, SUBLANES), :][SUBLANES - 1:], 0.0)
            after = jnp.where(i < T // ew_rows - 1, src[0, pl.ds(hi, SUBLANES), :][:1], 0.0)
            x_prev = jnp.where(row == 0, before, pltpu.roll(x, 1, axis=0))
            x_next = jnp.where(row == ew_rows - 1, after, pltpu.roll(x, ew_rows - 1, axis=0))
            cw = cw_ref[...]
            y = x_prev * cw[0:1] + x * cw[1:2] + x_next * cw[2:3]
            y = y * jax.nn.sigmoid(y)
            if scale is not None:
                sq = y * y
                s0 = jnp.sum(jnp.where(grp == 0, sq, 0.0), axis=-1, keepdims=True)
                s1 = jnp.sum(jnp.where(grp == 1, sq, 0.0), axis=-1, keepdims=True)
                y = y * lax.rsqrt(jnp.where(grp == 0, s0, s1) + EPS) * scale
            dst[pl.ds(c0, ew_rows), :] = y
        return carry

    lax.fori_loop(0, T // ew_rows, prep, 0)

    def chunk(c0, d):
        sl = pl.ds(c0, CHUNK)
        q = qs_ref[sl, :]
        k = ks_ref[sl, :]
        v = vs_ref[sl, :]
        ab = dab_ref[0, sl, :]
        gb = jnp.where(lane < 2 * H_D, nega_ref[...] * _softplus(ab + dtb_ref[...]), jax.nn.sigmoid(ab))
        gb_t = gb.T
        incl = (ri <= ci) if d else (ri >= ci)
        incl_t = (ri >= ci) if d else (ri <= ci)
        strict = (ri < ci) if d else (ri > ci)

        def head(h):
            jg = d * H_D + 2 * hp + h
            g_c = jnp.sum(jnp.where(lane == jg, gb, 0.0), axis=1, keepdims=True)
            beta_c = jnp.sum(jnp.where(lane == 2 * H_D + jg, gb, 0.0), axis=1, keepdims=True)
            g_r = jnp.sum(jnp.where(subl == jg, gb_t, 0.0), axis=0, keepdims=True)
            G_c = jnp.sum(jnp.where(incl, g_r, 0.0), axis=1, keepdims=True)
            G_r = jnp.sum(jnp.where(incl_t, g_c, 0.0), axis=0, keepdims=True)
            tot = jnp.sum(g_r, axis=1, keepdims=True)
            gam = jnp.where(incl, jnp.exp(jnp.where(incl, G_c - G_r, 0.0)), 0.0)
            kh = jnp.where(grp == h, k, 0.0)
            qh = jnp.where(grp == h, q, 0.0)
            vh = jnp.where(grp == h, v, 0.0)
            kb = kh * beta_c
            khb = kh.astype(BF16)
            a = jnp.where(strict, _dot_nt(kb.astype(BF16), khb) * gam, 0.0)
            a_qk = (_dot_nt(qh.astype(BF16), khb) * gam).astype(BF16)
            yield
            wide = jnp.concatenate([-a, eye], axis=1)
            for _ in range(6):
                w_hi, w_lo = _split_bf16(wide)
                prod = (jnp.dot(w_hi[:, :CHUNK], w_hi, preferred_element_type=F32)
                        + jnp.dot(w_lo[:, :CHUNK], w_hi, preferred_element_type=F32)
                        + jnp.dot(w_hi[:, :CHUNK], w_lo, preferred_element_type=F32))
                wide = prod + jnp.where(lane >= CHUNK, wide, 0.0)
                yield
            t_inv = wide[:, CHUNK:]
            e_g = jnp.exp(G_c)
            rhs = vh * beta_c + pltpu.roll(kb * e_g, HEAD_DIM, axis=1)
            sol = jnp.dot(t_inv.astype(BF16), rhs.astype(BF16), preferred_element_type=F32)
            return sol, a_qk, qh * e_g, kh * jnp.exp(tot - G_c), jnp.exp(tot)

        return head

    per_step = min(DELTA_CHUNKS_PER_STEP, n)
    n_steps = n // per_step
    n_sys = 4 * per_step

    def starts(i):
        c_f = [pl.multiple_of((i * per_step + j) * CHUNK, CHUNK) for j in range(per_step)]
        c_b = [pl.multiple_of((n - 1 - i * per_step - j) * CHUNK, CHUNK) for j in range(per_step)]
        return c_f, c_b

    def head_programs(i):
        programs = []
        for d, c0s in zip((0, 1), starts(i)):
            for c0 in c0s:
                head = chunk(c0, d)
                programs += [head(0), head(1)]
        return programs

    def store_heads(slot, results):
        for s, (r, a_qk, q_g, k_e, d_last) in enumerate(results):
            hw_ref[slot, s, 0] = r
            hw_ref[slot, s, 1] = q_g
            hw_ref[slot, s, 2] = k_e
            ha_ref[slot, s] = a_qk
            hd_ref[slot, s] = jnp.broadcast_to(d_last, (SUBLANES, LANES))

    def load_head(slot, s):
        return hw_ref[slot, s, 0], ha_ref[slot, s], hw_ref[slot, s, 1], hw_ref[slot, s, 2], hd_ref[slot, s][0:1, 0:1]

    def scan_step(d, slot, s):
        (r0, aqk0, qg0, ke0, d0), (r1, aqk1, qg1, ke1, d1) = load_head(slot, s), load_head(slot, s + 1)
        S = st_ref[d]
        Sb = S.astype(BF16)
        w_val = jnp.where(grp == 0, r0, r1)
        k_cum = pltpu.roll(jnp.where(grp == 0, r1, r0), HEAD_DIM, axis=1)
        v_new = w_val - jnp.dot(k_cum.astype(BF16), Sb, preferred_element_type=F32)
        o = jnp.dot((qg0 + qg1).astype(BF16), Sb, preferred_element_type=F32)
        yield
        vnb = v_new.astype(BF16)
        o = o + jnp.where(grp == 0, jnp.dot(aqk0, vnb, preferred_element_type=F32), 0.0)
        o = o + jnp.where(grp == 1, jnp.dot(aqk1, vnb, preferred_element_type=F32), 0.0)
        d_rows = jnp.where(subl // HEAD_DIM == 0, d0, d1)
        st_ref[d] = S * d_rows + jnp.where(bd, _dot_tn((ke0 + ke1).astype(BF16), vnb), 0.0)
        return o

    def scan_dir(d, i, slot):
        c0s = starts(i)[d]
        for j in range(per_step):
            o = yield from scan_step(d, slot, d * 2 * per_step + 2 * j)
            if d == 0:
                o_ref[0, pl.ds(c0s[j], CHUNK), :] = o
            else:
                ob_ref[pl.ds(c0s[j], CHUNK), :] = o

    store_heads(0, _lockstep(head_programs(0)))

    def body(i, carry):
        slot = i % 2
        results = _lockstep(head_programs(i) + [scan_dir(0, i - 1, 1 - slot), scan_dir(1, i - 1, 1 - slot)])
        store_heads(slot, results[:n_sys])
        return carry

    lax.fori_loop(1, n_steps, body, 0)
    last = n_steps - 1
    _lockstep([scan_dir(0, last, last % 2), scan_dir(1, last, last % 2)])
    sfin_ref[0, :, 0] = st_ref[...]

    def gate(i, carry):
        sl = pl.ds(pl.multiple_of(i * ew_rows, ew_rows), ew_rows)
        z = z_ref[0, sl, :]
        y = _head_rmsnorm(o_ref[0, sl, :] + ob_ref[sl, :], grp) * ng_ref[...]
        o_ref[0, sl, :] = y * (z * jax.nn.sigmoid(z))
        return carry

    lax.fori_loop(0, T // ew_rows, gate, 0)


OD_DQ, OD_DK, OD_DV, OD_DZ, OD_DAB = 6, 10, 14, 18, 22


def _delta_call(p, w_conv, a_log, dt_bias, norm_g, s0):
    B, T, _ = p.shape
    n_pair = H_D // 2
    n_sys = 4 * min(DELTA_CHUNKS_PER_STEP, T // CHUNK)
    nega = jnp.zeros((1, LANES), F32).at[0, :2 * H_D].set(-jnp.exp(a_log.reshape(-1)))
    dtb = jnp.zeros((1, LANES), F32).at[0, :2 * H_D].set(dt_bias.reshape(-1))
    ng = jnp.tile(norm_g, 2).reshape(1, LANES)
    col = lambda off: pl.BlockSpec((1, T, LANES), lambda b, hp: (b, 0, off + hp))
    cw = lambda off: pl.BlockSpec((SHORT_CONV, LANES), lambda b, hp: (0, off + hp))
    vec = lambda: pl.BlockSpec((1, LANES), lambda b, hp: (0, 0))
    st = lambda: pl.BlockSpec((1, 2, 1, LANES, LANES), lambda b, hp: (b, 0, hp, 0, 0))
    o, s_fin = pl.pallas_call(
        _delta_kernel,
        grid=(B, n_pair),
        in_specs=[col(OD_DQ), col(OD_DK), col(OD_DV), col(OD_DZ),
                  pl.BlockSpec((1, T, LANES), lambda b, hp: (b, 0, OD_DAB)),
                  cw(0), cw(n_pair), cw(2 * n_pair), vec(), vec(), vec(), st()],
        out_specs=[pl.BlockSpec((1, T, LANES), lambda b, hp: (b, 0, hp)), st()],
        out_shape=[jax.ShapeDtypeStruct((B, T, n_pair * LANES), F32),
                   jax.ShapeDtypeStruct((B, 2, n_pair, LANES, LANES), F32)],
        scratch_shapes=[pltpu.VMEM((T, LANES), F32), pltpu.VMEM((T, LANES), F32), pltpu.VMEM((T, LANES), F32),
                        pltpu.VMEM((T, LANES), F32), pltpu.VMEM((2, LANES, LANES), F32),
                        pltpu.VMEM((2, n_sys, 3, CHUNK, LANES), F32),
                        pltpu.VMEM((2, n_sys, CHUNK, CHUNK), BF16),
                        pltpu.VMEM((2, n_sys, SUBLANES, LANES), F32)],
        compiler_params=_params("parallel", "parallel"),
    )(p, p, p, p, p, w_conv, w_conv, w_conv, nega, dtb, ng, _pair_block_diag(s0))
    return o, _pair_block_diag_inv(s_fin)


def _split_cols(p, sizes):
    cuts = [int(s) for s in np.cumsum(sizes)[:-1]]
    return jnp.split(p, cuts, axis=-1)


def _even_mix(p, latent, rpb, w_g2, b_g, norm_g, ak_cache, av_cache, sb):
    B, T, _ = p.shape
    ak = p[..., H_A * HEAD_DIM:2 * H_A * HEAD_DIM]
    av = p[..., 2 * H_A * HEAD_DIM:3 * H_A * HEAD_DIM]
    n_pair = H_A // 2
    if latent:
        o_a = _nbr_attn_call(p, rpb, ak_cache, av_cache)
        k_h = v_h = None
    else:
        k_h = ak.reshape(B, T, H_A, HEAD_DIM).transpose(0, 2, 1, 3)
        v_h = av.reshape(B, T, H_A, HEAD_DIM).transpose(0, 2, 1, 3)
        o_a = _dense_attn_call(p, 0, n_pair, 2 * n_pair, None, False)
    o_b, s_b = _gla_call(p, w_g2, b_g, norm_g, sb)
    return o_a, o_b, k_h, v_h, s_b


def _odd_mix(p, latent, sink, w_conv, a_log, dt_bias, norm_g, ck_cache, cv_cache, sd):
    B, T, _ = p.shape
    ck = p[..., H_C * HEAD_DIM:(H_C + KV_C) * HEAD_DIM]
    cv = p[..., (H_C + KV_C) * HEAD_DIM:(H_C + 2 * KV_C) * HEAD_DIM]
    n_pair = H_C // 2
    if latent:
        o_c = _win_attn_call(p, sink, ck_cache, cv_cache)
        k_h = v_h = None
    else:
        k_h = ck.reshape(B, T, KV_C, HEAD_DIM).transpose(0, 2, 1, 3)
        v_h = cv.reshape(B, T, KV_C, HEAD_DIM).transpose(0, 2, 1, 3)
        o_c = _dense_attn_call(p, 0, n_pair, n_pair + 1, sink, True)
    o_d, s_d = _delta_call(p, w_conv, a_log, dt_bias, norm_g, sd)
    return o_c, o_d, k_h, v_h, s_d


def _reorder_cols(w, sizes, order):
    parts = _split_cols(w, sizes)
    out = jnp.concatenate([parts[i] for i in order], axis=-1)
    pad = -out.shape[-1] % LANES
    return jnp.pad(out, ((0, 0), (0, pad)))


def _reorder_even_w_in(w):
    return _reorder_cols(w, EV_SIZES, (0, 1, 2, 3, 4, 5, 7, 6))


def _reorder_odd_w_in(w):
    return _reorder_cols(w, OD_SIZES, (0, 1, 2, 3, 4, 5, 8, 6, 7))


TM_LATENT = 512
TM_CONTEXT = 256
TM_FFN_LATENT = 512


def kernel(x_prompt, x_sample, cache_a_k, cache_a_v, state_b, cache_c_k, cache_c_v, state_d, c, c_ctx, ada_w, ada_b, norm1_g, norm2_g, ffn_up, ffn_conv, ffn_down, ev_w_in, ev_w_out, a_rpb, b_w_g2, b_b_g, b_norm_g, od_w_in, od_w_out, c_sink, d_conv, d_a_log, d_dt_bias, d_norm_g, final_g):
    Bp, Tp, D = x_prompt.shape
    Bs, Ts, _ = x_sample.shape
    cond = jnp.concatenate([c, c_ctx[None, :], jnp.zeros((2 * SUBLANES - Bs - 1, D), F32)], axis=0)
    mods = _adaln(cond, ada_w, ada_b)
    xp, xs = x_prompt, x_sample
    streams = ((False, Bp, TM_CONTEXT), (True, Bs, TM_LATENT))
    ak_l, av_l, sb_l, ck_l, cv_l, sd_l = [], [], [], [], [], []
    for l in range(DEPTH):
        j = l // 2
        mod_s = [mods[l, :Bs, i * D:(i + 1) * D] for i in range(6)]
        mod_p = [jnp.broadcast_to(mods[l, Bs:Bs + 1, i * D:(i + 1) * D], (Bp, D)) for i in range(6)]
        w_in = (_reorder_even_w_in(ev_w_in[j]) if l % 2 == 0 else _reorder_odd_w_in(od_w_in[j])).astype(BF16)
        w_out = (ev_w_out[j] if l % 2 == 0 else od_w_out[j]).astype(BF16)
        w_up = ffn_up[l].astype(BF16)
        w_down = ffn_down[l].astype(BF16)
        new_x = []
        for (latent, B, tm), x, mod in zip(streams, (xp, xs), (mod_p, mod_s)):
            sh1, sc1, g1, sh2, sc2, g2 = mod
            p = _nm_matmul(x, norm1_g[l], sh1, sc1, w_in, tm)
            if l % 2 == 0:
                s0 = state_b[:, j] if latent else jnp.zeros((B, 2, H_B, DK_B, DV_B), F32)
                o1, o2, k_h, v_h, s_new = _even_mix(p, latent, a_rpb[j], b_w_g2[j], b_b_g[j], b_norm_g[j],
                                                    cache_a_k[:, j], cache_a_v[:, j], s0)
                if not latent:
                    ak_l.append(k_h)
                    av_l.append(v_h)
                    sb_l.append(s_new)
            else:
                s0 = state_d[:, j] if latent else jnp.zeros((B, 2, H_D, DK_D, DV_D), F32)
                o1, o2, k_h, v_h, s_new = _odd_mix(p, latent, c_sink[j], d_conv[j], d_a_log[j], d_dt_bias[j],
                                                   d_norm_g[j], cache_c_k[:, j], cache_c_v[:, j], s0)
                if not latent:
                    ck_l.append(k_h)
                    cv_l.append(v_h)
                    sd_l.append(s_new)
            x = _mix_ffn_res(x, o1, o2, w_out, g1, norm2_g[l], sh2, sc2, g2, w_up, ffn_conv[l], w_down,
                             TM_FFN_LATENT if latent else tm, final_g if l == DEPTH - 1 else None)
            new_x.append(x)
        xp, xs = new_x
    return (xp, xs, jnp.stack(ak_l, axis=1), jnp.stack(av_l, axis=1), jnp.stack(sb_l, axis=1),
            jnp.stack(ck_l, axis=1), jnp.stack(cv_l, axis=1), jnp.stack(sd_l, axis=1))
```
